```python
import jax
import jax.numpy as jnp
from jax import lax

D_MODEL = 1024
BATCH = 2
SEQ = 8192
DEPTH = 1
DEC_BATCH = 8
DEC_SEQ = 16
PAST_LEN = 1024

CHUNK = 64
PLE_DIM = 256
SB_HEADS = 8
SB_HEAD_DIM = 64
SB_WIDTH = SB_HEADS * SB_HEAD_DIM
SGU_GROUPS = 4
SGU_GROUP_W = 128
SGU_WIDTH = SGU_GROUPS * SGU_GROUP_W
SGU_CHUNK = 128
MIX_WIDTH = SB_WIDTH + SGU_WIDTH
IN_WIDTH = 4 * SB_WIDTH + 3 * SGU_WIDTH
SPLITS = (SB_WIDTH, 2 * SB_WIDTH, 3 * SB_WIDTH, 4 * SB_WIDTH,
          4 * SB_WIDTH + SGU_WIDTH, 4 * SB_WIDTH + 2 * SGU_WIDTH)
Q_BLOCK = 128
EPS = 1e-6

kernel_name = 'hybrid_stickbreak_sgu_stream_step'


def rmsnorm(x, g):
    xf = x.astype(jnp.float32)
    y = xf * lax.rsqrt(jnp.mean(xf * xf, axis=-1, keepdims=True) + EPS)
    return (y * g.astype(jnp.float32)).astype(x.dtype)


def sb_attend(q, k, v, q_pos, k_pos):
    z = jnp.einsum('bqhd,bkhd->bhqk', q.astype(jnp.float32), k.astype(jnp.float32)) * (SB_HEAD_DIM ** -0.5)
    visible = (k_pos[None, :] < q_pos[:, None])[None, None]
    log_keep = jnp.where(visible, jax.nn.log_sigmoid(-z), 0.0)
    log_after = lax.cumsum(log_keep, axis=3, reverse=True) - log_keep
    weights = jnp.where(visible, jnp.exp(jax.nn.log_sigmoid(z) + log_after), 0.0)
    return jnp.einsum('bhqk,bkhd->bqhd', weights, v.astype(jnp.float32))


def sb_prompt(q, k, v):
    b, t = q.shape[0], q.shape[1]
    nb = t // Q_BLOCK
    qb = q.reshape(b, nb, Q_BLOCK, SB_HEADS, SB_HEAD_DIM).transpose(1, 0, 2, 3, 4)
    k_pos = jnp.arange(t)

    def block(args):
        q_blk, i = args
        return sb_attend(q_blk, k, v, i * Q_BLOCK + jnp.arange(Q_BLOCK), k_pos)

    out = lax.map(block, (qb, jnp.arange(nb)))
    return out.transpose(1, 0, 2, 3, 4).reshape(b, t, SB_HEADS, SB_HEAD_DIM)


def mixer_inputs(x, norm_g, w_in, q_norm_g, k_norm_g, sgu_norm_g):
    b, t = x.shape[0], x.shape[1]
    z = rmsnorm(x, norm_g) @ w_in
    q, k, v, g_sb, u, vs, g_sgu = jnp.split(z, SPLITS, axis=-1)
    q = rmsnorm(q.reshape(b, t, SB_HEADS, SB_HEAD_DIM), q_norm_g)
    k = rmsnorm(k.reshape(b, t, SB_HEADS, SB_HEAD_DIM), k_norm_g)
    v = v.reshape(b, t, SB_HEADS, SB_HEAD_DIM)
    u = jax.nn.gelu(u)
    vs = rmsnorm(jax.nn.gelu(vs).reshape(b, t, SGU_GROUPS, SGU_GROUP_W), sgu_norm_g)
    return q, k, v, g_sb, u, vs, g_sgu


def causal_spatial(sgu_w):
    mask = jnp.tril(jnp.ones((SGU_CHUNK, SGU_CHUNK), dtype=bool))
    return jnp.where(mask, sgu_w, 0.0).astype(jnp.float32)


def sgu_prompt(vs, sgu_w, sgu_b):
    b, t = vs.shape[0], vs.shape[1]
    n = t // SGU_CHUNK
    vc = vs.reshape(b, n, SGU_CHUNK, SGU_GROUPS, SGU_GROUP_W).astype(jnp.float32)
    s = jnp.einsum('gts,bnsgc->bntgc', causal_spatial(sgu_w), vc) \
        + sgu_b.T.astype(jnp.float32)[None, None, :, :, None]
    return s.reshape(b, t, SGU_WIDTH)


def sgu_sample(vs, sgu_w, sgu_b):
    b, t = vs.shape[0], vs.shape[1]
    w = causal_spatial(sgu_w)[:, :t, :t]
    s = jnp.einsum('gts,bsgc->btgc', w, vs.astype(jnp.float32)) \
        + sgu_b[:, :t].T.astype(jnp.float32)[None, :, :, None]
    return s.reshape(b, t, SGU_WIDTH)


def mixer_output(x, o_sb, g_sb, u, s, g_sgu, w_out, p, ple_norm_g, w_ple_gate, w_ple_proj):
    b, t = x.shape[0], x.shape[1]
    o = jnp.concatenate([
        o_sb.reshape(b, t, SB_WIDTH) * jax.nn.silu(g_sb.astype(jnp.float32)),
        u.astype(jnp.float32) * s * jax.nn.silu(g_sgu.astype(jnp.float32)),
    ], axis=-1).astype(x.dtype)
    h = x + o @ w_out
    gate = jax.nn.sigmoid(rmsnorm(h, ple_norm_g) @ w_ple_gate)
    return h + gate * (p @ w_ple_proj)


def setup_inputs(seed: int = 0) -> dict:
    key = jax.random.key(seed)
    ks = jax.random.split(key, 20)
    f32 = jnp.float32
    nrm = lambda k, shape: jax.random.normal(k, shape, f32)
    return {
        'x_prompt': nrm(ks[0], (BATCH, SEQ, D_MODEL)),
        'x_sample': nrm(ks[1], (DEC_BATCH, DEC_SEQ, D_MODEL)),
        'cache_k': nrm(ks[2], (DEPTH, DEC_BATCH, PAST_LEN, SB_HEADS, SB_HEAD_DIM)),
        'cache_v': nrm(ks[3], (DEPTH, DEC_BATCH, PAST_LEN, SB_HEADS, SB_HEAD_DIM)),
        'p_prompt': nrm(ks[4], (DEPTH, BATCH, SEQ, PLE_DIM)),
        'p_sample': nrm(ks[5], (DEPTH, DEC_BATCH, DEC_SEQ, PLE_DIM)),
        'norm_g': 1.0 + 0.02 * nrm(ks[6], (DEPTH, D_MODEL)),
        'w_in': nrm(ks[7], (DEPTH, D_MODEL, IN_WIDTH)) * D_MODEL ** -0.5,
        'q_norm_g': 1.0 + 0.02 * nrm(ks[8], (DEPTH, SB_HEAD_DIM)),
        'k_norm_g': 1.0 + 0.02 * nrm(ks[9], (DEPTH, SB_HEAD_DIM)),
        'sgu_norm_g': 1.0 + 0.02 * nrm(ks[10], (DEPTH, SGU_GROUPS, SGU_GROUP_W)),
        'sgu_w': nrm(ks[11], (DEPTH, SGU_GROUPS, SGU_CHUNK, SGU_CHUNK)) * SGU_CHUNK ** -0.5,
        'sgu_b': 1.0 + 0.02 * nrm(ks[12], (DEPTH, SGU_GROUPS, SGU_CHUNK)),
        'w_out': nrm(ks[13], (DEPTH, MIX_WIDTH, D_MODEL)) * MIX_WIDTH ** -0.5,
        'ple_norm_g': 1.0 + 0.02 * nrm(ks[14], (DEPTH, D_MODEL)),
        'w_ple_gate': nrm(ks[15], (DEPTH, D_MODEL, D_MODEL)) * D_MODEL ** -0.5,
        'w_ple_proj': nrm(ks[16], (DEPTH, PLE_DIM, D_MODEL)) * PLE_DIM ** -0.5,
    }


def reference(x_prompt, x_sample, cache_k, cache_v, p_prompt, p_sample, norm_g, w_in, q_norm_g,
              k_norm_g, sgu_norm_g, sgu_w, sgu_b, w_out, ple_norm_g, w_ple_gate, w_ple_proj):
    past = cache_k.shape[2]
    t_s = x_sample.shape[1]
    q_pos_s = past + jnp.arange(t_s)
    k_pos_s = jnp.arange(past + t_s)
    x_p, x_s = x_prompt, x_sample
    k_p_rows, v_p_rows, k_s_rows, v_s_rows, sgu_s_rows = [], [], [], [], []
    for l in range(DEPTH):
        q, k, v, g_sb, u, vs, g_sgu = mixer_inputs(x_p, norm_g[l], w_in[l], q_norm_g[l], k_norm_g[l], sgu_norm_g[l])
        o_sb = sb_prompt(q, k, v)
        s = sgu_prompt(vs, sgu_w[l], sgu_b[l])
        x_p = mixer_output(x_p, o_sb, g_sb, u, s, g_sgu, w_out[l], p_prompt[l],
                           ple_norm_g[l], w_ple_gate[l], w_ple_proj[l])
        k_p_rows.append(k)
        v_p_rows.append(v)
        q, k, v, g_sb, u, vs, g_sgu = mixer_inputs(x_s, norm_g[l], w_in[l], q_norm_g[l], k_norm_g[l], sgu_norm_g[l])
        k_all = jnp.concatenate([cache_k[l].astype(k.dtype), k], axis=1)
        v_all = jnp.concatenate([cache_v[l].astype(v.dtype), v], axis=1)
        o_sb = sb_attend(q, k_all, v_all, q_pos_s, k_pos_s)
        s = sgu_sample(vs, sgu_w[l], sgu_b[l])
        x_s = mixer_output(x_s, o_sb, g_sb, u, s, g_sgu, w_out[l], p_sample[l],
                           ple_norm_g[l], w_ple_gate[l], w_ple_proj[l])
        k_s_rows.append(k)
        v_s_rows.append(v)
        sgu_s_rows.append(vs)
    k_prompt = jnp.stack(k_p_rows)
    v_prompt = jnp.stack(v_p_rows)
    k_sample = jnp.stack(k_s_rows)
    v_sample = jnp.stack(v_s_rows)
    sgu_v_sample = jnp.stack(sgu_s_rows)
    return (x_p, x_s, k_prompt, v_prompt, k_sample, v_sample, sgu_v_sample)
```

```python
import functools
import math

import numpy as np
import jax
import jax.numpy as jnp
from jax import lax
from jax.experimental import pallas as pl
from jax.experimental.pallas import tpu as pltpu

F32 = jnp.float32
BF16 = jnp.bfloat16

LANES = 128
VMEM_BYTES_V7X = 64 * 1024 * 1024

D_MODEL = 1024
PLE_DIM = 256
HEADS = 8
HEAD_DIM = 64
SB_WIDTH = HEADS * HEAD_DIM
GROUPS = 4
GROUP_W = 128
SGU_WIDTH = GROUPS * GROUP_W
SGU_CHUNK = 128
SEG = 512
N_SEG = 7
EPS = 1e-6

Q_SCALE = HEAD_DIM ** -0.5 * math.log2(math.e)

TQ = 256
TK = 256
TM_IN = 512
TM_OUT = 512


def _vmem_limit(block_bytes, temp_bytes):
    need = 2 * block_bytes + temp_bytes
    return int(min(need, VMEM_BYTES_V7X - 8 * 1024 * 1024))


def _nbytes(shape, dtype):
    return int(np.prod(shape)) * jnp.dtype(dtype).itemsize


def _cumsum_matrix(tk):
    j = np.arange(tk)[:, None]
    s = np.arange(tk)[None, :]
    after = np.where(j > s, -1.0, 0.0)
    total = -np.ones((tk, LANES))
    one = np.concatenate([after, total], axis=1)
    return jnp.asarray(np.concatenate([one, one], axis=0), dtype=BF16)


def _head_mean_matrix():
    a = np.arange(SB_WIDTH)
    bd = np.where(a[:, None] // HEAD_DIM == a[None, :] // HEAD_DIM, 1.0 / HEAD_DIM, 0.0)
    return jnp.asarray(np.concatenate([bd, bd], axis=0), dtype=BF16)


def _split_bf16(x):
    hi = x.astype(BF16)
    lo = (x - hi.astype(F32)).astype(BF16)
    return jnp.concatenate([hi, lo], axis=1)


def _gelu_tanh(x):
    return 0.5 * x * (1.0 + jnp.tanh(math.sqrt(2.0 / math.pi) * (x + 0.044715 * (x * x * x))))


def _silu(x):
    return x / (1.0 + jnp.exp(-x))


def _inproj_kernel(x_ref, ng_ref, w_ref, qg_ref, kg_ref, sg_ref, hm_ref, sw_ref, sb_ref,
                   *out_refs, tm, period, attn_layout):
    if attn_layout:
        q_ref, k_ref, v_ref, kt_ref, vb_ref, gsb_ref, osgu_ref = out_refs
    else:
        q_ref, k_ref, v_ref, gsb_ref, osgu_ref, vs_ref = out_refs

    x = x_ref[0]
    ms = jnp.mean(x * x, axis=-1, keepdims=True)
    xn = (x * lax.rsqrt(ms + EPS) * ng_ref[...]).astype(BF16)

    def seg(i):
        return jnp.dot(xn, w_ref[:, i * SEG:(i + 1) * SEG], preferred_element_type=F32)

    def head_rms(t, g):
        ms_h = jnp.dot(_split_bf16(t * t), hm_ref[...], preferred_element_type=F32)
        return t * lax.rsqrt(ms_h + EPS) * g

    q = head_rms(seg(0), qg_ref[...])
    q_ref[0] = (q * Q_SCALE).astype(BF16)

    k = head_rms(seg(1), kg_ref[...])
    k_ref[0] = k
    v = seg(2)
    v_ref[0] = v
    if attn_layout:
        kt_ref[0] = k.T.astype(BF16)
        vb_ref[0] = v.astype(BF16)

    gsb_ref[0] = _silu(seg(3)).astype(BF16)

    u = _gelu_tanh(seg(4))
    vs_raw = _gelu_tanh(seg(5))
    vs_groups = []
    for g in range(GROUPS):
        t = vs_raw[:, g * GROUP_W:(g + 1) * GROUP_W]
        ms_g = jnp.mean(t * t, axis=-1, keepdims=True)
        vs_groups.append(t * lax.rsqrt(ms_g + EPS) * sg_ref[:, g * GROUP_W:(g + 1) * GROUP_W])
    if not attn_layout:
        vs_ref[0] = jnp.concatenate(vs_groups, axis=1)

    row = lax.broadcasted_iota(jnp.int32, (SGU_CHUNK, SGU_CHUNK), 0)
    col = lax.broadcasted_iota(jnp.int32, (SGU_CHUNK, SGU_CHUNK), 1)
    keep = row >= col
    if period != SGU_CHUNK:
        keep = keep & ((row // period) == (col // period))
    s_groups = []
    for g in range(GROUPS):
        wm = jnp.where(keep, sw_ref[g], 0.0).astype(BF16)
        vg = vs_groups[g].astype(BF16)
        chunks = [jnp.dot(wm, vg[c * SGU_CHUNK:(c + 1) * SGU_CHUNK], preferred_element_type=F32)
                  + sb_ref[:, g * GROUP_W:(g + 1) * GROUP_W]
                  for c in range(tm // SGU_CHUNK)]
        s_groups.append(jnp.concatenate(chunks, axis=0) if len(chunks) > 1 else chunks[0])
    s = jnp.concatenate(s_groups, axis=1)

    osgu_ref[0] = (u * s * _silu(seg(6))).astype(BF16)


def _inproj(x, norm_g, w_in_bf, q_norm_g, k_norm_g, sgu_norm_g, sgu_w_tiled, sgu_bias, *,
            tm, period, attn_layout):
    b, t, _ = x.shape
    assert t % tm == 0 and tm % SGU_CHUNK == 0
    grid = (b, t // tm)
    tok = lambda w: pl.BlockSpec((1, tm, w), lambda bi, i: (bi, i, 0))
    const2 = lambda a: pl.BlockSpec(a.shape, lambda bi, i: (0, 0))
    const3 = lambda a: pl.BlockSpec(a.shape, lambda bi, i: (0, 0, 0))

    ng = norm_g.reshape(1, D_MODEL)
    qg = jnp.tile(q_norm_g, HEADS).reshape(1, SB_WIDTH)
    kg = jnp.tile(k_norm_g, HEADS).reshape(1, SB_WIDTH)
    sg = sgu_norm_g.reshape(1, SGU_WIDTH)
    hm = _head_mean_matrix()

    act = lambda dt: jax.ShapeDtypeStruct((b, t, SEG), dt)
    if attn_layout:
        out_shape = (act(BF16), act(F32), act(F32),
                     jax.ShapeDtypeStruct((b, SB_WIDTH, t), BF16), act(BF16), act(BF16), act(BF16))
        out_specs = (tok(SEG), tok(SEG), tok(SEG),
                     pl.BlockSpec((1, SB_WIDTH, tm), lambda bi, i: (bi, 0, i)),
                     tok(SEG), tok(SEG), tok(SEG))
    else:
        out_shape = (act(BF16), act(F32), act(F32), act(BF16), act(BF16), act(F32))
        out_specs = (tok(SEG),) * 6

    block_bytes = (_nbytes((tm, D_MODEL), F32) + _nbytes(w_in_bf.shape, BF16)
                   + _nbytes(hm.shape, BF16) + _nbytes(sgu_w_tiled.shape, F32)
                   + _nbytes(sgu_bias.shape, F32) + 7 * _nbytes((tm, SEG), F32))
    temp_bytes = 12 * _nbytes((tm, SEG), F32)
    return pl.pallas_call(
        functools.partial(_inproj_kernel, tm=tm, period=period, attn_layout=attn_layout),
        grid=grid,
        in_specs=[tok(D_MODEL), const2(ng), const2(w_in_bf), const2(qg), const2(kg), const2(sg),
                  const2(hm), const3(sgu_w_tiled), const2(sgu_bias)],
        out_specs=out_specs,
        out_shape=out_shape,
        compiler_params=pltpu.CompilerParams(
            dimension_semantics=("arbitrary", "arbitrary"),
            vmem_limit_bytes=_vmem_limit(block_bytes, temp_bytes)),
        name="inproj_prompt" if attn_layout else "inproj_sample",
    )(x, ng, w_in_bf, qg, kg, sg, hm, sgu_w_tiled, sgu_bias)


def _sb_block(z, vis, rem, uu, tk):
    a = jnp.exp2(-jnp.abs(z))
    sp = jnp.maximum(z, 0.0) + jnp.log2(1.0 + a)
    log_beta = z - sp
    if vis is not None:
        sp = jnp.where(vis, sp, 0.0)
    r = jnp.dot(_split_bf16(sp), uu, preferred_element_type=F32)
    log_after = r[:, :tk] + jnp.concatenate([rem] * (tk // LANES), axis=1)
    w = jnp.exp2(log_beta + log_after)
    if vis is not None:
        w = jnp.where(vis, w, 0.0)
    return w, rem + r[:, tk:]


def _attn_prompt_kernel(q_ref, kt_ref, v_ref, g_ref, uu_ref, o_ref, rem0_ref, rem1_ref, acc_ref,
                        *, tq, tk):
    qi = pl.program_id(2)
    q = q_ref[0]
    q0 = q[:, :HEAD_DIM]
    q1 = q[:, HEAD_DIM:]
    lane = lax.broadcasted_iota(jnp.int32, (tk, LANES), 1)
    first_head = lane < HEAD_DIM

    def block(j, vis):
        ks = pl.multiple_of(j * tk, tk)
        kt = kt_ref[0, :, pl.ds(ks, tk)]
        vv = v_ref[0, pl.ds(ks, tk), :]
        z0 = jnp.dot(q0, kt[:HEAD_DIM], preferred_element_type=F32)
        z1 = jnp.dot(q1, kt[HEAD_DIM:], preferred_element_type=F32)
        uu = uu_ref[...]
        w0, r0 = _sb_block(z0, vis, rem0_ref[...], uu, tk)
        w1, r1 = _sb_block(z1, vis, rem1_ref[...], uu, tk)
        rem0_ref[...] = r0
        rem1_ref[...] = r1
        zero = jnp.zeros_like(vv)
        v_bd = jnp.concatenate([jnp.where(first_head, vv, zero), jnp.where(first_head, zero, vv)],
                               axis=0)
        w = jnp.concatenate([w0.astype(BF16), w1.astype(BF16)], axis=1)
        acc_ref[...] += jnp.dot(w, v_bd, preferred_element_type=F32)

    rem0_ref[...] = jnp.zeros_like(rem0_ref)
    rem1_ref[...] = jnp.zeros_like(rem1_ref)
    acc_ref[...] = jnp.zeros_like(acc_ref)

    row = lax.broadcasted_iota(jnp.int32, (tq, tk), 0)
    col = lax.broadcasted_iota(jnp.int32, (tq, tk), 1)
    block(qi, col < row)

    def body(i, carry):
        block(qi - 1 - i, None)
        return carry

    lax.fori_loop(0, qi, body, 0)
    o_ref[0] = (acc_ref[...] * g_ref[0].astype(F32)).astype(BF16)


def _attn_prompt(q, kt, vb, gsb, *, tq, tk):
    b, t, _ = q.shape
    assert tq == tk and t % tq == 0
    pairs = HEADS // 2
    uu = _cumsum_matrix(tk)
    grid = (b, pairs, t // tq)
    row_blk = pl.BlockSpec((1, tq, 2 * HEAD_DIM), lambda bi, hp, i: (bi, i, hp))
    block_bytes = (3 * _nbytes((tq, LANES), BF16) + _nbytes((LANES, t), BF16)
                   + _nbytes((t, LANES), BF16) + _nbytes(uu.shape, BF16))
    temp_bytes = 3 * _nbytes((tq, LANES), F32) + 16 * _nbytes((tq, tk), F32)
    return pl.pallas_call(
        functools.partial(_attn_prompt_kernel, tq=tq, tk=tk),
        grid=grid,
        in_specs=[row_blk,
                  pl.BlockSpec((1, 2 * HEAD_DIM, t), lambda bi, hp, i: (bi, hp, 0)),
                  pl.BlockSpec((1, t, 2 * HEAD_DIM), lambda bi, hp, i: (bi, 0, hp)),
                  row_blk,
                  pl.BlockSpec(uu.shape, lambda bi, hp, i: (0, 0))],
        out_specs=row_blk,
        out_shape=jax.ShapeDtypeStruct((b, t, SB_WIDTH), BF16),
        scratch_shapes=[pltpu.VMEM((tq, LANES), F32), pltpu.VMEM((tq, LANES), F32),
                        pltpu.VMEM((tq, LANES), F32)],
        compiler_params=pltpu.CompilerParams(
            dimension_semantics=("arbitrary", "arbitrary", "arbitrary"),
            vmem_limit_bytes=_vmem_limit(block_bytes, temp_bytes)),
        name="sb_attn_prompt",
    )(q, kt, vb, gsb, uu)


def _attn_sample_kernel(q_ref, ck_ref, cv_ref, kn_ref, vn_ref, g_ref, uu_ref, uun_ref, o_ref,
                        *, t_new, past, tk):
    q = q_ref[0]
    rows = HEADS * t_new

    def logits(kblk):
        return jnp.concatenate(
            [lax.dot_general(q[:, h * HEAD_DIM:(h + 1) * HEAD_DIM],
                             kblk[:, h * HEAD_DIM:(h + 1) * HEAD_DIM],
                             (((1,), (1,)), ((), ())), preferred_element_type=F32)
             for h in range(HEADS)], axis=0)

    def attend(w, vblk):
        return [jnp.dot(w[h * t_new:(h + 1) * t_new].astype(BF16),
                        vblk[:, h * HEAD_DIM:(h + 1) * HEAD_DIM], preferred_element_type=F32)
                for h in range(HEADS)]

    pad = jnp.zeros((LANES - t_new, SB_WIDTH), F32)
    kn = jnp.concatenate([kn_ref[0], pad], axis=0).astype(BF16)
    vn = jnp.concatenate([vn_ref[0], pad], axis=0).astype(BF16)
    row = lax.broadcasted_iota(jnp.int32, (rows, LANES), 0)
    col = lax.broadcasted_iota(jnp.int32, (rows, LANES), 1)
    vis = col < (row % t_new)
    rem = jnp.zeros((rows, LANES), F32)
    w, rem = _sb_block(logits(kn), vis, rem, uun_ref[...], LANES)
    acc = attend(w, vn)

    for j in reversed(range(past // tk)):
        kb = ck_ref[0, j * tk:(j + 1) * tk, :].astype(BF16)
        vb = cv_ref[0, j * tk:(j + 1) * tk, :].astype(BF16)
        w, rem = _sb_block(logits(kb), None, rem, uu_ref[...], tk)
        acc = [a + d for a, d in zip(acc, attend(w, vb))]

    o = jnp.concatenate(acc, axis=1)
    o_ref[0] = (o * g_ref[0].astype(F32)).astype(BF16)


def _attn_sample(q, cache_k, cache_v, k_new, v_new, gsb, *, tk):
    b, t_new, _ = q.shape
    past = cache_k.shape[1]
    assert past % tk == 0 and t_new <= LANES
    uu = _cumsum_matrix(tk)
    uun = _cumsum_matrix(LANES)
    new_blk = pl.BlockSpec((1, t_new, SB_WIDTH), lambda bi: (bi, 0, 0))
    cache_blk = pl.BlockSpec((1, past, SB_WIDTH), lambda bi: (bi, 0, 0))
    block_bytes = (2 * _nbytes((past, SB_WIDTH), F32) + 6 * _nbytes((t_new, SB_WIDTH), F32)
                   + _nbytes(uu.shape, BF16) + _nbytes(uun.shape, BF16))
    temp_bytes = 24 * _nbytes((HEADS * t_new, tk), F32) + 4 * _nbytes((tk, SB_WIDTH), F32)
    return pl.pallas_call(
        functools.partial(_attn_sample_kernel, t_new=t_new, past=past, tk=tk),
        grid=(b,),
        in_specs=[new_blk, cache_blk, cache_blk, new_blk, new_blk, new_blk,
                  pl.BlockSpec(uu.shape, lambda bi: (0, 0)),
                  pl.BlockSpec(uun.shape, lambda bi: (0, 0))],
        out_specs=new_blk,
        out_shape=jax.ShapeDtypeStruct((b, t_new, SB_WIDTH), BF16),
        compiler_params=pltpu.CompilerParams(
            dimension_semantics=("arbitrary",),
            vmem_limit_bytes=_vmem_limit(block_bytes, temp_bytes)),
        name="sb_attn_sample",
    )(q, cache_k, cache_v, k_new, v_new, gsb, uu, uun)


def _outproj_kernel(x_ref, osb_ref, osgu_ref, p_ref, wo_ref, pg_ref, wg_ref, wp_ref, y_ref):
    h = (x_ref[0]
         + jnp.dot(osb_ref[0], wo_ref[:SB_WIDTH], preferred_element_type=F32)
         + jnp.dot(osgu_ref[0], wo_ref[SB_WIDTH:], preferred_element_type=F32))
    ms = jnp.mean(h * h, axis=-1, keepdims=True)
    hn = (h * lax.rsqrt(ms + EPS) * pg_ref[...]).astype(BF16)
    gate_logit = jnp.dot(hn, wg_ref[...], preferred_element_type=F32)
    gate = 1.0 / (1.0 + jnp.exp(-gate_logit))
    pp = jnp.dot(p_ref[0].astype(BF16), wp_ref[...], preferred_element_type=F32)
    y_ref[0] = h + gate * pp


def _outproj(x, osb, osgu, p, w_out_bf, ple_norm_g, w_gate_bf, w_proj_bf, *, tm, name):
    b, t, _ = x.shape
    assert t % tm == 0
    tok = lambda w: pl.BlockSpec((1, tm, w), lambda bi, i: (bi, i, 0))
    const2 = lambda a: pl.BlockSpec(a.shape, lambda bi, i: (0, 0))
    pg = ple_norm_g.reshape(1, D_MODEL)
    block_bytes = (2 * _nbytes((tm, D_MODEL), F32) + 2 * _nbytes((tm, SEG), BF16)
                   + _nbytes((tm, PLE_DIM), F32) + _nbytes(w_out_bf.shape, BF16)
                   + _nbytes(w_gate_bf.shape, BF16) + _nbytes(w_proj_bf.shape, BF16))
    temp_bytes = 6 * _nbytes((tm, D_MODEL), F32)
    return pl.pallas_call(
        _outproj_kernel,
        grid=(b, t // tm),
        in_specs=[tok(D_MODEL), tok(SEG), tok(SEG), tok(PLE_DIM), const2(w_out_bf), const2(pg),
                  const2(w_gate_bf), const2(w_proj_bf)],
        out_specs=tok(D_MODEL),
        out_shape=jax.ShapeDtypeStruct((b, t, D_MODEL), F32),
        compiler_params=pltpu.CompilerParams(
            dimension_semantics=("arbitrary", "arbitrary"),
            vmem_limit_bytes=_vmem_limit(block_bytes, temp_bytes)),
        name=name,
    )(x, osb, osgu, p, w_out_bf, pg, w_gate_bf, w_proj_bf)


def _sgu_bias_rows(sgu_b_l, period):
    per_pos = jnp.tile(sgu_b_l[:, :period].T, (SGU_CHUNK // period, 1))
    return jnp.repeat(per_pos, GROUP_W, axis=1)


def kernel(x_prompt, x_sample, cache_k, cache_v, p_prompt, p_sample, norm_g, w_in, q_norm_g,
           k_norm_g, sgu_norm_g, sgu_w, sgu_b, w_out, ple_norm_g, w_ple_gate, w_ple_proj):
    depth = w_in.shape[0]
    assert depth == 1, "one layer per call"
    l = 0
    b_p, t_p, _ = x_prompt.shape
    b_s, t_s, _ = x_sample.shape
    past = cache_k.shape[2]
    n_s = b_s * t_s
    assert n_s == SGU_CHUNK and SGU_CHUNK % t_s == 0

    w_in_bf = w_in[l].astype(BF16)
    w_out_bf = w_out[l].astype(BF16)
    w_gate_bf = w_ple_gate[l].astype(BF16)
    w_proj_bf = w_ple_proj[l].astype(BF16)

    q, k, v, kt, vb, gsb, osgu = _inproj(
        x_prompt, norm_g[l], w_in_bf, q_norm_g[l], k_norm_g[l], sgu_norm_g[l],
        sgu_w[l], _sgu_bias_rows(sgu_b[l], SGU_CHUNK),
        tm=TM_IN, period=SGU_CHUNK, attn_layout=True)
    osb = _attn_prompt(q, kt, vb, gsb, tq=TQ, tk=TK)
    y_prompt = _outproj(x_prompt, osb, osgu, p_prompt[l], w_out_bf, ple_norm_g[l], w_gate_bf,
                        w_proj_bf, tm=TM_OUT, name="outproj_prompt")

    rep = SGU_CHUNK // t_s
    sgu_w_s = jnp.tile(sgu_w[l][:, :t_s, :t_s], (1, rep, rep))
    xs = x_sample.reshape(1, n_s, D_MODEL)
    q_s, k_s, v_s, gsb_s, osgu_s, vs_s = _inproj(
        xs, norm_g[l], w_in_bf, q_norm_g[l], k_norm_g[l], sgu_norm_g[l],
        sgu_w_s, _sgu_bias_rows(sgu_b[l], t_s),
        tm=n_s, period=t_s, attn_layout=False)
    shp = (b_s, t_s, SB_WIDTH)
    osb_s = _attn_sample(q_s.reshape(shp), cache_k[l].reshape(b_s, past, SB_WIDTH),
                         cache_v[l].reshape(b_s, past, SB_WIDTH), k_s.reshape(shp),
                         v_s.reshape(shp), gsb_s.reshape(shp), tk=TK)
    y_sample = _outproj(xs, osb_s.reshape(1, n_s, SB_WIDTH), osgu_s, p_sample[l].reshape(1, n_s, PLE_DIM),
                        w_out_bf, ple_norm_g[l], w_gate_bf, w_proj_bf, tm=n_s,
                        name="outproj_sample").reshape(b_s, t_s, D_MODEL)

    head_shape = lambda a, bb, tt: a.reshape(1, bb, tt, HEADS, HEAD_DIM)
    return (y_prompt, y_sample,
            head_shape(k, b_p, t_p), head_shape(v, b_p, t_p),
            head_shape(k_s, b_s, t_s), head_shape(v_s, b_s, t_s),
            vs_s.reshape(1, b_s, t_s, GROUPS, GROUP_W))
```

```python
import functools
import math

import numpy as np
import jax
import jax.numpy as jnp
from jax import lax
from jax.experimental import pallas as pl
from jax.experimental.pallas import tpu as pltpu

F32 = jnp.float32
BF16 = jnp.bfloat16

LANES = 128
VMEM_BYTES_V7X = 64 * 1024 * 1024

D_MODEL = 1024
PLE_DIM = 256
HEADS = 8
HEAD_DIM = 64
SB_WIDTH = HEADS * HEAD_DIM
GROUPS = 4
GROUP_W = 128
SGU_WIDTH = GROUPS * GROUP_W
SGU_CHUNK = 128
SEG = 512
N_SEG = 7
EPS = 1e-6

Q_SCALE = HEAD_DIM ** -0.5 * math.log2(math.e)

KSUB = LANES
NSUB = 4
TQ = NSUB * KSUB
TM_IN = 512
TM_OUT = 512


def _vmem_limit(block_bytes, temp_bytes):
    need = 2 * block_bytes + temp_bytes
    return int(min(need, VMEM_BYTES_V7X - 8 * 1024 * 1024))


def _nbytes(shape, dtype):
    return int(np.prod(shape)) * jnp.dtype(dtype).itemsize


def _suffix_sum_matrix():
    j = np.arange(KSUB)[:, None]
    s = np.arange(KSUB)[None, :]
    one = np.where(j >= s, -1.0, 0.0)
    return jnp.asarray(np.concatenate([one, one], axis=0), dtype=BF16)


def _head_mean_matrix():
    a = np.arange(SB_WIDTH)
    bd = np.where(a[:, None] // HEAD_DIM == a[None, :] // HEAD_DIM, 1.0 / HEAD_DIM, 0.0)
    return jnp.asarray(np.concatenate([bd, bd], axis=0), dtype=BF16)


def _split_bf16(x):
    hi = x.astype(BF16)
    lo = (x - hi.astype(F32)).astype(BF16)
    return jnp.concatenate([hi, lo], axis=1)


def _gelu_tanh(x):
    return 0.5 * x * (1.0 + jnp.tanh(math.sqrt(2.0 / math.pi) * (x + 0.044715 * (x * x * x))))


def _silu(x):
    return x / (1.0 + jnp.exp(-x))


def _inproj_kernel(x_ref, ng_ref, w_ref, qg_ref, kg_ref, sg_ref, hm_ref, sw_ref, sb_ref,
                   *out_refs, tm, period, attn_layout):
    if attn_layout:
        q_ref, k_ref, v_ref, kt_ref, vb_ref, gsb_ref, osgu_ref = out_refs
    else:
        q_ref, k_ref, v_ref, gsb_ref, osgu_ref, vs_ref = out_refs

    x = x_ref[0]
    ms = jnp.mean(x * x, axis=-1, keepdims=True)
    xn = (x * lax.rsqrt(ms + EPS) * ng_ref[...]).astype(BF16)

    def seg(i):
        return jnp.dot(xn, w_ref[:, i * SEG:(i + 1) * SEG], preferred_element_type=F32)

    def head_rms(t, g):
        ms_h = jnp.dot(_split_bf16(t * t), hm_ref[...], preferred_element_type=F32)
        return t * lax.rsqrt(ms_h + EPS) * g

    q = head_rms(seg(0), qg_ref[...])
    q_ref[0] = (q * Q_SCALE).astype(BF16)

    k = head_rms(seg(1), kg_ref[...])
    k_ref[0] = k
    v = seg(2)
    v_ref[0] = v
    if attn_layout:
        kt_ref[0] = k.T.astype(BF16)
        vb_ref[0] = v.astype(BF16)

    gsb_ref[0] = _silu(seg(3)).astype(BF16)

    u = _gelu_tanh(seg(4))
    vs_raw = _gelu_tanh(seg(5))
    vs_groups = []
    for g in range(GROUPS):
        t = vs_raw[:, g * GROUP_W:(g + 1) * GROUP_W]
        ms_g = jnp.mean(t * t, axis=-1, keepdims=True)
        vs_groups.append(t * lax.rsqrt(ms_g + EPS) * sg_ref[:, g * GROUP_W:(g + 1) * GROUP_W])
    if not attn_layout:
        vs_ref[0] = jnp.concatenate(vs_groups, axis=1)

    row = lax.broadcasted_iota(jnp.int32, (SGU_CHUNK, SGU_CHUNK), 0)
    col = lax.broadcasted_iota(jnp.int32, (SGU_CHUNK, SGU_CHUNK), 1)
    keep = row >= col
    if period != SGU_CHUNK:
        keep = keep & ((row // period) == (col // period))
    s_groups = []
    for g in range(GROUPS):
        wm = jnp.where(keep, sw_ref[g], 0.0).astype(BF16)
        vg = vs_groups[g].astype(BF16)
        chunks = [jnp.dot(wm, vg[c * SGU_CHUNK:(c + 1) * SGU_CHUNK], preferred_element_type=F32)
                  + sb_ref[:, g * GROUP_W:(g + 1) * GROUP_W]
                  for c in range(tm // SGU_CHUNK)]
        s_groups.append(jnp.concatenate(chunks, axis=0) if len(chunks) > 1 else chunks[0])
    s = jnp.concatenate(s_groups, axis=1)

    osgu_ref[0] = (u * s * _silu(seg(6))).astype(BF16)


def _inproj(x, norm_g, w_in_bf, q_norm_g, k_norm_g, sgu_norm_g, sgu_w_tiled, sgu_bias, *,
            tm, period, attn_layout):
    b, t, _ = x.shape
    assert t % tm == 0 and tm % SGU_CHUNK == 0
    grid = (b, t // tm)
    tok = lambda w: pl.BlockSpec((1, tm, w), lambda bi, i: (bi, i, 0))
    const2 = lambda a: pl.BlockSpec(a.shape, lambda bi, i: (0, 0))
    const3 = lambda a: pl.BlockSpec(a.shape, lambda bi, i: (0, 0, 0))

    ng = norm_g.reshape(1, D_MODEL)
    qg = jnp.tile(q_norm_g, HEADS).reshape(1, SB_WIDTH)
    kg = jnp.tile(k_norm_g, HEADS).reshape(1, SB_WIDTH)
    sg = sgu_norm_g.reshape(1, SGU_WIDTH)
    hm = _head_mean_matrix()

    act = lambda dt: jax.ShapeDtypeStruct((b, t, SEG), dt)
    if attn_layout:
        out_shape = (act(BF16), act(F32), act(F32),
                     jax.ShapeDtypeStruct((b, SB_WIDTH, t), BF16), act(BF16), act(BF16), act(BF16))
        out_specs = (tok(SEG), tok(SEG), tok(SEG),
                     pl.BlockSpec((1, SB_WIDTH, tm), lambda bi, i: (bi, 0, i)),
                     tok(SEG), tok(SEG), tok(SEG))
    else:
        out_shape = (act(BF16), act(F32), act(F32), act(BF16), act(BF16), act(F32))
        out_specs = (tok(SEG),) * 6

    block_bytes = (_nbytes((tm, D_MODEL), F32) + _nbytes(w_in_bf.shape, BF16)
                   + _nbytes(hm.shape, BF16) + _nbytes(sgu_w_tiled.shape, F32)
                   + _nbytes(sgu_bias.shape, F32) + 7 * _nbytes((tm, SEG), F32))
    temp_bytes = 12 * _nbytes((tm, SEG), F32)
    return pl.pallas_call(
        functools.partial(_inproj_kernel, tm=tm, period=period, attn_layout=attn_layout),
        grid=grid,
        in_specs=[tok(D_MODEL), const2(ng), const2(w_in_bf), const2(qg), const2(kg), const2(sg),
                  const2(hm), const3(sgu_w_tiled), const2(sgu_bias)],
        out_specs=out_specs,
        out_shape=out_shape,
        compiler_params=pltpu.CompilerParams(
            dimension_semantics=("arbitrary", "arbitrary"),
            vmem_limit_bytes=_vmem_limit(block_bytes, temp_bytes)),
        name="inproj_prompt" if attn_layout else "inproj_sample",
    )(x, ng, w_in_bf, qg, kg, sg, hm, sgu_w_tiled, sgu_bias)


def _sb_block(z, vis, rem, uu):
    sign = jnp.uint32(0x80000000)
    neg_abs = lax.bitcast_convert_type(lax.bitcast_convert_type(z, jnp.uint32) | sign, F32)
    sp = jnp.maximum(z, 0.0) + jnp.log2(1.0 + jnp.exp2(neg_abs))
    if vis is not None:
        sp = jnp.where(vis, sp, 0.0)
    suffix = jnp.dot(_split_bf16(sp), uu, preferred_element_type=F32)
    w = jnp.exp2(z + suffix + rem)
    if vis is not None:
        w = jnp.where(vis, w, 0.0)
    return w.astype(BF16), rem - jnp.sum(sp, axis=1, keepdims=True)


def _attn_prompt_kernel(q_ref, kt_ref, v_ref, g_ref, uu_ref, o_ref, rem0_ref, rem1_ref, acc_ref,
                        *, tq):
    qi = pl.program_id(2)
    q = q_ref[0]
    qh = (q[:, :HEAD_DIM], q[:, HEAD_DIM:])
    uu = uu_ref[...]
    lane = lax.broadcasted_iota(jnp.int32, (KSUB, LANES), 1)
    first_head = lane < HEAD_DIM

    def pair_values(vv):
        zero = jnp.zeros_like(vv)
        return jnp.concatenate([jnp.where(first_head, vv, zero),
                                jnp.where(first_head, zero, vv)], axis=0)

    def sub_block(ks, r0, rems, vis):
        kt = kt_ref[0, :, pl.ds(ks, KSUB)]
        ws, new = [], []
        for h in range(2):
            z = jnp.dot(qh[h][r0:], kt[h * HEAD_DIM:(h + 1) * HEAD_DIM],
                        preferred_element_type=F32)
            w, r = _sb_block(z, vis, rems[h], uu)
            ws.append(w)
            new.append(r)
        return jnp.concatenate(ws, axis=1), new

    base = pl.multiple_of(qi * tq, tq)
    rems = [jnp.zeros((tq, LANES), F32)] * 2
    acc = jnp.zeros((tq, LANES), F32)
    for s in reversed(range(NSUB)):
        r0 = s * KSUB
        ks = pl.multiple_of(base + r0, KSUB)
        row = lax.broadcasted_iota(jnp.int32, (tq - r0, KSUB), 0)
        col = lax.broadcasted_iota(jnp.int32, (tq - r0, KSUB), 1)
        w, new = sub_block(ks, r0, [r[r0:] for r in rems], col < row)
        part = jnp.dot(w, pair_values(v_ref[0, pl.ds(ks, KSUB), :]), preferred_element_type=F32)
        if r0:
            rems = [jnp.concatenate([r[:r0], n], axis=0) for r, n in zip(rems, new)]
            acc = jnp.concatenate([acc[:r0], acc[r0:] + part], axis=0)
        else:
            rems, acc = new, acc + part
    rem0_ref[...] = rems[0]
    rem1_ref[...] = rems[1]
    acc_ref[...] = acc

    def body(i, carry):
        kb = pl.multiple_of((qi - 1 - i) * tq, tq)
        rems = [rem0_ref[...], rem1_ref[...]]
        ws, vs = [None] * NSUB, [None] * NSUB
        for s in reversed(range(NSUB)):
            ks = pl.multiple_of(kb + s * KSUB, KSUB)
            ws[s], rems = sub_block(ks, 0, rems, None)
            vs[s] = pair_values(v_ref[0, pl.ds(ks, KSUB), :])
        acc_ref[...] += jnp.dot(jnp.concatenate(ws, axis=1), jnp.concatenate(vs, axis=0),
                                preferred_element_type=F32)
        rem0_ref[...] = rems[0]
        rem1_ref[...] = rems[1]
        return carry

    lax.fori_loop(0, qi, body, 0)
    o_ref[0] = (acc_ref[...] * g_ref[0].astype(F32)).astype(BF16)


def _attn_prompt(q, kt, vb, gsb, *, tq):
    b, t, _ = q.shape
    assert t % tq == 0 and tq == NSUB * KSUB
    pairs = HEADS // 2
    uu = _suffix_sum_matrix()
    grid = (b, pairs, t // tq)
    row_blk = pl.BlockSpec((1, tq, 2 * HEAD_DIM), lambda bi, hp, i: (bi, i, hp))
    block_bytes = (3 * _nbytes((tq, LANES), BF16) + _nbytes((LANES, t), BF16)
                   + _nbytes((t, LANES), BF16) + _nbytes(uu.shape, BF16))
    temp_bytes = 3 * _nbytes((tq, LANES), F32) + 8 * NSUB * 2 * _nbytes((tq, KSUB), F32)
    return pl.pallas_call(
        functools.partial(_attn_prompt_kernel, tq=tq),
        grid=grid,
        in_specs=[row_blk,
                  pl.BlockSpec((1, 2 * HEAD_DIM, t), lambda bi, hp, i: (bi, hp, 0)),
                  pl.BlockSpec((1, t, 2 * HEAD_DIM), lambda bi, hp, i: (bi, 0, hp)),
                  row_blk,
                  pl.BlockSpec(uu.shape, lambda bi, hp, i: (0, 0))],
        out_specs=row_blk,
        out_shape=jax.ShapeDtypeStruct((b, t, SB_WIDTH), BF16),
        scratch_shapes=[pltpu.VMEM((tq, LANES), F32), pltpu.VMEM((tq, LANES), F32),
                        pltpu.VMEM((tq, LANES), F32)],
        compiler_params=pltpu.CompilerParams(
            dimension_semantics=("arbitrary", "arbitrary", "arbitrary"),
            vmem_limit_bytes=_vmem_limit(block_bytes, temp_bytes)),
        name="sb_attn_prompt",
    )(q, kt, vb, gsb, uu)


def _attn_sample_kernel(q_ref, ck_ref, cv_ref, kn_ref, vn_ref, g_ref, uu_ref, o_ref,
                        *, t_new, past):
    q = q_ref[0]
    rows = HEADS * t_new
    uu = uu_ref[...]

    def logits(kblk):
        return jnp.concatenate(
            [lax.dot_general(q[:, h * HEAD_DIM:(h + 1) * HEAD_DIM],
                             kblk[:, h * HEAD_DIM:(h + 1) * HEAD_DIM],
                             (((1,), (1,)), ((), ())), preferred_element_type=F32)
             for h in range(HEADS)], axis=0)

    def attend(w, vblk):
        return [jnp.dot(w[h * t_new:(h + 1) * t_new],
                        vblk[:, h * HEAD_DIM:(h + 1) * HEAD_DIM], preferred_element_type=F32)
                for h in range(HEADS)]

    pad = jnp.zeros((LANES - t_new, SB_WIDTH), F32)
    kn = jnp.concatenate([kn_ref[0], pad], axis=0).astype(BF16)
    vn = jnp.concatenate([vn_ref[0], pad], axis=0).astype(BF16)
    row = lax.broadcasted_iota(jnp.int32, (rows, LANES), 0)
    col = lax.broadcasted_iota(jnp.int32, (rows, LANES), 1)
    vis = col < (row % t_new)
    rem = jnp.zeros((rows, LANES), F32)
    w, rem = _sb_block(logits(kn), vis, rem, uu)
    acc = attend(w, vn)

    for j in reversed(range(past // KSUB)):
        kb = ck_ref[0, j * KSUB:(j + 1) * KSUB, :].astype(BF16)
        vb = cv_ref[0, j * KSUB:(j + 1) * KSUB, :].astype(BF16)
        w, rem = _sb_block(logits(kb), None, rem, uu)
        acc = [a + d for a, d in zip(acc, attend(w, vb))]

    o = jnp.concatenate(acc, axis=1)
    o_ref[0] = (o * g_ref[0].astype(F32)).astype(BF16)


def _attn_sample(q, cache_k, cache_v, k_new, v_new, gsb):
    b, t_new, _ = q.shape
    past = cache_k.shape[1]
    assert past % KSUB == 0 and t_new <= KSUB
    uu = _suffix_sum_matrix()
    new_blk = pl.BlockSpec((1, t_new, SB_WIDTH), lambda bi: (bi, 0, 0))
    cache_blk = pl.BlockSpec((1, past, SB_WIDTH), lambda bi: (bi, 0, 0))
    block_bytes = (2 * _nbytes((past, SB_WIDTH), F32) + 6 * _nbytes((t_new, SB_WIDTH), F32)
                   + _nbytes(uu.shape, BF16))
    temp_bytes = (8 * (past // KSUB + 1) * _nbytes((HEADS * t_new, KSUB), F32)
                  + 4 * _nbytes((past, SB_WIDTH), BF16))
    return pl.pallas_call(
        functools.partial(_attn_sample_kernel, t_new=t_new, past=past),
        grid=(b,),
        in_specs=[new_blk, cache_blk, cache_blk, new_blk, new_blk, new_blk,
                  pl.BlockSpec(uu.shape, lambda bi: (0, 0))],
        out_specs=new_blk,
        out_shape=jax.ShapeDtypeStruct((b, t_new, SB_WIDTH), BF16),
        compiler_params=pltpu.CompilerParams(
            dimension_semantics=("arbitrary",),
            vmem_limit_bytes=_vmem_limit(block_bytes, temp_bytes)),
        name="sb_attn_sample",
    )(q, cache_k, cache_v, k_new, v_new, gsb, uu)


def _outproj_kernel(x_ref, osb_ref, osgu_ref, p_ref, wo_ref, pg_ref, wg_ref, wp_ref, y_ref):
    h = (x_ref[0]
         + jnp.dot(osb_ref[0], wo_ref[:SB_WIDTH], preferred_element_type=F32)
         + jnp.dot(osgu_ref[0], wo_ref[SB_WIDTH:], preferred_element_type=F32))
    ms = jnp.mean(h * h, axis=-1, keepdims=True)
    hn = (h * lax.rsqrt(ms + EPS) * pg_ref[...]).astype(BF16)
    gate_logit = jnp.dot(hn, wg_ref[...], preferred_element_type=F32)
    gate = 1.0 / (1.0 + jnp.exp(-gate_logit))
    pp = jnp.dot(p_ref[0].astype(BF16), wp_ref[...], preferred_element_type=F32)
    y_ref[0] = h + gate * pp


def _outproj(x, osb, osgu, p, w_out_bf, ple_norm_g, w_gate_bf, w_proj_bf, *, tm, name):
    b, t, _ = x.shape
    assert t % tm == 0
    tok = lambda w: pl.BlockSpec((1, tm, w), lambda bi, i: (bi, i, 0))
    const2 = lambda a: pl.BlockSpec(a.shape, lambda bi, i: (0, 0))
    pg = ple_norm_g.reshape(1, D_MODEL)
    block_bytes = (2 * _nbytes((tm, D_MODEL), F32) + 2 * _nbytes((tm, SEG), BF16)
                   + _nbytes((tm, PLE_DIM), F32) + _nbytes(w_out_bf.shape, BF16)
                   + _nbytes(w_gate_bf.shape, BF16) + _nbytes(w_proj_bf.shape, BF16))
    temp_bytes = 6 * _nbytes((tm, D_MODEL), F32)
    return pl.pallas_call(
        _outproj_kernel,
        grid=(b, t // tm),
        in_specs=[tok(D_MODEL), tok(SEG), tok(SEG), tok(PLE_DIM), const2(w_out_bf), const2(pg),
                  const2(w_gate_bf), const2(w_proj_bf)],
        out_specs=tok(D_MODEL),
        out_shape=jax.ShapeDtypeStruct((b, t, D_MODEL), F32),
        compiler_params=pltpu.CompilerParams(
            dimension_semantics=("arbitrary", "arbitrary"),
            vmem_limit_bytes=_vmem_limit(block_bytes, temp_bytes)),
        name=name,
    )(x, osb, osgu, p, w_out_bf, pg, w_gate_bf, w_proj_bf)


def _sgu_bias_rows(sgu_b_l, period):
    per_pos = jnp.tile(sgu_b_l[:, :period].T, (SGU_CHUNK // period, 1))
    return jnp.repeat(per_pos, GROUP_W, axis=1)


def kernel(x_prompt, x_sample, cache_k, cache_v, p_prompt, p_sample, norm_g, w_in, q_norm_g,
           k_norm_g, sgu_norm_g, sgu_w, sgu_b, w_out, ple_norm_g, w_ple_gate, w_ple_proj):
    depth = w_in.shape[0]
    assert depth == 1, "one layer per call"
    l = 0
    b_p, t_p, _ = x_prompt.shape
    b_s, t_s, _ = x_sample.shape
    past = cache_k.shape[2]
    n_s = b_s * t_s
    assert n_s == SGU_CHUNK and SGU_CHUNK % t_s == 0

    w_in_bf = w_in[l].astype(BF16)
    w_out_bf = w_out[l].astype(BF16)
    w_gate_bf = w_ple_gate[l].astype(BF16)
    w_proj_bf = w_ple_proj[l].astype(BF16)

    q, k, v, kt, vb, gsb, osgu = _inproj(
        x_prompt, norm_g[l], w_in_bf, q_norm_g[l], k_norm_g[l], sgu_norm_g[l],
        sgu_w[l], _sgu_bias_rows(sgu_b[l], SGU_CHUNK),
        tm=TM_IN, period=SGU_CHUNK, attn_layout=True)
    osb = _attn_prompt(q, kt, vb, gsb, tq=TQ)
    y_prompt = _outproj(x_prompt, osb, osgu, p_prompt[l], w_out_bf, ple_norm_g[l], w_gate_bf,
                        w_proj_bf, tm=TM_OUT, name="outproj_prompt")

    rep = SGU_CHUNK // t_s
    sgu_w_s = jnp.tile(sgu_w[l][:, :t_s, :t_s], (1, rep, rep))
    xs = x_sample.reshape(1, n_s, D_MODEL)
    q_s, k_s, v_s, gsb_s, osgu_s, vs_s = _inproj(
        xs, norm_g[l], w_in_bf, q_norm_g[l], k_norm_g[l], sgu_norm_g[l],
        sgu_w_s, _sgu_bias_rows(sgu_b[l], t_s),
        tm=n_s, period=t_s, attn_layout=False)
    shp = (b_s, t_s, SB_WIDTH)
    osb_s = _attn_sample(q_s.reshape(shp), cache_k[l].reshape(b_s, past, SB_WIDTH),
                         cache_v[l].reshape(b_s, past, SB_WIDTH), k_s.reshape(shp),
                         v_s.reshape(shp), gsb_s.reshape(shp))
    y_sample = _outproj(xs, osb_s.reshape(1, n_s, SB_WIDTH), osgu_s, p_sample[l].reshape(1, n_s, PLE_DIM),
                        w_out_bf, ple_norm_g[l], w_gate_bf, w_proj_bf, tm=n_s,
                        name="outproj_sample").reshape(b_s, t_s, D_MODEL)

    head_shape = lambda a, bb, tt: a.reshape(1, bb, tt, HEADS, HEAD_DIM)
    return (y_prompt, y_sample,
            head_shape(k, b_p, t_p), head_shape(v, b_p, t_p),
            head_shape(k_s, b_s, t_s), head_shape(v_s, b_s, t_s),
            vs_s.reshape(1, b_s, t_s, GROUPS, GROUP_W))
```

```python
import functools
import math

import numpy as np
import jax
import jax.numpy as jnp
from jax import lax
from jax.experimental import pallas as pl
from jax.experimental.pallas import tpu as pltpu

F32 = jnp.float32
BF16 = jnp.bfloat16

LANES = 128
VMEM_BYTES_V7X = 64 * 1024 * 1024

D_MODEL = 1024
PLE_DIM = 256
HEADS = 8
HEAD_DIM = 64
SB_WIDTH = HEADS * HEAD_DIM
GROUPS = 4
GROUP_W = 128
SGU_WIDTH = GROUPS * GROUP_W
SGU_CHUNK = 128
SEG = 512
N_SEG = 7
EPS = 1e-6

Q_SCALE = HEAD_DIM ** -0.5 * math.log2(math.e)

KSUB = LANES
NSUB = 4
TQ = NSUB * KSUB
HEADS_PER_STEP = 4
TM_IN = 512
TM_OUT = 512


def _vmem_limit(block_bytes, temp_bytes):
    need = 2 * block_bytes + temp_bytes
    return int(min(need, VMEM_BYTES_V7X - 8 * 1024 * 1024))


def _nbytes(shape, dtype):
    return int(np.prod(shape)) * jnp.dtype(dtype).itemsize


def _suffix_sum_matrix():
    j = np.arange(KSUB)[:, None]
    s = np.arange(KSUB)[None, :]
    one = np.concatenate([np.where(j >= s, -1.0, 0.0), -np.ones((KSUB, LANES))], axis=1)
    return jnp.asarray(np.concatenate([one, one], axis=0), dtype=BF16)


def _head_mean_matrix():
    a = np.arange(SB_WIDTH)
    bd = np.where(a[:, None] // HEAD_DIM == a[None, :] // HEAD_DIM, 1.0 / HEAD_DIM, 0.0)
    return jnp.asarray(np.concatenate([bd, bd], axis=0), dtype=BF16)


def _split_bf16(x):
    hi = x.astype(BF16)
    lo = (x - hi.astype(F32)).astype(BF16)
    return jnp.concatenate([hi, lo], axis=1)


def _gelu_tanh(x):
    return 0.5 * x * (1.0 + jnp.tanh(math.sqrt(2.0 / math.pi) * (x + 0.044715 * (x * x * x))))


def _silu(x):
    return x / (1.0 + jnp.exp(-x))


def _inproj_kernel(x_ref, ng_ref, w_ref, qg_ref, kg_ref, sg_ref, hm_ref, sw_ref, sb_ref,
                   *out_refs, tm, period, attn_layout):
    if attn_layout:
        q_ref, k_ref, v_ref, kt_ref, vb_ref, gsb_ref, osgu_ref = out_refs
    else:
        q_ref, k_ref, v_ref, gsb_ref, osgu_ref, vs_ref = out_refs

    x = x_ref[0]
    ms = jnp.mean(x * x, axis=-1, keepdims=True)
    xn = (x * lax.rsqrt(ms + EPS) * ng_ref[...]).astype(BF16)

    def seg(i):
        return jnp.dot(xn, w_ref[:, i * SEG:(i + 1) * SEG], preferred_element_type=F32)

    def head_rms(t, g):
        ms_h = jnp.dot(_split_bf16(t * t), hm_ref[...], preferred_element_type=F32)
        return t * lax.rsqrt(ms_h + EPS) * g

    q = head_rms(seg(0), qg_ref[...])
    q_ref[0] = (q * Q_SCALE).astype(BF16)

    k = head_rms(seg(1), kg_ref[...])
    k_ref[0] = k
    v = seg(2)
    v_ref[0] = v
    if attn_layout:
        kt_ref[0] = k.T.astype(BF16)
        vb_ref[0] = v.astype(BF16)

    gsb_ref[0] = _silu(seg(3)).astype(BF16)

    u = _gelu_tanh(seg(4))
    vs_raw = _gelu_tanh(seg(5))
    vs_groups = []
    for g in range(GROUPS):
        t = vs_raw[:, g * GROUP_W:(g + 1) * GROUP_W]
        ms_g = jnp.mean(t * t, axis=-1, keepdims=True)
        vs_groups.append(t * lax.rsqrt(ms_g + EPS) * sg_ref[:, g * GROUP_W:(g + 1) * GROUP_W])
    if not attn_layout:
        vs_ref[0] = jnp.concatenate(vs_groups, axis=1)

    row = lax.broadcasted_iota(jnp.int32, (SGU_CHUNK, SGU_CHUNK), 0)
    col = lax.broadcasted_iota(jnp.int32, (SGU_CHUNK, SGU_CHUNK), 1)
    keep = row >= col
    if period != SGU_CHUNK:
        keep = keep & ((row // period) == (col // period))
    s_groups = []
    for g in range(GROUPS):
        wm = jnp.where(keep, sw_ref[g], 0.0).astype(BF16)
        vg = vs_groups[g].astype(BF16)
        chunks = [jnp.dot(wm, vg[c * SGU_CHUNK:(c + 1) * SGU_CHUNK], preferred_element_type=F32)
                  + sb_ref[:, g * GROUP_W:(g + 1) * GROUP_W]
                  for c in range(tm // SGU_CHUNK)]
        s_groups.append(jnp.concatenate(chunks, axis=0) if len(chunks) > 1 else chunks[0])
    s = jnp.concatenate(s_groups, axis=1)

    osgu_ref[0] = (u * s * _silu(seg(6))).astype(BF16)


def _inproj(x, norm_g, w_in_bf, q_norm_g, k_norm_g, sgu_norm_g, sgu_w_tiled, sgu_bias, *,
            tm, period, attn_layout):
    b, t, _ = x.shape
    assert t % tm == 0 and tm % SGU_CHUNK == 0
    grid = (b, t // tm)
    tok = lambda w: pl.BlockSpec((1, tm, w), lambda bi, i: (bi, i, 0))
    const2 = lambda a: pl.BlockSpec(a.shape, lambda bi, i: (0, 0))
    const3 = lambda a: pl.BlockSpec(a.shape, lambda bi, i: (0, 0, 0))

    ng = norm_g.reshape(1, D_MODEL)
    qg = jnp.tile(q_norm_g, HEADS).reshape(1, SB_WIDTH)
    kg = jnp.tile(k_norm_g, HEADS).reshape(1, SB_WIDTH)
    sg = sgu_norm_g.reshape(1, SGU_WIDTH)
    hm = _head_mean_matrix()

    act = lambda dt: jax.ShapeDtypeStruct((b, t, SEG), dt)
    if attn_layout:
        out_shape = (act(BF16), act(F32), act(F32),
                     jax.ShapeDtypeStruct((b, SB_WIDTH, t), BF16), act(BF16), act(BF16), act(BF16))
        out_specs = (tok(SEG), tok(SEG), tok(SEG),
                     pl.BlockSpec((1, SB_WIDTH, tm), lambda bi, i: (bi, 0, i)),
                     tok(SEG), tok(SEG), tok(SEG))
    else:
        out_shape = (act(BF16), act(F32), act(F32), act(BF16), act(BF16), act(F32))
        out_specs = (tok(SEG),) * 6

    block_bytes = (_nbytes((tm, D_MODEL), F32) + _nbytes(w_in_bf.shape, BF16)
                   + _nbytes(hm.shape, BF16) + _nbytes(sgu_w_tiled.shape, F32)
                   + _nbytes(sgu_bias.shape, F32) + 7 * _nbytes((tm, SEG), F32))
    temp_bytes = 12 * _nbytes((tm, SEG), F32)
    return pl.pallas_call(
        functools.partial(_inproj_kernel, tm=tm, period=period, attn_layout=attn_layout),
        grid=grid,
        in_specs=[tok(D_MODEL), const2(ng), const2(w_in_bf), const2(qg), const2(kg), const2(sg),
                  const2(hm), const3(sgu_w_tiled), const2(sgu_bias)],
        out_specs=out_specs,
        out_shape=out_shape,
        compiler_params=pltpu.CompilerParams(
            dimension_semantics=("arbitrary", "arbitrary"),
            vmem_limit_bytes=_vmem_limit(block_bytes, temp_bytes)),
        name="inproj_prompt" if attn_layout else "inproj_sample",
    )(x, ng, w_in_bf, qg, kg, sg, hm, sgu_w_tiled, sgu_bias)


def _sb_block(z, vis, rem, uu):
    sp = jnp.maximum(z, 0.0) + jnp.log2(1.0 + jnp.exp2(-jnp.abs(z)))
    if vis is not None:
        sp = jnp.where(vis, sp, 0.0)
    sums = jnp.dot(_split_bf16(sp), uu, preferred_element_type=F32)
    w = jnp.exp2(z + sums[:, :KSUB] + rem)
    if vis is not None:
        w = jnp.where(vis, w, 0.0)
    return w.astype(BF16), rem + sums[:, KSUB:]


def _attn_prompt_kernel(q_ref, kt_ref, v_ref, g_ref, uu_ref, o_ref, rem_ref, acc_ref, *, tq, hps):
    qi = pl.program_id(2)
    q = q_ref[0]
    qh = [q[:, h * HEAD_DIM:(h + 1) * HEAD_DIM] for h in range(hps)]
    uu = uu_ref[...]
    lane = lax.broadcasted_iota(jnp.int32, (KSUB, LANES), 1)
    first_head = lane < HEAD_DIM

    def pair_values(ks, p):
        vv = v_ref[0, pl.ds(ks, KSUB), p * LANES:(p + 1) * LANES]
        zero = jnp.zeros_like(vv)
        return jnp.concatenate([jnp.where(first_head, vv, zero),
                                jnp.where(first_head, zero, vv)], axis=0)

    def sub_block(ks, r0, rems, vis):
        kt = kt_ref[0, :, pl.ds(ks, KSUB)]
        ws, new = [], []
        for h in range(hps):
            z = jnp.dot(qh[h][r0:], kt[h * HEAD_DIM:(h + 1) * HEAD_DIM],
                        preferred_element_type=F32)
            w, r = _sb_block(z, vis, rems[h], uu)
            ws.append(w)
            new.append(r)
        return ws, new

    base = pl.multiple_of(qi * tq, tq)
    rems = [jnp.zeros((tq, LANES), F32)] * hps
    accs = [jnp.zeros((tq, LANES), F32)] * (hps // 2)
    for s in reversed(range(NSUB)):
        r0 = s * KSUB
        ks = pl.multiple_of(base + r0, KSUB)
        row = lax.broadcasted_iota(jnp.int32, (tq - r0, KSUB), 0)
        col = lax.broadcasted_iota(jnp.int32, (tq - r0, KSUB), 1)
        ws, new = sub_block(ks, r0, [r[r0:] for r in rems], col < row)
        parts = [jnp.dot(jnp.concatenate(ws[2 * p:2 * p + 2], axis=1), pair_values(ks, p),
                         preferred_element_type=F32) for p in range(hps // 2)]
        if r0:
            rems = [jnp.concatenate([r[:r0], n], axis=0) for r, n in zip(rems, new)]
            accs = [jnp.concatenate([a[:r0], a[r0:] + d], axis=0) for a, d in zip(accs, parts)]
        else:
            rems = new
            accs = [a + d for a, d in zip(accs, parts)]
    for h in range(hps):
        rem_ref[h] = rems[h]
    for p in range(hps // 2):
        acc_ref[p] = accs[p]

    def body(i, carry):
        kb = pl.multiple_of((qi - 1 - i) * tq, tq)
        rems = [rem_ref[h] for h in range(hps)]
        order = list(reversed(range(NSUB)))
        ws = [None] * NSUB
        for s in order:
            ks = pl.multiple_of(kb + s * KSUB, KSUB)
            ws[s], rems = sub_block(ks, 0, rems, None)
        for p in range(hps // 2):
            w_pair = jnp.concatenate([ws[s][2 * p + e] for s in order for e in range(2)], axis=1)
            v_pair = jnp.concatenate([pair_values(pl.multiple_of(kb + s * KSUB, KSUB), p)
                                      for s in order], axis=0)
            acc_ref[p] += jnp.dot(w_pair, v_pair, preferred_element_type=F32)
        for h in range(hps):
            rem_ref[h] = rems[h]
        return carry

    lax.fori_loop(0, qi, body, 0)
    acc = jnp.concatenate([acc_ref[p] for p in range(hps // 2)], axis=1)
    o_ref[0] = (acc * g_ref[0].astype(F32)).astype(BF16)


def _attn_prompt(q, kt, vb, gsb, *, tq, hps):
    b, t, _ = q.shape
    assert t % tq == 0 and tq == NSUB * KSUB and HEADS % hps == 0 and hps % 2 == 0
    width = hps * HEAD_DIM
    uu = _suffix_sum_matrix()
    grid = (b, HEADS // hps, t // tq)
    row_blk = pl.BlockSpec((1, tq, width), lambda bi, hg, i: (bi, i, hg))
    block_bytes = (3 * _nbytes((tq, width), BF16) + 2 * _nbytes((width, t), BF16)
                   + _nbytes(uu.shape, BF16))
    temp_bytes = (hps + hps // 2) * _nbytes((tq, LANES), F32) + 8 * NSUB * hps * _nbytes((tq, KSUB), F32)
    return pl.pallas_call(
        functools.partial(_attn_prompt_kernel, tq=tq, hps=hps),
        grid=grid,
        in_specs=[row_blk,
                  pl.BlockSpec((1, width, t), lambda bi, hg, i: (bi, hg, 0)),
                  pl.BlockSpec((1, t, width), lambda bi, hg, i: (bi, 0, hg)),
                  row_blk,
                  pl.BlockSpec(uu.shape, lambda bi, hg, i: (0, 0))],
        out_specs=row_blk,
        out_shape=jax.ShapeDtypeStruct((b, t, SB_WIDTH), BF16),
        scratch_shapes=[pltpu.VMEM((hps, tq, LANES), F32), pltpu.VMEM((hps // 2, tq, LANES), F32)],
        compiler_params=pltpu.CompilerParams(
            dimension_semantics=("arbitrary", "arbitrary", "arbitrary"),
            vmem_limit_bytes=_vmem_limit(block_bytes, temp_bytes)),
        name="sb_attn_prompt",
    )(q, kt, vb, gsb, uu)


def _attn_sample_kernel(q_ref, ck_ref, cv_ref, kn_ref, vn_ref, g_ref, uu_ref, o_ref,
                        *, t_new, past):
    q = q_ref[0]
    rows = HEADS * t_new
    uu = uu_ref[...]

    def logits(kblk):
        return jnp.concatenate(
            [lax.dot_general(q[:, h * HEAD_DIM:(h + 1) * HEAD_DIM],
                             kblk[:, h * HEAD_DIM:(h + 1) * HEAD_DIM],
                             (((1,), (1,)), ((), ())), preferred_element_type=F32)
             for h in range(HEADS)], axis=0)

    def attend(w, vblk):
        return [jnp.dot(w[h * t_new:(h + 1) * t_new],
                        vblk[:, h * HEAD_DIM:(h + 1) * HEAD_DIM], preferred_element_type=F32)
                for h in range(HEADS)]

    pad = jnp.zeros((LANES - t_new, SB_WIDTH), F32)
    kn = jnp.concatenate([kn_ref[0], pad], axis=0).astype(BF16)
    vn = jnp.concatenate([vn_ref[0], pad], axis=0).astype(BF16)
    row = lax.broadcasted_iota(jnp.int32, (rows, LANES), 0)
    col = lax.broadcasted_iota(jnp.int32, (rows, LANES), 1)
    vis = col < (row % t_new)
    rem = jnp.zeros((rows, LANES), F32)
    w, rem = _sb_block(logits(kn), vis, rem, uu)
    acc = attend(w, vn)

    for j in reversed(range(past // KSUB)):
        kb = ck_ref[0, j * KSUB:(j + 1) * KSUB, :].astype(BF16)
        vb = cv_ref[0, j * KSUB:(j + 1) * KSUB, :].astype(BF16)
        w, rem = _sb_block(logits(kb), None, rem, uu)
        acc = [a + d for a, d in zip(acc, attend(w, vb))]

    o = jnp.concatenate(acc, axis=1)
    o_ref[0] = (o * g_ref[0].astype(F32)).astype(BF16)


def _attn_sample(q, cache_k, cache_v, k_new, v_new, gsb):
    b, t_new, _ = q.shape
    past = cache_k.shape[1]
    assert past % KSUB == 0 and t_new <= KSUB
    uu = _suffix_sum_matrix()
    new_blk = pl.BlockSpec((1, t_new, SB_WIDTH), lambda bi: (bi, 0, 0))
    cache_blk = pl.BlockSpec((1, past, SB_WIDTH), lambda bi: (bi, 0, 0))
    block_bytes = (2 * _nbytes((past, SB_WIDTH), F32) + 6 * _nbytes((t_new, SB_WIDTH), F32)
                   + _nbytes(uu.shape, BF16))
    temp_bytes = (8 * (past // KSUB + 1) * _nbytes((HEADS * t_new, KSUB), F32)
                  + 4 * _nbytes((past, SB_WIDTH), BF16))
    return pl.pallas_call(
        functools.partial(_attn_sample_kernel, t_new=t_new, past=past),
        grid=(b,),
        in_specs=[new_blk, cache_blk, cache_blk, new_blk, new_blk, new_blk,
                  pl.BlockSpec(uu.shape, lambda bi: (0, 0))],
        out_specs=new_blk,
        out_shape=jax.ShapeDtypeStruct((b, t_new, SB_WIDTH), BF16),
        compiler_params=pltpu.CompilerParams(
            dimension_semantics=("arbitrary",),
            vmem_limit_bytes=_vmem_limit(block_bytes, temp_bytes)),
        name="sb_attn_sample",
    )(q, cache_k, cache_v, k_new, v_new, gsb, uu)


def _outproj_kernel(x_ref, osb_ref, osgu_ref, p_ref, wo_ref, pg_ref, wg_ref, wp_ref, y_ref):
    h = (x_ref[0]
         + jnp.dot(osb_ref[0], wo_ref[:SB_WIDTH], preferred_element_type=F32)
         + jnp.dot(osgu_ref[0], wo_ref[SB_WIDTH:], preferred_element_type=F32))
    ms = jnp.mean(h * h, axis=-1, keepdims=True)
    hn = (h * lax.rsqrt(ms + EPS) * pg_ref[...]).astype(BF16)
    gate_logit = jnp.dot(hn, wg_ref[...], preferred_element_type=F32)
    gate = 1.0 / (1.0 + jnp.exp(-gate_logit))
    pp = jnp.dot(p_ref[0].astype(BF16), wp_ref[...], preferred_element_type=F32)
    y_ref[0] = h + gate * pp


def _outproj(x, osb, osgu, p, w_out_bf, ple_norm_g, w_gate_bf, w_proj_bf, *, tm, name):
    b, t, _ = x.shape
    assert t % tm == 0
    tok = lambda w: pl.BlockSpec((1, tm, w), lambda bi, i: (bi, i, 0))
    const2 = lambda a: pl.BlockSpec(a.shape, lambda bi, i: (0, 0))
    pg = ple_norm_g.reshape(1, D_MODEL)
    block_bytes = (2 * _nbytes((tm, D_MODEL), F32) + 2 * _nbytes((tm, SEG), BF16)
                   + _nbytes((tm, PLE_DIM), F32) + _nbytes(w_out_bf.shape, BF16)
                   + _nbytes(w_gate_bf.shape, BF16) + _nbytes(w_proj_bf.shape, BF16))
    temp_bytes = 6 * _nbytes((tm, D_MODEL), F32)
    return pl.pallas_call(
        _outproj_kernel,
        grid=(b, t // tm),
        in_specs=[tok(D_MODEL), tok(SEG), tok(SEG), tok(PLE_DIM), const2(w_out_bf), const2(pg),
                  const2(w_gate_bf), const2(w_proj_bf)],
        out_specs=tok(D_MODEL),
        out_shape=jax.ShapeDtypeStruct((b, t, D_MODEL), F32),
        compiler_params=pltpu.CompilerParams(
            dimension_semantics=("arbitrary", "arbitrary"),
            vmem_limit_bytes=_vmem_limit(block_bytes, temp_bytes)),
        name=name,
    )(x, osb, osgu, p, w_out_bf, pg, w_gate_bf, w_proj_bf)


def _sgu_bias_rows(sgu_b_l, period):
    per_pos = jnp.tile(sgu_b_l[:, :period].T, (SGU_CHUNK // period, 1))
    return jnp.repeat(per_pos, GROUP_W, axis=1)


def kernel(x_prompt, x_sample, cache_k, cache_v, p_prompt, p_sample, norm_g, w_in, q_norm_g,
           k_norm_g, sgu_norm_g, sgu_w, sgu_b, w_out, ple_norm_g, w_ple_gate, w_ple_proj):
    depth = w_in.shape[0]
    assert depth == 1, "one layer per call"
    l = 0
    b_p, t_p, _ = x_prompt.shape
    b_s, t_s, _ = x_sample.shape
    past = cache_k.shape[2]
    n_s = b_s * t_s
    assert n_s == SGU_CHUNK and SGU_CHUNK % t_s == 0

    w_in_bf = w_in[l].astype(BF16)
    w_out_bf = w_out[l].astype(BF16)
    w_gate_bf = w_ple_gate[l].astype(BF16)
    w_proj_bf = w_ple_proj[l].astype(BF16)

    q, k, v, kt, vb, gsb, osgu = _inproj(
        x_prompt, norm_g[l], w_in_bf, q_norm_g[l], k_norm_g[l], sgu_norm_g[l],
        sgu_w[l], _sgu_bias_rows(sgu_b[l], SGU_CHUNK),
        tm=TM_IN, period=SGU_CHUNK, attn_layout=True)
    osb = _attn_prompt(q, kt, vb, gsb, tq=TQ, hps=HEADS_PER_STEP)
    y_prompt = _outproj(x_prompt, osb, osgu, p_prompt[l], w_out_bf, ple_norm_g[l], w_gate_bf,
                        w_proj_bf, tm=TM_OUT, name="outproj_prompt")

    rep = SGU_CHUNK // t_s
    sgu_w_s = jnp.tile(sgu_w[l][:, :t_s, :t_s], (1, rep, rep))
    xs = x_sample.reshape(1, n_s, D_MODEL)
    q_s, k_s, v_s, gsb_s, osgu_s, vs_s = _inproj(
        xs, norm_g[l], w_in_bf, q_norm_g[l], k_norm_g[l], sgu_norm_g[l],
        sgu_w_s, _sgu_bias_rows(sgu_b[l], t_s),
        tm=n_s, period=t_s, attn_layout=False)
    shp = (b_s, t_s, SB_WIDTH)
    osb_s = _attn_sample(q_s.reshape(shp), cache_k[l].reshape(b_s, past, SB_WIDTH),
                         cache_v[l].reshape(b_s, past, SB_WIDTH), k_s.reshape(shp),
                         v_s.reshape(shp), gsb_s.reshape(shp))
    y_sample = _outproj(xs, osb_s.reshape(1, n_s, SB_WIDTH), osgu_s, p_sample[l].reshape(1, n_s, PLE_DIM),
                        w_out_bf, ple_norm_g[l], w_gate_bf, w_proj_bf, tm=n_s,
                        name="outproj_sample").reshape(b_s, t_s, D_MODEL)

    head_shape = lambda a, bb, tt: a.reshape(1, bb, tt, HEADS, HEAD_DIM)
    return (y_prompt, y_sample,
            head_shape(k, b_p, t_p), head_shape(v, b_p, t_p),
            head_shape(k_s, b_s, t_s), head_shape(v_s, b_s, t_s),
            vs_s.reshape(1, b_s, t_s, GROUPS, GROUP_W))
```

```python
import functools
import math

import numpy as np
import jax
import jax.numpy as jnp
from jax import lax
from jax.experimental import pallas as pl
from jax.experimental.pallas import tpu as pltpu

F32 = jnp.float32
BF16 = jnp.bfloat16

LANES = 128
VMEM_BYTES_V7X = 64 * 1024 * 1024

D_MODEL = 1024
PLE_DIM = 256
HEADS = 8
HEAD_DIM = 64
SB_WIDTH = HEADS * HEAD_DIM
GROUPS = 4
GROUP_W = 128
SGU_WIDTH = GROUPS * GROUP_W
SGU_CHUNK = 128
SEG = 512
N_SEG = 7
EPS = 1e-6

Q_SCALE = HEAD_DIM ** -0.5 * math.log2(math.e)

KSUB = LANES
NSUB = 2
TQ = NSUB * KSUB
HEADS_PER_STEP = 8
REM_DEAD_LOG2 = -152.0
TM_IN = 512
TM_OUT = 512


def _vmem_limit(block_bytes, temp_bytes):
    need = 2 * block_bytes + temp_bytes
    return int(min(need, VMEM_BYTES_V7X - 8 * 1024 * 1024))


def _nbytes(shape, dtype):
    return int(np.prod(shape)) * jnp.dtype(dtype).itemsize


def _suffix_sum_matrix():
    j = np.arange(KSUB)[:, None]
    s = np.arange(KSUB)[None, :]
    one = np.concatenate([np.where(j >= s, -1.0, 0.0), -np.ones((KSUB, LANES))], axis=1)
    return jnp.asarray(np.concatenate([one, one], axis=0), dtype=BF16)


def _head_mean_matrix():
    a = np.arange(SB_WIDTH)
    bd = np.where(a[:, None] // HEAD_DIM == a[None, :] // HEAD_DIM, 1.0 / HEAD_DIM, 0.0)
    return jnp.asarray(np.concatenate([bd, bd], axis=0), dtype=BF16)


def _split_bf16(x):
    hi = x.astype(BF16)
    lo = (x - hi.astype(F32)).astype(BF16)
    return jnp.concatenate([hi, lo], axis=1)


def _gelu_tanh(x):
    return 0.5 * x * (1.0 + jnp.tanh(math.sqrt(2.0 / math.pi) * (x + 0.044715 * (x * x * x))))


def _silu(x):
    return x / (1.0 + jnp.exp(-x))


def _inproj_kernel(x_ref, ng_ref, w_ref, qg_ref, kg_ref, sg_ref, hm_ref, sw_ref, sb_ref,
                   *out_refs, tm, period, attn_layout):
    if attn_layout:
        q_ref, k_ref, v_ref, kt_ref, vb_ref, gsb_ref, osgu_ref = out_refs
    else:
        q_ref, k_ref, v_ref, gsb_ref, osgu_ref, vs_ref = out_refs

    x = x_ref[0]
    ms = jnp.mean(x * x, axis=-1, keepdims=True)
    xn = (x * lax.rsqrt(ms + EPS) * ng_ref[...]).astype(BF16)

    def seg(i):
        return jnp.dot(xn, w_ref[:, i * SEG:(i + 1) * SEG], preferred_element_type=F32)

    def head_rms(t, g):
        ms_h = jnp.dot(_split_bf16(t * t), hm_ref[...], preferred_element_type=F32)
        return t * lax.rsqrt(ms_h + EPS) * g

    q = head_rms(seg(0), qg_ref[...])
    q_ref[0] = (q * Q_SCALE).astype(BF16)

    k = head_rms(seg(1), kg_ref[...])
    k_ref[0] = k
    v = seg(2)
    v_ref[0] = v
    if attn_layout:
        kt_ref[0] = k.T.astype(BF16)
        vb_ref[0] = v.astype(BF16)

    gsb_ref[0] = _silu(seg(3)).astype(BF16)

    u = _gelu_tanh(seg(4))
    vs_raw = _gelu_tanh(seg(5))
    vs_groups = []
    for g in range(GROUPS):
        t = vs_raw[:, g * GROUP_W:(g + 1) * GROUP_W]
        ms_g = jnp.mean(t * t, axis=-1, keepdims=True)
        vs_groups.append(t * lax.rsqrt(ms_g + EPS) * sg_ref[:, g * GROUP_W:(g + 1) * GROUP_W])
    if not attn_layout:
        vs_ref[0] = jnp.concatenate(vs_groups, axis=1)

    row = lax.broadcasted_iota(jnp.int32, (SGU_CHUNK, SGU_CHUNK), 0)
    col = lax.broadcasted_iota(jnp.int32, (SGU_CHUNK, SGU_CHUNK), 1)
    keep = row >= col
    if period != SGU_CHUNK:
        keep = keep & ((row // period) == (col // period))
    s_groups = []
    for g in range(GROUPS):
        wm = jnp.where(keep, sw_ref[g], 0.0).astype(BF16)
        vg = vs_groups[g].astype(BF16)
        chunks = [jnp.dot(wm, vg[c * SGU_CHUNK:(c + 1) * SGU_CHUNK], preferred_element_type=F32)
                  + sb_ref[:, g * GROUP_W:(g + 1) * GROUP_W]
                  for c in range(tm // SGU_CHUNK)]
        s_groups.append(jnp.concatenate(chunks, axis=0) if len(chunks) > 1 else chunks[0])
    s = jnp.concatenate(s_groups, axis=1)

    osgu_ref[0] = (u * s * _silu(seg(6))).astype(BF16)


def _inproj(x, norm_g, w_in_bf, q_norm_g, k_norm_g, sgu_norm_g, sgu_w_tiled, sgu_bias, *,
            tm, period, attn_layout):
    b, t, _ = x.shape
    assert t % tm == 0 and tm % SGU_CHUNK == 0
    grid = (b, t // tm)
    tok = lambda w: pl.BlockSpec((1, tm, w), lambda bi, i: (bi, i, 0))
    const2 = lambda a: pl.BlockSpec(a.shape, lambda bi, i: (0, 0))
    const3 = lambda a: pl.BlockSpec(a.shape, lambda bi, i: (0, 0, 0))

    ng = norm_g.reshape(1, D_MODEL)
    qg = jnp.tile(q_norm_g, HEADS).reshape(1, SB_WIDTH)
    kg = jnp.tile(k_norm_g, HEADS).reshape(1, SB_WIDTH)
    sg = sgu_norm_g.reshape(1, SGU_WIDTH)
    hm = _head_mean_matrix()

    act = lambda dt: jax.ShapeDtypeStruct((b, t, SEG), dt)
    if attn_layout:
        out_shape = (act(BF16), act(F32), act(F32),
                     jax.ShapeDtypeStruct((b, SB_WIDTH, t), BF16), act(BF16), act(BF16), act(BF16))
        out_specs = (tok(SEG), tok(SEG), tok(SEG),
                     pl.BlockSpec((1, SB_WIDTH, tm), lambda bi, i: (bi, 0, i)),
                     tok(SEG), tok(SEG), tok(SEG))
    else:
        out_shape = (act(BF16), act(F32), act(F32), act(BF16), act(BF16), act(F32))
        out_specs = (tok(SEG),) * 6

    block_bytes = (_nbytes((tm, D_MODEL), F32) + _nbytes(w_in_bf.shape, BF16)
                   + _nbytes(hm.shape, BF16) + _nbytes(sgu_w_tiled.shape, F32)
                   + _nbytes(sgu_bias.shape, F32) + 7 * _nbytes((tm, SEG), F32))
    temp_bytes = 12 * _nbytes((tm, SEG), F32)
    return pl.pallas_call(
        functools.partial(_inproj_kernel, tm=tm, period=period, attn_layout=attn_layout),
        grid=grid,
        in_specs=[tok(D_MODEL), const2(ng), const2(w_in_bf), const2(qg), const2(kg), const2(sg),
                  const2(hm), const3(sgu_w_tiled), const2(sgu_bias)],
        out_specs=out_specs,
        out_shape=out_shape,
        compiler_params=pltpu.CompilerParams(
            dimension_semantics=("arbitrary", "arbitrary"),
            vmem_limit_bytes=_vmem_limit(block_bytes, temp_bytes)),
        name="inproj_prompt" if attn_layout else "inproj_sample",
    )(x, ng, w_in_bf, qg, kg, sg, hm, sgu_w_tiled, sgu_bias)


def _sb_block(z, vis, rem, uu):
    sp = jnp.maximum(z, 0.0) + jnp.log2(1.0 + jnp.exp2(-jnp.abs(z)))
    if vis is not None:
        sp = jnp.where(vis, sp, 0.0)
    sums = jnp.dot(_split_bf16(sp), uu, preferred_element_type=F32)
    w = jnp.exp2(z + sums[:, :KSUB] + rem)
    if vis is not None:
        w = jnp.where(vis, w, 0.0)
    return w.astype(BF16), rem + sums[:, KSUB:]


def _attn_prompt_kernel(q_ref, kt_ref, v_ref, g_ref, uu_ref, o_ref, rem_ref, acc_ref, *, tq, hps):
    qi = pl.program_id(2)
    q = q_ref[0]
    qh = [q[:, h * HEAD_DIM:(h + 1) * HEAD_DIM] for h in range(hps)]
    uu = uu_ref[...]
    lane = lax.broadcasted_iota(jnp.int32, (KSUB, LANES), 1)
    first_head = lane < HEAD_DIM

    def pair_values(ks, p):
        vv = v_ref[0, pl.ds(ks, KSUB), p * LANES:(p + 1) * LANES]
        zero = jnp.zeros_like(vv)
        return jnp.concatenate([jnp.where(first_head, vv, zero),
                                jnp.where(first_head, zero, vv)], axis=0)

    def sub_block(ks, r0, rems, vis):
        kt = kt_ref[0, :, pl.ds(ks, KSUB)]
        ws, new = [], []
        for h in range(hps):
            z = jnp.dot(qh[h][r0:], kt[h * HEAD_DIM:(h + 1) * HEAD_DIM],
                        preferred_element_type=F32)
            w, r = _sb_block(z, vis, rems[h], uu)
            ws.append(w)
            new.append(r)
        return ws, new

    base = pl.multiple_of(qi * tq, tq)
    rems = [jnp.zeros((tq, LANES), F32)] * hps
    accs = [jnp.zeros((tq, LANES), F32)] * (hps // 2)
    for s in reversed(range(NSUB)):
        r0 = s * KSUB
        ks = pl.multiple_of(base + r0, KSUB)
        row = lax.broadcasted_iota(jnp.int32, (tq - r0, KSUB), 0)
        col = lax.broadcasted_iota(jnp.int32, (tq - r0, KSUB), 1)
        ws, new = sub_block(ks, r0, [r[r0:] for r in rems], col < row)
        parts = [jnp.dot(jnp.concatenate(ws[2 * p:2 * p + 2], axis=1), pair_values(ks, p),
                         preferred_element_type=F32) for p in range(hps // 2)]
        if r0:
            rems = [jnp.concatenate([r[:r0], n], axis=0) for r, n in zip(rems, new)]
            accs = [jnp.concatenate([a[:r0], a[r0:] + d], axis=0) for a, d in zip(accs, parts)]
        else:
            rems = new
            accs = [a + d for a, d in zip(accs, parts)]
    for h in range(hps):
        rem_ref[h] = rems[h]
    for p in range(hps // 2):
        acc_ref[p] = accs[p]

    def key_block(n):
        kb = pl.multiple_of((qi - n) * tq, tq)
        rems = [rem_ref[h] for h in range(hps)]
        order = list(reversed(range(NSUB)))
        ws = [None] * NSUB
        for s in order:
            ks = pl.multiple_of(kb + s * KSUB, KSUB)
            ws[s], rems = sub_block(ks, 0, rems, None)
        for p in range(hps // 2):
            w_pair = jnp.concatenate([ws[s][2 * p + e] for s in order for e in range(2)], axis=1)
            v_pair = jnp.concatenate([pair_values(pl.multiple_of(kb + s * KSUB, KSUB), p)
                                      for s in order], axis=0)
            acc_ref[p] += jnp.dot(w_pair, v_pair, preferred_element_type=F32)
        for h in range(hps):
            rem_ref[h] = rems[h]
        most = functools.reduce(jnp.maximum, rems)
        return (jnp.max(most) > REM_DEAD_LOG2).astype(jnp.int32)

    @pl.when(qi > 0)
    def _():
        def cond(c):
            n, alive = c
            return jnp.logical_and(n <= qi, alive > 0)

        def body(c):
            n, _ = c
            return n + 1, key_block(n)

        lax.while_loop(cond, body, (jnp.int32(2), key_block(1)))

    acc = jnp.concatenate([acc_ref[p] for p in range(hps // 2)], axis=1)
    o_ref[0] = (acc * g_ref[0].astype(F32)).astype(BF16)


def _attn_prompt(q, kt, vb, gsb, *, tq, hps):
    b, t, _ = q.shape
    assert t % tq == 0 and tq == NSUB * KSUB and HEADS % hps == 0 and hps % 2 == 0
    width = hps * HEAD_DIM
    uu = _suffix_sum_matrix()
    grid = (b, HEADS // hps, t // tq)
    row_blk = pl.BlockSpec((1, tq, width), lambda bi, hg, i: (bi, i, hg))
    block_bytes = (3 * _nbytes((tq, width), BF16) + 2 * _nbytes((width, t), BF16)
                   + _nbytes(uu.shape, BF16))
    temp_bytes = (hps + hps // 2) * _nbytes((tq, LANES), F32) + 8 * NSUB * hps * _nbytes((tq, KSUB), F32)
    return pl.pallas_call(
        functools.partial(_attn_prompt_kernel, tq=tq, hps=hps),
        grid=grid,
        in_specs=[row_blk,
                  pl.BlockSpec((1, width, t), lambda bi, hg, i: (bi, hg, 0)),
                  pl.BlockSpec((1, t, width), lambda bi, hg, i: (bi, 0, hg)),
                  row_blk,
                  pl.BlockSpec(uu.shape, lambda bi, hg, i: (0, 0))],
        out_specs=row_blk,
        out_shape=jax.ShapeDtypeStruct((b, t, SB_WIDTH), BF16),
        scratch_shapes=[pltpu.VMEM((hps, tq, LANES), F32), pltpu.VMEM((hps // 2, tq, LANES), F32)],
        compiler_params=pltpu.CompilerParams(
            dimension_semantics=("arbitrary", "arbitrary", "arbitrary"),
            vmem_limit_bytes=_vmem_limit(block_bytes, temp_bytes)),
        name="sb_attn_prompt",
    )(q, kt, vb, gsb, uu)


def _attn_sample_kernel(q_ref, ck_ref, cv_ref, kn_ref, vn_ref, g_ref, uu_ref, o_ref,
                        *, t_new, past):
    q = q_ref[0]
    rows = HEADS * t_new
    uu = uu_ref[...]

    def logits(kblk):
        return jnp.concatenate(
            [lax.dot_general(q[:, h * HEAD_DIM:(h + 1) * HEAD_DIM],
                             kblk[:, h * HEAD_DIM:(h + 1) * HEAD_DIM],
                             (((1,), (1,)), ((), ())), preferred_element_type=F32)
             for h in range(HEADS)], axis=0)

    def attend(w, vblk):
        return [jnp.dot(w[h * t_new:(h + 1) * t_new],
                        vblk[:, h * HEAD_DIM:(h + 1) * HEAD_DIM], preferred_element_type=F32)
                for h in range(HEADS)]

    pad = jnp.zeros((LANES - t_new, SB_WIDTH), F32)
    kn = jnp.concatenate([kn_ref[0], pad], axis=0).astype(BF16)
    vn = jnp.concatenate([vn_ref[0], pad], axis=0).astype(BF16)
    row = lax.broadcasted_iota(jnp.int32, (rows, LANES), 0)
    col = lax.broadcasted_iota(jnp.int32, (rows, LANES), 1)
    vis = col < (row % t_new)
    rem = jnp.zeros((rows, LANES), F32)
    w, rem = _sb_block(logits(kn), vis, rem, uu)
    acc = attend(w, vn)

    for j in reversed(range(past // KSUB)):
        kb = ck_ref[0, j * KSUB:(j + 1) * KSUB, :].astype(BF16)
        vb = cv_ref[0, j * KSUB:(j + 1) * KSUB, :].astype(BF16)
        w, rem = _sb_block(logits(kb), None, rem, uu)
        acc = [a + d for a, d in zip(acc, attend(w, vb))]

    o = jnp.concatenate(acc, axis=1)
    o_ref[0] = (o * g_ref[0].astype(F32)).astype(BF16)


def _attn_sample(q, cache_k, cache_v, k_new, v_new, gsb):
    b, t_new, _ = q.shape
    past = cache_k.shape[1]
    assert past % KSUB == 0 and t_new <= KSUB
    uu = _suffix_sum_matrix()
    new_blk = pl.BlockSpec((1, t_new, SB_WIDTH), lambda bi: (bi, 0, 0))
    cache_blk = pl.BlockSpec((1, past, SB_WIDTH), lambda bi: (bi, 0, 0))
    block_bytes = (2 * _nbytes((past, SB_WIDTH), F32) + 6 * _nbytes((t_new, SB_WIDTH), F32)
                   + _nbytes(uu.shape, BF16))
    temp_bytes = (8 * (past // KSUB + 1) * _nbytes((HEADS * t_new, KSUB), F32)
                  + 4 * _nbytes((past, SB_WIDTH), BF16))
    return pl.pallas_call(
        functools.partial(_attn_sample_kernel, t_new=t_new, past=past),
        grid=(b,),
        in_specs=[new_blk, cache_blk, cache_blk, new_blk, new_blk, new_blk,
                  pl.BlockSpec(uu.shape, lambda bi: (0, 0))],
        out_specs=new_blk,
        out_shape=jax.ShapeDtypeStruct((b, t_new, SB_WIDTH), BF16),
        compiler_params=pltpu.CompilerParams(
            dimension_semantics=("arbitrary",),
            vmem_limit_bytes=_vmem_limit(block_bytes, temp_bytes)),
        name="sb_attn_sample",
    )(q, cache_k, cache_v, k_new, v_new, gsb, uu)


def _outproj_kernel(x_ref, osb_ref, osgu_ref, p_ref, wo_ref, pg_ref, wg_ref, wp_ref, y_ref):
    h = (x_ref[0]
         + jnp.dot(osb_ref[0], wo_ref[:SB_WIDTH], preferred_element_type=F32)
         + jnp.dot(osgu_ref[0], wo_ref[SB_WIDTH:], preferred_element_type=F32))
    ms = jnp.mean(h * h, axis=-1, keepdims=True)
    hn = (h * lax.rsqrt(ms + EPS) * pg_ref[...]).astype(BF16)
    gate_logit = jnp.dot(hn, wg_ref[...], preferred_element_type=F32)
    gate = 1.0 / (1.0 + jnp.exp(-gate_logit))
    pp = jnp.dot(p_ref[0].astype(BF16), wp_ref[...], preferred_element_type=F32)
    y_ref[0] = h + gate * pp


def _outproj(x, osb, osgu, p, w_out_bf, ple_norm_g, w_gate_bf, w_proj_bf, *, tm, name):
    b, t, _ = x.shape
    assert t % tm == 0
    tok = lambda w: pl.BlockSpec((1, tm, w), lambda bi, i: (bi, i, 0))
    const2 = lambda a: pl.BlockSpec(a.shape, lambda bi, i: (0, 0))
    pg = ple_norm_g.reshape(1, D_MODEL)
    block_bytes = (2 * _nbytes((tm, D_MODEL), F32) + 2 * _nbytes((tm, SEG), BF16)
                   + _nbytes((tm, PLE_DIM), F32) + _nbytes(w_out_bf.shape, BF16)
                   + _nbytes(w_gate_bf.shape, BF16) + _nbytes(w_proj_bf.shape, BF16))
    temp_bytes = 6 * _nbytes((tm, D_MODEL), F32)
    return pl.pallas_call(
        _outproj_kernel,
        grid=(b, t // tm),
        in_specs=[tok(D_MODEL), tok(SEG), tok(SEG), tok(PLE_DIM), const2(w_out_bf), const2(pg),
                  const2(w_gate_bf), const2(w_proj_bf)],
        out_specs=tok(D_MODEL),
        out_shape=jax.ShapeDtypeStruct((b, t, D_MODEL), F32),
        compiler_params=pltpu.CompilerParams(
            dimension_semantics=("arbitrary", "arbitrary"),
            vmem_limit_bytes=_vmem_limit(block_bytes, temp_bytes)),
        name=name,
    )(x, osb, osgu, p, w_out_bf, pg, w_gate_bf, w_proj_bf)


def _sgu_bias_rows(sgu_b_l, period):
    per_pos = jnp.tile(sgu_b_l[:, :period].T, (SGU_CHUNK // period, 1))
    return jnp.repeat(per_pos, GROUP_W, axis=1)


def kernel(x_prompt, x_sample, cache_k, cache_v, p_prompt, p_sample, norm_g, w_in, q_norm_g,
           k_norm_g, sgu_norm_g, sgu_w, sgu_b, w_out, ple_norm_g, w_ple_gate, w_ple_proj):
    depth = w_in.shape[0]
    assert depth == 1, "one layer per call"
    l = 0
    b_p, t_p, _ = x_prompt.shape
    b_s, t_s, _ = x_sample.shape
    past = cache_k.shape[2]
    n_s = b_s * t_s
    assert n_s == SGU_CHUNK and SGU_CHUNK % t_s == 0

    w_in_bf = w_in[l].astype(BF16)
    w_out_bf = w_out[l].astype(BF16)
    w_gate_bf = w_ple_gate[l].astype(BF16)
    w_proj_bf = w_ple_proj[l].astype(BF16)

    q, k, v, kt, vb, gsb, osgu = _inproj(
        x_prompt, norm_g[l], w_in_bf, q_norm_g[l], k_norm_g[l], sgu_norm_g[l],
        sgu_w[l], _sgu_bias_rows(sgu_b[l], SGU_CHUNK),
        tm=TM_IN, period=SGU_CHUNK, attn_layout=True)
    osb = _attn_prompt(q, kt, vb, gsb, tq=TQ, hps=HEADS_PER_STEP)
    y_prompt = _outproj(x_prompt, osb, osgu, p_prompt[l], w_out_bf, ple_norm_g[l], w_gate_bf,
                        w_proj_bf, tm=TM_OUT, name="outproj_prompt")

    rep = SGU_CHUNK // t_s
    sgu_w_s = jnp.tile(sgu_w[l][:, :t_s, :t_s], (1, rep, rep))
    xs = x_sample.reshape(1, n_s, D_MODEL)
    q_s, k_s, v_s, gsb_s, osgu_s, vs_s = _inproj(
        xs, norm_g[l], w_in_bf, q_norm_g[l], k_norm_g[l], sgu_norm_g[l],
        sgu_w_s, _sgu_bias_rows(sgu_b[l], t_s),
        tm=n_s, period=t_s, attn_layout=False)
    shp = (b_s, t_s, SB_WIDTH)
    osb_s = _attn_sample(q_s.reshape(shp), cache_k[l].reshape(b_s, past, SB_WIDTH),
                         cache_v[l].reshape(b_s, past, SB_WIDTH), k_s.reshape(shp),
                         v_s.reshape(shp), gsb_s.reshape(shp))
    y_sample = _outproj(xs, osb_s.reshape(1, n_s, SB_WIDTH), osgu_s, p_sample[l].reshape(1, n_s, PLE_DIM),
                        w_out_bf, ple_norm_g[l], w_gate_bf, w_proj_bf, tm=n_s,
                        name="outproj_sample").reshape(b_s, t_s, D_MODEL)

    head_shape = lambda a, bb, tt: a.reshape(1, bb, tt, HEADS, HEAD_DIM)
    return (y_prompt, y_sample,
            head_shape(k, b_p, t_p), head_shape(v, b_p, t_p),
            head_shape(k_s, b_s, t_s), head_shape(v_s, b_s, t_s),
            vs_s.reshape(1, b_s, t_s, GROUPS, GROUP_W))
```

```python
import functools
import math

import numpy as np
import jax
import jax.numpy as jnp
from jax import lax
from jax.experimental import pallas as pl
from jax.experimental.pallas import tpu as pltpu

F32 = jnp.float32
BF16 = jnp.bfloat16

LANES = 128
MXU_TILE = 256
VMEM_BYTES_V7X = 64 * 1024 * 1024

D_MODEL = 1024
PLE_DIM = 256
HEADS = 8
HEAD_DIM = 64
SB_WIDTH = HEADS * HEAD_DIM
GROUPS = 4
GROUP_W = 128
SGU_WIDTH = GROUPS * GROUP_W
SGU_CHUNK = 128
SEG = 512
N_SEG = 7
EPS = 1e-6

Q_SCALE = HEAD_DIM ** -0.5 * math.log2(math.e)

KSUB = LANES
NSUB = 2
TQ = NSUB * KSUB
HEADS_PER_STEP = 8
REM_DEAD_LOG2 = -152.0
TM_IN = 512
TM_OUT = 512


def _vmem_limit(block_bytes, temp_bytes):
    need = 2 * block_bytes + temp_bytes
    return int(min(need, VMEM_BYTES_V7X - 8 * 1024 * 1024))


def _nbytes(shape, dtype):
    return int(np.prod(shape)) * jnp.dtype(dtype).itemsize


def _suffix_sum_matrix():
    j = np.arange(KSUB)[:, None]
    s = np.arange(KSUB)[None, :]
    one = np.concatenate([np.where(j >= s, -1.0, 0.0), -np.ones((KSUB, LANES))], axis=1)
    return jnp.asarray(np.concatenate([one, one], axis=0), dtype=BF16)


def _head_mean_matrix():
    a = np.arange(MXU_TILE)
    bd = np.where(a[:, None] // HEAD_DIM == a[None, :] // HEAD_DIM, 1.0 / HEAD_DIM, 0.0)
    return jnp.asarray(np.concatenate([bd, bd], axis=0), dtype=BF16)


def _split_bf16(x):
    hi = x.astype(BF16)
    lo = (x - hi.astype(F32)).astype(BF16)
    return jnp.concatenate([hi, lo], axis=1)


def _gelu_tanh(x):
    return 0.5 * x * (1.0 + jnp.tanh(math.sqrt(2.0 / math.pi) * (x + 0.044715 * (x * x * x))))


def _silu(x):
    return x / (1.0 + jnp.exp(-x))


def _inproj_kernel(x_ref, ng_ref, w_ref, qg_ref, kg_ref, sg_ref, hm_ref, sw_ref, sb_ref,
                   *out_refs, tm, period, attn_layout):
    if attn_layout:
        q_ref, k_ref, v_ref, kt_ref, vb_ref, gsb_ref, osgu_ref = out_refs
    else:
        q_ref, k_ref, v_ref, gsb_ref, osgu_ref, vs_ref = out_refs

    x = x_ref[0]
    ms = jnp.mean(x * x, axis=-1, keepdims=True)
    xn = (x * lax.rsqrt(ms + EPS) * ng_ref[...]).astype(BF16)

    def seg(i):
        return jnp.dot(xn, w_ref[:, i * SEG:(i + 1) * SEG], preferred_element_type=F32)

    def head_rms(t, g):
        sq = t * t
        ms_h = jnp.concatenate(
            [jnp.dot(_split_bf16(sq[:, c:c + MXU_TILE]), hm_ref[...], preferred_element_type=F32)
             for c in range(0, SB_WIDTH, MXU_TILE)], axis=1)
        return t * lax.rsqrt(ms_h + EPS) * g

    q = head_rms(seg(0), qg_ref[...])
    q_ref[0] = (q * Q_SCALE).astype(BF16)

    k = head_rms(seg(1), kg_ref[...])
    k_ref[0] = k
    v = seg(2)
    v_ref[0] = v
    if attn_layout:
        kt_ref[0] = k.T.astype(BF16)
        vb_ref[0] = v.astype(BF16)

    gsb_ref[0] = _silu(seg(3)).astype(BF16)

    u = _gelu_tanh(seg(4))
    vs_raw = _gelu_tanh(seg(5))
    vs_groups = []
    for g in range(GROUPS):
        t = vs_raw[:, g * GROUP_W:(g + 1) * GROUP_W]
        ms_g = jnp.mean(t * t, axis=-1, keepdims=True)
        vs_groups.append(t * lax.rsqrt(ms_g + EPS) * sg_ref[:, g * GROUP_W:(g + 1) * GROUP_W])
    if not attn_layout:
        vs_ref[0] = jnp.concatenate(vs_groups, axis=1)

    row = lax.broadcasted_iota(jnp.int32, (SGU_CHUNK, SGU_CHUNK), 0)
    col = lax.broadcasted_iota(jnp.int32, (SGU_CHUNK, SGU_CHUNK), 1)
    keep = row >= col
    if period != SGU_CHUNK:
        keep = keep & ((row // period) == (col // period))
    s_groups = []
    for g in range(GROUPS):
        wm = jnp.where(keep, sw_ref[g], 0.0).astype(BF16)
        vg = vs_groups[g].astype(BF16)
        chunks = [jnp.dot(wm, vg[c * SGU_CHUNK:(c + 1) * SGU_CHUNK], preferred_element_type=F32)
                  + sb_ref[:, g * GROUP_W:(g + 1) * GROUP_W]
                  for c in range(tm // SGU_CHUNK)]
        s_groups.append(jnp.concatenate(chunks, axis=0) if len(chunks) > 1 else chunks[0])
    s = jnp.concatenate(s_groups, axis=1)

    osgu_ref[0] = (u * s * _silu(seg(6))).astype(BF16)


def _inproj(x, norm_g, w_in_bf, q_norm_g, k_norm_g, sgu_norm_g, sgu_w_tiled, sgu_bias, *,
            tm, period, attn_layout):
    b, t, _ = x.shape
    assert t % tm == 0 and tm % SGU_CHUNK == 0
    grid = (b, t // tm)
    tok = lambda w: pl.BlockSpec((1, tm, w), lambda bi, i: (bi, i, 0))
    const2 = lambda a: pl.BlockSpec(a.shape, lambda bi, i: (0, 0))
    const3 = lambda a: pl.BlockSpec(a.shape, lambda bi, i: (0, 0, 0))

    ng = norm_g.reshape(1, D_MODEL)
    qg = jnp.tile(q_norm_g, HEADS).reshape(1, SB_WIDTH)
    kg = jnp.tile(k_norm_g, HEADS).reshape(1, SB_WIDTH)
    sg = sgu_norm_g.reshape(1, SGU_WIDTH)
    hm = _head_mean_matrix()

    act = lambda dt: jax.ShapeDtypeStruct((b, t, SEG), dt)
    if attn_layout:
        out_shape = (act(BF16), act(F32), act(F32),
                     jax.ShapeDtypeStruct((b, SB_WIDTH, t), BF16), act(BF16), act(BF16), act(BF16))
        out_specs = (tok(SEG), tok(SEG), tok(SEG),
                     pl.BlockSpec((1, SB_WIDTH, tm), lambda bi, i: (bi, 0, i)),
                     tok(SEG), tok(SEG), tok(SEG))
    else:
        out_shape = (act(BF16), act(F32), act(F32), act(BF16), act(BF16), act(F32))
        out_specs = (tok(SEG),) * 6

    block_bytes = (_nbytes((tm, D_MODEL), F32) + _nbytes(w_in_bf.shape, BF16)
                   + _nbytes(hm.shape, BF16) + _nbytes(sgu_w_tiled.shape, F32)
                   + _nbytes(sgu_bias.shape, F32) + 7 * _nbytes((tm, SEG), F32))
    temp_bytes = 12 * _nbytes((tm, SEG), F32)
    return pl.pallas_call(
        functools.partial(_inproj_kernel, tm=tm, period=period, attn_layout=attn_layout),
        grid=grid,
        in_specs=[tok(D_MODEL), const2(ng), const2(w_in_bf), const2(qg), const2(kg), const2(sg),
                  const2(hm), const3(sgu_w_tiled), const2(sgu_bias)],
        out_specs=out_specs,
        out_shape=out_shape,
        compiler_params=pltpu.CompilerParams(
            dimension_semantics=("arbitrary", "arbitrary"),
            vmem_limit_bytes=_vmem_limit(block_bytes, temp_bytes)),
        name="inproj_prompt" if attn_layout else "inproj_sample",
    )(x, ng, w_in_bf, qg, kg, sg, hm, sgu_w_tiled, sgu_bias)


def _sb_block(z, vis, rem, uu):
    sums = jnp.dot(_split_bf16(_neg_log2_keep(z, vis)), uu, preferred_element_type=F32)
    return _sb_weights(z, vis, sums, rem)


def _neg_log2_keep(z, vis):
    sp = jnp.maximum(z, 0.0) + jnp.log2(1.0 + jnp.exp2(-jnp.abs(z)))
    return sp if vis is None else jnp.where(vis, sp, 0.0)


def _sb_weights(z, vis, sums, rem):
    w = jnp.exp2(z + sums[:, :KSUB] + rem)
    if vis is not None:
        w = jnp.where(vis, w, 0.0)
    return w.astype(BF16), rem + sums[:, KSUB:]


def _attn_prompt_kernel(q_ref, kt_ref, v_ref, g_ref, uu_ref, o_ref, rem_ref, acc_ref, *, tq, hps):
    qi = pl.program_id(2)
    q = q_ref[0]
    qh = [q[:, h * HEAD_DIM:(h + 1) * HEAD_DIM] for h in range(hps)]
    uu = uu_ref[...]
    lane = lax.broadcasted_iota(jnp.int32, (KSUB, LANES), 1)
    first_head = lane < HEAD_DIM

    def pair_values(ks, p):
        vv = v_ref[0, pl.ds(ks, KSUB), p * LANES:(p + 1) * LANES]
        zero = jnp.zeros_like(vv)
        return jnp.concatenate([jnp.where(first_head, vv, zero),
                                jnp.where(first_head, zero, vv)], axis=0)

    def sweep(tiles, rems, accs):
        zs, splits = [], []
        for ks, r0, diagonal in tiles:
            kt = kt_ref[0, :, pl.ds(ks, KSUB)]
            vis = None
            if diagonal:
                row = lax.broadcasted_iota(jnp.int32, (tq - r0, KSUB), 0)
                col = lax.broadcasted_iota(jnp.int32, (tq - r0, KSUB), 1)
                vis = col < row
            for h in range(hps):
                z = jnp.dot(qh[h][r0:], kt[h * HEAD_DIM:(h + 1) * HEAD_DIM],
                            preferred_element_type=F32)
                zs.append((z, vis))
                splits.append(_split_bf16(_neg_log2_keep(z, vis)))
        sums = jnp.dot(jnp.concatenate(splits, axis=0), uu, preferred_element_type=F32)
        rems, accs = list(rems), list(accs)
        off, i = 0, 0
        for ks, r0, _ in tiles:
            ws = []
            for h in range(hps):
                z, vis = zs[i]
                i += 1
                m = tq - r0
                w, new = _sb_weights(z, vis, sums[off:off + m], rems[h][r0:])
                off += m
                rems[h] = jnp.concatenate([rems[h][:r0], new], axis=0) if r0 else new
                ws.append(w)
            for p in range(hps // 2):
                d = jnp.dot(jnp.concatenate(ws[2 * p:2 * p + 2], axis=1), pair_values(ks, p),
                            preferred_element_type=F32)
                accs[p] = (jnp.concatenate([accs[p][:r0], accs[p][r0:] + d], axis=0) if r0
                           else accs[p] + d)
        return rems, accs

    def sub_blocks(first_key, diagonal):
        order = reversed(range(NSUB))
        return [(pl.multiple_of(first_key + s * KSUB, KSUB), s * KSUB if diagonal else 0, diagonal)
                for s in order]

    def diagonal_tiles():
        return sub_blocks(pl.multiple_of(qi * tq, tq), True)

    def key_block_tiles(n):
        return sub_blocks(pl.multiple_of((qi - n) * tq, tq), False)

    zero_rems = [jnp.zeros((tq, LANES), F32)] * hps
    zero_accs = [jnp.zeros((tq, LANES), F32)] * (hps // 2)

    def finish(accs):
        acc = jnp.concatenate(accs, axis=1)
        o_ref[0] = (acc * g_ref[0].astype(F32)).astype(BF16)

    def any_alive(rems):
        most = functools.reduce(jnp.maximum, rems)
        return (jnp.max(most) > REM_DEAD_LOG2).astype(jnp.int32)

    @pl.when(qi == 0)
    def _():
        finish(sweep(diagonal_tiles(), zero_rems, zero_accs)[1])

    @pl.when(qi > 0)
    def _():
        rems, accs = sweep(diagonal_tiles() + key_block_tiles(1), zero_rems, zero_accs)
        for h in range(hps):
            rem_ref[h] = rems[h]
        for p in range(hps // 2):
            acc_ref[p] = accs[p]

        def cond(c):
            n, alive = c
            return jnp.logical_and(n <= qi, alive > 0)

        def body(c):
            n, _ = c
            rems, accs = sweep(key_block_tiles(n), [rem_ref[h] for h in range(hps)],
                               [acc_ref[p] for p in range(hps // 2)])
            for h in range(hps):
                rem_ref[h] = rems[h]
            for p in range(hps // 2):
                acc_ref[p] = accs[p]
            return n + 1, any_alive(rems)

        lax.while_loop(cond, body, (jnp.int32(2), any_alive(rems)))
        finish([acc_ref[p] for p in range(hps // 2)])


def _attn_prompt(q, kt, vb, gsb, *, tq, hps):
    b, t, _ = q.shape
    assert t % tq == 0 and tq == NSUB * KSUB and HEADS % hps == 0 and hps % 2 == 0
    width = hps * HEAD_DIM
    uu = _suffix_sum_matrix()
    grid = (b, HEADS // hps, t // tq)
    row_blk = pl.BlockSpec((1, tq, width), lambda bi, hg, i: (bi, i, hg))
    block_bytes = (3 * _nbytes((tq, width), BF16) + 2 * _nbytes((width, t), BF16)
                   + _nbytes(uu.shape, BF16))
    temp_bytes = (hps + hps // 2) * _nbytes((tq, LANES), F32) + 8 * NSUB * hps * _nbytes((tq, KSUB), F32)
    return pl.pallas_call(
        functools.partial(_attn_prompt_kernel, tq=tq, hps=hps),
        grid=grid,
        in_specs=[row_blk,
                  pl.BlockSpec((1, width, t), lambda bi, hg, i: (bi, hg, 0)),
                  pl.BlockSpec((1, t, width), lambda bi, hg, i: (bi, 0, hg)),
                  row_blk,
                  pl.BlockSpec(uu.shape, lambda bi, hg, i: (0, 0))],
        out_specs=row_blk,
        out_shape=jax.ShapeDtypeStruct((b, t, SB_WIDTH), BF16),
        scratch_shapes=[pltpu.VMEM((hps, tq, LANES), F32), pltpu.VMEM((hps // 2, tq, LANES), F32)],
        compiler_params=pltpu.CompilerParams(
            dimension_semantics=("arbitrary", "arbitrary", "arbitrary"),
            vmem_limit_bytes=_vmem_limit(block_bytes, temp_bytes)),
        name="sb_attn_prompt",
    )(q, kt, vb, gsb, uu)


def _attn_sample_kernel(q_ref, ck_ref, cv_ref, kn_ref, vn_ref, g_ref, uu_ref, o_ref,
                        *, t_new, past):
    q = q_ref[0]
    rows = HEADS * t_new
    uu = uu_ref[...]

    def logits(kblk):
        return jnp.concatenate(
            [lax.dot_general(q[:, h * HEAD_DIM:(h + 1) * HEAD_DIM],
                             kblk[:, h * HEAD_DIM:(h + 1) * HEAD_DIM],
                             (((1,), (1,)), ((), ())), preferred_element_type=F32)
             for h in range(HEADS)], axis=0)

    def attend(w, vblk):
        return [jnp.dot(w[h * t_new:(h + 1) * t_new],
                        vblk[:, h * HEAD_DIM:(h + 1) * HEAD_DIM], preferred_element_type=F32)
                for h in range(HEADS)]

    pad = jnp.zeros((LANES - t_new, SB_WIDTH), F32)
    kn = jnp.concatenate([kn_ref[0], pad], axis=0).astype(BF16)
    vn = jnp.concatenate([vn_ref[0], pad], axis=0).astype(BF16)
    row = lax.broadcasted_iota(jnp.int32, (rows, LANES), 0)
    col = lax.broadcasted_iota(jnp.int32, (rows, LANES), 1)
    vis = col < (row % t_new)
    rem = jnp.zeros((rows, LANES), F32)
    w, rem = _sb_block(logits(kn), vis, rem, uu)
    acc = attend(w, vn)

    for j in reversed(range(past // KSUB)):
        kb = ck_ref[0, j * KSUB:(j + 1) * KSUB, :].astype(BF16)
        vb = cv_ref[0, j * KSUB:(j + 1) * KSUB, :].astype(BF16)
        w, rem = _sb_block(logits(kb), None, rem, uu)
        acc = [a + d for a, d in zip(acc, attend(w, vb))]

    o = jnp.concatenate(acc, axis=1)
    o_ref[0] = (o * g_ref[0].astype(F32)).astype(BF16)


def _attn_sample(q, cache_k, cache_v, k_new, v_new, gsb):
    b, t_new, _ = q.shape
    past = cache_k.shape[1]
    assert past % KSUB == 0 and t_new <= KSUB
    uu = _suffix_sum_matrix()
    new_blk = pl.BlockSpec((1, t_new, SB_WIDTH), lambda bi: (bi, 0, 0))
    cache_blk = pl.BlockSpec((1, past, SB_WIDTH), lambda bi: (bi, 0, 0))
    block_bytes = (2 * _nbytes((past, SB_WIDTH), F32) + 6 * _nbytes((t_new, SB_WIDTH), F32)
                   + _nbytes(uu.shape, BF16))
    temp_bytes = (8 * (past // KSUB + 1) * _nbytes((HEADS * t_new, KSUB), F32)
                  + 4 * _nbytes((past, SB_WIDTH), BF16))
    return pl.pallas_call(
        functools.partial(_attn_sample_kernel, t_new=t_new, past=past),
        grid=(b,),
        in_specs=[new_blk, cache_blk, cache_blk, new_blk, new_blk, new_blk,
                  pl.BlockSpec(uu.shape, lambda bi: (0, 0))],
        out_specs=new_blk,
        out_shape=jax.ShapeDtypeStruct((b, t_new, SB_WIDTH), BF16),
        compiler_params=pltpu.CompilerParams(
            dimension_semantics=("arbitrary",),
            vmem_limit_bytes=_vmem_limit(block_bytes, temp_bytes)),
        name="sb_attn_sample",
    )(q, cache_k, cache_v, k_new, v_new, gsb, uu)


def _outproj_kernel(x_ref, osb_ref, osgu_ref, p_ref, wo_ref, pg_ref, wg_ref, wp_ref, y_ref):
    h = (x_ref[0]
         + jnp.dot(osb_ref[0], wo_ref[:SB_WIDTH], preferred_element_type=F32)
         + jnp.dot(osgu_ref[0], wo_ref[SB_WIDTH:], preferred_element_type=F32))
    ms = jnp.mean(h * h, axis=-1, keepdims=True)
    hn = (h * lax.rsqrt(ms + EPS) * pg_ref[...]).astype(BF16)
    gate_logit = jnp.dot(hn, wg_ref[...], preferred_element_type=F32)
    gate = 1.0 / (1.0 + jnp.exp(-gate_logit))
    pp = jnp.dot(p_ref[0].astype(BF16), wp_ref[...], preferred_element_type=F32)
    y_ref[0] = h + gate * pp


def _outproj(x, osb, osgu, p, w_out_bf, ple_norm_g, w_gate_bf, w_proj_bf, *, tm, name):
    b, t, _ = x.shape
    assert t % tm == 0
    tok = lambda w: pl.BlockSpec((1, tm, w), lambda bi, i: (bi, i, 0))
    const2 = lambda a: pl.BlockSpec(a.shape, lambda bi, i: (0, 0))
    pg = ple_norm_g.reshape(1, D_MODEL)
    block_bytes = (2 * _nbytes((tm, D_MODEL), F32) + 2 * _nbytes((tm, SEG), BF16)
                   + _nbytes((tm, PLE_DIM), F32) + _nbytes(w_out_bf.shape, BF16)
                   + _nbytes(w_gate_bf.shape, BF16) + _nbytes(w_proj_bf.shape, BF16))
    temp_bytes = 6 * _nbytes((tm, D_MODEL), F32)
    return pl.pallas_call(
        _outproj_kernel,
        grid=(b, t // tm),
        in_specs=[tok(D_MODEL), tok(SEG), tok(SEG), tok(PLE_DIM), const2(w_out_bf), const2(pg),
                  const2(w_gate_bf), const2(w_proj_bf)],
        out_specs=tok(D_MODEL),
        out_shape=jax.ShapeDtypeStruct((b, t, D_MODEL), F32),
        compiler_params=pltpu.CompilerParams(
            dimension_semantics=("arbitrary", "arbitrary"),
            vmem_limit_bytes=_vmem_limit(block_bytes, temp_bytes)),
        name=name,
    )(x, osb, osgu, p, w_out_bf, pg, w_gate_bf, w_proj_bf)


def _sgu_bias_rows(sgu_b_l, period):
    per_pos = jnp.tile(sgu_b_l[:, :period].T, (SGU_CHUNK // period, 1))
    return jnp.repeat(per_pos, GROUP_W, axis=1)


def kernel(x_prompt, x_sample, cache_k, cache_v, p_prompt, p_sample, norm_g, w_in, q_norm_g,
           k_norm_g, sgu_norm_g, sgu_w, sgu_b, w_out, ple_norm_g, w_ple_gate, w_ple_proj):
    depth = w_in.shape[0]
    assert depth == 1, "one layer per call"
    l = 0
    b_p, t_p, _ = x_prompt.shape
    b_s, t_s, _ = x_sample.shape
    past = cache_k.shape[2]
    n_s = b_s * t_s
    assert n_s == SGU_CHUNK and SGU_CHUNK % t_s == 0

    w_in_bf = w_in[l].astype(BF16)
    w_out_bf = w_out[l].astype(BF16)
    w_gate_bf = w_ple_gate[l].astype(BF16)
    w_proj_bf = w_ple_proj[l].astype(BF16)

    q, k, v, kt, vb, gsb, osgu = _inproj(
        x_prompt, norm_g[l], w_in_bf, q_norm_g[l], k_norm_g[l], sgu_norm_g[l],
        sgu_w[l], _sgu_bias_rows(sgu_b[l], SGU_CHUNK),
        tm=TM_IN, period=SGU_CHUNK, attn_layout=True)
    osb = _attn_prompt(q, kt, vb, gsb, tq=TQ, hps=HEADS_PER_STEP)
    y_prompt = _outproj(x_prompt, osb, osgu, p_prompt[l], w_out_bf, ple_norm_g[l], w_gate_bf,
                        w_proj_bf, tm=TM_OUT, name="outproj_prompt")

    rep = SGU_CHUNK // t_s
    sgu_w_s = jnp.tile(sgu_w[l][:, :t_s, :t_s], (1, rep, rep))
    xs = x_sample.reshape(1, n_s, D_MODEL)
    q_s, k_s, v_s, gsb_s, osgu_s, vs_s = _inproj(
        xs, norm_g[l], w_in_bf, q_norm_g[l], k_norm_g[l], sgu_norm_g[l],
        sgu_w_s, _sgu_bias_rows(sgu_b[l], t_s),
        tm=n_s, period=t_s, attn_layout=False)
    shp = (b_s, t_s, SB_WIDTH)
    osb_s = _attn_sample(q_s.reshape(shp), cache_k[l].reshape(b_s, past, SB_WIDTH),
                         cache_v[l].reshape(b_s, past, SB_WIDTH), k_s.reshape(shp),
                         v_s.reshape(shp), gsb_s.reshape(shp))
    y_sample = _outproj(xs, osb_s.reshape(1, n_s, SB_WIDTH), osgu_s, p_sample[l].reshape(1, n_s, PLE_DIM),
                        w_out_bf, ple_norm_g[l], w_gate_bf, w_proj_bf, tm=n_s,
                        name="outproj_sample").reshape(b_s, t_s, D_MODEL)

    head_shape = lambda a, bb, tt: a.reshape(1, bb, tt, HEADS, HEAD_DIM)
    return (y_prompt, y_sample,
            head_shape(k, b_p, t_p), head_shape(v, b_p, t_p),
            head_shape(k_s, b_s, t_s), head_shape(v_s, b_s, t_s),
            vs_s.reshape(1, b_s, t_s, GROUPS, GROUP_W))
```

```python
import functools
import math

import numpy as np
import jax
import jax.numpy as jnp
from jax import lax
from jax.experimental import pallas as pl
from jax.experimental.pallas import tpu as pltpu

F32 = jnp.float32
BF16 = jnp.bfloat16

LANES = 128
MXU_TILE = 256
VMEM_BYTES_V7X = 64 * 1024 * 1024

D_MODEL = 1024
PLE_DIM = 256
HEADS = 8
HEAD_DIM = 64
SB_WIDTH = HEADS * HEAD_DIM
GROUPS = 4
GROUP_W = 128
SGU_WIDTH = GROUPS * GROUP_W
SGU_CHUNK = 128
SEG = 512
N_SEG = 7
EPS = 1e-6

Q_SCALE = HEAD_DIM ** -0.5 * math.log2(math.e)

KSUB = LANES
NSUB = 2
TQ = NSUB * KSUB
HEADS_PER_STEP = 8
REM_DEAD_LOG2 = -152.0
TM_IN = 1024
TM_OUT = 1024


def _vmem_limit(block_bytes, temp_bytes):
    need = 2 * block_bytes + temp_bytes
    return int(min(need, VMEM_BYTES_V7X - 8 * 1024 * 1024))


def _nbytes(shape, dtype):
    return int(np.prod(shape)) * jnp.dtype(dtype).itemsize


def _suffix_sum_matrix():
    j = np.arange(KSUB)[:, None]
    s = np.arange(KSUB)[None, :]
    one = np.concatenate([np.where(j >= s, -1.0, 0.0), -np.ones((KSUB, LANES))], axis=1)
    return jnp.asarray(np.concatenate([one, one], axis=0), dtype=BF16)


def _head_mean_matrix():
    a = np.arange(MXU_TILE)
    bd = np.where(a[:, None] // HEAD_DIM == a[None, :] // HEAD_DIM, 1.0 / HEAD_DIM, 0.0)
    return jnp.asarray(np.concatenate([bd, bd], axis=0), dtype=BF16)


def _split_bf16(x):
    hi = x.astype(BF16)
    lo = (x - hi.astype(F32)).astype(BF16)
    return jnp.concatenate([hi, lo], axis=1)


def _gelu_tanh(x):
    return 0.5 * x * (1.0 + jnp.tanh(math.sqrt(2.0 / math.pi) * (x + 0.044715 * (x * x * x))))


def _silu(x):
    return x / (1.0 + jnp.exp(-x))


def _inproj_kernel(x_ref, ng_ref, w_ref, qg_ref, kg_ref, sg_ref, hm_ref, sw_ref, sb_ref,
                   *out_refs, tm, period, attn_layout):
    if attn_layout:
        q_ref, k_ref, v_ref, kt_ref, vb_ref, gsb_ref, osgu_ref = out_refs
    else:
        q_ref, k_ref, v_ref, gsb_ref, osgu_ref, vs_ref = out_refs

    x = x_ref[0]
    ms = jnp.mean(x * x, axis=-1, keepdims=True)
    xn = (x * lax.rsqrt(ms + EPS) * ng_ref[...]).astype(BF16)

    def seg(i):
        return jnp.dot(xn, w_ref[:, i * SEG:(i + 1) * SEG], preferred_element_type=F32)

    def head_rms(t, g):
        sq = t * t
        ms_h = jnp.concatenate(
            [jnp.dot(_split_bf16(sq[:, c:c + MXU_TILE]), hm_ref[...], preferred_element_type=F32)
             for c in range(0, SB_WIDTH, MXU_TILE)], axis=1)
        return t * lax.rsqrt(ms_h + EPS) * g

    q = head_rms(seg(0), qg_ref[...])
    q_ref[0] = (q * Q_SCALE).astype(BF16)

    k = head_rms(seg(1), kg_ref[...])
    k_ref[0] = k
    v = seg(2)
    v_ref[0] = v
    if attn_layout:
        kt_ref[0] = k.T.astype(BF16)
        vb_ref[0] = v.astype(BF16)

    gsb_ref[0] = _silu(seg(3)).astype(BF16)

    u = _gelu_tanh(seg(4))
    vs_raw = _gelu_tanh(seg(5))
    vs_groups = []
    for g in range(GROUPS):
        t = vs_raw[:, g * GROUP_W:(g + 1) * GROUP_W]
        ms_g = jnp.mean(t * t, axis=-1, keepdims=True)
        vs_groups.append(t * lax.rsqrt(ms_g + EPS) * sg_ref[:, g * GROUP_W:(g + 1) * GROUP_W])
    if not attn_layout:
        vs_ref[0] = jnp.concatenate(vs_groups, axis=1)

    row = lax.broadcasted_iota(jnp.int32, (SGU_CHUNK, SGU_CHUNK), 0)
    col = lax.broadcasted_iota(jnp.int32, (SGU_CHUNK, SGU_CHUNK), 1)
    keep = row >= col
    if period != SGU_CHUNK:
        keep = keep & ((row // period) == (col // period))
    s_groups = []
    for g in range(GROUPS):
        wm = jnp.where(keep, sw_ref[g], 0.0).astype(BF16)
        vg = vs_groups[g].astype(BF16)
        chunks = [jnp.dot(wm, vg[c * SGU_CHUNK:(c + 1) * SGU_CHUNK], preferred_element_type=F32)
                  + sb_ref[:, g * GROUP_W:(g + 1) * GROUP_W]
                  for c in range(tm // SGU_CHUNK)]
        s_groups.append(jnp.concatenate(chunks, axis=0) if len(chunks) > 1 else chunks[0])
    s = jnp.concatenate(s_groups, axis=1)

    osgu_ref[0] = (u * s * _silu(seg(6))).astype(BF16)


def _inproj(x, norm_g, w_in_bf, q_norm_g, k_norm_g, sgu_norm_g, sgu_w_tiled, sgu_bias, *,
            tm, period, attn_layout):
    b, t, _ = x.shape
    assert t % tm == 0 and tm % SGU_CHUNK == 0
    grid = (b, t // tm)
    tok = lambda w: pl.BlockSpec((1, tm, w), lambda bi, i: (bi, i, 0))
    const2 = lambda a: pl.BlockSpec(a.shape, lambda bi, i: (0, 0))
    const3 = lambda a: pl.BlockSpec(a.shape, lambda bi, i: (0, 0, 0))

    ng = norm_g.reshape(1, D_MODEL)
    qg = jnp.tile(q_norm_g, HEADS).reshape(1, SB_WIDTH)
    kg = jnp.tile(k_norm_g, HEADS).reshape(1, SB_WIDTH)
    sg = sgu_norm_g.reshape(1, SGU_WIDTH)
    hm = _head_mean_matrix()

    act = lambda dt: jax.ShapeDtypeStruct((b, t, SEG), dt)
    if attn_layout:
        out_shape = (act(BF16), act(F32), act(F32),
                     jax.ShapeDtypeStruct((b, SB_WIDTH, t), BF16), act(BF16), act(BF16), act(BF16))
        out_specs = (tok(SEG), tok(SEG), tok(SEG),
                     pl.BlockSpec((1, SB_WIDTH, tm), lambda bi, i: (bi, 0, i)),
                     tok(SEG), tok(SEG), tok(SEG))
    else:
        out_shape = (act(BF16), act(F32), act(F32), act(BF16), act(BF16), act(F32))
        out_specs = (tok(SEG),) * 6

    block_bytes = (_nbytes((tm, D_MODEL), F32) + _nbytes(w_in_bf.shape, BF16)
                   + _nbytes(hm.shape, BF16) + _nbytes(sgu_w_tiled.shape, F32)
                   + _nbytes(sgu_bias.shape, F32) + 7 * _nbytes((tm, SEG), F32))
    temp_bytes = 12 * _nbytes((tm, SEG), F32)
    return pl.pallas_call(
        functools.partial(_inproj_kernel, tm=tm, period=period, attn_layout=attn_layout),
        grid=grid,
        in_specs=[tok(D_MODEL), const2(ng), const2(w_in_bf), const2(qg), const2(kg), const2(sg),
                  const2(hm), const3(sgu_w_tiled), const2(sgu_bias)],
        out_specs=out_specs,
        out_shape=out_shape,
        compiler_params=pltpu.CompilerParams(
            dimension_semantics=("arbitrary", "arbitrary"),
            vmem_limit_bytes=_vmem_limit(block_bytes, temp_bytes)),
        name="inproj_prompt" if attn_layout else "inproj_sample",
    )(x, ng, w_in_bf, qg, kg, sg, hm, sgu_w_tiled, sgu_bias)


def _sb_block(z, vis, rem, uu):
    sums = jnp.dot(_split_bf16(_neg_log2_keep(z, vis)), uu, preferred_element_type=F32)
    return _sb_weights(z, vis, sums, rem)


def _neg_log2_keep(z, vis):
    sp = jnp.maximum(z, 0.0) + jnp.log2(1.0 + jnp.exp2(-jnp.abs(z)))
    return sp if vis is None else jnp.where(vis, sp, 0.0)


def _sb_weights(z, vis, sums, rem):
    w = jnp.exp2(z + sums[:, :KSUB] + rem)
    if vis is not None:
        w = jnp.where(vis, w, 0.0)
    return w.astype(BF16), rem + sums[:, KSUB:]


def _attn_prompt_kernel(q_ref, kt_ref, v_ref, g_ref, uu_ref, o_ref, rem_ref, acc_ref, *, tq, hps):
    qi = pl.program_id(2)
    q = q_ref[0]
    qh = [q[:, h * HEAD_DIM:(h + 1) * HEAD_DIM] for h in range(hps)]
    uu = uu_ref[...]
    lane = lax.broadcasted_iota(jnp.int32, (KSUB, LANES), 1)
    first_head = lane < HEAD_DIM

    def pair_values(ks, p):
        vv = v_ref[0, pl.ds(ks, KSUB), p * LANES:(p + 1) * LANES]
        zero = jnp.zeros_like(vv)
        return jnp.concatenate([jnp.where(first_head, vv, zero),
                                jnp.where(first_head, zero, vv)], axis=0)

    def sweep(tiles, rems, accs):
        def put(full, r0, r1, part):
            pieces = ([full[:r0]] if r0 else []) + [part] + ([full[r1:]] if r1 < tq else [])
            return jnp.concatenate(pieces, axis=0) if len(pieces) > 1 else part

        zs, splits = [], []
        for ks, r0, r1, diagonal in tiles:
            kt = kt_ref[0, :, pl.ds(ks, KSUB)]
            vis = None
            if diagonal:
                row = lax.broadcasted_iota(jnp.int32, (r1 - r0, KSUB), 0)
                col = lax.broadcasted_iota(jnp.int32, (r1 - r0, KSUB), 1)
                vis = col < row
            for h in range(hps):
                z = jnp.dot(qh[h][r0:r1], kt[h * HEAD_DIM:(h + 1) * HEAD_DIM],
                            preferred_element_type=F32)
                zs.append((z, vis))
                splits.append(_split_bf16(_neg_log2_keep(z, vis)))
        sums = jnp.dot(jnp.concatenate(splits, axis=0), uu, preferred_element_type=F32)
        rems, accs = list(rems), list(accs)
        off, i = 0, 0
        for ks, r0, r1, _ in tiles:
            ws = []
            for h in range(hps):
                z, vis = zs[i]
                i += 1
                w, new = _sb_weights(z, vis, sums[off:off + r1 - r0], rems[h][r0:r1])
                off += r1 - r0
                rems[h] = put(rems[h], r0, r1, new)
                ws.append(w)
            for p in range(hps // 2):
                d = jnp.dot(jnp.concatenate(ws[2 * p:2 * p + 2], axis=1), pair_values(ks, p),
                            preferred_element_type=F32)
                accs[p] = put(accs[p], r0, r1, accs[p][r0:r1] + d)
        return rems, accs

    def diagonal_tiles():
        base = pl.multiple_of(qi * tq, tq)
        return [(pl.multiple_of(base + s * KSUB, KSUB), s * KSUB, tq, True)
                for s in reversed(range(NSUB))]

    def key_block_tiles(n, rows=None):
        kb = pl.multiple_of((qi - n) * tq, tq)
        rows = rows or {s: (0, tq) for s in range(NSUB)}
        return [(pl.multiple_of(kb + s * KSUB, KSUB), *rows[s], False)
                for s in reversed(range(NSUB)) if s in rows]

    zero_rems = [jnp.zeros((tq, LANES), F32)] * hps
    zero_accs = [jnp.zeros((tq, LANES), F32)] * (hps // 2)

    def finish(accs):
        acc = jnp.concatenate(accs, axis=1)
        o_ref[0] = (acc * g_ref[0].astype(F32)).astype(BF16)

    def any_alive(rems):
        most = functools.reduce(jnp.maximum, rems)
        return (jnp.max(most) > REM_DEAD_LOG2).astype(jnp.int32)

    @pl.when(qi == 0)
    def _():
        finish(sweep(diagonal_tiles(), zero_rems, zero_accs)[1])

    def load_state():
        return [rem_ref[h] for h in range(hps)], [acc_ref[p] for p in range(hps // 2)]

    def store_state(rems, accs):
        for h in range(hps):
            rem_ref[h] = rems[h]
        for p in range(hps // 2):
            acc_ref[p] = accs[p]

    eager_rows = {s: min(tq, (s + 3 - NSUB) * KSUB) for s in range(NSUB) if s + 3 - NSUB > 0}

    @pl.when(qi > 0)
    def _():
        first = key_block_tiles(1, {s: (0, e) for s, e in eager_rows.items()})
        rems, accs = sweep(diagonal_tiles() + first, zero_rems, zero_accs)
        store_state(rems, accs)
        alive = any_alive(rems)
        for s in reversed(range(NSUB)):
            e = eager_rows.get(s, 0)
            if e < tq:
                @pl.when(any_alive([r[e:] for r in rems]) > 0)
                def _():
                    store_state(*sweep(key_block_tiles(1, {s: (e, tq)}), *load_state()))

        def cond(c):
            n, alive = c
            return jnp.logical_and(n <= qi, alive > 0)

        def body(c):
            n, _ = c
            rems, accs = sweep(key_block_tiles(n), *load_state())
            store_state(rems, accs)
            return n + 1, any_alive(rems)

        lax.while_loop(cond, body, (jnp.int32(2), alive))
        finish(load_state()[1])


def _attn_prompt(q, kt, vb, gsb, *, tq, hps):
    b, t, _ = q.shape
    assert t % tq == 0 and tq == NSUB * KSUB and HEADS % hps == 0 and hps % 2 == 0
    width = hps * HEAD_DIM
    uu = _suffix_sum_matrix()
    grid = (b, HEADS // hps, t // tq)
    row_blk = pl.BlockSpec((1, tq, width), lambda bi, hg, i: (bi, i, hg))
    block_bytes = (3 * _nbytes((tq, width), BF16) + 2 * _nbytes((width, t), BF16)
                   + _nbytes(uu.shape, BF16))
    temp_bytes = (hps + hps // 2) * _nbytes((tq, LANES), F32) + 8 * NSUB * hps * _nbytes((tq, KSUB), F32)
    return pl.pallas_call(
        functools.partial(_attn_prompt_kernel, tq=tq, hps=hps),
        grid=grid,
        in_specs=[row_blk,
                  pl.BlockSpec((1, width, t), lambda bi, hg, i: (bi, hg, 0)),
                  pl.BlockSpec((1, t, width), lambda bi, hg, i: (bi, 0, hg)),
                  row_blk,
                  pl.BlockSpec(uu.shape, lambda bi, hg, i: (0, 0))],
        out_specs=row_blk,
        out_shape=jax.ShapeDtypeStruct((b, t, SB_WIDTH), BF16),
        scratch_shapes=[pltpu.VMEM((hps, tq, LANES), F32), pltpu.VMEM((hps // 2, tq, LANES), F32)],
        compiler_params=pltpu.CompilerParams(
            dimension_semantics=("arbitrary", "arbitrary", "arbitrary"),
            vmem_limit_bytes=_vmem_limit(block_bytes, temp_bytes)),
        name="sb_attn_prompt",
    )(q, kt, vb, gsb, uu)


def _attn_sample_kernel(q_ref, ck_ref, cv_ref, kn_ref, vn_ref, g_ref, uu_ref, o_ref,
                        *, t_new, past):
    q = q_ref[0]
    rows = HEADS * t_new
    uu = uu_ref[...]

    def logits(kblk):
        return jnp.concatenate(
            [lax.dot_general(q[:, h * HEAD_DIM:(h + 1) * HEAD_DIM],
                             kblk[:, h * HEAD_DIM:(h + 1) * HEAD_DIM],
                             (((1,), (1,)), ((), ())), preferred_element_type=F32)
             for h in range(HEADS)], axis=0)

    def attend(w, vblk):
        return [jnp.dot(w[h * t_new:(h + 1) * t_new],
                        vblk[:, h * HEAD_DIM:(h + 1) * HEAD_DIM], preferred_element_type=F32)
                for h in range(HEADS)]

    pad = jnp.zeros((LANES - t_new, SB_WIDTH), F32)
    kn = jnp.concatenate([kn_ref[0], pad], axis=0).astype(BF16)
    vn = jnp.concatenate([vn_ref[0], pad], axis=0).astype(BF16)
    row = lax.broadcasted_iota(jnp.int32, (rows, LANES), 0)
    col = lax.broadcasted_iota(jnp.int32, (rows, LANES), 1)
    vis = col < (row % t_new)
    rem = jnp.zeros((rows, LANES), F32)
    w, rem = _sb_block(logits(kn), vis, rem, uu)
    acc = attend(w, vn)

    for j in reversed(range(past // KSUB)):
        kb = ck_ref[0, j * KSUB:(j + 1) * KSUB, :].astype(BF16)
        vb = cv_ref[0, j * KSUB:(j + 1) * KSUB, :].astype(BF16)
        w, rem = _sb_block(logits(kb), None, rem, uu)
        acc = [a + d for a, d in zip(acc, attend(w, vb))]

    o = jnp.concatenate(acc, axis=1)
    o_ref[0] = (o * g_ref[0].astype(F32)).astype(BF16)


def _attn_sample(q, cache_k, cache_v, k_new, v_new, gsb):
    b, t_new, _ = q.shape
    past = cache_k.shape[1]
    assert past % KSUB == 0 and t_new <= KSUB
    uu = _suffix_sum_matrix()
    new_blk = pl.BlockSpec((1, t_new, SB_WIDTH), lambda bi: (bi, 0, 0))
    cache_blk = pl.BlockSpec((1, past, SB_WIDTH), lambda bi: (bi, 0, 0))
    block_bytes = (2 * _nbytes((past, SB_WIDTH), F32) + 6 * _nbytes((t_new, SB_WIDTH), F32)
                   + _nbytes(uu.shape, BF16))
    temp_bytes = (8 * (past // KSUB + 1) * _nbytes((HEADS * t_new, KSUB), F32)
                  + 4 * _nbytes((past, SB_WIDTH), BF16))
    return pl.pallas_call(
        functools.partial(_attn_sample_kernel, t_new=t_new, past=past),
        grid=(b,),
        in_specs=[new_blk, cache_blk, cache_blk, new_blk, new_blk, new_blk,
                  pl.BlockSpec(uu.shape, lambda bi: (0, 0))],
        out_specs=new_blk,
        out_shape=jax.ShapeDtypeStruct((b, t_new, SB_WIDTH), BF16),
        compiler_params=pltpu.CompilerParams(
            dimension_semantics=("arbitrary",),
            vmem_limit_bytes=_vmem_limit(block_bytes, temp_bytes)),
        name="sb_attn_sample",
    )(q, cache_k, cache_v, k_new, v_new, gsb, uu)


def _outproj_kernel(x_ref, osb_ref, osgu_ref, p_ref, wo_ref, pg_ref, wg_ref, wp_ref, y_ref):
    h = (x_ref[0]
         + jnp.dot(osb_ref[0], wo_ref[:SB_WIDTH], preferred_element_type=F32)
         + jnp.dot(osgu_ref[0], wo_ref[SB_WIDTH:], preferred_element_type=F32))
    ms = jnp.mean(h * h, axis=-1, keepdims=True)
    hn = (h * lax.rsqrt(ms + EPS) * pg_ref[...]).astype(BF16)
    gate_logit = jnp.dot(hn, wg_ref[...], preferred_element_type=F32)
    gate = 1.0 / (1.0 + jnp.exp(-gate_logit))
    pp = jnp.dot(p_ref[0].astype(BF16), wp_ref[...], preferred_element_type=F32)
    y_ref[0] = h + gate * pp


def _outproj(x, osb, osgu, p, w_out_bf, ple_norm_g, w_gate_bf, w_proj_bf, *, tm, name):
    b, t, _ = x.shape
    assert t % tm == 0
    tok = lambda w: pl.BlockSpec((1, tm, w), lambda bi, i: (bi, i, 0))
    const2 = lambda a: pl.BlockSpec(a.shape, lambda bi, i: (0, 0))
    pg = ple_norm_g.reshape(1, D_MODEL)
    block_bytes = (2 * _nbytes((tm, D_MODEL), F32) + 2 * _nbytes((tm, SEG), BF16)
                   + _nbytes((tm, PLE_DIM), F32) + _nbytes(w_out_bf.shape, BF16)
                   + _nbytes(w_gate_bf.shape, BF16) + _nbytes(w_proj_bf.shape, BF16))
    temp_bytes = 6 * _nbytes((tm, D_MODEL), F32)
    return pl.pallas_call(
        _outproj_kernel,
        grid=(b, t // tm),
        in_specs=[tok(D_MODEL), tok(SEG), tok(SEG), tok(PLE_DIM), const2(w_out_bf), const2(pg),
                  const2(w_gate_bf), const2(w_proj_bf)],
        out_specs=tok(D_MODEL),
        out_shape=jax.ShapeDtypeStruct((b, t, D_MODEL), F32),
        compiler_params=pltpu.CompilerParams(
            dimension_semantics=("arbitrary", "arbitrary"),
            vmem_limit_bytes=_vmem_limit(block_bytes, temp_bytes)),
        name=name,
    )(x, osb, osgu, p, w_out_bf, pg, w_gate_bf, w_proj_bf)


def _sgu_bias_rows(sgu_b_l, period):
    per_pos = jnp.tile(sgu_b_l[:, :period].T, (SGU_CHUNK // period, 1))
    return jnp.repeat(per_pos, GROUP_W, axis=1)


def kernel(x_prompt, x_sample, cache_k, cache_v, p_prompt, p_sample, norm_g, w_in, q_norm_g,
           k_norm_g, sgu_norm_g, sgu_w, sgu_b, w_out, ple_norm_g, w_ple_gate, w_ple_proj):
    depth = w_in.shape[0]
    assert depth == 1, "one layer per call"
    l = 0
    b_p, t_p, _ = x_prompt.shape
    b_s, t_s, _ = x_sample.shape
    past = cache_k.shape[2]
    n_s = b_s * t_s
    assert n_s == SGU_CHUNK and SGU_CHUNK % t_s == 0

    w_in_bf = w_in[l].astype(BF16)
    w_out_bf = w_out[l].astype(BF16)
    w_gate_bf = w_ple_gate[l].astype(BF16)
    w_proj_bf = w_ple_proj[l].astype(BF16)

    q, k, v, kt, vb, gsb, osgu = _inproj(
        x_prompt, norm_g[l], w_in_bf, q_norm_g[l], k_norm_g[l], sgu_norm_g[l],
        sgu_w[l], _sgu_bias_rows(sgu_b[l], SGU_CHUNK),
        tm=TM_IN, period=SGU_CHUNK, attn_layout=True)
    osb = _attn_prompt(q, kt, vb, gsb, tq=TQ, hps=HEADS_PER_STEP)
    y_prompt = _outproj(x_prompt, osb, osgu, p_prompt[l], w_out_bf, ple_norm_g[l], w_gate_bf,
                        w_proj_bf, tm=TM_OUT, name="outproj_prompt")

    rep = SGU_CHUNK // t_s
    sgu_w_s = jnp.tile(sgu_w[l][:, :t_s, :t_s], (1, rep, rep))
    xs = x_sample.reshape(1, n_s, D_MODEL)
    q_s, k_s, v_s, gsb_s, osgu_s, vs_s = _inproj(
        xs, norm_g[l], w_in_bf, q_norm_g[l], k_norm_g[l], sgu_norm_g[l],
        sgu_w_s, _sgu_bias_rows(sgu_b[l], t_s),
        tm=n_s, period=t_s, attn_layout=False)
    shp = (b_s, t_s, SB_WIDTH)
    osb_s = _attn_sample(q_s.reshape(shp), cache_k[l].reshape(b_s, past, SB_WIDTH),
                         cache_v[l].reshape(b_s, past, SB_WIDTH), k_s.reshape(shp),
                         v_s.reshape(shp), gsb_s.reshape(shp))
    y_sample = _outproj(xs, osb_s.reshape(1, n_s, SB_WIDTH), osgu_s, p_sample[l].reshape(1, n_s, PLE_DIM),
                        w_out_bf, ple_norm_g[l], w_gate_bf, w_proj_bf, tm=n_s,
                        name="outproj_sample").reshape(b_s, t_s, D_MODEL)

    head_shape = lambda a, bb, tt: a.reshape(1, bb, tt, HEADS, HEAD_DIM)
    return (y_prompt, y_sample,
            head_shape(k, b_p, t_p), head_shape(v, b_p, t_p),
            head_shape(k_s, b_s, t_s), head_shape(v_s, b_s, t_s),
            vs_s.reshape(1, b_s, t_s, GROUPS, GROUP_W))
```

```python
import functools
import math

import numpy as np
import jax
import jax.numpy as jnp
from jax import lax
from jax.experimental import pallas as pl
from jax.experimental.pallas import tpu as pltpu

F32 = jnp.float32
BF16 = jnp.bfloat16

LANES = 128
MXU_TILE = 256
VMEM_BYTES_V7X = 64 * 1024 * 1024

D_MODEL = 1024
PLE_DIM = 256
HEADS = 8
HEAD_DIM = 64
SB_WIDTH = HEADS * HEAD_DIM
GROUPS = 4
GROUP_W = 128
SGU_WIDTH = GROUPS * GROUP_W
SGU_CHUNK = 128
SEG = 512
N_SEG = 7
EPS = 1e-6

Q_SCALE = HEAD_DIM ** -0.5 * math.log2(math.e)

KSUB = LANES
NSUB = 2
TQ = NSUB * KSUB
HEADS_PER_STEP = 8
REM_DEAD_LOG2 = -152.0
SOFTPLUS_LINEAR = 64.0
TM_IN = 1024
TM_OUT = 1024


def _vmem_limit(block_bytes, temp_bytes):
    need = 2 * block_bytes + temp_bytes
    return int(min(need, VMEM_BYTES_V7X - 8 * 1024 * 1024))


def _nbytes(shape, dtype):
    return int(np.prod(shape)) * jnp.dtype(dtype).itemsize


def _suffix_sum_matrix():
    j = np.arange(KSUB)[:, None]
    s = np.arange(KSUB)[None, :]
    one = np.concatenate([np.where(j >= s, -1.0, 0.0), -np.ones((KSUB, LANES))], axis=1)
    return jnp.asarray(np.concatenate([one, one], axis=0), dtype=BF16)


def _head_mean_matrix():
    a = np.arange(MXU_TILE)
    bd = np.where(a[:, None] // HEAD_DIM == a[None, :] // HEAD_DIM, 1.0 / HEAD_DIM, 0.0)
    return jnp.asarray(np.concatenate([bd, bd], axis=0), dtype=BF16)


def _split_bf16(x):
    hi = x.astype(BF16)
    lo = (x - hi.astype(F32)).astype(BF16)
    return jnp.concatenate([hi, lo], axis=1)


def _gelu_tanh(x):
    return 0.5 * x * (1.0 + jnp.tanh(math.sqrt(2.0 / math.pi) * (x + 0.044715 * (x * x * x))))


def _silu(x):
    return x / (1.0 + jnp.exp(-x))


def _inproj_kernel(x_ref, ng_ref, w_ref, qg_ref, kg_ref, sg_ref, hm_ref, sw_ref, sb_ref,
                   *out_refs, tm, period, attn_layout):
    if attn_layout:
        q_ref, k_ref, v_ref, kt_ref, vb_ref, gsb_ref, osgu_ref = out_refs
    else:
        q_ref, k_ref, v_ref, gsb_ref, osgu_ref, vs_ref = out_refs

    x = x_ref[0]
    ms = jnp.mean(x * x, axis=-1, keepdims=True)
    xn = (x * lax.rsqrt(ms + EPS) * ng_ref[...]).astype(BF16)

    def seg(i):
        return jnp.dot(xn, w_ref[:, i * SEG:(i + 1) * SEG], preferred_element_type=F32)

    def head_rms(t, g):
        sq = t * t
        ms_h = jnp.concatenate(
            [jnp.dot(_split_bf16(sq[:, c:c + MXU_TILE]), hm_ref[...], preferred_element_type=F32)
             for c in range(0, SB_WIDTH, MXU_TILE)], axis=1)
        return t * lax.rsqrt(ms_h + EPS) * g

    q = head_rms(seg(0), qg_ref[...])
    q_ref[0] = (q * Q_SCALE).astype(BF16)

    k = head_rms(seg(1), kg_ref[...])
    k_ref[0] = k
    v = seg(2)
    v_ref[0] = v
    if attn_layout:
        kt_ref[0] = k.T.astype(BF16)
        vb_ref[0] = v.astype(BF16)

    gsb_ref[0] = _silu(seg(3)).astype(BF16)

    u = _gelu_tanh(seg(4))
    vs_raw = _gelu_tanh(seg(5))
    vs_groups = []
    for g in range(GROUPS):
        t = vs_raw[:, g * GROUP_W:(g + 1) * GROUP_W]
        ms_g = jnp.mean(t * t, axis=-1, keepdims=True)
        vs_groups.append(t * lax.rsqrt(ms_g + EPS) * sg_ref[:, g * GROUP_W:(g + 1) * GROUP_W])
    if not attn_layout:
        vs_ref[0] = jnp.concatenate(vs_groups, axis=1)

    row = lax.broadcasted_iota(jnp.int32, (SGU_CHUNK, SGU_CHUNK), 0)
    col = lax.broadcasted_iota(jnp.int32, (SGU_CHUNK, SGU_CHUNK), 1)
    keep = row >= col
    if period != SGU_CHUNK:
        keep = keep & ((row // period) == (col // period))
    s_groups = []
    for g in range(GROUPS):
        wm = jnp.where(keep, sw_ref[g], 0.0).astype(BF16)
        vg = vs_groups[g].astype(BF16)
        chunks = [jnp.dot(wm, vg[c * SGU_CHUNK:(c + 1) * SGU_CHUNK], preferred_element_type=F32)
                  + sb_ref[:, g * GROUP_W:(g + 1) * GROUP_W]
                  for c in range(tm // SGU_CHUNK)]
        s_groups.append(jnp.concatenate(chunks, axis=0) if len(chunks) > 1 else chunks[0])
    s = jnp.concatenate(s_groups, axis=1)

    osgu_ref[0] = (u * s * _silu(seg(6))).astype(BF16)


def _inproj(x, norm_g, w_in_bf, q_norm_g, k_norm_g, sgu_norm_g, sgu_w_tiled, sgu_bias, *,
            tm, period, attn_layout):
    b, t, _ = x.shape
    assert t % tm == 0 and tm % SGU_CHUNK == 0
    grid = (b, t // tm)
    tok = lambda w: pl.BlockSpec((1, tm, w), lambda bi, i: (bi, i, 0))
    const2 = lambda a: pl.BlockSpec(a.shape, lambda bi, i: (0, 0))
    const3 = lambda a: pl.BlockSpec(a.shape, lambda bi, i: (0, 0, 0))

    ng = norm_g.reshape(1, D_MODEL)
    qg = jnp.tile(q_norm_g, HEADS).reshape(1, SB_WIDTH)
    kg = jnp.tile(k_norm_g, HEADS).reshape(1, SB_WIDTH)
    sg = sgu_norm_g.reshape(1, SGU_WIDTH)
    hm = _head_mean_matrix()

    act = lambda dt: jax.ShapeDtypeStruct((b, t, SEG), dt)
    if attn_layout:
        out_shape = (act(BF16), act(F32), act(F32),
                     jax.ShapeDtypeStruct((b, SB_WIDTH, t), BF16), act(BF16), act(BF16), act(BF16))
        out_specs = (tok(SEG), tok(SEG), tok(SEG),
                     pl.BlockSpec((1, SB_WIDTH, tm), lambda bi, i: (bi, 0, i)),
                     tok(SEG), tok(SEG), tok(SEG))
    else:
        out_shape = (act(BF16), act(F32), act(F32), act(BF16), act(BF16), act(F32))
        out_specs = (tok(SEG),) * 6

    block_bytes = (_nbytes((tm, D_MODEL), F32) + _nbytes(w_in_bf.shape, BF16)
                   + _nbytes(hm.shape, BF16) + _nbytes(sgu_w_tiled.shape, F32)
                   + _nbytes(sgu_bias.shape, F32) + 7 * _nbytes((tm, SEG), F32))
    temp_bytes = 12 * _nbytes((tm, SEG), F32)
    return pl.pallas_call(
        functools.partial(_inproj_kernel, tm=tm, period=period, attn_layout=attn_layout),
        grid=grid,
        in_specs=[tok(D_MODEL), const2(ng), const2(w_in_bf), const2(qg), const2(kg), const2(sg),
                  const2(hm), const3(sgu_w_tiled), const2(sgu_bias)],
        out_specs=out_specs,
        out_shape=out_shape,
        compiler_params=pltpu.CompilerParams(
            dimension_semantics=("arbitrary", "arbitrary"),
            vmem_limit_bytes=_vmem_limit(block_bytes, temp_bytes)),
        name="inproj_prompt" if attn_layout else "inproj_sample",
    )(x, ng, w_in_bf, qg, kg, sg, hm, sgu_w_tiled, sgu_bias)


def _sb_block(z, vis, rem, uu):
    sums = jnp.dot(_split_bf16(_neg_log2_keep(z, vis)), uu, preferred_element_type=F32)
    return _sb_weights(z, vis, sums, rem)


def _neg_log2_keep(z, vis):
    sp = jnp.maximum(z, jnp.log2(1.0 + jnp.exp2(jnp.minimum(z, SOFTPLUS_LINEAR))))
    return sp if vis is None else jnp.where(vis, sp, 0.0)


def _sb_weights(z, vis, sums, rem):
    w = jnp.exp2(z + sums[:, :KSUB] + rem)
    if vis is not None:
        w = jnp.where(vis, w, 0.0)
    return w.astype(BF16), rem + sums[:, KSUB:]


def _attn_prompt_kernel(q_ref, kt_ref, v_ref, g_ref, uu_ref, o_ref, rem_ref, acc_ref, *, tq, hps):
    qi = pl.program_id(2)
    q = q_ref[0]
    qh = [q[:, h * HEAD_DIM:(h + 1) * HEAD_DIM] for h in range(hps)]
    uu = uu_ref[...]
    lane = lax.broadcasted_iota(jnp.int32, (KSUB, LANES), 1)
    first_head = lane < HEAD_DIM

    def pair_values(ks, p):
        vv = v_ref[0, pl.ds(ks, KSUB), p * LANES:(p + 1) * LANES]
        zero = jnp.zeros_like(vv)
        return jnp.concatenate([jnp.where(first_head, vv, zero),
                                jnp.where(first_head, zero, vv)], axis=0)

    def sweep(tiles, rems, accs):
        def put(full, r0, r1, part):
            pieces = ([full[:r0]] if r0 else []) + [part] + ([full[r1:]] if r1 < tq else [])
            return jnp.concatenate(pieces, axis=0) if len(pieces) > 1 else part

        zs, splits = [], []
        for ks, r0, r1, diagonal in tiles:
            kt = kt_ref[0, :, pl.ds(ks, KSUB)]
            vis = None
            if diagonal:
                row = lax.broadcasted_iota(jnp.int32, (r1 - r0, KSUB), 0)
                col = lax.broadcasted_iota(jnp.int32, (r1 - r0, KSUB), 1)
                vis = col < row
            for h in range(hps):
                z = jnp.dot(qh[h][r0:r1], kt[h * HEAD_DIM:(h + 1) * HEAD_DIM],
                            preferred_element_type=F32)
                zs.append((z, vis))
                splits.append(_split_bf16(_neg_log2_keep(z, vis)))
        sums = jnp.dot(jnp.concatenate(splits, axis=0), uu, preferred_element_type=F32)
        rems, accs = list(rems), list(accs)
        off, i = 0, 0
        for ks, r0, r1, _ in tiles:
            ws = []
            for h in range(hps):
                z, vis = zs[i]
                i += 1
                w, new = _sb_weights(z, vis, sums[off:off + r1 - r0], rems[h][r0:r1])
                off += r1 - r0
                rems[h] = put(rems[h], r0, r1, new)
                ws.append(w)
            for p in range(hps // 2):
                d = jnp.dot(jnp.concatenate(ws[2 * p:2 * p + 2], axis=1), pair_values(ks, p),
                            preferred_element_type=F32)
                accs[p] = put(accs[p], r0, r1, accs[p][r0:r1] + d)
        return rems, accs

    def diagonal_tiles():
        base = pl.multiple_of(qi * tq, tq)
        return [(pl.multiple_of(base + s * KSUB, KSUB), s * KSUB, tq, True)
                for s in reversed(range(NSUB))]

    def key_block_tiles(n, rows=None):
        kb = pl.multiple_of((qi - n) * tq, tq)
        rows = rows or {s: (0, tq) for s in range(NSUB)}
        return [(pl.multiple_of(kb + s * KSUB, KSUB), *rows[s], False)
                for s in reversed(range(NSUB)) if s in rows]

    zero_rems = [jnp.zeros((tq, LANES), F32)] * hps
    zero_accs = [jnp.zeros((tq, LANES), F32)] * (hps // 2)

    def finish(accs):
        acc = jnp.concatenate(accs, axis=1)
        o_ref[0] = (acc * g_ref[0].astype(F32)).astype(BF16)

    def any_alive(rems):
        most = functools.reduce(jnp.maximum, rems)
        return (jnp.max(most) > REM_DEAD_LOG2).astype(jnp.int32)

    @pl.when(qi == 0)
    def _():
        finish(sweep(diagonal_tiles(), zero_rems, zero_accs)[1])

    def load_state():
        return [rem_ref[h] for h in range(hps)], [acc_ref[p] for p in range(hps // 2)]

    def store_state(rems, accs):
        for h in range(hps):
            rem_ref[h] = rems[h]
        for p in range(hps // 2):
            acc_ref[p] = accs[p]

    eager_rows = {s: min(tq, (s + 3 - NSUB) * KSUB) for s in range(NSUB) if s + 3 - NSUB > 0}

    @pl.when(qi > 0)
    def _():
        first = key_block_tiles(1, {s: (0, e) for s, e in eager_rows.items()})
        rems, accs = sweep(diagonal_tiles() + first, zero_rems, zero_accs)
        store_state(rems, accs)
        alive = any_alive(rems)
        for s in reversed(range(NSUB)):
            e = eager_rows.get(s, 0)
            if e < tq:
                @pl.when(any_alive([r[e:] for r in rems]) > 0)
                def _():
                    store_state(*sweep(key_block_tiles(1, {s: (e, tq)}), *load_state()))

        def cond(c):
            n, alive = c
            return jnp.logical_and(n <= qi, alive > 0)

        def body(c):
            n, _ = c
            rems, accs = sweep(key_block_tiles(n), *load_state())
            store_state(rems, accs)
            return n + 1, any_alive(rems)

        lax.while_loop(cond, body, (jnp.int32(2), alive))
        finish(load_state()[1])


def _attn_prompt(q, kt, vb, gsb, *, tq, hps):
    b, t, _ = q.shape
    assert t % tq == 0 and tq == NSUB * KSUB and HEADS % hps == 0 and hps % 2 == 0
    width = hps * HEAD_DIM
    uu = _suffix_sum_matrix()
    grid = (b, HEADS // hps, t // tq)
    row_blk = pl.BlockSpec((1, tq, width), lambda bi, hg, i: (bi, i, hg))
    block_bytes = (3 * _nbytes((tq, width), BF16) + 2 * _nbytes((width, t), BF16)
                   + _nbytes(uu.shape, BF16))
    temp_bytes = (hps + hps // 2) * _nbytes((tq, LANES), F32) + 8 * NSUB * hps * _nbytes((tq, KSUB), F32)
    return pl.pallas_call(
        functools.partial(_attn_prompt_kernel, tq=tq, hps=hps),
        grid=grid,
        in_specs=[row_blk,
                  pl.BlockSpec((1, width, t), lambda bi, hg, i: (bi, hg, 0)),
                  pl.BlockSpec((1, t, width), lambda bi, hg, i: (bi, 0, hg)),
                  row_blk,
                  pl.BlockSpec(uu.shape, lambda bi, hg, i: (0, 0))],
        out_specs=row_blk,
        out_shape=jax.ShapeDtypeStruct((b, t, SB_WIDTH), BF16),
        scratch_shapes=[pltpu.VMEM((hps, tq, LANES), F32), pltpu.VMEM((hps // 2, tq, LANES), F32)],
        compiler_params=pltpu.CompilerParams(
            dimension_semantics=("arbitrary", "arbitrary", "arbitrary"),
            vmem_limit_bytes=_vmem_limit(block_bytes, temp_bytes)),
        name="sb_attn_prompt",
    )(q, kt, vb, gsb, uu)


def _attn_sample_kernel(q_ref, ck_ref, cv_ref, kn_ref, vn_ref, g_ref, uu_ref, o_ref,
                        *, t_new, past):
    q = q_ref[0]
    rows = HEADS * t_new
    uu = uu_ref[...]

    def logits(kblk):
        return jnp.concatenate(
            [lax.dot_general(q[:, h * HEAD_DIM:(h + 1) * HEAD_DIM],
                             kblk[:, h * HEAD_DIM:(h + 1) * HEAD_DIM],
                             (((1,), (1,)), ((), ())), preferred_element_type=F32)
             for h in range(HEADS)], axis=0)

    def attend(w, vblk):
        return [jnp.dot(w[h * t_new:(h + 1) * t_new],
                        vblk[:, h * HEAD_DIM:(h + 1) * HEAD_DIM], preferred_element_type=F32)
                for h in range(HEADS)]

    pad = jnp.zeros((LANES - t_new, SB_WIDTH), F32)
    kn = jnp.concatenate([kn_ref[0], pad], axis=0).astype(BF16)
    vn = jnp.concatenate([vn_ref[0], pad], axis=0).astype(BF16)
    row = lax.broadcasted_iota(jnp.int32, (rows, LANES), 0)
    col = lax.broadcasted_iota(jnp.int32, (rows, LANES), 1)
    vis = col < (row % t_new)
    rem = jnp.zeros((rows, LANES), F32)
    w, rem = _sb_block(logits(kn), vis, rem, uu)
    acc = attend(w, vn)

    for j in reversed(range(past // KSUB)):
        kb = ck_ref[0, j * KSUB:(j + 1) * KSUB, :].astype(BF16)
        vb = cv_ref[0, j * KSUB:(j + 1) * KSUB, :].astype(BF16)
        w, rem = _sb_block(logits(kb), None, rem, uu)
        acc = [a + d for a, d in zip(acc, attend(w, vb))]

    o = jnp.concatenate(acc, axis=1)
    o_ref[0] = (o * g_ref[0].astype(F32)).astype(BF16)


def _attn_sample(q, cache_k, cache_v, k_new, v_new, gsb):
    b, t_new, _ = q.shape
    past = cache_k.shape[1]
    assert past % KSUB == 0 and t_new <= KSUB
    uu = _suffix_sum_matrix()
    new_blk = pl.BlockSpec((1, t_new, SB_WIDTH), lambda bi: (bi, 0, 0))
    cache_blk = pl.BlockSpec((1, past, SB_WIDTH), lambda bi: (bi, 0, 0))
    block_bytes = (2 * _nbytes((past, SB_WIDTH), F32) + 6 * _nbytes((t_new, SB_WIDTH), F32)
                   + _nbytes(uu.shape, BF16))
    temp_bytes = (8 * (past // KSUB + 1) * _nbytes((HEADS * t_new, KSUB), F32)
                  + 4 * _nbytes((past, SB_WIDTH), BF16))
    return pl.pallas_call(
        functools.partial(_attn_sample_kernel, t_new=t_new, past=past),
        grid=(b,),
        in_specs=[new_blk, cache_blk, cache_blk, new_blk, new_blk, new_blk,
                  pl.BlockSpec(uu.shape, lambda bi: (0, 0))],
        out_specs=new_blk,
        out_shape=jax.ShapeDtypeStruct((b, t_new, SB_WIDTH), BF16),
        compiler_params=pltpu.CompilerParams(
            dimension_semantics=("arbitrary",),
            vmem_limit_bytes=_vmem_limit(block_bytes, temp_bytes)),
        name="sb_attn_sample",
    )(q, cache_k, cache_v, k_new, v_new, gsb, uu)


def _outproj_kernel(x_ref, osb_ref, osgu_ref, p_ref, wo_ref, pg_ref, wg_ref, wp_ref, y_ref):
    h = (x_ref[0]
         + jnp.dot(osb_ref[0], wo_ref[:SB_WIDTH], preferred_element_type=F32)
         + jnp.dot(osgu_ref[0], wo_ref[SB_WIDTH:], preferred_element_type=F32))
    ms = jnp.mean(h * h, axis=-1, keepdims=True)
    hn = (h * lax.rsqrt(ms + EPS) * pg_ref[...]).astype(BF16)
    gate_logit = jnp.dot(hn, wg_ref[...], preferred_element_type=F32)
    gate = 1.0 / (1.0 + jnp.exp(-gate_logit))
    pp = jnp.dot(p_ref[0].astype(BF16), wp_ref[...], preferred_element_type=F32)
    y_ref[0] = h + gate * pp


def _outproj(x, osb, osgu, p, w_out_bf, ple_norm_g, w_gate_bf, w_proj_bf, *, tm, name):
    b, t, _ = x.shape
    assert t % tm == 0
    tok = lambda w: pl.BlockSpec((1, tm, w), lambda bi, i: (bi, i, 0))
    const2 = lambda a: pl.BlockSpec(a.shape, lambda bi, i: (0, 0))
    pg = ple_norm_g.reshape(1, D_MODEL)
    block_bytes = (2 * _nbytes((tm, D_MODEL), F32) + 2 * _nbytes((tm, SEG), BF16)
                   + _nbytes((tm, PLE_DIM), F32) + _nbytes(w_out_bf.shape, BF16)
                   + _nbytes(w_gate_bf.shape, BF16) + _nbytes(w_proj_bf.shape, BF16))
    temp_bytes = 6 * _nbytes((tm, D_MODEL), F32)
    return pl.pallas_call(
        _outproj_kernel,
        grid=(b, t // tm),
        in_specs=[tok(D_MODEL), tok(SEG), tok(SEG), tok(PLE_DIM), const2(w_out_bf), const2(pg),
                  const2(w_gate_bf), const2(w_proj_bf)],
        out_specs=tok(D_MODEL),
        out_shape=jax.ShapeDtypeStruct((b, t, D_MODEL), F32),
        compiler_params=pltpu.CompilerParams(
            dimension_semantics=("arbitrary", "arbitrary"),
            vmem_limit_bytes=_vmem_limit(block_bytes, temp_bytes)),
        name=name,
    )(x, osb, osgu, p, w_out_bf, pg, w_gate_bf, w_proj_bf)


def _sgu_bias_rows(sgu_b_l, period):
    per_pos = jnp.tile(sgu_b_l[:, :period].T, (SGU_CHUNK // period, 1))
    return jnp.repeat(per_pos, GROUP_W, axis=1)


def kernel(x_prompt, x_sample, cache_k, cache_v, p_prompt, p_sample, norm_g, w_in, q_norm_g,
           k_norm_g, sgu_norm_g, sgu_w, sgu_b, w_out, ple_norm_g, w_ple_gate, w_ple_proj):
    depth = w_in.shape[0]
    assert depth == 1, "one layer per call"
    l = 0
    b_p, t_p, _ = x_prompt.shape
    b_s, t_s, _ = x_sample.shape
    past = cache_k.shape[2]
    n_s = b_s * t_s
    assert n_s == SGU_CHUNK and SGU_CHUNK % t_s == 0

    w_in_bf = w_in[l].astype(BF16)
    w_out_bf = w_out[l].astype(BF16)
    w_gate_bf = w_ple_gate[l].astype(BF16)
    w_proj_bf = w_ple_proj[l].astype(BF16)

    q, k, v, kt, vb, gsb, osgu = _inproj(
        x_prompt, norm_g[l], w_in_bf, q_norm_g[l], k_norm_g[l], sgu_norm_g[l],
        sgu_w[l], _sgu_bias_rows(sgu_b[l], SGU_CHUNK),
        tm=TM_IN, period=SGU_CHUNK, attn_layout=True)
    osb = _attn_prompt(q, kt, vb, gsb, tq=TQ, hps=HEADS_PER_STEP)
    y_prompt = _outproj(x_prompt, osb, osgu, p_prompt[l], w_out_bf, ple_norm_g[l], w_gate_bf,
                        w_proj_bf, tm=TM_OUT, name="outproj_prompt")

    rep = SGU_CHUNK // t_s
    sgu_w_s = jnp.tile(sgu_w[l][:, :t_s, :t_s], (1, rep, rep))
    xs = x_sample.reshape(1, n_s, D_MODEL)
    q_s, k_s, v_s, gsb_s, osgu_s, vs_s = _inproj(
        xs, norm_g[l], w_in_bf, q_norm_g[l], k_norm_g[l], sgu_norm_g[l],
        sgu_w_s, _sgu_bias_rows(sgu_b[l], t_s),
        tm=n_s, period=t_s, attn_layout=False)
    shp = (b_s, t_s, SB_WIDTH)
    osb_s = _attn_sample(q_s.reshape(shp), cache_k[l].reshape(b_s, past, SB_WIDTH),
                         cache_v[l].reshape(b_s, past, SB_WIDTH), k_s.reshape(shp),
                         v_s.reshape(shp), gsb_s.reshape(shp))
    y_sample = _outproj(xs, osb_s.reshape(1, n_s, SB_WIDTH), osgu_s, p_sample[l].reshape(1, n_s, PLE_DIM),
                        w_out_bf, ple_norm_g[l], w_gate_bf, w_proj_bf, tm=n_s,
                        name="outproj_sample").reshape(b_s, t_s, D_MODEL)

    head_shape = lambda a, bb, tt: a.reshape(1, bb, tt, HEADS, HEAD_DIM)
    return (y_prompt, y_sample,
            head_shape(k, b_p, t_p), head_shape(v, b_p, t_p),
            head_shape(k_s, b_s, t_s), head_shape(v_s, b_s, t_s),
            vs_s.reshape(1, b_s, t_s, GROUPS, GROUP_W))
```

```python
import functools
import math

import numpy as np
import jax
import jax.numpy as jnp
from jax import lax
from jax.experimental import pallas as pl
from jax.experimental.pallas import tpu as pltpu

F32 = jnp.float32
BF16 = jnp.bfloat16

LANES = 128
MXU_TILE = 256
VMEM_BYTES_V7X = 64 * 1024 * 1024

D_MODEL = 1024
PLE_DIM = 256
HEADS = 8
HEAD_DIM = 64
SB_WIDTH = HEADS * HEAD_DIM
GROUPS = 4
GROUP_W = 128
SGU_WIDTH = GROUPS * GROUP_W
SGU_CHUNK = 128
SEG = 512
N_SEG = 7
EPS = 1e-6

Q_SCALE = HEAD_DIM ** -0.5 * math.log2(math.e)

KSUB = LANES
NSUB = 2
TQ = NSUB * KSUB
HEADS_PER_STEP = 8
REM_DEAD_LOG2 = -152.0
SOFTPLUS_LINEAR = 64.0
EAGER_CACHE_SUBS = 2
TM_IN = 1024
TM_OUT = 1024


def _vmem_limit(block_bytes, temp_bytes):
    need = 2 * block_bytes + temp_bytes
    return int(min(need, VMEM_BYTES_V7X - 8 * 1024 * 1024))


def _nbytes(shape, dtype):
    return int(np.prod(shape)) * jnp.dtype(dtype).itemsize


def _suffix_sum_matrix():
    j = np.arange(KSUB)[:, None]
    s = np.arange(KSUB)[None, :]
    one = np.concatenate([np.where(j >= s, -1.0, 0.0), -np.ones((KSUB, LANES))], axis=1)
    return jnp.asarray(np.concatenate([one, one], axis=0), dtype=BF16)


def _head_mean_matrix():
    a = np.arange(MXU_TILE)
    bd = np.where(a[:, None] // HEAD_DIM == a[None, :] // HEAD_DIM, 1.0 / HEAD_DIM, 0.0)
    return jnp.asarray(np.concatenate([bd, bd], axis=0), dtype=BF16)


def _split_bf16(x):
    hi = x.astype(BF16)
    lo = (x - hi.astype(F32)).astype(BF16)
    return jnp.concatenate([hi, lo], axis=1)


def _gelu_tanh(x):
    return 0.5 * x * (1.0 + jnp.tanh(math.sqrt(2.0 / math.pi) * (x + 0.044715 * (x * x * x))))


def _silu(x):
    return x / (1.0 + jnp.exp(-x))


def _inproj_kernel(x_ref, ng_ref, w_ref, qg_ref, kg_ref, sg_ref, hm_ref, sw_ref, sb_ref,
                   *out_refs, tm, period, attn_layout):
    if attn_layout:
        q_ref, k_ref, v_ref, kt_ref, vb_ref, gsb_ref, osgu_ref = out_refs
    else:
        q_ref, k_ref, v_ref, gsb_ref, osgu_ref, vs_ref = out_refs

    x = x_ref[0]
    ms = jnp.mean(x * x, axis=-1, keepdims=True)
    xn = (x * lax.rsqrt(ms + EPS) * ng_ref[...]).astype(BF16)

    def seg(i):
        return jnp.dot(xn, w_ref[:, i * SEG:(i + 1) * SEG], preferred_element_type=F32)

    def head_rms(t, g):
        sq = t * t
        ms_h = jnp.concatenate(
            [jnp.dot(_split_bf16(sq[:, c:c + MXU_TILE]), hm_ref[...], preferred_element_type=F32)
             for c in range(0, SB_WIDTH, MXU_TILE)], axis=1)
        return t * lax.rsqrt(ms_h + EPS) * g

    q = head_rms(seg(0), qg_ref[...])
    q_ref[0] = (q * Q_SCALE).astype(BF16)

    k = head_rms(seg(1), kg_ref[...])
    k_ref[0] = k
    v = seg(2)
    v_ref[0] = v
    if attn_layout:
        kt_ref[0] = k.T.astype(BF16)
        vb_ref[0] = v.astype(BF16)

    gsb_ref[0] = _silu(seg(3)).astype(BF16)

    u = _gelu_tanh(seg(4))
    vs_raw = _gelu_tanh(seg(5))
    vs_groups = []
    for g in range(GROUPS):
        t = vs_raw[:, g * GROUP_W:(g + 1) * GROUP_W]
        ms_g = jnp.mean(t * t, axis=-1, keepdims=True)
        vs_groups.append(t * lax.rsqrt(ms_g + EPS) * sg_ref[:, g * GROUP_W:(g + 1) * GROUP_W])
    if not attn_layout:
        vs_ref[0] = jnp.concatenate(vs_groups, axis=1)

    row = lax.broadcasted_iota(jnp.int32, (SGU_CHUNK, SGU_CHUNK), 0)
    col = lax.broadcasted_iota(jnp.int32, (SGU_CHUNK, SGU_CHUNK), 1)
    keep = row >= col
    if period != SGU_CHUNK:
        keep = keep & ((row // period) == (col // period))
    s_groups = []
    for g in range(GROUPS):
        wm = jnp.where(keep, sw_ref[g], 0.0).astype(BF16)
        vg = vs_groups[g].astype(BF16)
        chunks = [jnp.dot(wm, vg[c * SGU_CHUNK:(c + 1) * SGU_CHUNK], preferred_element_type=F32)
                  + sb_ref[:, g * GROUP_W:(g + 1) * GROUP_W]
                  for c in range(tm // SGU_CHUNK)]
        s_groups.append(jnp.concatenate(chunks, axis=0) if len(chunks) > 1 else chunks[0])
    s = jnp.concatenate(s_groups, axis=1)

    osgu_ref[0] = (u * s * _silu(seg(6))).astype(BF16)


def _inproj(x, norm_g, w_in_bf, q_norm_g, k_norm_g, sgu_norm_g, sgu_w_tiled, sgu_bias, *,
            tm, period, attn_layout):
    b, t, _ = x.shape
    assert t % tm == 0 and tm % SGU_CHUNK == 0
    grid = (b, t // tm)
    tok = lambda w: pl.BlockSpec((1, tm, w), lambda bi, i: (bi, i, 0))
    const2 = lambda a: pl.BlockSpec(a.shape, lambda bi, i: (0, 0))
    const3 = lambda a: pl.BlockSpec(a.shape, lambda bi, i: (0, 0, 0))

    ng = norm_g.reshape(1, D_MODEL)
    qg = jnp.tile(q_norm_g, HEADS).reshape(1, SB_WIDTH)
    kg = jnp.tile(k_norm_g, HEADS).reshape(1, SB_WIDTH)
    sg = sgu_norm_g.reshape(1, SGU_WIDTH)
    hm = _head_mean_matrix()

    act = lambda dt: jax.ShapeDtypeStruct((b, t, SEG), dt)
    if attn_layout:
        out_shape = (act(BF16), act(F32), act(F32),
                     jax.ShapeDtypeStruct((b, SB_WIDTH, t), BF16), act(BF16), act(BF16), act(BF16))
        out_specs = (tok(SEG), tok(SEG), tok(SEG),
                     pl.BlockSpec((1, SB_WIDTH, tm), lambda bi, i: (bi, 0, i)),
                     tok(SEG), tok(SEG), tok(SEG))
    else:
        out_shape = (act(BF16), act(F32), act(F32), act(BF16), act(BF16), act(F32))
        out_specs = (tok(SEG),) * 6

    block_bytes = (_nbytes((tm, D_MODEL), F32) + _nbytes(w_in_bf.shape, BF16)
                   + _nbytes(hm.shape, BF16) + _nbytes(sgu_w_tiled.shape, F32)
                   + _nbytes(sgu_bias.shape, F32) + 7 * _nbytes((tm, SEG), F32))
    temp_bytes = 12 * _nbytes((tm, SEG), F32)
    return pl.pallas_call(
        functools.partial(_inproj_kernel, tm=tm, period=period, attn_layout=attn_layout),
        grid=grid,
        in_specs=[tok(D_MODEL), const2(ng), const2(w_in_bf), const2(qg), const2(kg), const2(sg),
                  const2(hm), const3(sgu_w_tiled), const2(sgu_bias)],
        out_specs=out_specs,
        out_shape=out_shape,
        compiler_params=pltpu.CompilerParams(
            dimension_semantics=("arbitrary", "arbitrary"),
            vmem_limit_bytes=_vmem_limit(block_bytes, temp_bytes)),
        name="inproj_prompt" if attn_layout else "inproj_sample",
    )(x, ng, w_in_bf, qg, kg, sg, hm, sgu_w_tiled, sgu_bias)


def _neg_log2_keep(z, vis):
    sp = jnp.maximum(z, jnp.log2(1.0 + jnp.exp2(jnp.minimum(z, SOFTPLUS_LINEAR))))
    return sp if vis is None else jnp.where(vis, sp, 0.0)


def _sb_weights(z, vis, sums, rem):
    w = jnp.exp2(z + sums[:, :KSUB] + rem)
    if vis is not None:
        w = jnp.where(vis, w, 0.0)
    return w.astype(BF16), rem + sums[:, KSUB:]


def _attn_prompt_kernel(q_ref, kt_ref, v_ref, g_ref, uu_ref, o_ref, rem_ref, acc_ref, *, tq, hps):
    qi = pl.program_id(2)
    q = q_ref[0]
    qh = [q[:, h * HEAD_DIM:(h + 1) * HEAD_DIM] for h in range(hps)]
    uu = uu_ref[...]
    lane = lax.broadcasted_iota(jnp.int32, (KSUB, LANES), 1)
    first_head = lane < HEAD_DIM

    def pair_values(ks, p):
        vv = v_ref[0, pl.ds(ks, KSUB), p * LANES:(p + 1) * LANES]
        zero = jnp.zeros_like(vv)
        return jnp.concatenate([jnp.where(first_head, vv, zero),
                                jnp.where(first_head, zero, vv)], axis=0)

    def sweep(tiles, rems, accs):
        def put(full, r0, r1, part):
            pieces = ([full[:r0]] if r0 else []) + [part] + ([full[r1:]] if r1 < tq else [])
            return jnp.concatenate(pieces, axis=0) if len(pieces) > 1 else part

        zs, splits = [], []
        for ks, r0, r1, diagonal in tiles:
            kt = kt_ref[0, :, pl.ds(ks, KSUB)]
            vis = None
            if diagonal:
                row = lax.broadcasted_iota(jnp.int32, (r1 - r0, KSUB), 0)
                col = lax.broadcasted_iota(jnp.int32, (r1 - r0, KSUB), 1)
                vis = col < row
            for h in range(hps):
                z = jnp.dot(qh[h][r0:r1], kt[h * HEAD_DIM:(h + 1) * HEAD_DIM],
                            preferred_element_type=F32)
                zs.append((z, vis))
                splits.append(_split_bf16(_neg_log2_keep(z, vis)))
        sums = jnp.dot(jnp.concatenate(splits, axis=0), uu, preferred_element_type=F32)
        rems, accs = list(rems), list(accs)
        off, i = 0, 0
        for ks, r0, r1, _ in tiles:
            ws = []
            for h in range(hps):
                z, vis = zs[i]
                i += 1
                w, new = _sb_weights(z, vis, sums[off:off + r1 - r0], rems[h][r0:r1])
                off += r1 - r0
                rems[h] = put(rems[h], r0, r1, new)
                ws.append(w)
            for p in range(hps // 2):
                d = jnp.dot(jnp.concatenate(ws[2 * p:2 * p + 2], axis=1), pair_values(ks, p),
                            preferred_element_type=F32)
                accs[p] = put(accs[p], r0, r1, accs[p][r0:r1] + d)
        return rems, accs

    def diagonal_tiles():
        base = pl.multiple_of(qi * tq, tq)
        return [(pl.multiple_of(base + s * KSUB, KSUB), s * KSUB, tq, True)
                for s in reversed(range(NSUB))]

    def key_block_tiles(n, rows=None):
        kb = pl.multiple_of((qi - n) * tq, tq)
        rows = rows or {s: (0, tq) for s in range(NSUB)}
        return [(pl.multiple_of(kb + s * KSUB, KSUB), *rows[s], False)
                for s in reversed(range(NSUB)) if s in rows]

    zero_rems = [jnp.zeros((tq, LANES), F32)] * hps
    zero_accs = [jnp.zeros((tq, LANES), F32)] * (hps // 2)

    def finish(accs):
        acc = jnp.concatenate(accs, axis=1)
        o_ref[0] = (acc * g_ref[0].astype(F32)).astype(BF16)

    def any_alive(rems):
        most = functools.reduce(jnp.maximum, rems)
        return (jnp.max(most) > REM_DEAD_LOG2).astype(jnp.int32)

    @pl.when(qi == 0)
    def _():
        finish(sweep(diagonal_tiles(), zero_rems, zero_accs)[1])

    def load_state():
        return [rem_ref[h] for h in range(hps)], [acc_ref[p] for p in range(hps // 2)]

    def store_state(rems, accs):
        for h in range(hps):
            rem_ref[h] = rems[h]
        for p in range(hps // 2):
            acc_ref[p] = accs[p]

    eager_rows = {s: min(tq, (s + 3 - NSUB) * KSUB) for s in range(NSUB) if s + 3 - NSUB > 0}

    @pl.when(qi > 0)
    def _():
        first = key_block_tiles(1, {s: (0, e) for s, e in eager_rows.items()})
        rems, accs = sweep(diagonal_tiles() + first, zero_rems, zero_accs)
        store_state(rems, accs)
        alive = any_alive(rems)
        for s in reversed(range(NSUB)):
            e = eager_rows.get(s, 0)
            if e < tq:
                @pl.when(any_alive([r[e:] for r in rems]) > 0)
                def _():
                    store_state(*sweep(key_block_tiles(1, {s: (e, tq)}), *load_state()))

        def cond(c):
            n, alive = c
            return jnp.logical_and(n <= qi, alive > 0)

        def body(c):
            n, _ = c
            rems, accs = sweep(key_block_tiles(n), *load_state())
            store_state(rems, accs)
            return n + 1, any_alive(rems)

        lax.while_loop(cond, body, (jnp.int32(2), alive))
        finish(load_state()[1])


def _attn_prompt(q, kt, vb, gsb, *, tq, hps):
    b, t, _ = q.shape
    assert t % tq == 0 and tq == NSUB * KSUB and HEADS % hps == 0 and hps % 2 == 0
    width = hps * HEAD_DIM
    uu = _suffix_sum_matrix()
    grid = (b, HEADS // hps, t // tq)
    row_blk = pl.BlockSpec((1, tq, width), lambda bi, hg, i: (bi, i, hg))
    block_bytes = (3 * _nbytes((tq, width), BF16) + 2 * _nbytes((width, t), BF16)
                   + _nbytes(uu.shape, BF16))
    temp_bytes = (hps + hps // 2) * _nbytes((tq, LANES), F32) + 8 * NSUB * hps * _nbytes((tq, KSUB), F32)
    return pl.pallas_call(
        functools.partial(_attn_prompt_kernel, tq=tq, hps=hps),
        grid=grid,
        in_specs=[row_blk,
                  pl.BlockSpec((1, width, t), lambda bi, hg, i: (bi, hg, 0)),
                  pl.BlockSpec((1, t, width), lambda bi, hg, i: (bi, 0, hg)),
                  row_blk,
                  pl.BlockSpec(uu.shape, lambda bi, hg, i: (0, 0))],
        out_specs=row_blk,
        out_shape=jax.ShapeDtypeStruct((b, t, SB_WIDTH), BF16),
        scratch_shapes=[pltpu.VMEM((hps, tq, LANES), F32), pltpu.VMEM((hps // 2, tq, LANES), F32)],
        compiler_params=pltpu.CompilerParams(
            dimension_semantics=("arbitrary", "arbitrary", "arbitrary"),
            vmem_limit_bytes=_vmem_limit(block_bytes, temp_bytes)),
        name="sb_attn_prompt",
    )(q, kt, vb, gsb, uu)


def _attn_sample_kernel(q_ref, kn_ref, vn_ref, g_ref, uu_ref, ck_hbm, cv_hbm, o_ref,
                        kwin, vwin, sem, rem_s, acc_s, *, t_new, past, nb):
    rows = HEADS * t_new
    uu = uu_ref[...]
    n_sub = past // KSUB

    def fetch(which, b, j, slot):
        src, dst = ((ck_hbm, kwin), (cv_hbm, vwin))[which]
        first = pl.multiple_of(j * KSUB, KSUB)
        return [pltpu.make_async_copy(src.at[b, pl.ds(first, KSUB), h], dst.at[b, slot, h],
                                      sem.at[which, b, slot]) for h in range(HEADS)]

    def start(copies):
        for c in copies:
            c.start()

    def wait(copies):
        for c in copies:
            c.wait()

    for b in range(nb):
        for slot in range(EAGER_CACHE_SUBS):
            for which in range(2):
                start(fetch(which, b, n_sub - 1 - slot, slot))

    row = lax.broadcasted_iota(jnp.int32, (rows, KSUB), 0)
    col = lax.broadcasted_iota(jnp.int32, (rows, KSUB), 1)
    new_vis = col < (row % t_new)
    pad = jnp.zeros((KSUB - t_new, SB_WIDTH), F32)

    def new_block(ref, b):
        blk = jnp.concatenate([ref[b], pad], axis=0).astype(BF16)
        return [blk[:, h * HEAD_DIM:(h + 1) * HEAD_DIM] for h in range(HEADS)]

    def cached(win, b, slot):
        return [win[b, slot, h].astype(BF16) for h in range(HEADS)]

    def sweep(b, tiles, rem, acc):
        q = q_ref[b]
        zs = [jnp.concatenate(
            [lax.dot_general(q[:, h * HEAD_DIM:(h + 1) * HEAD_DIM], keys[h],
                             (((1,), (1,)), ((), ())), preferred_element_type=F32)
             for h in range(HEADS)], axis=0) for keys, _, _ in tiles]
        splits = [_split_bf16(_neg_log2_keep(z, vis)) for z, (_, _, vis) in zip(zs, tiles)]
        sums = jnp.dot(jnp.concatenate(splits, axis=0), uu, preferred_element_type=F32)
        for i, (z, (_, vals, vis)) in enumerate(zip(zs, tiles)):
            w, rem = _sb_weights(z, vis, sums[i * rows:(i + 1) * rows], rem)
            acc = [a + jnp.dot(w[h * t_new:(h + 1) * t_new], vals[h], preferred_element_type=F32)
                   for h, a in enumerate(acc)]
        return rem, acc

    def alive_of(rem):
        return (jnp.max(rem) > REM_DEAD_LOG2).astype(jnp.int32)

    alive = jnp.int32(0)
    for b in range(nb):
        for slot in range(EAGER_CACHE_SUBS):
            for which in range(2):
                wait(fetch(which, b, n_sub - 1 - slot, slot))
        tiles = [(new_block(kn_ref, b), new_block(vn_ref, b), new_vis)]
        tiles += [(cached(kwin, b, s), cached(vwin, b, s), None) for s in range(EAGER_CACHE_SUBS)]
        rem, acc = sweep(b, tiles, jnp.zeros((rows, LANES), F32),
                         [jnp.zeros((t_new, HEAD_DIM), F32)] * HEADS)
        rem_s[b] = rem
        acc_s[b] = jnp.concatenate(acc, axis=1)
        alive = jnp.maximum(alive, alive_of(rem))

    def cond(c):
        j, alive = c
        return jnp.logical_and(j >= 0, alive > 0)

    def body(c):
        j, _ = c
        for b in range(nb):
            for which in range(2):
                start(fetch(which, b, j, 0))
        alive = jnp.int32(0)
        for b in range(nb):
            for which in range(2):
                wait(fetch(which, b, j, 0))
            acc_b = acc_s[b]
            rem, acc = sweep(b, [(cached(kwin, b, 0), cached(vwin, b, 0), None)], rem_s[b],
                             [acc_b[:, h * HEAD_DIM:(h + 1) * HEAD_DIM] for h in range(HEADS)])
            rem_s[b] = rem
            acc_s[b] = jnp.concatenate(acc, axis=1)
            alive = jnp.maximum(alive, alive_of(rem))
        return j - 1, alive

    lax.while_loop(cond, body, (jnp.int32(n_sub - 1 - EAGER_CACHE_SUBS), alive))
    for b in range(nb):
        o_ref[b] = (acc_s[b] * g_ref[b].astype(F32)).astype(BF16)


def _attn_sample(q, cache_k, cache_v, k_new, v_new, gsb):
    nb, t_new, _ = q.shape
    past = cache_k.shape[1]
    assert past % KSUB == 0 and past // KSUB >= EAGER_CACHE_SUBS and t_new <= KSUB
    uu = _suffix_sum_matrix()
    rows = HEADS * t_new
    whole = lambda a: pl.BlockSpec(a.shape, lambda i: (0,) * a.ndim)
    window = (nb, EAGER_CACHE_SUBS, HEADS, KSUB, HEAD_DIM)
    padded_window = _nbytes(window[:-1] + (LANES,), F32)
    block_bytes = sum(_nbytes(a.shape, a.dtype) for a in (q, k_new, v_new, gsb, uu, q))
    temp_bytes = (2 * padded_window + _nbytes((nb, rows, LANES), F32)
                  + _nbytes((nb, t_new, SB_WIDTH), F32)
                  + 16 * (EAGER_CACHE_SUBS + 1) * _nbytes((rows, KSUB), F32))
    return pl.pallas_call(
        functools.partial(_attn_sample_kernel, t_new=t_new, past=past, nb=nb),
        grid=(1,),
        in_specs=[whole(q), whole(k_new), whole(v_new), whole(gsb), whole(uu),
                  pl.BlockSpec(memory_space=pl.ANY), pl.BlockSpec(memory_space=pl.ANY)],
        out_specs=whole(q),
        out_shape=jax.ShapeDtypeStruct((nb, t_new, SB_WIDTH), BF16),
        scratch_shapes=[pltpu.VMEM(window, F32), pltpu.VMEM(window, F32),
                        pltpu.SemaphoreType.DMA((2, nb, EAGER_CACHE_SUBS)),
                        pltpu.VMEM((nb, rows, LANES), F32),
                        pltpu.VMEM((nb, t_new, SB_WIDTH), F32)],
        compiler_params=pltpu.CompilerParams(
            dimension_semantics=("arbitrary",),
            vmem_limit_bytes=_vmem_limit(block_bytes, temp_bytes)),
        name="sb_attn_sample",
    )(q, k_new, v_new, gsb, uu, cache_k, cache_v)


def _outproj_kernel(x_ref, osb_ref, osgu_ref, p_ref, wo_ref, pg_ref, wg_ref, wp_ref, y_ref):
    h = (x_ref[0]
         + jnp.dot(osb_ref[0], wo_ref[:SB_WIDTH], preferred_element_type=F32)
         + jnp.dot(osgu_ref[0], wo_ref[SB_WIDTH:], preferred_element_type=F32))
    ms = jnp.mean(h * h, axis=-1, keepdims=True)
    hn = (h * lax.rsqrt(ms + EPS) * pg_ref[...]).astype(BF16)
    gate_logit = jnp.dot(hn, wg_ref[...], preferred_element_type=F32)
    gate = 1.0 / (1.0 + jnp.exp(-gate_logit))
    pp = jnp.dot(p_ref[0].astype(BF16), wp_ref[...], preferred_element_type=F32)
    y_ref[0] = h + gate * pp


def _outproj(x, osb, osgu, p, w_out_bf, ple_norm_g, w_gate_bf, w_proj_bf, *, tm, name):
    b, t, _ = x.shape
    assert t % tm == 0
    tok = lambda w: pl.BlockSpec((1, tm, w), lambda bi, i: (bi, i, 0))
    const2 = lambda a: pl.BlockSpec(a.shape, lambda bi, i: (0, 0))
    pg = ple_norm_g.reshape(1, D_MODEL)
    block_bytes = (2 * _nbytes((tm, D_MODEL), F32) + 2 * _nbytes((tm, SEG), BF16)
                   + _nbytes((tm, PLE_DIM), F32) + _nbytes(w_out_bf.shape, BF16)
                   + _nbytes(w_gate_bf.shape, BF16) + _nbytes(w_proj_bf.shape, BF16))
    temp_bytes = 6 * _nbytes((tm, D_MODEL), F32)
    return pl.pallas_call(
        _outproj_kernel,
        grid=(b, t // tm),
        in_specs=[tok(D_MODEL), tok(SEG), tok(SEG), tok(PLE_DIM), const2(w_out_bf), const2(pg),
                  const2(w_gate_bf), const2(w_proj_bf)],
        out_specs=tok(D_MODEL),
        out_shape=jax.ShapeDtypeStruct((b, t, D_MODEL), F32),
        compiler_params=pltpu.CompilerParams(
            dimension_semantics=("arbitrary", "arbitrary"),
            vmem_limit_bytes=_vmem_limit(block_bytes, temp_bytes)),
        name=name,
    )(x, osb, osgu, p, w_out_bf, pg, w_gate_bf, w_proj_bf)


def _sgu_bias_rows(sgu_b_l, period):
    per_pos = jnp.tile(sgu_b_l[:, :period].T, (SGU_CHUNK // period, 1))
    return jnp.repeat(per_pos, GROUP_W, axis=1)


def kernel(x_prompt, x_sample, cache_k, cache_v, p_prompt, p_sample, norm_g, w_in, q_norm_g,
           k_norm_g, sgu_norm_g, sgu_w, sgu_b, w_out, ple_norm_g, w_ple_gate, w_ple_proj):
    depth = w_in.shape[0]
    assert depth == 1, "one layer per call"
    l = 0
    b_p, t_p, _ = x_prompt.shape
    b_s, t_s, _ = x_sample.shape
    past = cache_k.shape[2]
    n_s = b_s * t_s
    assert n_s == SGU_CHUNK and SGU_CHUNK % t_s == 0

    w_in_bf = w_in[l].astype(BF16)
    w_out_bf = w_out[l].astype(BF16)
    w_gate_bf = w_ple_gate[l].astype(BF16)
    w_proj_bf = w_ple_proj[l].astype(BF16)

    q, k, v, kt, vb, gsb, osgu = _inproj(
        x_prompt, norm_g[l], w_in_bf, q_norm_g[l], k_norm_g[l], sgu_norm_g[l],
        sgu_w[l], _sgu_bias_rows(sgu_b[l], SGU_CHUNK),
        tm=TM_IN, period=SGU_CHUNK, attn_layout=True)
    osb = _attn_prompt(q, kt, vb, gsb, tq=TQ, hps=HEADS_PER_STEP)
    y_prompt = _outproj(x_prompt, osb, osgu, p_prompt[l], w_out_bf, ple_norm_g[l], w_gate_bf,
                        w_proj_bf, tm=TM_OUT, name="outproj_prompt")

    rep = SGU_CHUNK // t_s
    sgu_w_s = jnp.tile(sgu_w[l][:, :t_s, :t_s], (1, rep, rep))
    xs = x_sample.reshape(1, n_s, D_MODEL)
    q_s, k_s, v_s, gsb_s, osgu_s, vs_s = _inproj(
        xs, norm_g[l], w_in_bf, q_norm_g[l], k_norm_g[l], sgu_norm_g[l],
        sgu_w_s, _sgu_bias_rows(sgu_b[l], t_s),
        tm=n_s, period=t_s, attn_layout=False)
    shp = (b_s, t_s, SB_WIDTH)
    cache_shape = (b_s, past, HEADS, HEAD_DIM)
    osb_s = _attn_sample(q_s.reshape(shp), cache_k.reshape(cache_shape),
                         cache_v.reshape(cache_shape), k_s.reshape(shp), v_s.reshape(shp),
                         gsb_s.reshape(shp))
    y_sample = _outproj(xs, osb_s.reshape(1, n_s, SB_WIDTH), osgu_s, p_sample[l].reshape(1, n_s, PLE_DIM),
                        w_out_bf, ple_norm_g[l], w_gate_bf, w_proj_bf, tm=n_s,
                        name="outproj_sample").reshape(b_s, t_s, D_MODEL)

    head_shape = lambda a, bb, tt: a.reshape(1, bb, tt, HEADS, HEAD_DIM)
    return (y_prompt, y_sample,
            head_shape(k, b_p, t_p), head_shape(v, b_p, t_p),
            head_shape(k_s, b_s, t_s), head_shape(v_s, b_s, t_s),
            vs_s.reshape(1, b_s, t_s, GROUPS, GROUP_W))
```

```python
import functools
import math

import numpy as np
import jax
import jax.numpy as jnp
from jax import lax
from jax.experimental import pallas as pl
from jax.experimental.pallas import tpu as pltpu

F32 = jnp.float32
BF16 = jnp.bfloat16

LANES = 128
MXU_TILE = 256
VMEM_BYTES_V7X = 64 * 1024 * 1024

D_MODEL = 1024
PLE_DIM = 256
HEADS = 8
HEAD_DIM = 64
SB_WIDTH = HEADS * HEAD_DIM
GROUPS = 4
GROUP_W = 128
SGU_WIDTH = GROUPS * GROUP_W
SGU_CHUNK = 128
SEG = 512
N_SEG = 7
EPS = 1e-6

Q_SCALE = HEAD_DIM ** -0.5 * math.log2(math.e)

KSUB = LANES
NSUB = 2
TQ = NSUB * KSUB
HEADS_PER_STEP = 8
REM_DEAD_LOG2 = -152.0
SOFTPLUS_LINEAR = 64.0
EAGER_CACHE_SUBS = 2
TM_IN = 1024
TM_OUT = 1024


def _vmem_limit(block_bytes, temp_bytes):
    need = 2 * block_bytes + temp_bytes
    return int(min(need, VMEM_BYTES_V7X - 8 * 1024 * 1024))


def _nbytes(shape, dtype):
    return int(np.prod(shape)) * jnp.dtype(dtype).itemsize


def _suffix_sum_matrix():
    j = np.arange(KSUB)[:, None]
    s = np.arange(KSUB)[None, :]
    one = np.concatenate([np.where(j >= s, -1.0, 0.0), -np.ones((KSUB, LANES))], axis=1)
    return jnp.asarray(np.concatenate([one, one], axis=0), dtype=BF16)


def _head_mean_matrix():
    a = np.arange(MXU_TILE)
    bd = np.where(a[:, None] // HEAD_DIM == a[None, :] // HEAD_DIM, 1.0 / HEAD_DIM, 0.0)
    return jnp.asarray(np.concatenate([bd, bd], axis=0), dtype=BF16)


def _split_bf16(x):
    hi = x.astype(BF16)
    lo = (x - hi.astype(F32)).astype(BF16)
    return jnp.concatenate([hi, lo], axis=1)


def _gelu_tanh(x):
    return 0.5 * x * (1.0 + jnp.tanh(math.sqrt(2.0 / math.pi) * (x + 0.044715 * (x * x * x))))


def _silu(x):
    return x / (1.0 + jnp.exp(-x))


def _inproj_kernel(x_ref, ng_ref, w_ref, qg_ref, kg_ref, sg_ref, hm_ref, sw_ref, sb_ref,
                   *refs, tm, period, attn_layout):
    if attn_layout:
        (q_ref, k_hbm, v_hbm, kt_ref, vb_ref, gsb_ref, osgu_ref, kstage, vstage, sem) = refs
    else:
        q_ref, k_ref, v_ref, gsb_ref, osgu_ref, vs_ref = refs

    step = pl.program_id(0) * pl.num_programs(1) + pl.program_id(1)
    last_step = pl.num_programs(0) * pl.num_programs(1) - 1
    slot = step % 2

    def head_copies(which, slot_):
        stage, hbm = ((kstage, k_hbm), (vstage, v_hbm))[which]
        rows = pl.ds(pl.multiple_of(pl.program_id(1) * tm, tm), tm)
        return [pltpu.make_async_copy(stage.at[slot_, h], hbm.at[pl.program_id(0), rows, h],
                                      sem.at[which, slot_]) for h in range(HEADS)]

    def emit_heads(which, value):
        stage = (kstage, vstage)[which]
        for h in range(HEADS):
            stage[slot, h] = value[:, h * HEAD_DIM:(h + 1) * HEAD_DIM]
        for c in head_copies(which, slot):
            c.start()

    if attn_layout:
        @pl.when(step >= 2)
        def _():
            for which in range(2):
                for c in head_copies(which, slot):
                    c.wait()

    x = x_ref[0]
    ms = jnp.mean(x * x, axis=-1, keepdims=True)
    xn = (x * lax.rsqrt(ms + EPS) * ng_ref[...]).astype(BF16)

    def seg(i):
        return jnp.dot(xn, w_ref[:, i * SEG:(i + 1) * SEG], preferred_element_type=F32)

    def head_rms(t, g):
        sq = t * t
        ms_h = jnp.concatenate(
            [jnp.dot(_split_bf16(sq[:, c:c + MXU_TILE]), hm_ref[...], preferred_element_type=F32)
             for c in range(0, SB_WIDTH, MXU_TILE)], axis=1)
        return t * lax.rsqrt(ms_h + EPS) * g

    q = head_rms(seg(0), qg_ref[...])
    q_ref[0] = (q * Q_SCALE).astype(BF16)

    k = head_rms(seg(1), kg_ref[...])
    v = seg(2)
    if attn_layout:
        emit_heads(0, k)
        emit_heads(1, v)
        kt_ref[0] = k.T.astype(BF16)
        vb_ref[0] = v.astype(BF16)
    else:
        k_ref[0] = k
        v_ref[0] = v

    gsb_ref[0] = _silu(seg(3)).astype(BF16)

    u = _gelu_tanh(seg(4))
    vs_raw = _gelu_tanh(seg(5))
    vs_groups = []
    for g in range(GROUPS):
        t = vs_raw[:, g * GROUP_W:(g + 1) * GROUP_W]
        ms_g = jnp.mean(t * t, axis=-1, keepdims=True)
        vs_groups.append(t * lax.rsqrt(ms_g + EPS) * sg_ref[:, g * GROUP_W:(g + 1) * GROUP_W])
    if not attn_layout:
        vs_ref[0] = jnp.concatenate(vs_groups, axis=1)

    row = lax.broadcasted_iota(jnp.int32, (SGU_CHUNK, SGU_CHUNK), 0)
    col = lax.broadcasted_iota(jnp.int32, (SGU_CHUNK, SGU_CHUNK), 1)
    keep = row >= col
    if period != SGU_CHUNK:
        keep = keep & ((row // period) == (col // period))
    s_groups = []
    for g in range(GROUPS):
        wm = jnp.where(keep, sw_ref[g], 0.0).astype(BF16)
        vg = vs_groups[g].astype(BF16)
        chunks = [jnp.dot(wm, vg[c * SGU_CHUNK:(c + 1) * SGU_CHUNK], preferred_element_type=F32)
                  + sb_ref[:, g * GROUP_W:(g + 1) * GROUP_W]
                  for c in range(tm // SGU_CHUNK)]
        s_groups.append(jnp.concatenate(chunks, axis=0) if len(chunks) > 1 else chunks[0])
    s = jnp.concatenate(s_groups, axis=1)

    osgu_ref[0] = (u * s * _silu(seg(6))).astype(BF16)

    if attn_layout:
        @pl.when(step == last_step)
        def _():
            for which in range(2):
                for c in head_copies(which, slot):
                    c.wait()

        @pl.when(jnp.logical_and(step == last_step, step >= 1))
        def _():
            for which in range(2):
                for c in head_copies(which, 1 - slot):
                    c.wait()


def _inproj(x, norm_g, w_in_bf, q_norm_g, k_norm_g, sgu_norm_g, sgu_w_tiled, sgu_bias, *,
            tm, period, attn_layout):
    b, t, _ = x.shape
    assert t % tm == 0 and tm % SGU_CHUNK == 0
    grid = (b, t // tm)
    tok = lambda w: pl.BlockSpec((1, tm, w), lambda bi, i: (bi, i, 0))
    const2 = lambda a: pl.BlockSpec(a.shape, lambda bi, i: (0, 0))
    const3 = lambda a: pl.BlockSpec(a.shape, lambda bi, i: (0, 0, 0))

    ng = norm_g.reshape(1, D_MODEL)
    qg = jnp.tile(q_norm_g, HEADS).reshape(1, SB_WIDTH)
    kg = jnp.tile(k_norm_g, HEADS).reshape(1, SB_WIDTH)
    sg = sgu_norm_g.reshape(1, SGU_WIDTH)
    hm = _head_mean_matrix()

    act = lambda dt: jax.ShapeDtypeStruct((b, t, SEG), dt)
    scratch = []
    if attn_layout:
        heads = jax.ShapeDtypeStruct((b, t, HEADS, HEAD_DIM), F32)
        in_hbm = pl.BlockSpec(memory_space=pl.ANY)
        out_shape = (act(BF16), heads, heads,
                     jax.ShapeDtypeStruct((b, SB_WIDTH, t), BF16), act(BF16), act(BF16), act(BF16))
        out_specs = (tok(SEG), in_hbm, in_hbm,
                     pl.BlockSpec((1, SB_WIDTH, tm), lambda bi, i: (bi, 0, i)),
                     tok(SEG), tok(SEG), tok(SEG))
        stage = pltpu.VMEM((2, HEADS, tm, HEAD_DIM), F32)
        scratch = [stage, stage, pltpu.SemaphoreType.DMA((2, 2))]
    else:
        out_shape = (act(BF16), act(F32), act(F32), act(BF16), act(BF16), act(F32))
        out_specs = (tok(SEG),) * 6

    block_bytes = (_nbytes((tm, D_MODEL), F32) + _nbytes(w_in_bf.shape, BF16)
                   + _nbytes(hm.shape, BF16) + _nbytes(sgu_w_tiled.shape, F32)
                   + _nbytes(sgu_bias.shape, F32) + 7 * _nbytes((tm, SEG), F32))
    temp_bytes = 12 * _nbytes((tm, SEG), F32)
    if attn_layout:
        temp_bytes += 2 * _nbytes((2, HEADS, tm, LANES), F32)
    return pl.pallas_call(
        functools.partial(_inproj_kernel, tm=tm, period=period, attn_layout=attn_layout),
        grid=grid,
        in_specs=[tok(D_MODEL), const2(ng), const2(w_in_bf), const2(qg), const2(kg), const2(sg),
                  const2(hm), const3(sgu_w_tiled), const2(sgu_bias)],
        out_specs=out_specs,
        out_shape=out_shape,
        scratch_shapes=scratch,
        compiler_params=pltpu.CompilerParams(
            dimension_semantics=("arbitrary", "arbitrary"),
            vmem_limit_bytes=_vmem_limit(block_bytes, temp_bytes)),
        name="inproj_prompt" if attn_layout else "inproj_sample",
    )(x, ng, w_in_bf, qg, kg, sg, hm, sgu_w_tiled, sgu_bias)


def _neg_log2_keep(z, vis):
    sp = jnp.maximum(z, jnp.log2(1.0 + jnp.exp2(jnp.minimum(z, SOFTPLUS_LINEAR))))
    return sp if vis is None else jnp.where(vis, sp, 0.0)


def _sb_weights(z, vis, sums, rem):
    w = jnp.exp2(z + sums[:, :KSUB] + rem)
    if vis is not None:
        w = jnp.where(vis, w, 0.0)
    return w.astype(BF16), rem + sums[:, KSUB:]


def _attn_prompt_kernel(q_ref, kt_ref, v_ref, g_ref, uu_ref, o_ref, rem_ref, acc_ref, *, tq, hps):
    qi = pl.program_id(2)
    q = q_ref[0]
    qh = [q[:, h * HEAD_DIM:(h + 1) * HEAD_DIM] for h in range(hps)]
    uu = uu_ref[...]
    lane = lax.broadcasted_iota(jnp.int32, (KSUB, LANES), 1)
    first_head = lane < HEAD_DIM

    def pair_values(ks, p):
        vv = v_ref[0, pl.ds(ks, KSUB), p * LANES:(p + 1) * LANES]
        zero = jnp.zeros_like(vv)
        return jnp.concatenate([jnp.where(first_head, vv, zero),
                                jnp.where(first_head, zero, vv)], axis=0)

    def sweep(tiles, rems, accs):
        def put(full, r0, r1, part):
            pieces = ([full[:r0]] if r0 else []) + [part] + ([full[r1:]] if r1 < tq else [])
            return jnp.concatenate(pieces, axis=0) if len(pieces) > 1 else part

        zs, splits = [], []
        for ks, r0, r1, diagonal in tiles:
            kt = kt_ref[0, :, pl.ds(ks, KSUB)]
            vis = None
            if diagonal:
                row = lax.broadcasted_iota(jnp.int32, (r1 - r0, KSUB), 0)
                col = lax.broadcasted_iota(jnp.int32, (r1 - r0, KSUB), 1)
                vis = col < row
            for h in range(hps):
                z = jnp.dot(qh[h][r0:r1], kt[h * HEAD_DIM:(h + 1) * HEAD_DIM],
                            preferred_element_type=F32)
                zs.append((z, vis))
                splits.append(_split_bf16(_neg_log2_keep(z, vis)))
        sums = jnp.dot(jnp.concatenate(splits, axis=0), uu, preferred_element_type=F32)
        rems, accs = list(rems), list(accs)
        off, i = 0, 0
        for ks, r0, r1, _ in tiles:
            ws = []
            for h in range(hps):
                z, vis = zs[i]
                i += 1
                w, new = _sb_weights(z, vis, sums[off:off + r1 - r0], rems[h][r0:r1])
                off += r1 - r0
                rems[h] = put(rems[h], r0, r1, new)
                ws.append(w)
            for p in range(hps // 2):
                d = jnp.dot(jnp.concatenate(ws[2 * p:2 * p + 2], axis=1), pair_values(ks, p),
                            preferred_element_type=F32)
                accs[p] = put(accs[p], r0, r1, accs[p][r0:r1] + d)
        return rems, accs

    def diagonal_tiles():
        base = pl.multiple_of(qi * tq, tq)
        return [(pl.multiple_of(base + s * KSUB, KSUB), s * KSUB, tq, True)
                for s in reversed(range(NSUB))]

    def key_block_tiles(n, rows=None):
        kb = pl.multiple_of((qi - n) * tq, tq)
        rows = rows or {s: (0, tq) for s in range(NSUB)}
        return [(pl.multiple_of(kb + s * KSUB, KSUB), *rows[s], False)
                for s in reversed(range(NSUB)) if s in rows]

    zero_rems = [jnp.zeros((tq, LANES), F32)] * hps
    zero_accs = [jnp.zeros((tq, LANES), F32)] * (hps // 2)

    def finish(accs):
        acc = jnp.concatenate(accs, axis=1)
        o_ref[0] = (acc * g_ref[0].astype(F32)).astype(BF16)

    def any_alive(rems):
        most = functools.reduce(jnp.maximum, rems)
        return (jnp.max(most) > REM_DEAD_LOG2).astype(jnp.int32)

    @pl.when(qi == 0)
    def _():
        finish(sweep(diagonal_tiles(), zero_rems, zero_accs)[1])

    def load_state():
        return [rem_ref[h] for h in range(hps)], [acc_ref[p] for p in range(hps // 2)]

    def store_state(rems, accs):
        for h in range(hps):
            rem_ref[h] = rems[h]
        for p in range(hps // 2):
            acc_ref[p] = accs[p]

    eager_rows = {s: min(tq, (s + 3 - NSUB) * KSUB) for s in range(NSUB) if s + 3 - NSUB > 0}

    @pl.when(qi > 0)
    def _():
        first = key_block_tiles(1, {s: (0, e) for s, e in eager_rows.items()})
        rems, accs = sweep(diagonal_tiles() + first, zero_rems, zero_accs)
        store_state(rems, accs)
        alive = any_alive(rems)
        for s in reversed(range(NSUB)):
            e = eager_rows.get(s, 0)
            if e < tq:
                @pl.when(any_alive([r[e:] for r in rems]) > 0)
                def _():
                    store_state(*sweep(key_block_tiles(1, {s: (e, tq)}), *load_state()))

        def cond(c):
            n, alive = c
            return jnp.logical_and(n <= qi, alive > 0)

        def body(c):
            n, _ = c
            rems, accs = sweep(key_block_tiles(n), *load_state())
            store_state(rems, accs)
            return n + 1, any_alive(rems)

        lax.while_loop(cond, body, (jnp.int32(2), alive))
        finish(load_state()[1])


def _attn_prompt(q, kt, vb, gsb, *, tq, hps):
    b, t, _ = q.shape
    assert t % tq == 0 and tq == NSUB * KSUB and HEADS % hps == 0 and hps % 2 == 0
    width = hps * HEAD_DIM
    uu = _suffix_sum_matrix()
    grid = (b, HEADS // hps, t // tq)
    row_blk = pl.BlockSpec((1, tq, width), lambda bi, hg, i: (bi, i, hg))
    block_bytes = (3 * _nbytes((tq, width), BF16) + 2 * _nbytes((width, t), BF16)
                   + _nbytes(uu.shape, BF16))
    temp_bytes = (hps + hps // 2) * _nbytes((tq, LANES), F32) + 8 * NSUB * hps * _nbytes((tq, KSUB), F32)
    return pl.pallas_call(
        functools.partial(_attn_prompt_kernel, tq=tq, hps=hps),
        grid=grid,
        in_specs=[row_blk,
                  pl.BlockSpec((1, width, t), lambda bi, hg, i: (bi, hg, 0)),
                  pl.BlockSpec((1, t, width), lambda bi, hg, i: (bi, 0, hg)),
                  row_blk,
                  pl.BlockSpec(uu.shape, lambda bi, hg, i: (0, 0))],
        out_specs=row_blk,
        out_shape=jax.ShapeDtypeStruct((b, t, SB_WIDTH), BF16),
        scratch_shapes=[pltpu.VMEM((hps, tq, LANES), F32), pltpu.VMEM((hps // 2, tq, LANES), F32)],
        compiler_params=pltpu.CompilerParams(
            dimension_semantics=("arbitrary", "arbitrary", "arbitrary"),
            vmem_limit_bytes=_vmem_limit(block_bytes, temp_bytes)),
        name="sb_attn_prompt",
    )(q, kt, vb, gsb, uu)


def _attn_sample_kernel(q_ref, kn_ref, vn_ref, g_ref, uu_ref, ck_hbm, cv_hbm, o_ref,
                        kwin, vwin, sem, rem_s, acc_s, *, t_new, past, nb):
    rows = HEADS * t_new
    uu = uu_ref[...]
    n_sub = past // KSUB

    def fetch(which, b, j, slot):
        src, dst = ((ck_hbm, kwin), (cv_hbm, vwin))[which]
        first = pl.multiple_of(j * KSUB, KSUB)
        return [pltpu.make_async_copy(src.at[b, pl.ds(first, KSUB), h], dst.at[b, slot, h],
                                      sem.at[which, b, slot]) for h in range(HEADS)]

    def start(copies):
        for c in copies:
            c.start()

    def wait(copies):
        for c in copies:
            c.wait()

    for b in range(nb):
        for slot in range(EAGER_CACHE_SUBS):
            for which in range(2):
                start(fetch(which, b, n_sub - 1 - slot, slot))

    row = lax.broadcasted_iota(jnp.int32, (rows, KSUB), 0)
    col = lax.broadcasted_iota(jnp.int32, (rows, KSUB), 1)
    new_vis = col < (row % t_new)
    pad = jnp.zeros((KSUB - t_new, SB_WIDTH), F32)

    def new_block(ref, b):
        blk = jnp.concatenate([ref[b], pad], axis=0).astype(BF16)
        return [blk[:, h * HEAD_DIM:(h + 1) * HEAD_DIM] for h in range(HEADS)]

    def cached(win, b, slot):
        return [win[b, slot, h].astype(BF16) for h in range(HEADS)]

    def sweep(b, tiles, rem, acc):
        q = q_ref[b]
        zs = [jnp.concatenate(
            [lax.dot_general(q[:, h * HEAD_DIM:(h + 1) * HEAD_DIM], keys[h],
                             (((1,), (1,)), ((), ())), preferred_element_type=F32)
             for h in range(HEADS)], axis=0) for keys, _, _ in tiles]
        splits = [_split_bf16(_neg_log2_keep(z, vis)) for z, (_, _, vis) in zip(zs, tiles)]
        sums = jnp.dot(jnp.concatenate(splits, axis=0), uu, preferred_element_type=F32)
        for i, (z, (_, vals, vis)) in enumerate(zip(zs, tiles)):
            w, rem = _sb_weights(z, vis, sums[i * rows:(i + 1) * rows], rem)
            acc = [a + jnp.dot(w[h * t_new:(h + 1) * t_new], vals[h], preferred_element_type=F32)
                   for h, a in enumerate(acc)]
        return rem, acc

    def alive_of(rem):
        return (jnp.max(rem) > REM_DEAD_LOG2).astype(jnp.int32)

    alive = jnp.int32(0)
    for b in range(nb):
        for slot in range(EAGER_CACHE_SUBS):
            for which in range(2):
                wait(fetch(which, b, n_sub - 1 - slot, slot))
        tiles = [(new_block(kn_ref, b), new_block(vn_ref, b), new_vis)]
        tiles += [(cached(kwin, b, s), cached(vwin, b, s), None) for s in range(EAGER_CACHE_SUBS)]
        rem, acc = sweep(b, tiles, jnp.zeros((rows, LANES), F32),
                         [jnp.zeros((t_new, HEAD_DIM), F32)] * HEADS)
        rem_s[b] = rem
        acc_s[b] = jnp.concatenate(acc, axis=1)
        alive = jnp.maximum(alive, alive_of(rem))

    def cond(c):
        j, alive = c
        return jnp.logical_and(j >= 0, alive > 0)

    def body(c):
        j, _ = c
        for b in range(nb):
            for which in range(2):
                start(fetch(which, b, j, 0))
        alive = jnp.int32(0)
        for b in range(nb):
            for which in range(2):
                wait(fetch(which, b, j, 0))
            acc_b = acc_s[b]
            rem, acc = sweep(b, [(cached(kwin, b, 0), cached(vwin, b, 0), None)], rem_s[b],
                             [acc_b[:, h * HEAD_DIM:(h + 1) * HEAD_DIM] for h in range(HEADS)])
            rem_s[b] = rem
            acc_s[b] = jnp.concatenate(acc, axis=1)
            alive = jnp.maximum(alive, alive_of(rem))
        return j - 1, alive

    lax.while_loop(cond, body, (jnp.int32(n_sub - 1 - EAGER_CACHE_SUBS), alive))
    for b in range(nb):
        o_ref[b] = (acc_s[b] * g_ref[b].astype(F32)).astype(BF16)


def _attn_sample(q, cache_k, cache_v, k_new, v_new, gsb):
    nb, t_new, _ = q.shape
    past = cache_k.shape[1]
    assert past % KSUB == 0 and past // KSUB >= EAGER_CACHE_SUBS and t_new <= KSUB
    uu = _suffix_sum_matrix()
    rows = HEADS * t_new
    whole = lambda a: pl.BlockSpec(a.shape, lambda i: (0,) * a.ndim)
    window = (nb, EAGER_CACHE_SUBS, HEADS, KSUB, HEAD_DIM)
    padded_window = _nbytes(window[:-1] + (LANES,), F32)
    block_bytes = sum(_nbytes(a.shape, a.dtype) for a in (q, k_new, v_new, gsb, uu, q))
    temp_bytes = (2 * padded_window + _nbytes((nb, rows, LANES), F32)
                  + _nbytes((nb, t_new, SB_WIDTH), F32)
                  + 16 * (EAGER_CACHE_SUBS + 1) * _nbytes((rows, KSUB), F32))
    return pl.pallas_call(
        functools.partial(_attn_sample_kernel, t_new=t_new, past=past, nb=nb),
        grid=(1,),
        in_specs=[whole(q), whole(k_new), whole(v_new), whole(gsb), whole(uu),
                  pl.BlockSpec(memory_space=pl.ANY), pl.BlockSpec(memory_space=pl.ANY)],
        out_specs=whole(q),
        out_shape=jax.ShapeDtypeStruct((nb, t_new, SB_WIDTH), BF16),
        scratch_shapes=[pltpu.VMEM(window, F32), pltpu.VMEM(window, F32),
                        pltpu.SemaphoreType.DMA((2, nb, EAGER_CACHE_SUBS)),
                        pltpu.VMEM((nb, rows, LANES), F32),
                        pltpu.VMEM((nb, t_new, SB_WIDTH), F32)],
        compiler_params=pltpu.CompilerParams(
            dimension_semantics=("arbitrary",),
            vmem_limit_bytes=_vmem_limit(block_bytes, temp_bytes)),
        name="sb_attn_sample",
    )(q, k_new, v_new, gsb, uu, cache_k, cache_v)


def _outproj_kernel(x_ref, osb_ref, osgu_ref, p_ref, wo_ref, pg_ref, wg_ref, wp_ref, y_ref):
    h = (x_ref[0]
         + jnp.dot(osb_ref[0], wo_ref[:SB_WIDTH], preferred_element_type=F32)
         + jnp.dot(osgu_ref[0], wo_ref[SB_WIDTH:], preferred_element_type=F32))
    ms = jnp.mean(h * h, axis=-1, keepdims=True)
    hn = (h * lax.rsqrt(ms + EPS) * pg_ref[...]).astype(BF16)
    gate_logit = jnp.dot(hn, wg_ref[...], preferred_element_type=F32)
    gate = 1.0 / (1.0 + jnp.exp(-gate_logit))
    pp = jnp.dot(p_ref[0].astype(BF16), wp_ref[...], preferred_element_type=F32)
    y_ref[0] = h + gate * pp


def _outproj(x, osb, osgu, p, w_out_bf, ple_norm_g, w_gate_bf, w_proj_bf, *, tm, name):
    b, t, _ = x.shape
    assert t % tm == 0
    tok = lambda w: pl.BlockSpec((1, tm, w), lambda bi, i: (bi, i, 0))
    const2 = lambda a: pl.BlockSpec(a.shape, lambda bi, i: (0, 0))
    pg = ple_norm_g.reshape(1, D_MODEL)
    block_bytes = (2 * _nbytes((tm, D_MODEL), F32) + 2 * _nbytes((tm, SEG), BF16)
                   + _nbytes((tm, PLE_DIM), F32) + _nbytes(w_out_bf.shape, BF16)
                   + _nbytes(w_gate_bf.shape, BF16) + _nbytes(w_proj_bf.shape, BF16))
    temp_bytes = 6 * _nbytes((tm, D_MODEL), F32)
    return pl.pallas_call(
        _outproj_kernel,
        grid=(b, t // tm),
        in_specs=[tok(D_MODEL), tok(SEG), tok(SEG), tok(PLE_DIM), const2(w_out_bf), const2(pg),
                  const2(w_gate_bf), const2(w_proj_bf)],
        out_specs=tok(D_MODEL),
        out_shape=jax.ShapeDtypeStruct((b, t, D_MODEL), F32),
        compiler_params=pltpu.CompilerParams(
            dimension_semantics=("arbitrary", "arbitrary"),
            vmem_limit_bytes=_vmem_limit(block_bytes, temp_bytes)),
        name=name,
    )(x, osb, osgu, p, w_out_bf, pg, w_gate_bf, w_proj_bf)


def _sgu_bias_rows(sgu_b_l, period):
    per_pos = jnp.tile(sgu_b_l[:, :period].T, (SGU_CHUNK // period, 1))
    return jnp.repeat(per_pos, GROUP_W, axis=1)


def kernel(x_prompt, x_sample, cache_k, cache_v, p_prompt, p_sample, norm_g, w_in, q_norm_g,
           k_norm_g, sgu_norm_g, sgu_w, sgu_b, w_out, ple_norm_g, w_ple_gate, w_ple_proj):
    depth = w_in.shape[0]
    assert depth == 1, "one layer per call"
    l = 0
    b_p, t_p, _ = x_prompt.shape
    b_s, t_s, _ = x_sample.shape
    past = cache_k.shape[2]
    n_s = b_s * t_s
    assert n_s == SGU_CHUNK and SGU_CHUNK % t_s == 0

    w_in_bf = w_in[l].astype(BF16)
    w_out_bf = w_out[l].astype(BF16)
    w_gate_bf = w_ple_gate[l].astype(BF16)
    w_proj_bf = w_ple_proj[l].astype(BF16)

    q, k, v, kt, vb, gsb, osgu = _inproj(
        x_prompt, norm_g[l], w_in_bf, q_norm_g[l], k_norm_g[l], sgu_norm_g[l],
        sgu_w[l], _sgu_bias_rows(sgu_b[l], SGU_CHUNK),
        tm=TM_IN, period=SGU_CHUNK, attn_layout=True)
    osb = _attn_prompt(q, kt, vb, gsb, tq=TQ, hps=HEADS_PER_STEP)
    y_prompt = _outproj(x_prompt, osb, osgu, p_prompt[l], w_out_bf, ple_norm_g[l], w_gate_bf,
                        w_proj_bf, tm=TM_OUT, name="outproj_prompt")

    rep = SGU_CHUNK // t_s
    sgu_w_s = jnp.tile(sgu_w[l][:, :t_s, :t_s], (1, rep, rep))
    xs = x_sample.reshape(1, n_s, D_MODEL)
    q_s, k_s, v_s, gsb_s, osgu_s, vs_s = _inproj(
        xs, norm_g[l], w_in_bf, q_norm_g[l], k_norm_g[l], sgu_norm_g[l],
        sgu_w_s, _sgu_bias_rows(sgu_b[l], t_s),
        tm=n_s, period=t_s, attn_layout=False)
    shp = (b_s, t_s, SB_WIDTH)
    cache_shape = (b_s, past, HEADS, HEAD_DIM)
    osb_s = _attn_sample(q_s.reshape(shp), cache_k.reshape(cache_shape),
                         cache_v.reshape(cache_shape), k_s.reshape(shp), v_s.reshape(shp),
                         gsb_s.reshape(shp))
    y_sample = _outproj(xs, osb_s.reshape(1, n_s, SB_WIDTH), osgu_s, p_sample[l].reshape(1, n_s, PLE_DIM),
                        w_out_bf, ple_norm_g[l], w_gate_bf, w_proj_bf, tm=n_s,
                        name="outproj_sample").reshape(b_s, t_s, D_MODEL)

    head_shape = lambda a, bb, tt: a.reshape(1, bb, tt, HEADS, HEAD_DIM)
    return (y_prompt, y_sample,
            head_shape(k, b_p, t_p), head_shape(v, b_p, t_p),
            head_shape(k_s, b_s, t_s), head_shape(v_s, b_s, t_s),
            vs_s.reshape(1, b_s, t_s, GROUPS, GROUP_W))
```

```python
import functools
import math

import numpy as np
import jax
import jax.numpy as jnp
from jax import lax
from jax.experimental import pallas as pl
from jax.experimental.pallas import tpu as pltpu

F32 = jnp.float32
BF16 = jnp.bfloat16

LANES = 128
MXU_TILE = 256
VMEM_BYTES_V7X = 64 * 1024 * 1024

D_MODEL = 1024
PLE_DIM = 256
HEADS = 8
HEAD_DIM = 64
SB_WIDTH = HEADS * HEAD_DIM
GROUPS = 4
GROUP_W = 128
SGU_WIDTH = GROUPS * GROUP_W
SGU_CHUNK = 128
SEG = 512
N_SEG = 7
EPS = 1e-6

Q_SCALE = HEAD_DIM ** -0.5 * math.log2(math.e)

KSUB = LANES
NSUB = 2
TQ = NSUB * KSUB
HEADS_PER_STEP = 8
REM_DEAD_LOG2 = -152.0
SOFTPLUS_LINEAR = 64.0
EAGER_CACHE_SUBS = 2
TM_IN = 1024
TM_OUT = 1024


def _vmem_limit(block_bytes, temp_bytes):
    need = 2 * block_bytes + temp_bytes
    return int(min(need, VMEM_BYTES_V7X - 8 * 1024 * 1024))


def _nbytes(shape, dtype):
    return int(np.prod(shape)) * jnp.dtype(dtype).itemsize


def _suffix_sum_matrix():
    j = np.arange(KSUB)[:, None]
    s = np.arange(KSUB)[None, :]
    one = np.concatenate([np.where(j >= s, -1.0, 0.0), -np.ones((KSUB, LANES))], axis=1)
    return jnp.asarray(np.concatenate([one, one], axis=0), dtype=BF16)


def _head_mean_matrix():
    a = np.arange(MXU_TILE)
    bd = np.where(a[:, None] // HEAD_DIM == a[None, :] // HEAD_DIM, 1.0 / HEAD_DIM, 0.0)
    return jnp.asarray(np.concatenate([bd, bd], axis=0), dtype=BF16)


def _split_bf16(x):
    hi = x.astype(BF16)
    lo = (x - hi.astype(F32)).astype(BF16)
    return jnp.concatenate([hi, lo], axis=1)


def _gelu_tanh(x):
    return 0.5 * x * (1.0 + jnp.tanh(math.sqrt(2.0 / math.pi) * (x + 0.044715 * (x * x * x))))


def _silu(x):
    return x / (1.0 + jnp.exp(-x))


def _inproj_kernel(x_ref, ng_ref, w_ref, qg_ref, kg_ref, sg_ref, hm_ref, sw_ref, sb_ref,
                   *out_refs, tm, period, attn_layout):
    if attn_layout:
        q_ref, ktf_ref, vtf_ref, kt_ref, vb_ref, gsb_ref, osgu_ref = out_refs
    else:
        q_ref, k_ref, v_ref, gsb_ref, osgu_ref, vs_ref = out_refs

    x = x_ref[0]
    ms = jnp.mean(x * x, axis=-1, keepdims=True)
    xn = (x * lax.rsqrt(ms + EPS) * ng_ref[...]).astype(BF16)

    def seg(i):
        return jnp.dot(xn, w_ref[:, i * SEG:(i + 1) * SEG], preferred_element_type=F32)

    def head_rms(t, g):
        sq = t * t
        ms_h = jnp.concatenate(
            [jnp.dot(_split_bf16(sq[:, c:c + MXU_TILE]), hm_ref[...], preferred_element_type=F32)
             for c in range(0, SB_WIDTH, MXU_TILE)], axis=1)
        return t * lax.rsqrt(ms_h + EPS) * g

    q = head_rms(seg(0), qg_ref[...])
    q_ref[0] = (q * Q_SCALE).astype(BF16)

    k = head_rms(seg(1), kg_ref[...])
    v = seg(2)
    if attn_layout:
        k_t = k.T
        ktf_ref[0] = k_t
        kt_ref[0] = k_t.astype(BF16)
        vtf_ref[0] = v.T
        vb_ref[0] = v.astype(BF16)
    else:
        k_ref[0] = k
        v_ref[0] = v

    gsb_ref[0] = _silu(seg(3)).astype(BF16)

    u = _gelu_tanh(seg(4))
    vs_raw = _gelu_tanh(seg(5))
    vs_groups = []
    for g in range(GROUPS):
        t = vs_raw[:, g * GROUP_W:(g + 1) * GROUP_W]
        ms_g = jnp.mean(t * t, axis=-1, keepdims=True)
        vs_groups.append(t * lax.rsqrt(ms_g + EPS) * sg_ref[:, g * GROUP_W:(g + 1) * GROUP_W])
    if not attn_layout:
        vs_ref[0] = jnp.concatenate(vs_groups, axis=1)

    row = lax.broadcasted_iota(jnp.int32, (SGU_CHUNK, SGU_CHUNK), 0)
    col = lax.broadcasted_iota(jnp.int32, (SGU_CHUNK, SGU_CHUNK), 1)
    keep = row >= col
    if period != SGU_CHUNK:
        keep = keep & ((row // period) == (col // period))
    s_groups = []
    for g in range(GROUPS):
        wm = jnp.where(keep, sw_ref[g], 0.0).astype(BF16)
        vg = vs_groups[g].astype(BF16)
        chunks = [jnp.dot(wm, vg[c * SGU_CHUNK:(c + 1) * SGU_CHUNK], preferred_element_type=F32)
                  + sb_ref[:, g * GROUP_W:(g + 1) * GROUP_W]
                  for c in range(tm // SGU_CHUNK)]
        s_groups.append(jnp.concatenate(chunks, axis=0) if len(chunks) > 1 else chunks[0])
    s = jnp.concatenate(s_groups, axis=1)

    osgu_ref[0] = (u * s * _silu(seg(6))).astype(BF16)


def _inproj(x, norm_g, w_in_bf, q_norm_g, k_norm_g, sgu_norm_g, sgu_w_tiled, sgu_bias, *,
            tm, period, attn_layout):
    b, t, _ = x.shape
    assert t % tm == 0 and tm % SGU_CHUNK == 0
    grid = (b, t // tm)
    tok = lambda w: pl.BlockSpec((1, tm, w), lambda bi, i: (bi, i, 0))
    const2 = lambda a: pl.BlockSpec(a.shape, lambda bi, i: (0, 0))
    const3 = lambda a: pl.BlockSpec(a.shape, lambda bi, i: (0, 0, 0))

    ng = norm_g.reshape(1, D_MODEL)
    qg = jnp.tile(q_norm_g, HEADS).reshape(1, SB_WIDTH)
    kg = jnp.tile(k_norm_g, HEADS).reshape(1, SB_WIDTH)
    sg = sgu_norm_g.reshape(1, SGU_WIDTH)
    hm = _head_mean_matrix()

    act = lambda dt: jax.ShapeDtypeStruct((b, t, SEG), dt)
    if attn_layout:
        feat = lambda dt: jax.ShapeDtypeStruct((b, SB_WIDTH, t), dt)
        feat_blk = pl.BlockSpec((1, SB_WIDTH, tm), lambda bi, i: (bi, 0, i))
        out_shape = (act(BF16), feat(F32), feat(F32), feat(BF16), act(BF16), act(BF16), act(BF16))
        out_specs = (tok(SEG), feat_blk, feat_blk, feat_blk, tok(SEG), tok(SEG), tok(SEG))
    else:
        out_shape = (act(BF16), act(F32), act(F32), act(BF16), act(BF16), act(F32))
        out_specs = (tok(SEG),) * 6

    block_bytes = (_nbytes((tm, D_MODEL), F32) + _nbytes(w_in_bf.shape, BF16)
                   + _nbytes(hm.shape, BF16) + _nbytes(sgu_w_tiled.shape, F32)
                   + _nbytes(sgu_bias.shape, F32) + 7 * _nbytes((tm, SEG), F32))
    temp_bytes = 12 * _nbytes((tm, SEG), F32)
    return pl.pallas_call(
        functools.partial(_inproj_kernel, tm=tm, period=period, attn_layout=attn_layout),
        grid=grid,
        in_specs=[tok(D_MODEL), const2(ng), const2(w_in_bf), const2(qg), const2(kg), const2(sg),
                  const2(hm), const3(sgu_w_tiled), const2(sgu_bias)],
        out_specs=out_specs,
        out_shape=out_shape,
        compiler_params=pltpu.CompilerParams(
            dimension_semantics=("arbitrary", "arbitrary"),
            vmem_limit_bytes=_vmem_limit(block_bytes, temp_bytes)),
        name="inproj_prompt" if attn_layout else "inproj_sample",
    )(x, ng, w_in_bf, qg, kg, sg, hm, sgu_w_tiled, sgu_bias)


def _neg_log2_keep(z, vis):
    sp = jnp.maximum(z, jnp.log2(1.0 + jnp.exp2(jnp.minimum(z, SOFTPLUS_LINEAR))))
    return sp if vis is None else jnp.where(vis, sp, 0.0)


def _sb_weights(z, vis, sums, rem):
    w = jnp.exp2(z + sums[:, :KSUB] + rem)
    if vis is not None:
        w = jnp.where(vis, w, 0.0)
    return w.astype(BF16), rem + sums[:, KSUB:]


def _attn_prompt_kernel(q_ref, kt_ref, v_ref, g_ref, uu_ref, o_ref, rem_ref, acc_ref, *, tq, hps):
    qi = pl.program_id(2)
    q = q_ref[0]
    qh = [q[:, h * HEAD_DIM:(h + 1) * HEAD_DIM] for h in range(hps)]
    uu = uu_ref[...]
    lane = lax.broadcasted_iota(jnp.int32, (KSUB, LANES), 1)
    first_head = lane < HEAD_DIM

    def pair_values(ks, p):
        vv = v_ref[0, pl.ds(ks, KSUB), p * LANES:(p + 1) * LANES]
        zero = jnp.zeros_like(vv)
        return jnp.concatenate([jnp.where(first_head, vv, zero),
                                jnp.where(first_head, zero, vv)], axis=0)

    def sweep(tiles, rems, accs):
        def put(full, r0, r1, part):
            pieces = ([full[:r0]] if r0 else []) + [part] + ([full[r1:]] if r1 < tq else [])
            return jnp.concatenate(pieces, axis=0) if len(pieces) > 1 else part

        zs, splits = [], []
        for ks, r0, r1, diagonal in tiles:
            kt = kt_ref[0, :, pl.ds(ks, KSUB)]
            vis = None
            if diagonal:
                row = lax.broadcasted_iota(jnp.int32, (r1 - r0, KSUB), 0)
                col = lax.broadcasted_iota(jnp.int32, (r1 - r0, KSUB), 1)
                vis = col < row
            for h in range(hps):
                z = jnp.dot(qh[h][r0:r1], kt[h * HEAD_DIM:(h + 1) * HEAD_DIM],
                            preferred_element_type=F32)
                zs.append((z, vis))
                splits.append(_split_bf16(_neg_log2_keep(z, vis)))
        sums = jnp.dot(jnp.concatenate(splits, axis=0), uu, preferred_element_type=F32)
        rems, accs = list(rems), list(accs)
        off, i = 0, 0
        for ks, r0, r1, _ in tiles:
            ws = []
            for h in range(hps):
                z, vis = zs[i]
                i += 1
                w, new = _sb_weights(z, vis, sums[off:off + r1 - r0], rems[h][r0:r1])
                off += r1 - r0
                rems[h] = put(rems[h], r0, r1, new)
                ws.append(w)
            for p in range(hps // 2):
                d = jnp.dot(jnp.concatenate(ws[2 * p:2 * p + 2], axis=1), pair_values(ks, p),
                            preferred_element_type=F32)
                accs[p] = put(accs[p], r0, r1, accs[p][r0:r1] + d)
        return rems, accs

    def diagonal_tiles():
        base = pl.multiple_of(qi * tq, tq)
        return [(pl.multiple_of(base + s * KSUB, KSUB), s * KSUB, tq, True)
                for s in reversed(range(NSUB))]

    def key_block_tiles(n, rows=None):
        kb = pl.multiple_of((qi - n) * tq, tq)
        rows = rows or {s: (0, tq) for s in range(NSUB)}
        return [(pl.multiple_of(kb + s * KSUB, KSUB), *rows[s], False)
                for s in reversed(range(NSUB)) if s in rows]

    zero_rems = [jnp.zeros((tq, LANES), F32)] * hps
    zero_accs = [jnp.zeros((tq, LANES), F32)] * (hps // 2)

    def finish(accs):
        acc = jnp.concatenate(accs, axis=1)
        o_ref[0] = (acc * g_ref[0].astype(F32)).astype(BF16)

    def any_alive(rems):
        most = functools.reduce(jnp.maximum, rems)
        return (jnp.max(most) > REM_DEAD_LOG2).astype(jnp.int32)

    @pl.when(qi == 0)
    def _():
        finish(sweep(diagonal_tiles(), zero_rems, zero_accs)[1])

    def load_state():
        return [rem_ref[h] for h in range(hps)], [acc_ref[p] for p in range(hps // 2)]

    def store_state(rems, accs):
        for h in range(hps):
            rem_ref[h] = rems[h]
        for p in range(hps // 2):
            acc_ref[p] = accs[p]

    eager_rows = {s: min(tq, (s + 3 - NSUB) * KSUB) for s in range(NSUB) if s + 3 - NSUB > 0}

    @pl.when(qi > 0)
    def _():
        first = key_block_tiles(1, {s: (0, e) for s, e in eager_rows.items()})
        rems, accs = sweep(diagonal_tiles() + first, zero_rems, zero_accs)
        store_state(rems, accs)
        alive = any_alive(rems)
        for s in reversed(range(NSUB)):
            e = eager_rows.get(s, 0)
            if e < tq:
                @pl.when(any_alive([r[e:] for r in rems]) > 0)
                def _():
                    store_state(*sweep(key_block_tiles(1, {s: (e, tq)}), *load_state()))

        def cond(c):
            n, alive = c
            return jnp.logical_and(n <= qi, alive > 0)

        def body(c):
            n, _ = c
            rems, accs = sweep(key_block_tiles(n), *load_state())
            store_state(rems, accs)
            return n + 1, any_alive(rems)

        lax.while_loop(cond, body, (jnp.int32(2), alive))
        finish(load_state()[1])


def _attn_prompt(q, kt, vb, gsb, *, tq, hps):
    b, t, _ = q.shape
    assert t % tq == 0 and tq == NSUB * KSUB and HEADS % hps == 0 and hps % 2 == 0
    width = hps * HEAD_DIM
    uu = _suffix_sum_matrix()
    grid = (b, HEADS // hps, t // tq)
    row_blk = pl.BlockSpec((1, tq, width), lambda bi, hg, i: (bi, i, hg))
    block_bytes = (3 * _nbytes((tq, width), BF16) + 2 * _nbytes((width, t), BF16)
                   + _nbytes(uu.shape, BF16))
    temp_bytes = (hps + hps // 2) * _nbytes((tq, LANES), F32) + 8 * NSUB * hps * _nbytes((tq, KSUB), F32)
    return pl.pallas_call(
        functools.partial(_attn_prompt_kernel, tq=tq, hps=hps),
        grid=grid,
        in_specs=[row_blk,
                  pl.BlockSpec((1, width, t), lambda bi, hg, i: (bi, hg, 0)),
                  pl.BlockSpec((1, t, width), lambda bi, hg, i: (bi, 0, hg)),
                  row_blk,
                  pl.BlockSpec(uu.shape, lambda bi, hg, i: (0, 0))],
        out_specs=row_blk,
        out_shape=jax.ShapeDtypeStruct((b, t, SB_WIDTH), BF16),
        scratch_shapes=[pltpu.VMEM((hps, tq, LANES), F32), pltpu.VMEM((hps // 2, tq, LANES), F32)],
        compiler_params=pltpu.CompilerParams(
            dimension_semantics=("arbitrary", "arbitrary", "arbitrary"),
            vmem_limit_bytes=_vmem_limit(block_bytes, temp_bytes)),
        name="sb_attn_prompt",
    )(q, kt, vb, gsb, uu)


def _attn_sample_kernel(q_ref, kn_ref, vn_ref, g_ref, uu_ref, ck_hbm, cv_hbm, o_ref,
                        kwin, vwin, sem, rem_s, acc_s, *, t_new, past, nb):
    rows = HEADS * t_new
    uu = uu_ref[...]
    n_sub = past // KSUB
    contract_last = (((1,), (1,)), ((), ()))

    def fetch(which, b, j, slot):
        src, dst = ((ck_hbm, kwin), (cv_hbm, vwin))[which]
        first = pl.multiple_of(j * KSUB, KSUB)
        return pltpu.make_async_copy(src.at[b, :, :, pl.ds(first, KSUB)], dst.at[b, slot],
                                     sem.at[which, b, slot])

    for b in range(nb):
        for slot in range(EAGER_CACHE_SUBS):
            for which in range(2):
                fetch(which, b, n_sub - 1 - slot, slot).start()

    row = lax.broadcasted_iota(jnp.int32, (rows, KSUB), 0)
    col = lax.broadcasted_iota(jnp.int32, (rows, KSUB), 1)
    new_vis = col < (row % t_new)
    pad = jnp.zeros((KSUB - t_new, SB_WIDTH), F32)

    def new_block(ref, b):
        blk = jnp.concatenate([ref[b], pad], axis=0).astype(BF16)
        return [blk[:, h * HEAD_DIM:(h + 1) * HEAD_DIM] for h in range(HEADS)]

    def cached(win, b, slot):
        return [win[b, slot, h].astype(BF16) for h in range(HEADS)]

    def sweep(b, tiles, rem, acc):
        q = q_ref[b]
        qh = [q[:, h * HEAD_DIM:(h + 1) * HEAD_DIM] for h in range(HEADS)]
        zs = [jnp.concatenate(
            [jnp.dot(qh[h], keys[h], preferred_element_type=F32) if dim_major else
             lax.dot_general(qh[h], keys[h], contract_last, preferred_element_type=F32)
             for h in range(HEADS)], axis=0) for keys, _, dim_major, _ in tiles]
        splits = [_split_bf16(_neg_log2_keep(z, t[3])) for z, t in zip(zs, tiles)]
        sums = jnp.dot(jnp.concatenate(splits, axis=0), uu, preferred_element_type=F32)
        for i, (z, (_, vals, dim_major, vis)) in enumerate(zip(zs, tiles)):
            w, rem = _sb_weights(z, vis, sums[i * rows:(i + 1) * rows], rem)
            wh = [w[h * t_new:(h + 1) * t_new] for h in range(HEADS)]
            acc = [a + (lax.dot_general(wh[h], vals[h], contract_last, preferred_element_type=F32)
                        if dim_major else jnp.dot(wh[h], vals[h], preferred_element_type=F32))
                   for h, a in enumerate(acc)]
        return rem, acc

    def alive_of(rem):
        return (jnp.max(rem) > REM_DEAD_LOG2).astype(jnp.int32)

    alive = jnp.int32(0)
    for b in range(nb):
        for slot in range(EAGER_CACHE_SUBS):
            for which in range(2):
                fetch(which, b, n_sub - 1 - slot, slot).wait()
        tiles = [(new_block(kn_ref, b), new_block(vn_ref, b), False, new_vis)]
        tiles += [(cached(kwin, b, s), cached(vwin, b, s), True, None)
                  for s in range(EAGER_CACHE_SUBS)]
        rem, acc = sweep(b, tiles, jnp.zeros((rows, LANES), F32),
                         [jnp.zeros((t_new, HEAD_DIM), F32)] * HEADS)
        rem_s[b] = rem
        acc_s[b] = jnp.concatenate(acc, axis=1)
        alive = jnp.maximum(alive, alive_of(rem))

    def cond(c):
        j, alive = c
        return jnp.logical_and(j >= 0, alive > 0)

    def body(c):
        j, _ = c
        for b in range(nb):
            for which in range(2):
                fetch(which, b, j, 0).start()
        alive = jnp.int32(0)
        for b in range(nb):
            for which in range(2):
                fetch(which, b, j, 0).wait()
            acc_b = acc_s[b]
            rem, acc = sweep(b, [(cached(kwin, b, 0), cached(vwin, b, 0), True, None)], rem_s[b],
                             [acc_b[:, h * HEAD_DIM:(h + 1) * HEAD_DIM] for h in range(HEADS)])
            rem_s[b] = rem
            acc_s[b] = jnp.concatenate(acc, axis=1)
            alive = jnp.maximum(alive, alive_of(rem))
        return j - 1, alive

    lax.while_loop(cond, body, (jnp.int32(n_sub - 1 - EAGER_CACHE_SUBS), alive))
    for b in range(nb):
        o_ref[b] = (acc_s[b] * g_ref[b].astype(F32)).astype(BF16)


def _attn_sample(q, cache_k, cache_v, k_new, v_new, gsb):
    nb, t_new, _ = q.shape
    past = cache_k.shape[3]
    assert cache_k.shape == (nb, HEADS, HEAD_DIM, past)
    assert past % KSUB == 0 and past // KSUB >= EAGER_CACHE_SUBS and t_new <= KSUB
    uu = _suffix_sum_matrix()
    rows = HEADS * t_new
    whole = lambda a: pl.BlockSpec(a.shape, lambda i: (0,) * a.ndim)
    window = (nb, EAGER_CACHE_SUBS, HEADS, HEAD_DIM, KSUB)
    block_bytes = sum(_nbytes(a.shape, a.dtype) for a in (q, k_new, v_new, gsb, uu, q))
    temp_bytes = (2 * _nbytes(window, F32) + _nbytes((nb, rows, LANES), F32)
                  + _nbytes((nb, t_new, SB_WIDTH), F32)
                  + 16 * (EAGER_CACHE_SUBS + 1) * _nbytes((rows, KSUB), F32))
    return pl.pallas_call(
        functools.partial(_attn_sample_kernel, t_new=t_new, past=past, nb=nb),
        grid=(1,),
        in_specs=[whole(q), whole(k_new), whole(v_new), whole(gsb), whole(uu),
                  pl.BlockSpec(memory_space=pl.ANY), pl.BlockSpec(memory_space=pl.ANY)],
        out_specs=whole(q),
        out_shape=jax.ShapeDtypeStruct((nb, t_new, SB_WIDTH), BF16),
        scratch_shapes=[pltpu.VMEM(window, F32), pltpu.VMEM(window, F32),
                        pltpu.SemaphoreType.DMA((2, nb, EAGER_CACHE_SUBS)),
                        pltpu.VMEM((nb, rows, LANES), F32),
                        pltpu.VMEM((nb, t_new, SB_WIDTH), F32)],
        compiler_params=pltpu.CompilerParams(
            dimension_semantics=("arbitrary",),
            vmem_limit_bytes=_vmem_limit(block_bytes, temp_bytes)),
        name="sb_attn_sample",
    )(q, k_new, v_new, gsb, uu, cache_k, cache_v)


def _outproj_kernel(x_ref, osb_ref, osgu_ref, p_ref, wo_ref, pg_ref, wg_ref, wp_ref, y_ref):
    h = (x_ref[0]
         + jnp.dot(osb_ref[0], wo_ref[:SB_WIDTH], preferred_element_type=F32)
         + jnp.dot(osgu_ref[0], wo_ref[SB_WIDTH:], preferred_element_type=F32))
    ms = jnp.mean(h * h, axis=-1, keepdims=True)
    hn = (h * lax.rsqrt(ms + EPS) * pg_ref[...]).astype(BF16)
    gate_logit = jnp.dot(hn, wg_ref[...], preferred_element_type=F32)
    gate = 1.0 / (1.0 + jnp.exp(-gate_logit))
    pp = jnp.dot(p_ref[0].astype(BF16), wp_ref[...], preferred_element_type=F32)
    y_ref[0] = h + gate * pp


def _outproj(x, osb, osgu, p, w_out_bf, ple_norm_g, w_gate_bf, w_proj_bf, *, tm, name):
    b, t, _ = x.shape
    assert t % tm == 0
    tok = lambda w: pl.BlockSpec((1, tm, w), lambda bi, i: (bi, i, 0))
    const2 = lambda a: pl.BlockSpec(a.shape, lambda bi, i: (0, 0))
    pg = ple_norm_g.reshape(1, D_MODEL)
    block_bytes = (2 * _nbytes((tm, D_MODEL), F32) + 2 * _nbytes((tm, SEG), BF16)
                   + _nbytes((tm, PLE_DIM), F32) + _nbytes(w_out_bf.shape, BF16)
                   + _nbytes(w_gate_bf.shape, BF16) + _nbytes(w_proj_bf.shape, BF16))
    temp_bytes = 6 * _nbytes((tm, D_MODEL), F32)
    return pl.pallas_call(
        _outproj_kernel,
        grid=(b, t // tm),
        in_specs=[tok(D_MODEL), tok(SEG), tok(SEG), tok(PLE_DIM), const2(w_out_bf), const2(pg),
                  const2(w_gate_bf), const2(w_proj_bf)],
        out_specs=tok(D_MODEL),
        out_shape=jax.ShapeDtypeStruct((b, t, D_MODEL), F32),
        compiler_params=pltpu.CompilerParams(
            dimension_semantics=("arbitrary", "arbitrary"),
            vmem_limit_bytes=_vmem_limit(block_bytes, temp_bytes)),
        name=name,
    )(x, osb, osgu, p, w_out_bf, pg, w_gate_bf, w_proj_bf)


def _sgu_bias_rows(sgu_b_l, period):
    per_pos = jnp.tile(sgu_b_l[:, :period].T, (SGU_CHUNK // period, 1))
    return jnp.repeat(per_pos, GROUP_W, axis=1)


def kernel(x_prompt, x_sample, cache_k, cache_v, p_prompt, p_sample, norm_g, w_in, q_norm_g,
           k_norm_g, sgu_norm_g, sgu_w, sgu_b, w_out, ple_norm_g, w_ple_gate, w_ple_proj):
    depth = w_in.shape[0]
    assert depth == 1, "one layer per call"
    l = 0
    b_p, t_p, _ = x_prompt.shape
    b_s, t_s, _ = x_sample.shape
    past = cache_k.shape[2]
    n_s = b_s * t_s
    assert n_s == SGU_CHUNK and SGU_CHUNK % t_s == 0

    w_in_bf = w_in[l].astype(BF16)
    w_out_bf = w_out[l].astype(BF16)
    w_gate_bf = w_ple_gate[l].astype(BF16)
    w_proj_bf = w_ple_proj[l].astype(BF16)

    q, k_feat, v_feat, kt, vb, gsb, osgu = _inproj(
        x_prompt, norm_g[l], w_in_bf, q_norm_g[l], k_norm_g[l], sgu_norm_g[l],
        sgu_w[l], _sgu_bias_rows(sgu_b[l], SGU_CHUNK),
        tm=TM_IN, period=SGU_CHUNK, attn_layout=True)
    osb = _attn_prompt(q, kt, vb, gsb, tq=TQ, hps=HEADS_PER_STEP)
    y_prompt = _outproj(x_prompt, osb, osgu, p_prompt[l], w_out_bf, ple_norm_g[l], w_gate_bf,
                        w_proj_bf, tm=TM_OUT, name="outproj_prompt")

    rep = SGU_CHUNK // t_s
    sgu_w_s = jnp.tile(sgu_w[l][:, :t_s, :t_s], (1, rep, rep))
    xs = x_sample.reshape(1, n_s, D_MODEL)
    q_s, k_s, v_s, gsb_s, osgu_s, vs_s = _inproj(
        xs, norm_g[l], w_in_bf, q_norm_g[l], k_norm_g[l], sgu_norm_g[l],
        sgu_w_s, _sgu_bias_rows(sgu_b[l], t_s),
        tm=n_s, period=t_s, attn_layout=False)
    shp = (b_s, t_s, SB_WIDTH)
    to_feat = lambda c: jnp.transpose(c[l], (0, 2, 3, 1))
    osb_s = _attn_sample(q_s.reshape(shp), to_feat(cache_k), to_feat(cache_v),
                         k_s.reshape(shp), v_s.reshape(shp),
                         gsb_s.reshape(shp))
    y_sample = _outproj(xs, osb_s.reshape(1, n_s, SB_WIDTH), osgu_s, p_sample[l].reshape(1, n_s, PLE_DIM),
                        w_out_bf, ple_norm_g[l], w_gate_bf, w_proj_bf, tm=n_s,
                        name="outproj_sample").reshape(b_s, t_s, D_MODEL)

    head_shape = lambda a, bb, tt: a.reshape(1, bb, tt, HEADS, HEAD_DIM)
    from_feat = lambda a: jnp.transpose(a.reshape(b_p, HEADS, HEAD_DIM, t_p), (0, 3, 1, 2))[None]
    return (y_prompt, y_sample,
            from_feat(k_feat), from_feat(v_feat),
            head_shape(k_s, b_s, t_s), head_shape(v_s, b_s, t_s),
            vs_s.reshape(1, b_s, t_s, GROUPS, GROUP_W))
```

```python
import functools
import math

import numpy as np
import jax
import jax.numpy as jnp
from jax import lax
from jax.experimental import pallas as pl
from jax.experimental.pallas import tpu as pltpu

F32 = jnp.float32
BF16 = jnp.bfloat16

LANES = 128
MXU_TILE = 256
VMEM_BYTES_V7X = 64 * 1024 * 1024

D_MODEL = 1024
PLE_DIM = 256
HEADS = 8
HEAD_DIM = 64
SB_WIDTH = HEADS * HEAD_DIM
GROUPS = 4
GROUP_W = 128
SGU_WIDTH = GROUPS * GROUP_W
SGU_CHUNK = 128
SEG = 512
N_SEG = 7
EPS = 1e-6

Q_SCALE = HEAD_DIM ** -0.5 * math.log2(math.e)

KSUB = LANES
NSUB = 2
TQ = NSUB * KSUB
HEADS_PER_STEP = 8
REM_DEAD_LOG2 = -152.0
SOFTPLUS_LINEAR = 64.0
EAGER_CACHE_SUBS = 2
TM_IN = 1024
TM_OUT = 1024


def _vmem_limit(block_bytes, temp_bytes):
    need = 2 * block_bytes + temp_bytes
    return int(min(need, VMEM_BYTES_V7X - 8 * 1024 * 1024))


def _nbytes(shape, dtype):
    return int(np.prod(shape)) * jnp.dtype(dtype).itemsize


def _suffix_sum_matrix():
    j = np.arange(KSUB)[:, None]
    s = np.arange(KSUB)[None, :]
    one = np.concatenate([np.where(j >= s, -1.0, 0.0), -np.ones((KSUB, LANES))], axis=1)
    return jnp.asarray(np.concatenate([one, one], axis=0), dtype=BF16)


def _head_mean_matrix():
    a = np.arange(MXU_TILE)
    bd = np.where(a[:, None] // HEAD_DIM == a[None, :] // HEAD_DIM, 1.0 / HEAD_DIM, 0.0)
    return jnp.asarray(np.concatenate([bd, bd], axis=0), dtype=BF16)


def _split_bf16(x):
    hi = x.astype(BF16)
    lo = (x - hi.astype(F32)).astype(BF16)
    return jnp.concatenate([hi, lo], axis=1)


def _gelu_tanh(x):
    return 0.5 * x * (1.0 + jnp.tanh(math.sqrt(2.0 / math.pi) * (x + 0.044715 * (x * x * x))))


def _silu(x):
    return x / (1.0 + jnp.exp(-x))


def _inproj_kernel(x_ref, ng_ref, w_ref, qg_ref, kg_ref, sg_ref, hm_ref, sw_ref, sb_ref,
                   *out_refs, tm, period, attn_layout):
    if attn_layout:
        q_ref, ktf_ref, vtf_ref, kt_ref, vb_ref, gsb_ref, osgu_ref = out_refs
    else:
        q_ref, k_ref, v_ref, gsb_ref, osgu_ref, vs_ref = out_refs

    x = x_ref[0]
    ms = jnp.mean(x * x, axis=-1, keepdims=True)
    xn = (x * lax.rsqrt(ms + EPS) * ng_ref[...]).astype(BF16)

    def seg(i):
        return jnp.dot(xn, w_ref[:, i * SEG:(i + 1) * SEG], preferred_element_type=F32)

    def head_rms(t, g):
        sq = t * t
        ms_h = jnp.concatenate(
            [jnp.dot(_split_bf16(sq[:, c:c + MXU_TILE]), hm_ref[...], preferred_element_type=F32)
             for c in range(0, SB_WIDTH, MXU_TILE)], axis=1)
        return t * lax.rsqrt(ms_h + EPS) * g

    q = head_rms(seg(0), qg_ref[...])
    q_ref[0] = (q * Q_SCALE).astype(BF16)

    k = head_rms(seg(1), kg_ref[...])
    v = seg(2)
    if attn_layout:
        k_t = k.T
        ktf_ref[0] = k_t
        kt_ref[0] = k_t.astype(BF16)
        vtf_ref[0] = v.T
        vb_ref[0] = v.astype(BF16)
    else:
        k_ref[0] = k
        v_ref[0] = v

    gsb_ref[0] = _silu(seg(3)).astype(BF16)

    u = _gelu_tanh(seg(4))
    vs_raw = _gelu_tanh(seg(5))
    vs_groups = []
    for g in range(GROUPS):
        t = vs_raw[:, g * GROUP_W:(g + 1) * GROUP_W]
        ms_g = jnp.mean(t * t, axis=-1, keepdims=True)
        vs_groups.append(t * lax.rsqrt(ms_g + EPS) * sg_ref[:, g * GROUP_W:(g + 1) * GROUP_W])
    if not attn_layout:
        vs_ref[0] = jnp.concatenate(vs_groups, axis=1)

    row = lax.broadcasted_iota(jnp.int32, (SGU_CHUNK, SGU_CHUNK), 0)
    col = lax.broadcasted_iota(jnp.int32, (SGU_CHUNK, SGU_CHUNK), 1)
    keep = row >= col
    if period != SGU_CHUNK:
        keep = keep & ((row // period) == (col // period))
    s_groups = []
    for g in range(GROUPS):
        wm = jnp.where(keep, sw_ref[g], 0.0).astype(BF16)
        vg = vs_groups[g].astype(BF16)
        chunks = [jnp.dot(wm, vg[c * SGU_CHUNK:(c + 1) * SGU_CHUNK], preferred_element_type=F32)
                  + sb_ref[:, g * GROUP_W:(g + 1) * GROUP_W]
                  for c in range(tm // SGU_CHUNK)]
        s_groups.append(jnp.concatenate(chunks, axis=0) if len(chunks) > 1 else chunks[0])
    s = jnp.concatenate(s_groups, axis=1)

    osgu_ref[0] = (u * s * _silu(seg(6))).astype(BF16)


def _inproj(x, norm_g, w_in_bf, q_norm_g, k_norm_g, sgu_norm_g, sgu_w_tiled, sgu_bias, *,
            tm, period, attn_layout):
    b, t, _ = x.shape
    assert t % tm == 0 and tm % SGU_CHUNK == 0
    grid = (b, t // tm)
    tok = lambda w: pl.BlockSpec((1, tm, w), lambda bi, i: (bi, i, 0))
    const2 = lambda a: pl.BlockSpec(a.shape, lambda bi, i: (0, 0))
    const3 = lambda a: pl.BlockSpec(a.shape, lambda bi, i: (0, 0, 0))

    ng = norm_g.reshape(1, D_MODEL)
    qg = jnp.tile(q_norm_g, HEADS).reshape(1, SB_WIDTH)
    kg = jnp.tile(k_norm_g, HEADS).reshape(1, SB_WIDTH)
    sg = sgu_norm_g.reshape(1, SGU_WIDTH)
    hm = _head_mean_matrix()

    act = lambda dt: jax.ShapeDtypeStruct((b, t, SEG), dt)
    if attn_layout:
        feat = lambda dt: jax.ShapeDtypeStruct((b, SB_WIDTH, t), dt)
        feat_blk = pl.BlockSpec((1, SB_WIDTH, tm), lambda bi, i: (bi, 0, i))
        out_shape = (act(BF16), feat(F32), feat(F32), feat(BF16), act(BF16), act(BF16), act(BF16))
        out_specs = (tok(SEG), feat_blk, feat_blk, feat_blk, tok(SEG), tok(SEG), tok(SEG))
    else:
        out_shape = (act(BF16), act(F32), act(F32), act(BF16), act(BF16), act(F32))
        out_specs = (tok(SEG),) * 6

    block_bytes = (_nbytes((tm, D_MODEL), F32) + _nbytes(w_in_bf.shape, BF16)
                   + _nbytes(hm.shape, BF16) + _nbytes(sgu_w_tiled.shape, F32)
                   + _nbytes(sgu_bias.shape, F32) + 7 * _nbytes((tm, SEG), F32))
    temp_bytes = 12 * _nbytes((tm, SEG), F32)
    return pl.pallas_call(
        functools.partial(_inproj_kernel, tm=tm, period=period, attn_layout=attn_layout),
        grid=grid,
        in_specs=[tok(D_MODEL), const2(ng), const2(w_in_bf), const2(qg), const2(kg), const2(sg),
                  const2(hm), const3(sgu_w_tiled), const2(sgu_bias)],
        out_specs=out_specs,
        out_shape=out_shape,
        compiler_params=pltpu.CompilerParams(
            dimension_semantics=("arbitrary", "arbitrary"),
            vmem_limit_bytes=_vmem_limit(block_bytes, temp_bytes)),
        name="inproj_prompt" if attn_layout else "inproj_sample",
    )(x, ng, w_in_bf, qg, kg, sg, hm, sgu_w_tiled, sgu_bias)


def _neg_log2_keep(z, vis):
    sp = jnp.maximum(z, jnp.log2(1.0 + jnp.exp2(jnp.minimum(z, SOFTPLUS_LINEAR))))
    return sp if vis is None else jnp.where(vis, sp, 0.0)


def _sb_weights(z, vis, sums, rem):
    w = jnp.exp2(z + sums[:, :KSUB] + rem)
    if vis is not None:
        w = jnp.where(vis, w, 0.0)
    return w.astype(BF16), rem + sums[:, KSUB:]


def _attn_prompt_kernel(q_ref, kt_ref, v_ref, g_ref, uu_ref, o_ref, rem_ref, acc_ref, *, tq, hps):
    qi = pl.program_id(2)
    q = q_ref[0]
    qh = [q[:, h * HEAD_DIM:(h + 1) * HEAD_DIM] for h in range(hps)]
    uu = uu_ref[...]
    lane = lax.broadcasted_iota(jnp.int32, (KSUB, LANES), 1)
    first_head = lane < HEAD_DIM

    def pair_values(ks, p):
        vv = v_ref[0, pl.ds(ks, KSUB), p * LANES:(p + 1) * LANES]
        zero = jnp.zeros_like(vv)
        return jnp.concatenate([jnp.where(first_head, vv, zero),
                                jnp.where(first_head, zero, vv)], axis=0)

    def sweep(tiles, rems, accs):
        def put(full, r0, r1, part):
            pieces = ([full[:r0]] if r0 else []) + [part] + ([full[r1:]] if r1 < tq else [])
            return jnp.concatenate(pieces, axis=0) if len(pieces) > 1 else part

        zs, splits = [], []
        for ks, r0, r1, diagonal in tiles:
            kt = kt_ref[0, :, pl.ds(ks, KSUB)]
            vis = None
            if diagonal:
                row = lax.broadcasted_iota(jnp.int32, (r1 - r0, KSUB), 0)
                col = lax.broadcasted_iota(jnp.int32, (r1 - r0, KSUB), 1)
                vis = col < row
            for h in range(hps):
                z = jnp.dot(qh[h][r0:r1], kt[h * HEAD_DIM:(h + 1) * HEAD_DIM],
                            preferred_element_type=F32)
                zs.append((z, vis))
                splits.append(_split_bf16(_neg_log2_keep(z, vis)))
        sums = jnp.dot(jnp.concatenate(splits, axis=0), uu, preferred_element_type=F32)
        rems, accs = list(rems), list(accs)
        off, i = 0, 0
        for ks, r0, r1, _ in tiles:
            ws = []
            for h in range(hps):
                z, vis = zs[i]
                i += 1
                w, new = _sb_weights(z, vis, sums[off:off + r1 - r0], rems[h][r0:r1])
                off += r1 - r0
                rems[h] = put(rems[h], r0, r1, new)
                ws.append(w)
            for p in range(hps // 2):
                d = jnp.dot(jnp.concatenate(ws[2 * p:2 * p + 2], axis=1), pair_values(ks, p),
                            preferred_element_type=F32)
                accs[p] = put(accs[p], r0, r1, accs[p][r0:r1] + d)
        return rems, accs

    def diagonal_tiles():
        base = pl.multiple_of(qi * tq, tq)
        return [(pl.multiple_of(base + s * KSUB, KSUB), s * KSUB, tq, True)
                for s in reversed(range(NSUB))]

    def key_block_tiles(n, rows=None):
        kb = pl.multiple_of((qi - n) * tq, tq)
        rows = rows or {s: (0, tq) for s in range(NSUB)}
        return [(pl.multiple_of(kb + s * KSUB, KSUB), *rows[s], False)
                for s in reversed(range(NSUB)) if s in rows]

    zero_rems = [jnp.zeros((tq, LANES), F32)] * hps
    zero_accs = [jnp.zeros((tq, LANES), F32)] * (hps // 2)

    def finish(accs):
        acc = jnp.concatenate(accs, axis=1)
        o_ref[0] = (acc * g_ref[0].astype(F32)).astype(BF16)

    def any_alive(rems):
        most = functools.reduce(jnp.maximum, rems)
        return (jnp.max(most) > REM_DEAD_LOG2).astype(jnp.int32)

    @pl.when(qi == 0)
    def _():
        finish(sweep(diagonal_tiles(), zero_rems, zero_accs)[1])

    def load_state():
        return [rem_ref[h] for h in range(hps)], [acc_ref[p] for p in range(hps // 2)]

    def store_state(rems, accs):
        for h in range(hps):
            rem_ref[h] = rems[h]
        for p in range(hps // 2):
            acc_ref[p] = accs[p]

    eager_rows = {s: min(tq, (s + 3 - NSUB) * KSUB) for s in range(NSUB) if s + 3 - NSUB > 0}

    @pl.when(qi > 0)
    def _():
        first = key_block_tiles(1, {s: (0, e) for s, e in eager_rows.items()})
        rems, accs = sweep(diagonal_tiles() + first, zero_rems, zero_accs)
        store_state(rems, accs)
        alive = any_alive(rems)
        for s in reversed(range(NSUB)):
            e = eager_rows.get(s, 0)
            if e < tq:
                @pl.when(any_alive([r[e:] for r in rems]) > 0)
                def _():
                    store_state(*sweep(key_block_tiles(1, {s: (e, tq)}), *load_state()))

        def cond(c):
            n, alive = c
            return jnp.logical_and(n <= qi, alive > 0)

        def body(c):
            n, _ = c
            rems, accs = sweep(key_block_tiles(n), *load_state())
            store_state(rems, accs)
            return n + 1, any_alive(rems)

        lax.while_loop(cond, body, (jnp.int32(2), alive))
        finish(load_state()[1])


def _attn_prompt(q, kt, vb, gsb, *, tq, hps):
    b, t, _ = q.shape
    assert t % tq == 0 and tq == NSUB * KSUB and HEADS % hps == 0 and hps % 2 == 0
    width = hps * HEAD_DIM
    uu = _suffix_sum_matrix()
    grid = (b, HEADS // hps, t // tq)
    row_blk = pl.BlockSpec((1, tq, width), lambda bi, hg, i: (bi, i, hg))
    block_bytes = (3 * _nbytes((tq, width), BF16) + 2 * _nbytes((width, t), BF16)
                   + _nbytes(uu.shape, BF16))
    temp_bytes = (hps + hps // 2) * _nbytes((tq, LANES), F32) + 8 * NSUB * hps * _nbytes((tq, KSUB), F32)
    return pl.pallas_call(
        functools.partial(_attn_prompt_kernel, tq=tq, hps=hps),
        grid=grid,
        in_specs=[row_blk,
                  pl.BlockSpec((1, width, t), lambda bi, hg, i: (bi, hg, 0)),
                  pl.BlockSpec((1, t, width), lambda bi, hg, i: (bi, 0, hg)),
                  row_blk,
                  pl.BlockSpec(uu.shape, lambda bi, hg, i: (0, 0))],
        out_specs=row_blk,
        out_shape=jax.ShapeDtypeStruct((b, t, SB_WIDTH), BF16),
        scratch_shapes=[pltpu.VMEM((hps, tq, LANES), F32), pltpu.VMEM((hps // 2, tq, LANES), F32)],
        compiler_params=pltpu.CompilerParams(
            dimension_semantics=("arbitrary", "arbitrary", "arbitrary"),
            vmem_limit_bytes=_vmem_limit(block_bytes, temp_bytes)),
        name="sb_attn_prompt",
    )(q, kt, vb, gsb, uu)


def _attn_sample_kernel(q_ref, kn_ref, vn_ref, g_ref, uu_ref, ck_hbm, cv_hbm, o_ref,
                        kwin, vwin, sem, rem_s, acc_s, *, t_new, past, nb):
    rows = HEADS * t_new
    uu = uu_ref[...]
    n_sub = past // KSUB
    contract_last = (((1,), (1,)), ((), ()))

    def fetch(which, b, j, slot):
        src, dst = ((ck_hbm, kwin), (cv_hbm, vwin))[which]
        first = pl.multiple_of(j * KSUB, KSUB)
        return pltpu.make_async_copy(src.at[b, :, :, pl.ds(first, KSUB)], dst.at[b, slot],
                                     sem.at[which, b, slot])

    for b in range(nb):
        for slot in range(EAGER_CACHE_SUBS):
            for which in range(2):
                fetch(which, b, n_sub - 1 - slot, slot).start()

    row = lax.broadcasted_iota(jnp.int32, (rows, KSUB), 0)
    col = lax.broadcasted_iota(jnp.int32, (rows, KSUB), 1)
    new_vis = col < (row % t_new)
    pad = jnp.zeros((KSUB - t_new, SB_WIDTH), F32)

    def new_block(ref, b):
        blk = jnp.concatenate([ref[b], pad], axis=0).astype(BF16)
        return [blk[:, h * HEAD_DIM:(h + 1) * HEAD_DIM] for h in range(HEADS)]

    def cached(win, b, slot):
        return [win[b, slot, h].astype(BF16) for h in range(HEADS)]

    def sweep(b, tiles, rem, acc):
        q = q_ref[b]
        qh = [q[:, h * HEAD_DIM:(h + 1) * HEAD_DIM] for h in range(HEADS)]
        zs = [jnp.concatenate(
            [jnp.dot(qh[h], keys[h], preferred_element_type=F32) if dim_major else
             lax.dot_general(qh[h], keys[h], contract_last, preferred_element_type=F32)
             for h in range(HEADS)], axis=0) for keys, _, dim_major, _ in tiles]
        splits = [_split_bf16(_neg_log2_keep(z, t[3])) for z, t in zip(zs, tiles)]
        sums = jnp.dot(jnp.concatenate(splits, axis=0), uu, preferred_element_type=F32)
        for i, (z, (_, vals, dim_major, vis)) in enumerate(zip(zs, tiles)):
            w, rem = _sb_weights(z, vis, sums[i * rows:(i + 1) * rows], rem)
            wh = [w[h * t_new:(h + 1) * t_new] for h in range(HEADS)]
            acc = [a + (lax.dot_general(wh[h], vals[h], contract_last, preferred_element_type=F32)
                        if dim_major else jnp.dot(wh[h], vals[h], preferred_element_type=F32))
                   for h, a in enumerate(acc)]
        return rem, acc

    def alive_of(rem):
        return (jnp.max(rem) > REM_DEAD_LOG2).astype(jnp.int32)

    alive = jnp.int32(0)
    for b in range(nb):
        for slot in range(EAGER_CACHE_SUBS):
            for which in range(2):
                fetch(which, b, n_sub - 1 - slot, slot).wait()
        tiles = [(new_block(kn_ref, b), new_block(vn_ref, b), False, new_vis)]
        tiles += [(cached(kwin, b, s), cached(vwin, b, s), True, None)
                  for s in range(EAGER_CACHE_SUBS)]
        rem, acc = sweep(b, tiles, jnp.zeros((rows, LANES), F32),
                         [jnp.zeros((t_new, HEAD_DIM), F32)] * HEADS)
        rem_s[b] = rem
        acc_s[b] = jnp.concatenate(acc, axis=1)
        alive = jnp.maximum(alive, alive_of(rem))

    def cond(c):
        j, alive = c
        return jnp.logical_and(j >= 0, alive > 0)

    def body(c):
        j, _ = c
        for b in range(nb):
            for which in range(2):
                fetch(which, b, j, 0).start()
        alive = jnp.int32(0)
        for b in range(nb):
            for which in range(2):
                fetch(which, b, j, 0).wait()
            acc_b = acc_s[b]
            rem, acc = sweep(b, [(cached(kwin, b, 0), cached(vwin, b, 0), True, None)], rem_s[b],
                             [acc_b[:, h * HEAD_DIM:(h + 1) * HEAD_DIM] for h in range(HEADS)])
            rem_s[b] = rem
            acc_s[b] = jnp.concatenate(acc, axis=1)
            alive = jnp.maximum(alive, alive_of(rem))
        return j - 1, alive

    lax.while_loop(cond, body, (jnp.int32(n_sub - 1 - EAGER_CACHE_SUBS), alive))
    for b in range(nb):
        o_ref[b] = (acc_s[b] * g_ref[b].astype(F32)).astype(BF16)


def _attn_sample(q, cache_k, cache_v, k_new, v_new, gsb):
    nb, t_new, _ = q.shape
    past = cache_k.shape[3]
    assert cache_k.shape == (nb, HEADS, HEAD_DIM, past)
    assert past % KSUB == 0 and past // KSUB >= EAGER_CACHE_SUBS and t_new <= KSUB
    uu = _suffix_sum_matrix()
    rows = HEADS * t_new
    whole = lambda a: pl.BlockSpec(a.shape, lambda i: (0,) * a.ndim)
    window = (nb, EAGER_CACHE_SUBS, HEADS, HEAD_DIM, KSUB)
    block_bytes = sum(_nbytes(a.shape, a.dtype) for a in (q, k_new, v_new, gsb, uu, q))
    temp_bytes = (2 * _nbytes(window, F32) + _nbytes((nb, rows, LANES), F32)
                  + _nbytes((nb, t_new, SB_WIDTH), F32)
                  + 16 * (EAGER_CACHE_SUBS + 1) * _nbytes((rows, KSUB), F32))
    return pl.pallas_call(
        functools.partial(_attn_sample_kernel, t_new=t_new, past=past, nb=nb),
        grid=(1,),
        in_specs=[whole(q), whole(k_new), whole(v_new), whole(gsb), whole(uu),
                  pl.BlockSpec(memory_space=pltpu.HBM), pl.BlockSpec(memory_space=pltpu.HBM)],
        out_specs=whole(q),
        out_shape=jax.ShapeDtypeStruct((nb, t_new, SB_WIDTH), BF16),
        scratch_shapes=[pltpu.VMEM(window, F32), pltpu.VMEM(window, F32),
                        pltpu.SemaphoreType.DMA((2, nb, EAGER_CACHE_SUBS)),
                        pltpu.VMEM((nb, rows, LANES), F32),
                        pltpu.VMEM((nb, t_new, SB_WIDTH), F32)],
        compiler_params=pltpu.CompilerParams(
            dimension_semantics=("arbitrary",),
            vmem_limit_bytes=max(_vmem_limit(block_bytes, temp_bytes),
                                 VMEM_BYTES_V7X - _nbytes(cache_k.shape, F32))),
        name="sb_attn_sample",
    )(q, k_new, v_new, gsb, uu, cache_k, cache_v)


def _outproj_kernel(x_ref, osb_ref, osgu_ref, p_ref, wo_ref, pg_ref, wg_ref, wp_ref, y_ref):
    h = (x_ref[0]
         + jnp.dot(osb_ref[0], wo_ref[:SB_WIDTH], preferred_element_type=F32)
         + jnp.dot(osgu_ref[0], wo_ref[SB_WIDTH:], preferred_element_type=F32))
    ms = jnp.mean(h * h, axis=-1, keepdims=True)
    hn = (h * lax.rsqrt(ms + EPS) * pg_ref[...]).astype(BF16)
    gate_logit = jnp.dot(hn, wg_ref[...], preferred_element_type=F32)
    gate = 1.0 / (1.0 + jnp.exp(-gate_logit))
    pp = jnp.dot(p_ref[0].astype(BF16), wp_ref[...], preferred_element_type=F32)
    y_ref[0] = h + gate * pp


def _outproj(x, osb, osgu, p, w_out_bf, ple_norm_g, w_gate_bf, w_proj_bf, *, tm, name):
    b, t, _ = x.shape
    assert t % tm == 0
    tok = lambda w: pl.BlockSpec((1, tm, w), lambda bi, i: (bi, i, 0))
    const2 = lambda a: pl.BlockSpec(a.shape, lambda bi, i: (0, 0))
    pg = ple_norm_g.reshape(1, D_MODEL)
    block_bytes = (2 * _nbytes((tm, D_MODEL), F32) + 2 * _nbytes((tm, SEG), BF16)
                   + _nbytes((tm, PLE_DIM), F32) + _nbytes(w_out_bf.shape, BF16)
                   + _nbytes(w_gate_bf.shape, BF16) + _nbytes(w_proj_bf.shape, BF16))
    temp_bytes = 6 * _nbytes((tm, D_MODEL), F32)
    return pl.pallas_call(
        _outproj_kernel,
        grid=(b, t // tm),
        in_specs=[tok(D_MODEL), tok(SEG), tok(SEG), tok(PLE_DIM), const2(w_out_bf), const2(pg),
                  const2(w_gate_bf), const2(w_proj_bf)],
        out_specs=tok(D_MODEL),
        out_shape=jax.ShapeDtypeStruct((b, t, D_MODEL), F32),
        compiler_params=pltpu.CompilerParams(
            dimension_semantics=("arbitrary", "arbitrary"),
            vmem_limit_bytes=_vmem_limit(block_bytes, temp_bytes)),
        name=name,
    )(x, osb, osgu, p, w_out_bf, pg, w_gate_bf, w_proj_bf)


def _sgu_bias_rows(sgu_b_l, period):
    per_pos = jnp.tile(sgu_b_l[:, :period].T, (SGU_CHUNK // period, 1))
    return jnp.repeat(per_pos, GROUP_W, axis=1)


def kernel(x_prompt, x_sample, cache_k, cache_v, p_prompt, p_sample, norm_g, w_in, q_norm_g,
           k_norm_g, sgu_norm_g, sgu_w, sgu_b, w_out, ple_norm_g, w_ple_gate, w_ple_proj):
    depth = w_in.shape[0]
    assert depth == 1, "one layer per call"
    l = 0
    b_p, t_p, _ = x_prompt.shape
    b_s, t_s, _ = x_sample.shape
    past = cache_k.shape[2]
    n_s = b_s * t_s
    assert n_s == SGU_CHUNK and SGU_CHUNK % t_s == 0

    w_in_bf = w_in[l].astype(BF16)
    w_out_bf = w_out[l].astype(BF16)
    w_gate_bf = w_ple_gate[l].astype(BF16)
    w_proj_bf = w_ple_proj[l].astype(BF16)

    q, k_feat, v_feat, kt, vb, gsb, osgu = _inproj(
        x_prompt, norm_g[l], w_in_bf, q_norm_g[l], k_norm_g[l], sgu_norm_g[l],
        sgu_w[l], _sgu_bias_rows(sgu_b[l], SGU_CHUNK),
        tm=TM_IN, period=SGU_CHUNK, attn_layout=True)
    osb = _attn_prompt(q, kt, vb, gsb, tq=TQ, hps=HEADS_PER_STEP)
    y_prompt = _outproj(x_prompt, osb, osgu, p_prompt[l], w_out_bf, ple_norm_g[l], w_gate_bf,
                        w_proj_bf, tm=TM_OUT, name="outproj_prompt")

    rep = SGU_CHUNK // t_s
    sgu_w_s = jnp.tile(sgu_w[l][:, :t_s, :t_s], (1, rep, rep))
    xs = x_sample.reshape(1, n_s, D_MODEL)
    q_s, k_s, v_s, gsb_s, osgu_s, vs_s = _inproj(
        xs, norm_g[l], w_in_bf, q_norm_g[l], k_norm_g[l], sgu_norm_g[l],
        sgu_w_s, _sgu_bias_rows(sgu_b[l], t_s),
        tm=n_s, period=t_s, attn_layout=False)
    shp = (b_s, t_s, SB_WIDTH)
    to_feat = lambda c: jnp.transpose(c[l], (0, 2, 3, 1))
    osb_s = _attn_sample(q_s.reshape(shp), to_feat(cache_k), to_feat(cache_v),
                         k_s.reshape(shp), v_s.reshape(shp),
                         gsb_s.reshape(shp))
    y_sample = _outproj(xs, osb_s.reshape(1, n_s, SB_WIDTH), osgu_s, p_sample[l].reshape(1, n_s, PLE_DIM),
                        w_out_bf, ple_norm_g[l], w_gate_bf, w_proj_bf, tm=n_s,
                        name="outproj_sample").reshape(b_s, t_s, D_MODEL)

    head_shape = lambda a, bb, tt: a.reshape(1, bb, tt, HEADS, HEAD_DIM)
    from_feat = lambda a: jnp.transpose(a.reshape(b_p, HEADS, HEAD_DIM, t_p), (0, 3, 1, 2))[None]
    return (y_prompt, y_sample,
            from_feat(k_feat), from_feat(v_feat),
            head_shape(k_s, b_s, t_s), head_shape(v_s, b_s, t_s),
            vs_s.reshape(1, b_s, t_s, GROUPS, GROUP_W))
```

```python
import functools
import math

import numpy as np
import jax
import jax.numpy as jnp
from jax import lax
from jax.experimental import pallas as pl
from jax.experimental.pallas import tpu as pltpu

F32 = jnp.float32
BF16 = jnp.bfloat16

LANES = 128
MXU_TILE = 256
VMEM_BYTES_V7X = 64 * 1024 * 1024

D_MODEL = 1024
PLE_DIM = 256
HEADS = 8
HEAD_DIM = 64
SB_WIDTH = HEADS * HEAD_DIM
GROUPS = 4
GROUP_W = 128
SGU_WIDTH = GROUPS * GROUP_W
SGU_CHUNK = 128
SEG = 512
N_SEG = 7
EPS = 1e-6

Q_SCALE = HEAD_DIM ** -0.5 * math.log2(math.e)

KSUB = LANES
NSUB = 2
TQ = NSUB * KSUB
HEADS_PER_STEP = 8
REM_DEAD_LOG2 = -152.0
SOFTPLUS_LINEAR = 64.0
EAGER_CACHE_SUBS = 2
TM_IN = 1024
TM_OUT = 1024


def _vmem_limit(block_bytes, temp_bytes):
    need = 2 * block_bytes + temp_bytes
    return int(min(need, VMEM_BYTES_V7X - 8 * 1024 * 1024))


def _nbytes(shape, dtype):
    return int(np.prod(shape)) * jnp.dtype(dtype).itemsize


def _suffix_sum_matrix():
    j = np.arange(KSUB)[:, None]
    s = np.arange(KSUB)[None, :]
    one = np.concatenate([np.where(j >= s, -1.0, 0.0), -np.ones((KSUB, LANES))], axis=1)
    return jnp.asarray(np.concatenate([one, one], axis=0), dtype=BF16)


def _head_mean_matrix():
    a = np.arange(MXU_TILE)
    bd = np.where(a[:, None] // HEAD_DIM == a[None, :] // HEAD_DIM, 1.0 / HEAD_DIM, 0.0)
    return jnp.asarray(np.concatenate([bd, bd], axis=0), dtype=BF16)


def _split_bf16(x):
    hi = x.astype(BF16)
    lo = (x - hi.astype(F32)).astype(BF16)
    return jnp.concatenate([hi, lo], axis=1)


def _gelu_tanh(x):
    return 0.5 * x * (1.0 + jnp.tanh(math.sqrt(2.0 / math.pi) * (x + 0.044715 * (x * x * x))))


def _silu(x):
    return x / (1.0 + jnp.exp(-x))


def _inproj_kernel(x_ref, ng_ref, w_ref, qg_ref, kg_ref, sg_ref, hm_ref, sw_ref, sb_ref,
                   *out_refs, tm, period, attn_layout):
    if attn_layout:
        q_ref, ktf_ref, vtf_ref, kt_ref, vb_ref, gsb_ref, osgu_ref = out_refs
    else:
        q_ref, k_ref, v_ref, gsb_ref, osgu_ref, vs_ref = out_refs

    x = x_ref[0]
    ms = jnp.mean(x * x, axis=-1, keepdims=True)
    xn = (x * lax.rsqrt(ms + EPS) * ng_ref[...]).astype(BF16)

    def seg(i):
        return jnp.dot(xn, w_ref[:, i * SEG:(i + 1) * SEG], preferred_element_type=F32)

    def head_rms(t, g):
        sq = t * t
        ms_h = jnp.concatenate(
            [jnp.dot(_split_bf16(sq[:, c:c + MXU_TILE]), hm_ref[...], preferred_element_type=F32)
             for c in range(0, SB_WIDTH, MXU_TILE)], axis=1)
        return t * lax.rsqrt(ms_h + EPS) * g

    q = head_rms(seg(0), qg_ref[...])
    q_ref[0] = (q * Q_SCALE).astype(BF16)

    k = head_rms(seg(1), kg_ref[...])
    v = seg(2)
    if attn_layout:
        k_t = k.T
        ktf_ref[0] = k_t
        kt_ref[0] = k_t.astype(BF16)
        vtf_ref[0] = v.T
        vb_ref[0] = v.astype(BF16)
    else:
        k_ref[0] = k
        v_ref[0] = v

    gsb_ref[0] = _silu(seg(3)).astype(BF16)

    u = _gelu_tanh(seg(4))
    vs_raw = _gelu_tanh(seg(5))
    vs_groups = []
    for g in range(GROUPS):
        t = vs_raw[:, g * GROUP_W:(g + 1) * GROUP_W]
        ms_g = jnp.mean(t * t, axis=-1, keepdims=True)
        vs_groups.append(t * lax.rsqrt(ms_g + EPS) * sg_ref[:, g * GROUP_W:(g + 1) * GROUP_W])
    if not attn_layout:
        vs_ref[0] = jnp.concatenate(vs_groups, axis=1)

    row = lax.broadcasted_iota(jnp.int32, (SGU_CHUNK, SGU_CHUNK), 0)
    col = lax.broadcasted_iota(jnp.int32, (SGU_CHUNK, SGU_CHUNK), 1)
    keep = row >= col
    if period != SGU_CHUNK:
        keep = keep & ((row // period) == (col // period))
    s_groups = []
    for g in range(GROUPS):
        wm = jnp.where(keep, sw_ref[g], 0.0).astype(BF16)
        vg = vs_groups[g].astype(BF16)
        chunks = [jnp.dot(wm, vg[c * SGU_CHUNK:(c + 1) * SGU_CHUNK], preferred_element_type=F32)
                  + sb_ref[:, g * GROUP_W:(g + 1) * GROUP_W]
                  for c in range(tm // SGU_CHUNK)]
        s_groups.append(jnp.concatenate(chunks, axis=0) if len(chunks) > 1 else chunks[0])
    s = jnp.concatenate(s_groups, axis=1)

    osgu_ref[0] = (u * s * _silu(seg(6))).astype(BF16)


def _inproj(x, norm_g, w_in_bf, q_norm_g, k_norm_g, sgu_norm_g, sgu_w_tiled, sgu_bias, *,
            tm, period, attn_layout):
    b, t, _ = x.shape
    assert t % tm == 0 and tm % SGU_CHUNK == 0
    grid = (b, t // tm)
    tok = lambda w: pl.BlockSpec((1, tm, w), lambda bi, i: (bi, i, 0))
    const2 = lambda a: pl.BlockSpec(a.shape, lambda bi, i: (0, 0))
    const3 = lambda a: pl.BlockSpec(a.shape, lambda bi, i: (0, 0, 0))

    ng = norm_g.reshape(1, D_MODEL)
    qg = jnp.tile(q_norm_g, HEADS).reshape(1, SB_WIDTH)
    kg = jnp.tile(k_norm_g, HEADS).reshape(1, SB_WIDTH)
    sg = sgu_norm_g.reshape(1, SGU_WIDTH)
    hm = _head_mean_matrix()

    act = lambda dt: jax.ShapeDtypeStruct((b, t, SEG), dt)
    if attn_layout:
        feat = lambda dt: jax.ShapeDtypeStruct((b, SB_WIDTH, t), dt)
        feat_blk = pl.BlockSpec((1, SB_WIDTH, tm), lambda bi, i: (bi, 0, i))
        out_shape = (act(BF16), feat(F32), feat(F32), feat(BF16), act(BF16), act(BF16), act(BF16))
        out_specs = (tok(SEG), feat_blk, feat_blk, feat_blk, tok(SEG), tok(SEG), tok(SEG))
    else:
        out_shape = (act(BF16), act(F32), act(F32), act(BF16), act(BF16), act(F32))
        out_specs = (tok(SEG),) * 6

    block_bytes = (_nbytes((tm, D_MODEL), F32) + _nbytes(w_in_bf.shape, BF16)
                   + _nbytes(hm.shape, BF16) + _nbytes(sgu_w_tiled.shape, F32)
                   + _nbytes(sgu_bias.shape, F32) + 7 * _nbytes((tm, SEG), F32))
    temp_bytes = 12 * _nbytes((tm, SEG), F32)
    return pl.pallas_call(
        functools.partial(_inproj_kernel, tm=tm, period=period, attn_layout=attn_layout),
        grid=grid,
        in_specs=[tok(D_MODEL), const2(ng), const2(w_in_bf), const2(qg), const2(kg), const2(sg),
                  const2(hm), const3(sgu_w_tiled), const2(sgu_bias)],
        out_specs=out_specs,
        out_shape=out_shape,
        compiler_params=pltpu.CompilerParams(
            dimension_semantics=("arbitrary", "arbitrary"),
            vmem_limit_bytes=_vmem_limit(block_bytes, temp_bytes)),
        name="inproj_prompt" if attn_layout else "inproj_sample",
    )(x, ng, w_in_bf, qg, kg, sg, hm, sgu_w_tiled, sgu_bias)


def _neg_log2_keep(z, vis):
    sp = jnp.maximum(z, jnp.log2(1.0 + jnp.exp2(jnp.minimum(z, SOFTPLUS_LINEAR))))
    return sp if vis is None else jnp.where(vis, sp, 0.0)


def _sb_weights(z, vis, sums, rem):
    w = jnp.exp2(z + sums[:, :KSUB] + rem)
    if vis is not None:
        w = jnp.where(vis, w, 0.0)
    return w.astype(BF16), rem + sums[:, KSUB:]


def _attn_prompt_kernel(q_ref, g_ref, uu_ref, kt_hbm, v_hbm, o_ref,
                        kwin, vwin, kold, vold, sem, rem_ref, acc_ref, *, tq, hps, nb, nq):
    qi = pl.program_id(0)
    slot = qi % 2
    nv = nb * hps

    def window(step, slot_):
        first = pl.multiple_of(jnp.maximum(step - 1, 0) * tq, tq)
        return ([pltpu.make_async_copy(kt_hbm.at[r, :, pl.ds(first, 2 * tq)], kwin.at[slot_, r],
                                       sem.at[0, slot_]) for r in range(nb)]
                + [pltpu.make_async_copy(v_hbm.at[r, pl.ds(first, 2 * tq), :], vwin.at[slot_, r],
                                         sem.at[1, slot_]) for r in range(nb)])

    def older(n):
        first = pl.multiple_of((qi - n) * tq, tq)
        return ([pltpu.make_async_copy(kt_hbm.at[r, :, pl.ds(first, tq)], kold.at[r], sem.at[0, 2])
                 for r in range(nb)]
                + [pltpu.make_async_copy(v_hbm.at[r, pl.ds(first, tq), :], vold.at[r], sem.at[1, 2])
                   for r in range(nb)])

    @pl.when(qi == 0)
    def _():
        for c in window(qi, slot):
            c.start()

    for c in window(qi, slot):
        c.wait()

    @pl.when(qi + 1 < nq)
    def _():
        for c in window(qi + 1, 1 - slot):
            c.start()

    qh = [q_ref[v // hps][:, (v % hps) * HEAD_DIM:(v % hps + 1) * HEAD_DIM] for v in range(nv)]
    uu = uu_ref[...]
    lane = lax.broadcasted_iota(jnp.int32, (KSUB, LANES), 1)
    first_head = lane < HEAD_DIM

    def keys_at(src, r, ks):
        return (kwin[slot, r, :, pl.ds(ks, KSUB)] if src == "window"
                else kold[r, :, pl.ds(ks, KSUB)])

    def pair_values(src, ks, p):
        pairs = hps // 2
        lanes = pl.ds((p % pairs) * LANES, LANES)
        vv = (vwin[slot, p // pairs, pl.ds(ks, KSUB), lanes] if src == "window"
              else vold[p // pairs, pl.ds(ks, KSUB), lanes])
        zero = jnp.zeros_like(vv)
        return jnp.concatenate([jnp.where(first_head, vv, zero),
                                jnp.where(first_head, zero, vv)], axis=0)

    def sweep(tiles, rems, accs):
        def put(full, r0, r1, part):
            pieces = ([full[:r0]] if r0 else []) + [part] + ([full[r1:]] if r1 < tq else [])
            return jnp.concatenate(pieces, axis=0) if len(pieces) > 1 else part

        zs, splits = [], []
        for src, ks, r0, r1, diagonal in tiles:
            kts = [keys_at(src, r, ks) for r in range(nb)]
            vis = None
            if diagonal:
                row = lax.broadcasted_iota(jnp.int32, (r1 - r0, KSUB), 0)
                col = lax.broadcasted_iota(jnp.int32, (r1 - r0, KSUB), 1)
                vis = col < row
            for v in range(nv):
                h = v % hps
                z = jnp.dot(qh[v][r0:r1], kts[v // hps][h * HEAD_DIM:(h + 1) * HEAD_DIM],
                            preferred_element_type=F32)
                zs.append((z, vis))
                splits.append(_split_bf16(_neg_log2_keep(z, vis)))
        sums = jnp.dot(jnp.concatenate(splits, axis=0), uu, preferred_element_type=F32)
        rems, accs = list(rems), list(accs)
        off, i = 0, 0
        for src, ks, r0, r1, _ in tiles:
            ws = []
            for h in range(nv):
                z, vis = zs[i]
                i += 1
                w, new = _sb_weights(z, vis, sums[off:off + r1 - r0], rems[h][r0:r1])
                off += r1 - r0
                rems[h] = put(rems[h], r0, r1, new)
                ws.append(w)
            for p in range(nv // 2):
                d = jnp.dot(jnp.concatenate(ws[2 * p:2 * p + 2], axis=1), pair_values(src, ks, p),
                            preferred_element_type=F32)
                accs[p] = put(accs[p], r0, r1, accs[p][r0:r1] + d)
        return rems, accs

    def diagonal_tiles(at):
        return [("window", at + s * KSUB, s * KSUB, tq, True) for s in reversed(range(NSUB))]

    def key_block_tiles(src, at, rows=None):
        rows = rows or {s: (0, tq) for s in range(NSUB)}
        return [(src, at + s * KSUB, *rows[s], False) for s in reversed(range(NSUB)) if s in rows]

    zero_rems = [jnp.zeros((tq, LANES), F32)] * nv
    zero_accs = [jnp.zeros((tq, LANES), F32)] * (nv // 2)

    def finish(accs):
        pairs = hps // 2
        for r in range(nb):
            acc = jnp.concatenate(accs[r * pairs:(r + 1) * pairs], axis=1)
            o_ref[r] = (acc * g_ref[r].astype(F32)).astype(BF16)

    def any_alive(rems):
        most = functools.reduce(jnp.maximum, rems)
        return (jnp.max(most) > REM_DEAD_LOG2).astype(jnp.int32)

    @pl.when(qi == 0)
    def _():
        finish(sweep(diagonal_tiles(0), zero_rems, zero_accs)[1])

    def load_state():
        return [rem_ref[h] for h in range(nv)], [acc_ref[p] for p in range(nv // 2)]

    def store_state(rems, accs):
        for h in range(nv):
            rem_ref[h] = rems[h]
        for p in range(nv // 2):
            acc_ref[p] = accs[p]

    eager_rows = {s: min(tq, (s + 3 - NSUB) * KSUB) for s in range(NSUB) if s + 3 - NSUB > 0}

    @pl.when(qi > 0)
    def _():
        first = key_block_tiles("window", 0, {s: (0, e) for s, e in eager_rows.items()})
        rems, accs = sweep(diagonal_tiles(tq) + first, zero_rems, zero_accs)
        store_state(rems, accs)
        alive = any_alive(rems)
        for s in reversed(range(NSUB)):
            e = eager_rows.get(s, 0)
            if e < tq:
                @pl.when(any_alive([r[e:] for r in rems]) > 0)
                def _():
                    store_state(*sweep(key_block_tiles("window", 0, {s: (e, tq)}), *load_state()))

        def cond(c):
            n, alive = c
            return jnp.logical_and(n <= qi, alive > 0)

        def body(c):
            n, _ = c
            for cp in older(n):
                cp.start()
            for cp in older(n):
                cp.wait()
            rems, accs = sweep(key_block_tiles("older", 0), *load_state())
            store_state(rems, accs)
            return n + 1, any_alive(rems)

        lax.while_loop(cond, body, (jnp.int32(2), alive))
        finish(load_state()[1])


def _attn_prompt(q, kt, vb, gsb, *, tq, hps):
    nb, t, _ = q.shape
    assert t % tq == 0 and t >= 2 * tq and tq == NSUB * KSUB and hps == HEADS
    nv = nb * hps
    nq = t // tq
    uu = _suffix_sum_matrix()
    row_blk = pl.BlockSpec((nb, tq, SB_WIDTH), lambda i: (0, i, 0))
    in_hbm = pl.BlockSpec(memory_space=pltpu.HBM)
    block_bytes = 3 * _nbytes((nb, tq, SB_WIDTH), BF16) + _nbytes(uu.shape, BF16)
    temp_bytes = ((nv + nv // 2) * _nbytes((tq, LANES), F32)
                  + 5 * _nbytes((nb, SB_WIDTH, 2 * tq), BF16)
                  + 5 * (NSUB + 2) * nv * _nbytes((tq, KSUB), F32))
    return pl.pallas_call(
        functools.partial(_attn_prompt_kernel, tq=tq, hps=hps, nb=nb, nq=nq),
        grid=(nq,),
        in_specs=[row_blk, row_blk, pl.BlockSpec(uu.shape, lambda i: (0, 0)), in_hbm, in_hbm],
        out_specs=row_blk,
        out_shape=jax.ShapeDtypeStruct((nb, t, SB_WIDTH), BF16),
        scratch_shapes=[pltpu.VMEM((2, nb, SB_WIDTH, 2 * tq), BF16),
                        pltpu.VMEM((2, nb, 2 * tq, SB_WIDTH), BF16),
                        pltpu.VMEM((nb, SB_WIDTH, tq), BF16),
                        pltpu.VMEM((nb, tq, SB_WIDTH), BF16),
                        pltpu.SemaphoreType.DMA((2, 3)),
                        pltpu.VMEM((nv, tq, LANES), F32), pltpu.VMEM((nv // 2, tq, LANES), F32)],
        compiler_params=pltpu.CompilerParams(
            dimension_semantics=("arbitrary",),
            vmem_limit_bytes=_vmem_limit(block_bytes, temp_bytes)),
        name="sb_attn_prompt",
    )(q, gsb, uu, kt, vb)


def _attn_sample_kernel(q_ref, kn_ref, vn_ref, g_ref, uu_ref, ck_hbm, cv_hbm, o_ref,
                        kwin, vwin, sem, rem_s, acc_s, *, t_new, past, nb):
    rows = HEADS * t_new
    uu = uu_ref[...]
    n_sub = past // KSUB
    contract_last = (((1,), (1,)), ((), ()))

    def fetch(which, b, j, slot):
        src, dst = ((ck_hbm, kwin), (cv_hbm, vwin))[which]
        first = pl.multiple_of(j * KSUB, KSUB)
        return pltpu.make_async_copy(src.at[b, :, :, pl.ds(first, KSUB)], dst.at[b, slot],
                                     sem.at[which, b, slot])

    for b in range(nb):
        for slot in range(EAGER_CACHE_SUBS):
            for which in range(2):
                fetch(which, b, n_sub - 1 - slot, slot).start()

    row = lax.broadcasted_iota(jnp.int32, (rows, KSUB), 0)
    col = lax.broadcasted_iota(jnp.int32, (rows, KSUB), 1)
    new_vis = col < (row % t_new)
    pad = jnp.zeros((KSUB - t_new, SB_WIDTH), F32)

    def new_block(ref, b):
        blk = jnp.concatenate([ref[b], pad], axis=0).astype(BF16)
        return [blk[:, h * HEAD_DIM:(h + 1) * HEAD_DIM] for h in range(HEADS)]

    def cached(win, b, slot):
        return [win[b, slot, h].astype(BF16) for h in range(HEADS)]

    def sweep(b, tiles, rem, acc):
        q = q_ref[b]
        qh = [q[:, h * HEAD_DIM:(h + 1) * HEAD_DIM] for h in range(HEADS)]
        zs = [jnp.concatenate(
            [jnp.dot(qh[h], keys[h], preferred_element_type=F32) if dim_major else
             lax.dot_general(qh[h], keys[h], contract_last, preferred_element_type=F32)
             for h in range(HEADS)], axis=0) for keys, _, dim_major, _ in tiles]
        splits = [_split_bf16(_neg_log2_keep(z, t[3])) for z, t in zip(zs, tiles)]
        sums = jnp.dot(jnp.concatenate(splits, axis=0), uu, preferred_element_type=F32)
        for i, (z, (_, vals, dim_major, vis)) in enumerate(zip(zs, tiles)):
            w, rem = _sb_weights(z, vis, sums[i * rows:(i + 1) * rows], rem)
            wh = [w[h * t_new:(h + 1) * t_new] for h in range(HEADS)]
            acc = [a + (lax.dot_general(wh[h], vals[h], contract_last, preferred_element_type=F32)
                        if dim_major else jnp.dot(wh[h], vals[h], preferred_element_type=F32))
                   for h, a in enumerate(acc)]
        return rem, acc

    def alive_of(rem):
        return (jnp.max(rem) > REM_DEAD_LOG2).astype(jnp.int32)

    alive = jnp.int32(0)
    for b in range(nb):
        for slot in range(EAGER_CACHE_SUBS):
            for which in range(2):
                fetch(which, b, n_sub - 1 - slot, slot).wait()
        tiles = [(new_block(kn_ref, b), new_block(vn_ref, b), False, new_vis)]
        tiles += [(cached(kwin, b, s), cached(vwin, b, s), True, None)
                  for s in range(EAGER_CACHE_SUBS)]
        rem, acc = sweep(b, tiles, jnp.zeros((rows, LANES), F32),
                         [jnp.zeros((t_new, HEAD_DIM), F32)] * HEADS)
        rem_s[b] = rem
        acc_s[b] = jnp.concatenate(acc, axis=1)
        alive = jnp.maximum(alive, alive_of(rem))

    def cond(c):
        j, alive = c
        return jnp.logical_and(j >= 0, alive > 0)

    def body(c):
        j, _ = c
        for b in range(nb):
            for which in range(2):
                fetch(which, b, j, 0).start()
        alive = jnp.int32(0)
        for b in range(nb):
            for which in range(2):
                fetch(which, b, j, 0).wait()
            acc_b = acc_s[b]
            rem, acc = sweep(b, [(cached(kwin, b, 0), cached(vwin, b, 0), True, None)], rem_s[b],
                             [acc_b[:, h * HEAD_DIM:(h + 1) * HEAD_DIM] for h in range(HEADS)])
            rem_s[b] = rem
            acc_s[b] = jnp.concatenate(acc, axis=1)
            alive = jnp.maximum(alive, alive_of(rem))
        return j - 1, alive

    lax.while_loop(cond, body, (jnp.int32(n_sub - 1 - EAGER_CACHE_SUBS), alive))
    for b in range(nb):
        o_ref[b] = (acc_s[b] * g_ref[b].astype(F32)).astype(BF16)


def _attn_sample(q, cache_k, cache_v, k_new, v_new, gsb):
    nb, t_new, _ = q.shape
    past = cache_k.shape[3]
    assert cache_k.shape == (nb, HEADS, HEAD_DIM, past)
    assert past % KSUB == 0 and past // KSUB >= EAGER_CACHE_SUBS and t_new <= KSUB
    uu = _suffix_sum_matrix()
    rows = HEADS * t_new
    whole = lambda a: pl.BlockSpec(a.shape, lambda i: (0,) * a.ndim)
    window = (nb, EAGER_CACHE_SUBS, HEADS, HEAD_DIM, KSUB)
    block_bytes = sum(_nbytes(a.shape, a.dtype) for a in (q, k_new, v_new, gsb, uu, q))
    temp_bytes = (2 * _nbytes(window, F32) + _nbytes((nb, rows, LANES), F32)
                  + _nbytes((nb, t_new, SB_WIDTH), F32)
                  + 16 * (EAGER_CACHE_SUBS + 1) * _nbytes((rows, KSUB), F32))
    return pl.pallas_call(
        functools.partial(_attn_sample_kernel, t_new=t_new, past=past, nb=nb),
        grid=(1,),
        in_specs=[whole(q), whole(k_new), whole(v_new), whole(gsb), whole(uu),
                  pl.BlockSpec(memory_space=pltpu.HBM), pl.BlockSpec(memory_space=pltpu.HBM)],
        out_specs=whole(q),
        out_shape=jax.ShapeDtypeStruct((nb, t_new, SB_WIDTH), BF16),
        scratch_shapes=[pltpu.VMEM(window, F32), pltpu.VMEM(window, F32),
                        pltpu.SemaphoreType.DMA((2, nb, EAGER_CACHE_SUBS)),
                        pltpu.VMEM((nb, rows, LANES), F32),
                        pltpu.VMEM((nb, t_new, SB_WIDTH), F32)],
        compiler_params=pltpu.CompilerParams(
            dimension_semantics=("arbitrary",),
            vmem_limit_bytes=max(_vmem_limit(block_bytes, temp_bytes),
                                 VMEM_BYTES_V7X - _nbytes(cache_k.shape, F32))),
        name="sb_attn_sample",
    )(q, k_new, v_new, gsb, uu, cache_k, cache_v)


def _outproj_kernel(x_ref, osb_ref, osgu_ref, p_ref, wo_ref, pg_ref, wg_ref, wp_ref, y_ref):
    h = (x_ref[0]
         + jnp.dot(osb_ref[0], wo_ref[:SB_WIDTH], preferred_element_type=F32)
         + jnp.dot(osgu_ref[0], wo_ref[SB_WIDTH:], preferred_element_type=F32))
    ms = jnp.mean(h * h, axis=-1, keepdims=True)
    hn = (h * lax.rsqrt(ms + EPS) * pg_ref[...]).astype(BF16)
    gate_logit = jnp.dot(hn, wg_ref[...], preferred_element_type=F32)
    gate = 1.0 / (1.0 + jnp.exp(-gate_logit))
    pp = jnp.dot(p_ref[0].astype(BF16), wp_ref[...], preferred_element_type=F32)
    y_ref[0] = h + gate * pp


def _outproj(x, osb, osgu, p, w_out_bf, ple_norm_g, w_gate_bf, w_proj_bf, *, tm, name):
    b, t, _ = x.shape
    assert t % tm == 0
    tok = lambda w: pl.BlockSpec((1, tm, w), lambda bi, i: (bi, i, 0))
    const2 = lambda a: pl.BlockSpec(a.shape, lambda bi, i: (0, 0))
    pg = ple_norm_g.reshape(1, D_MODEL)
    block_bytes = (2 * _nbytes((tm, D_MODEL), F32) + 2 * _nbytes((tm, SEG), BF16)
                   + _nbytes((tm, PLE_DIM), F32) + _nbytes(w_out_bf.shape, BF16)
                   + _nbytes(w_gate_bf.shape, BF16) + _nbytes(w_proj_bf.shape, BF16))
    temp_bytes = 6 * _nbytes((tm, D_MODEL), F32)
    return pl.pallas_call(
        _outproj_kernel,
        grid=(b, t // tm),
        in_specs=[tok(D_MODEL), tok(SEG), tok(SEG), tok(PLE_DIM), const2(w_out_bf), const2(pg),
                  const2(w_gate_bf), const2(w_proj_bf)],
        out_specs=tok(D_MODEL),
        out_shape=jax.ShapeDtypeStruct((b, t, D_MODEL), F32),
        compiler_params=pltpu.CompilerParams(
            dimension_semantics=("arbitrary", "arbitrary"),
            vmem_limit_bytes=_vmem_limit(block_bytes, temp_bytes)),
        name=name,
    )(x, osb, osgu, p, w_out_bf, pg, w_gate_bf, w_proj_bf)


def _sgu_bias_rows(sgu_b_l, period):
    per_pos = jnp.tile(sgu_b_l[:, :period].T, (SGU_CHUNK // period, 1))
    return jnp.repeat(per_pos, GROUP_W, axis=1)


def kernel(x_prompt, x_sample, cache_k, cache_v, p_prompt, p_sample, norm_g, w_in, q_norm_g,
           k_norm_g, sgu_norm_g, sgu_w, sgu_b, w_out, ple_norm_g, w_ple_gate, w_ple_proj):
    depth = w_in.shape[0]
    assert depth == 1, "one layer per call"
    l = 0
    b_p, t_p, _ = x_prompt.shape
    b_s, t_s, _ = x_sample.shape
    past = cache_k.shape[2]
    n_s = b_s * t_s
    assert n_s == SGU_CHUNK and SGU_CHUNK % t_s == 0

    w_in_bf = w_in[l].astype(BF16)
    w_out_bf = w_out[l].astype(BF16)
    w_gate_bf = w_ple_gate[l].astype(BF16)
    w_proj_bf = w_ple_proj[l].astype(BF16)

    q, k_feat, v_feat, kt, vb, gsb, osgu = _inproj(
        x_prompt, norm_g[l], w_in_bf, q_norm_g[l], k_norm_g[l], sgu_norm_g[l],
        sgu_w[l], _sgu_bias_rows(sgu_b[l], SGU_CHUNK),
        tm=TM_IN, period=SGU_CHUNK, attn_layout=True)
    osb = _attn_prompt(q, kt, vb, gsb, tq=TQ, hps=HEADS_PER_STEP)
    y_prompt = _outproj(x_prompt, osb, osgu, p_prompt[l], w_out_bf, ple_norm_g[l], w_gate_bf,
                        w_proj_bf, tm=TM_OUT, name="outproj_prompt")

    rep = SGU_CHUNK // t_s
    sgu_w_s = jnp.tile(sgu_w[l][:, :t_s, :t_s], (1, rep, rep))
    xs = x_sample.reshape(1, n_s, D_MODEL)
    q_s, k_s, v_s, gsb_s, osgu_s, vs_s = _inproj(
        xs, norm_g[l], w_in_bf, q_norm_g[l], k_norm_g[l], sgu_norm_g[l],
        sgu_w_s, _sgu_bias_rows(sgu_b[l], t_s),
        tm=n_s, period=t_s, attn_layout=False)
    shp = (b_s, t_s, SB_WIDTH)
    to_feat = lambda c: jnp.transpose(c[l], (0, 2, 3, 1))
    osb_s = _attn_sample(q_s.reshape(shp), to_feat(cache_k), to_feat(cache_v),
                         k_s.reshape(shp), v_s.reshape(shp),
                         gsb_s.reshape(shp))
    y_sample = _outproj(xs, osb_s.reshape(1, n_s, SB_WIDTH), osgu_s, p_sample[l].reshape(1, n_s, PLE_DIM),
                        w_out_bf, ple_norm_g[l], w_gate_bf, w_proj_bf, tm=n_s,
                        name="outproj_sample").reshape(b_s, t_s, D_MODEL)

    head_shape = lambda a, bb, tt: a.reshape(1, bb, tt, HEADS, HEAD_DIM)
    from_feat = lambda a: jnp.transpose(a.reshape(b_p, HEADS, HEAD_DIM, t_p), (0, 3, 1, 2))[None]
    return (y_prompt, y_sample,
            from_feat(k_feat), from_feat(v_feat),
            head_shape(k_s, b_s, t_s), head_shape(v_s, b_s, t_s),
            vs_s.reshape(1, b_s, t_s, GROUPS, GROUP_W))
```

```python
import functools
import math

import numpy as np
import jax
import jax.numpy as jnp
from jax import lax
from jax.experimental import pallas as pl
from jax.experimental.pallas import tpu as pltpu

F32 = jnp.float32
BF16 = jnp.bfloat16

LANES = 128
MXU_TILE = 256
VMEM_BYTES_V7X = 64 * 1024 * 1024

D_MODEL = 1024
PLE_DIM = 256
HEADS = 8
HEAD_DIM = 64
SB_WIDTH = HEADS * HEAD_DIM
GROUPS = 4
GROUP_W = 128
SGU_WIDTH = GROUPS * GROUP_W
SGU_CHUNK = 128
SEG = 512
N_SEG = 7
EPS = 1e-6

Q_SCALE = HEAD_DIM ** -0.5 * math.log2(math.e)

KSUB = LANES
NSUB = 2
TQ = NSUB * KSUB
HEADS_PER_STEP = 8
REM_DEAD_LOG2 = -152.0
SOFTPLUS_LINEAR = 64.0
EAGER_CACHE_SUBS = 2
TM_IN = 1024
TM_OUT = 1024


def _vmem_limit(block_bytes, temp_bytes):
    need = 2 * block_bytes + temp_bytes
    return int(min(need, VMEM_BYTES_V7X - 8 * 1024 * 1024))


def _nbytes(shape, dtype):
    return int(np.prod(shape)) * jnp.dtype(dtype).itemsize


def _suffix_sum_matrix():
    j = np.arange(KSUB)[:, None]
    s = np.arange(KSUB)[None, :]
    one = np.concatenate([np.where(j >= s, -1.0, 0.0), -np.ones((KSUB, LANES))], axis=1)
    return jnp.asarray(np.concatenate([one, one], axis=0), dtype=BF16)


def _head_mean_matrix():
    a = np.arange(MXU_TILE)
    bd = np.where(a[:, None] // HEAD_DIM == a[None, :] // HEAD_DIM, 1.0 / HEAD_DIM, 0.0)
    return jnp.asarray(bd, dtype=BF16)


def _split_bf16(x):
    hi = x.astype(BF16)
    lo = (x - hi.astype(F32)).astype(BF16)
    return jnp.concatenate([hi, lo], axis=1)


def _gelu_tanh(x):
    return 0.5 * x * (1.0 + jnp.tanh(math.sqrt(2.0 / math.pi) * (x + 0.044715 * (x * x * x))))


def _silu(x):
    return x / (1.0 + jnp.exp(-x))


def _inproj_kernel(x_ref, ng_ref, w_ref, qg_ref, kg_ref, sg_ref, hm_ref, sw_ref, sb_ref,
                   *out_refs, tm, period, attn_layout):
    if attn_layout:
        q_ref, ktf_ref, vtf_ref, kt_ref, vb_ref, gsb_ref, osgu_ref = out_refs
    else:
        q_ref, k_ref, v_ref, gsb_ref, osgu_ref, vs_ref = out_refs

    x = x_ref[0]
    ms = jnp.mean(x * x, axis=-1, keepdims=True)
    xn = (x * lax.rsqrt(ms + EPS) * ng_ref[...]).astype(BF16)

    def seg(i):
        return jnp.dot(xn, w_ref[:, i * SEG:(i + 1) * SEG], preferred_element_type=F32)

    def head_rms(t, g):
        sq = (t * t).astype(BF16)
        ms_h = jnp.concatenate(
            [jnp.dot(sq[:, c:c + MXU_TILE], hm_ref[...], preferred_element_type=F32)
             for c in range(0, SB_WIDTH, MXU_TILE)], axis=1)
        return t * lax.rsqrt(ms_h + EPS) * g

    q = head_rms(seg(0), qg_ref[...])
    q_ref[0] = (q * Q_SCALE).astype(BF16)

    k = head_rms(seg(1), kg_ref[...])
    v = seg(2)
    if attn_layout:
        k_t = k.T
        ktf_ref[0] = k_t
        kt_ref[0] = k_t.astype(BF16)
        vtf_ref[0] = v.T
        vb_ref[0] = v.astype(BF16)
    else:
        k_ref[0] = k
        v_ref[0] = v

    gsb_ref[0] = _silu(seg(3)).astype(BF16)

    u = _gelu_tanh(seg(4))
    vs_raw = _gelu_tanh(seg(5))
    vs_groups = []
    for g in range(GROUPS):
        t = vs_raw[:, g * GROUP_W:(g + 1) * GROUP_W]
        ms_g = jnp.mean(t * t, axis=-1, keepdims=True)
        vs_groups.append(t * lax.rsqrt(ms_g + EPS) * sg_ref[:, g * GROUP_W:(g + 1) * GROUP_W])
    if not attn_layout:
        vs_ref[0] = jnp.concatenate(vs_groups, axis=1)

    row = lax.broadcasted_iota(jnp.int32, (SGU_CHUNK, SGU_CHUNK), 0)
    col = lax.broadcasted_iota(jnp.int32, (SGU_CHUNK, SGU_CHUNK), 1)
    keep = row >= col
    if period != SGU_CHUNK:
        keep = keep & ((row // period) == (col // period))
    s_groups = []
    for g in range(GROUPS):
        wm = jnp.where(keep, sw_ref[g], 0.0).astype(BF16)
        vg = vs_groups[g].astype(BF16)
        chunks = [jnp.dot(wm, vg[c * SGU_CHUNK:(c + 1) * SGU_CHUNK], preferred_element_type=F32)
                  + sb_ref[:, g * GROUP_W:(g + 1) * GROUP_W]
                  for c in range(tm // SGU_CHUNK)]
        s_groups.append(jnp.concatenate(chunks, axis=0) if len(chunks) > 1 else chunks[0])
    s = jnp.concatenate(s_groups, axis=1)

    osgu_ref[0] = (u * s * _silu(seg(6))).astype(BF16)


def _inproj(x, norm_g, w_in_bf, q_norm_g, k_norm_g, sgu_norm_g, sgu_w_tiled, sgu_bias, *,
            tm, period, attn_layout):
    b, t, _ = x.shape
    assert t % tm == 0 and tm % SGU_CHUNK == 0
    grid = (b, t // tm)
    tok = lambda w: pl.BlockSpec((1, tm, w), lambda bi, i: (bi, i, 0))
    const2 = lambda a: pl.BlockSpec(a.shape, lambda bi, i: (0, 0))
    const3 = lambda a: pl.BlockSpec(a.shape, lambda bi, i: (0, 0, 0))

    ng = norm_g.reshape(1, D_MODEL)
    qg = jnp.tile(q_norm_g, HEADS).reshape(1, SB_WIDTH)
    kg = jnp.tile(k_norm_g, HEADS).reshape(1, SB_WIDTH)
    sg = sgu_norm_g.reshape(1, SGU_WIDTH)
    hm = _head_mean_matrix()

    act = lambda dt: jax.ShapeDtypeStruct((b, t, SEG), dt)
    if attn_layout:
        feat = lambda dt: jax.ShapeDtypeStruct((b, SB_WIDTH, t), dt)
        feat_blk = pl.BlockSpec((1, SB_WIDTH, tm), lambda bi, i: (bi, 0, i))
        out_shape = (act(BF16), feat(F32), feat(F32), feat(BF16), act(BF16), act(BF16), act(BF16))
        out_specs = (tok(SEG), feat_blk, feat_blk, feat_blk, tok(SEG), tok(SEG), tok(SEG))
    else:
        out_shape = (act(BF16), act(F32), act(F32), act(BF16), act(BF16), act(F32))
        out_specs = (tok(SEG),) * 6

    block_bytes = (_nbytes((tm, D_MODEL), F32) + _nbytes(w_in_bf.shape, BF16)
                   + _nbytes(hm.shape, BF16) + _nbytes(sgu_w_tiled.shape, F32)
                   + _nbytes(sgu_bias.shape, F32) + 7 * _nbytes((tm, SEG), F32))
    temp_bytes = 12 * _nbytes((tm, SEG), F32)
    return pl.pallas_call(
        functools.partial(_inproj_kernel, tm=tm, period=period, attn_layout=attn_layout),
        grid=grid,
        in_specs=[tok(D_MODEL), const2(ng), const2(w_in_bf), const2(qg), const2(kg), const2(sg),
                  const2(hm), const3(sgu_w_tiled), const2(sgu_bias)],
        out_specs=out_specs,
        out_shape=out_shape,
        compiler_params=pltpu.CompilerParams(
            dimension_semantics=("arbitrary", "arbitrary"),
            vmem_limit_bytes=_vmem_limit(block_bytes, temp_bytes)),
        name="inproj_prompt" if attn_layout else "inproj_sample",
    )(x, ng, w_in_bf, qg, kg, sg, hm, sgu_w_tiled, sgu_bias)


def _neg_log2_keep(z, vis):
    sp = jnp.maximum(z, jnp.log2(1.0 + jnp.exp2(jnp.minimum(z, SOFTPLUS_LINEAR))))
    return sp if vis is None else jnp.where(vis, sp, 0.0)


def _sb_weights(z, vis, sums, rem):
    w = jnp.exp2(z + sums[:, :KSUB] + rem)
    if vis is not None:
        w = jnp.where(vis, w, 0.0)
    return w.astype(BF16), rem + sums[:, KSUB:]


def _attn_prompt_kernel(q_ref, g_ref, uu_ref, kt_hbm, v_hbm, o_ref,
                        kwin, vwin, kold, vold, sem, rem_ref, acc_ref, *, tq, hps, nb, nq):
    qi = pl.program_id(0)
    slot = qi % 2
    nv = nb * hps

    def window(step, slot_):
        first = pl.multiple_of(jnp.maximum(step - 1, 0) * tq, tq)
        return ([pltpu.make_async_copy(kt_hbm.at[r, :, pl.ds(first, 2 * tq)], kwin.at[slot_, r],
                                       sem.at[0, slot_]) for r in range(nb)]
                + [pltpu.make_async_copy(v_hbm.at[r, pl.ds(first, 2 * tq), :], vwin.at[slot_, r],
                                         sem.at[1, slot_]) for r in range(nb)])

    def older(n):
        first = pl.multiple_of((qi - n) * tq, tq)
        return ([pltpu.make_async_copy(kt_hbm.at[r, :, pl.ds(first, tq)], kold.at[r], sem.at[0, 2])
                 for r in range(nb)]
                + [pltpu.make_async_copy(v_hbm.at[r, pl.ds(first, tq), :], vold.at[r], sem.at[1, 2])
                   for r in range(nb)])

    @pl.when(qi == 0)
    def _():
        for c in window(qi, slot):
            c.start()

    for c in window(qi, slot):
        c.wait()

    @pl.when(qi + 1 < nq)
    def _():
        for c in window(qi + 1, 1 - slot):
            c.start()

    qh = [q_ref[v // hps][:, (v % hps) * HEAD_DIM:(v % hps + 1) * HEAD_DIM] for v in range(nv)]
    uu = uu_ref[...]
    lane = lax.broadcasted_iota(jnp.int32, (KSUB, LANES), 1)
    first_head = lane < HEAD_DIM

    def keys_at(src, r, ks):
        return (kwin[slot, r, :, pl.ds(ks, KSUB)] if src == "window"
                else kold[r, :, pl.ds(ks, KSUB)])

    def pair_values(src, ks, p):
        pairs = hps // 2
        lanes = pl.ds((p % pairs) * LANES, LANES)
        vv = (vwin[slot, p // pairs, pl.ds(ks, KSUB), lanes] if src == "window"
              else vold[p // pairs, pl.ds(ks, KSUB), lanes])
        zero = jnp.zeros_like(vv)
        return jnp.concatenate([jnp.where(first_head, vv, zero),
                                jnp.where(first_head, zero, vv)], axis=0)

    def sweep(tiles, rems, accs):
        def put(full, r0, r1, part):
            pieces = ([full[:r0]] if r0 else []) + [part] + ([full[r1:]] if r1 < tq else [])
            return jnp.concatenate(pieces, axis=0) if len(pieces) > 1 else part

        zs, splits = [], []
        for src, ks, r0, r1, diagonal in tiles:
            kts = [keys_at(src, r, ks) for r in range(nb)]
            vis = None
            if diagonal:
                row = lax.broadcasted_iota(jnp.int32, (r1 - r0, KSUB), 0)
                col = lax.broadcasted_iota(jnp.int32, (r1 - r0, KSUB), 1)
                vis = col < row
            for v in range(nv):
                h = v % hps
                z = jnp.dot(qh[v][r0:r1], kts[v // hps][h * HEAD_DIM:(h + 1) * HEAD_DIM],
                            preferred_element_type=F32)
                zs.append((z, vis))
                splits.append(_split_bf16(_neg_log2_keep(z, vis)))
        sums = jnp.dot(jnp.concatenate(splits, axis=0), uu, preferred_element_type=F32)
        rems, accs = list(rems), list(accs)
        off, i = 0, 0
        for src, ks, r0, r1, _ in tiles:
            ws = []
            for h in range(nv):
                z, vis = zs[i]
                i += 1
                w, new = _sb_weights(z, vis, sums[off:off + r1 - r0], rems[h][r0:r1])
                off += r1 - r0
                rems[h] = put(rems[h], r0, r1, new)
                ws.append(w)
            for p in range(nv // 2):
                d = jnp.dot(jnp.concatenate(ws[2 * p:2 * p + 2], axis=1), pair_values(src, ks, p),
                            preferred_element_type=F32)
                accs[p] = put(accs[p], r0, r1, accs[p][r0:r1] + d)
        return rems, accs

    def diagonal_tiles(at):
        return [("window", at + s * KSUB, s * KSUB, tq, True) for s in reversed(range(NSUB))]

    def key_block_tiles(src, at, rows=None):
        rows = rows or {s: (0, tq) for s in range(NSUB)}
        return [(src, at + s * KSUB, *rows[s], False) for s in reversed(range(NSUB)) if s in rows]

    zero_rems = [jnp.zeros((tq, LANES), F32)] * nv
    zero_accs = [jnp.zeros((tq, LANES), F32)] * (nv // 2)

    def finish(accs):
        pairs = hps // 2
        for r in range(nb):
            acc = jnp.concatenate(accs[r * pairs:(r + 1) * pairs], axis=1)
            o_ref[r] = (acc * g_ref[r].astype(F32)).astype(BF16)

    def any_alive(rems):
        most = functools.reduce(jnp.maximum, rems)
        return (jnp.max(most) > REM_DEAD_LOG2).astype(jnp.int32)

    @pl.when(qi == 0)
    def _():
        finish(sweep(diagonal_tiles(0), zero_rems, zero_accs)[1])

    def load_state():
        return [rem_ref[h] for h in range(nv)], [acc_ref[p] for p in range(nv // 2)]

    def store_state(rems, accs):
        for h in range(nv):
            rem_ref[h] = rems[h]
        for p in range(nv // 2):
            acc_ref[p] = accs[p]

    eager_rows = {s: min(tq, (s + 3 - NSUB) * KSUB) for s in range(NSUB) if s + 3 - NSUB > 0}

    @pl.when(qi > 0)
    def _():
        first = key_block_tiles("window", 0, {s: (0, e) for s, e in eager_rows.items()})
        rems, accs = sweep(diagonal_tiles(tq) + first, zero_rems, zero_accs)
        store_state(rems, accs)
        alive = any_alive(rems)
        for s in reversed(range(NSUB)):
            e = eager_rows.get(s, 0)
            if e < tq:
                @pl.when(any_alive([r[e:] for r in rems]) > 0)
                def _():
                    store_state(*sweep(key_block_tiles("window", 0, {s: (e, tq)}), *load_state()))

        def cond(c):
            n, alive = c
            return jnp.logical_and(n <= qi, alive > 0)

        def body(c):
            n, _ = c
            for cp in older(n):
                cp.start()
            for cp in older(n):
                cp.wait()
            rems, accs = sweep(key_block_tiles("older", 0), *load_state())
            store_state(rems, accs)
            return n + 1, any_alive(rems)

        lax.while_loop(cond, body, (jnp.int32(2), alive))
        finish(load_state()[1])


def _attn_prompt(q, kt, vb, gsb, *, tq, hps):
    nb, t, _ = q.shape
    assert t % tq == 0 and t >= 2 * tq and tq == NSUB * KSUB and hps == HEADS
    nv = nb * hps
    nq = t // tq
    uu = _suffix_sum_matrix()
    row_blk = pl.BlockSpec((nb, tq, SB_WIDTH), lambda i: (0, i, 0))
    in_hbm = pl.BlockSpec(memory_space=pltpu.HBM)
    block_bytes = 3 * _nbytes((nb, tq, SB_WIDTH), BF16) + _nbytes(uu.shape, BF16)
    temp_bytes = ((nv + nv // 2) * _nbytes((tq, LANES), F32)
                  + 5 * _nbytes((nb, SB_WIDTH, 2 * tq), BF16)
                  + 5 * (NSUB + 2) * nv * _nbytes((tq, KSUB), F32))
    return pl.pallas_call(
        functools.partial(_attn_prompt_kernel, tq=tq, hps=hps, nb=nb, nq=nq),
        grid=(nq,),
        in_specs=[row_blk, row_blk, pl.BlockSpec(uu.shape, lambda i: (0, 0)), in_hbm, in_hbm],
        out_specs=row_blk,
        out_shape=jax.ShapeDtypeStruct((nb, t, SB_WIDTH), BF16),
        scratch_shapes=[pltpu.VMEM((2, nb, SB_WIDTH, 2 * tq), BF16),
                        pltpu.VMEM((2, nb, 2 * tq, SB_WIDTH), BF16),
                        pltpu.VMEM((nb, SB_WIDTH, tq), BF16),
                        pltpu.VMEM((nb, tq, SB_WIDTH), BF16),
                        pltpu.SemaphoreType.DMA((2, 3)),
                        pltpu.VMEM((nv, tq, LANES), F32), pltpu.VMEM((nv // 2, tq, LANES), F32)],
        compiler_params=pltpu.CompilerParams(
            dimension_semantics=("arbitrary",),
            vmem_limit_bytes=_vmem_limit(block_bytes, temp_bytes)),
        name="sb_attn_prompt",
    )(q, gsb, uu, kt, vb)


def _attn_sample_kernel(q_ref, kn_ref, vn_ref, g_ref, uu_ref, ck_hbm, cv_hbm, o_ref,
                        kwin, vwin, sem, rem_s, acc_s, *, t_new, past, nb):
    rows = HEADS * t_new
    uu = uu_ref[...]
    n_sub = past // KSUB
    contract_last = (((1,), (1,)), ((), ()))

    def fetch(which, b, j, slot):
        src, dst = ((ck_hbm, kwin), (cv_hbm, vwin))[which]
        first = pl.multiple_of(j * KSUB, KSUB)
        return pltpu.make_async_copy(src.at[b, :, :, pl.ds(first, KSUB)], dst.at[b, slot],
                                     sem.at[which, b, slot])

    for b in range(nb):
        for slot in range(EAGER_CACHE_SUBS):
            for which in range(2):
                fetch(which, b, n_sub - 1 - slot, slot).start()

    row = lax.broadcasted_iota(jnp.int32, (rows, KSUB), 0)
    col = lax.broadcasted_iota(jnp.int32, (rows, KSUB), 1)
    new_vis = col < (row % t_new)
    pad = jnp.zeros((KSUB - t_new, SB_WIDTH), F32)

    def new_block(ref, b):
        blk = jnp.concatenate([ref[b], pad], axis=0).astype(BF16)
        return [blk[:, h * HEAD_DIM:(h + 1) * HEAD_DIM] for h in range(HEADS)]

    def cached(win, b, slot):
        return [win[b, slot, h].astype(BF16) for h in range(HEADS)]

    def sweep(b, tiles, rem, acc):
        q = q_ref[b]
        qh = [q[:, h * HEAD_DIM:(h + 1) * HEAD_DIM] for h in range(HEADS)]
        zs = [jnp.concatenate(
            [jnp.dot(qh[h], keys[h], preferred_element_type=F32) if dim_major else
             lax.dot_general(qh[h], keys[h], contract_last, preferred_element_type=F32)
             for h in range(HEADS)], axis=0) for keys, _, dim_major, _ in tiles]
        splits = [_split_bf16(_neg_log2_keep(z, t[3])) for z, t in zip(zs, tiles)]
        sums = jnp.dot(jnp.concatenate(splits, axis=0), uu, preferred_element_type=F32)
        for i, (z, (_, vals, dim_major, vis)) in enumerate(zip(zs, tiles)):
            w, rem = _sb_weights(z, vis, sums[i * rows:(i + 1) * rows], rem)
            wh = [w[h * t_new:(h + 1) * t_new] for h in range(HEADS)]
            acc = [a + (lax.dot_general(wh[h], vals[h], contract_last, preferred_element_type=F32)
                        if dim_major else jnp.dot(wh[h], vals[h], preferred_element_type=F32))
                   for h, a in enumerate(acc)]
        return rem, acc

    def alive_of(rem):
        return (jnp.max(rem) > REM_DEAD_LOG2).astype(jnp.int32)

    alive = jnp.int32(0)
    for b in range(nb):
        for slot in range(EAGER_CACHE_SUBS):
            for which in range(2):
                fetch(which, b, n_sub - 1 - slot, slot).wait()
        tiles = [(new_block(kn_ref, b), new_block(vn_ref, b), False, new_vis)]
        tiles += [(cached(kwin, b, s), cached(vwin, b, s), True, None)
                  for s in range(EAGER_CACHE_SUBS)]
        rem, acc = sweep(b, tiles, jnp.zeros((rows, LANES), F32),
                         [jnp.zeros((t_new, HEAD_DIM), F32)] * HEADS)
        rem_s[b] = rem
        acc_s[b] = jnp.concatenate(acc, axis=1)
        alive = jnp.maximum(alive, alive_of(rem))

    def cond(c):
        j, alive = c
        return jnp.logical_and(j >= 0, alive > 0)

    def body(c):
        j, _ = c
        for b in range(nb):
            for which in range(2):
                fetch(which, b, j, 0).start()
        alive = jnp.int32(0)
        for b in range(nb):
            for which in range(2):
                fetch(which, b, j, 0).wait()
            acc_b = acc_s[b]
            rem, acc = sweep(b, [(cached(kwin, b, 0), cached(vwin, b, 0), True, None)], rem_s[b],
                             [acc_b[:, h * HEAD_DIM:(h + 1) * HEAD_DIM] for h in range(HEADS)])
            rem_s[b] = rem
            acc_s[b] = jnp.concatenate(acc, axis=1)
            alive = jnp.maximum(alive, alive_of(rem))
        return j - 1, alive

    lax.while_loop(cond, body, (jnp.int32(n_sub - 1 - EAGER_CACHE_SUBS), alive))
    for b in range(nb):
        o_ref[b] = (acc_s[b] * g_ref[b].astype(F32)).astype(BF16)


def _attn_sample(q, cache_k, cache_v, k_new, v_new, gsb):
    nb, t_new, _ = q.shape
    past = cache_k.shape[3]
    assert cache_k.shape == (nb, HEADS, HEAD_DIM, past)
    assert past % KSUB == 0 and past // KSUB >= EAGER_CACHE_SUBS and t_new <= KSUB
    uu = _suffix_sum_matrix()
    rows = HEADS * t_new
    whole = lambda a: pl.BlockSpec(a.shape, lambda i: (0,) * a.ndim)
    window = (nb, EAGER_CACHE_SUBS, HEADS, HEAD_DIM, KSUB)
    block_bytes = sum(_nbytes(a.shape, a.dtype) for a in (q, k_new, v_new, gsb, uu, q))
    temp_bytes = (2 * _nbytes(window, F32) + _nbytes((nb, rows, LANES), F32)
                  + _nbytes((nb, t_new, SB_WIDTH), F32)
                  + 16 * (EAGER_CACHE_SUBS + 1) * _nbytes((rows, KSUB), F32))
    return pl.pallas_call(
        functools.partial(_attn_sample_kernel, t_new=t_new, past=past, nb=nb),
        grid=(1,),
        in_specs=[whole(q), whole(k_new), whole(v_new), whole(gsb), whole(uu),
                  pl.BlockSpec(memory_space=pltpu.HBM), pl.BlockSpec(memory_space=pltpu.HBM)],
        out_specs=whole(q),
        out_shape=jax.ShapeDtypeStruct((nb, t_new, SB_WIDTH), BF16),
        scratch_shapes=[pltpu.VMEM(window, F32), pltpu.VMEM(window, F32),
                        pltpu.SemaphoreType.DMA((2, nb, EAGER_CACHE_SUBS)),
                        pltpu.VMEM((nb, rows, LANES), F32),
                        pltpu.VMEM((nb, t_new, SB_WIDTH), F32)],
        compiler_params=pltpu.CompilerParams(
            dimension_semantics=("arbitrary",),
            vmem_limit_bytes=max(_vmem_limit(block_bytes, temp_bytes),
                                 VMEM_BYTES_V7X - _nbytes(cache_k.shape, F32))),
        name="sb_attn_sample",
    )(q, k_new, v_new, gsb, uu, cache_k, cache_v)


def _outproj_kernel(x_ref, osb_ref, osgu_ref, p_ref, wo_ref, pg_ref, wg_ref, wp_ref, y_ref):
    h = (x_ref[0]
         + jnp.dot(osb_ref[0], wo_ref[:SB_WIDTH], preferred_element_type=F32)
         + jnp.dot(osgu_ref[0], wo_ref[SB_WIDTH:], preferred_element_type=F32))
    ms = jnp.mean(h * h, axis=-1, keepdims=True)
    hn = (h * lax.rsqrt(ms + EPS) * pg_ref[...]).astype(BF16)
    gate_logit = jnp.dot(hn, wg_ref[...], preferred_element_type=F32)
    gate = 1.0 / (1.0 + jnp.exp(-gate_logit))
    pp = jnp.dot(p_ref[0].astype(BF16), wp_ref[...], preferred_element_type=F32)
    y_ref[0] = h + gate * pp


def _outproj(x, osb, osgu, p, w_out_bf, ple_norm_g, w_gate_bf, w_proj_bf, *, tm, name):
    b, t, _ = x.shape
    assert t % tm == 0
    tok = lambda w: pl.BlockSpec((1, tm, w), lambda bi, i: (bi, i, 0))
    const2 = lambda a: pl.BlockSpec(a.shape, lambda bi, i: (0, 0))
    pg = ple_norm_g.reshape(1, D_MODEL)
    block_bytes = (2 * _nbytes((tm, D_MODEL), F32) + 2 * _nbytes((tm, SEG), BF16)
                   + _nbytes((tm, PLE_DIM), F32) + _nbytes(w_out_bf.shape, BF16)
                   + _nbytes(w_gate_bf.shape, BF16) + _nbytes(w_proj_bf.shape, BF16))
    temp_bytes = 6 * _nbytes((tm, D_MODEL), F32)
    return pl.pallas_call(
        _outproj_kernel,
        grid=(b, t // tm),
        in_specs=[tok(D_MODEL), tok(SEG), tok(SEG), tok(PLE_DIM), const2(w_out_bf), const2(pg),
                  const2(w_gate_bf), const2(w_proj_bf)],
        out_specs=tok(D_MODEL),
        out_shape=jax.ShapeDtypeStruct((b, t, D_MODEL), F32),
        compiler_params=pltpu.CompilerParams(
            dimension_semantics=("arbitrary", "arbitrary"),
            vmem_limit_bytes=_vmem_limit(block_bytes, temp_bytes)),
        name=name,
    )(x, osb, osgu, p, w_out_bf, pg, w_gate_bf, w_proj_bf)


def _sgu_bias_rows(sgu_b_l, period):
    per_pos = jnp.tile(sgu_b_l[:, :period].T, (SGU_CHUNK // period, 1))
    return jnp.repeat(per_pos, GROUP_W, axis=1)


def kernel(x_prompt, x_sample, cache_k, cache_v, p_prompt, p_sample, norm_g, w_in, q_norm_g,
           k_norm_g, sgu_norm_g, sgu_w, sgu_b, w_out, ple_norm_g, w_ple_gate, w_ple_proj):
    depth = w_in.shape[0]
    assert depth == 1, "one layer per call"
    l = 0
    b_p, t_p, _ = x_prompt.shape
    b_s, t_s, _ = x_sample.shape
    past = cache_k.shape[2]
    n_s = b_s * t_s
    assert n_s == SGU_CHUNK and SGU_CHUNK % t_s == 0

    w_in_bf = w_in[l].astype(BF16)
    w_out_bf = w_out[l].astype(BF16)
    w_gate_bf = w_ple_gate[l].astype(BF16)
    w_proj_bf = w_ple_proj[l].astype(BF16)

    q, k_feat, v_feat, kt, vb, gsb, osgu = _inproj(
        x_prompt, norm_g[l], w_in_bf, q_norm_g[l], k_norm_g[l], sgu_norm_g[l],
        sgu_w[l], _sgu_bias_rows(sgu_b[l], SGU_CHUNK),
        tm=TM_IN, period=SGU_CHUNK, attn_layout=True)
    osb = _attn_prompt(q, kt, vb, gsb, tq=TQ, hps=HEADS_PER_STEP)
    y_prompt = _outproj(x_prompt, osb, osgu, p_prompt[l], w_out_bf, ple_norm_g[l], w_gate_bf,
                        w_proj_bf, tm=TM_OUT, name="outproj_prompt")

    rep = SGU_CHUNK // t_s
    sgu_w_s = jnp.tile(sgu_w[l][:, :t_s, :t_s], (1, rep, rep))
    xs = x_sample.reshape(1, n_s, D_MODEL)
    q_s, k_s, v_s, gsb_s, osgu_s, vs_s = _inproj(
        xs, norm_g[l], w_in_bf, q_norm_g[l], k_norm_g[l], sgu_norm_g[l],
        sgu_w_s, _sgu_bias_rows(sgu_b[l], t_s),
        tm=n_s, period=t_s, attn_layout=False)
    shp = (b_s, t_s, SB_WIDTH)
    to_feat = lambda c: jnp.transpose(c[l], (0, 2, 3, 1))
    osb_s = _attn_sample(q_s.reshape(shp), to_feat(cache_k), to_feat(cache_v),
                         k_s.reshape(shp), v_s.reshape(shp),
                         gsb_s.reshape(shp))
    y_sample = _outproj(xs, osb_s.reshape(1, n_s, SB_WIDTH), osgu_s, p_sample[l].reshape(1, n_s, PLE_DIM),
                        w_out_bf, ple_norm_g[l], w_gate_bf, w_proj_bf, tm=n_s,
                        name="outproj_sample").reshape(b_s, t_s, D_MODEL)

    head_shape = lambda a, bb, tt: a.reshape(1, bb, tt, HEADS, HEAD_DIM)
    from_feat = lambda a: jnp.transpose(a.reshape(b_p, HEADS, HEAD_DIM, t_p), (0, 3, 1, 2))[None]
    return (y_prompt, y_sample,
            from_feat(k_feat), from_feat(v_feat),
            head_shape(k_s, b_s, t_s), head_shape(v_s, b_s, t_s),
            vs_s.reshape(1, b_s, t_s, GROUPS, GROUP_W))
```

```python
import functools
import math

import numpy as np
import jax
import jax.numpy as jnp
from jax import lax
from jax.experimental import pallas as pl
from jax.experimental.pallas import tpu as pltpu

F32 = jnp.float32
BF16 = jnp.bfloat16

LANES = 128
MXU_TILE = 256
VMEM_BYTES_V7X = 64 * 1024 * 1024

D_MODEL = 1024
PLE_DIM = 256
HEADS = 8
HEAD_DIM = 64
SB_WIDTH = HEADS * HEAD_DIM
GROUPS = 4
GROUP_W = 128
SGU_WIDTH = GROUPS * GROUP_W
SGU_CHUNK = 128
SEG = 512
N_SEG = 7
EPS = 1e-6

Q_SCALE = HEAD_DIM ** -0.5 * math.log2(math.e)

KSUB = LANES
NSUB = 2
TQ = NSUB * KSUB
HEADS_PER_STEP = 8
REM_DEAD_LOG2 = -152.0
SOFTPLUS_LINEAR = 64.0
EAGER_CACHE_SUBS = 2
EAGER_SEEN = KSUB + 48
TM_IN = 1024
TM_OUT = 1024


def _vmem_limit(block_bytes, temp_bytes):
    need = 2 * block_bytes + temp_bytes
    return int(min(need, VMEM_BYTES_V7X - 8 * 1024 * 1024))


def _nbytes(shape, dtype):
    return int(np.prod(shape)) * jnp.dtype(dtype).itemsize


def _suffix_sum_matrix():
    j = np.arange(KSUB)[:, None]
    s = np.arange(KSUB)[None, :]
    one = np.concatenate([np.where(j >= s, -1.0, 0.0), -np.ones((KSUB, LANES))], axis=1)
    return jnp.asarray(np.concatenate([one, one], axis=0), dtype=BF16)


def _head_mean_matrix():
    a = np.arange(MXU_TILE)
    bd = np.where(a[:, None] // HEAD_DIM == a[None, :] // HEAD_DIM, 1.0 / HEAD_DIM, 0.0)
    return jnp.asarray(bd, dtype=BF16)


def _split_bf16(x):
    hi = x.astype(BF16)
    lo = (x - hi.astype(F32)).astype(BF16)
    return jnp.concatenate([hi, lo], axis=1)


def _gelu_tanh(x):
    return 0.5 * x * (1.0 + jnp.tanh(math.sqrt(2.0 / math.pi) * (x + 0.044715 * (x * x * x))))


def _silu(x):
    return x / (1.0 + jnp.exp(-x))


def _inproj_kernel(x_ref, ng_ref, w_ref, qg_ref, kg_ref, sg_ref, hm_ref, sw_ref, sb_ref,
                   *out_refs, tm, period, attn_layout):
    if attn_layout:
        q_ref, ktf_ref, vtf_ref, kt_ref, vb_ref, gsb_ref, osgu_ref = out_refs
    else:
        q_ref, k_ref, v_ref, gsb_ref, osgu_ref, vs_ref = out_refs

    x = x_ref[0]
    ms = jnp.mean(x * x, axis=-1, keepdims=True)
    xn = (x * lax.rsqrt(ms + EPS) * ng_ref[...]).astype(BF16)

    def seg(i):
        return jnp.dot(xn, w_ref[:, i * SEG:(i + 1) * SEG], preferred_element_type=F32)

    def head_rms(t, g):
        sq = (t * t).astype(BF16)
        ms_h = jnp.concatenate(
            [jnp.dot(sq[:, c:c + MXU_TILE], hm_ref[...], preferred_element_type=F32)
             for c in range(0, SB_WIDTH, MXU_TILE)], axis=1)
        return t * lax.rsqrt(ms_h + EPS) * g

    q = head_rms(seg(0), qg_ref[...])
    q_ref[0] = (q * Q_SCALE).astype(BF16)

    k = head_rms(seg(1), kg_ref[...])
    v = seg(2)
    if attn_layout:
        k_t = k.T
        ktf_ref[0] = k_t
        kt_ref[0] = k_t.astype(BF16)
        vtf_ref[0] = v.T
        vb_ref[0] = v.astype(BF16)
    else:
        k_ref[0] = k
        v_ref[0] = v

    gsb_ref[0] = _silu(seg(3)).astype(BF16)

    u = _gelu_tanh(seg(4))
    vs_raw = _gelu_tanh(seg(5))
    vs_groups = []
    for g in range(GROUPS):
        t = vs_raw[:, g * GROUP_W:(g + 1) * GROUP_W]
        ms_g = jnp.mean(t * t, axis=-1, keepdims=True)
        vs_groups.append(t * lax.rsqrt(ms_g + EPS) * sg_ref[:, g * GROUP_W:(g + 1) * GROUP_W])
    if not attn_layout:
        vs_ref[0] = jnp.concatenate(vs_groups, axis=1)

    row = lax.broadcasted_iota(jnp.int32, (SGU_CHUNK, SGU_CHUNK), 0)
    col = lax.broadcasted_iota(jnp.int32, (SGU_CHUNK, SGU_CHUNK), 1)
    keep = row >= col
    if period != SGU_CHUNK:
        keep = keep & ((row // period) == (col // period))
    s_groups = []
    for g in range(GROUPS):
        wm = jnp.where(keep, sw_ref[g], 0.0).astype(BF16)
        vg = vs_groups[g].astype(BF16)
        chunks = [jnp.dot(wm, vg[c * SGU_CHUNK:(c + 1) * SGU_CHUNK], preferred_element_type=F32)
                  + sb_ref[:, g * GROUP_W:(g + 1) * GROUP_W]
                  for c in range(tm // SGU_CHUNK)]
        s_groups.append(jnp.concatenate(chunks, axis=0) if len(chunks) > 1 else chunks[0])
    s = jnp.concatenate(s_groups, axis=1)

    osgu_ref[0] = (u * s * _silu(seg(6))).astype(BF16)


def _inproj(x, norm_g, w_in_bf, q_norm_g, k_norm_g, sgu_norm_g, sgu_w_tiled, sgu_bias, *,
            tm, period, attn_layout):
    b, t, _ = x.shape
    assert t % tm == 0 and tm % SGU_CHUNK == 0
    grid = (b, t // tm)
    tok = lambda w: pl.BlockSpec((1, tm, w), lambda bi, i: (bi, i, 0))
    const2 = lambda a: pl.BlockSpec(a.shape, lambda bi, i: (0, 0))
    const3 = lambda a: pl.BlockSpec(a.shape, lambda bi, i: (0, 0, 0))

    ng = norm_g.reshape(1, D_MODEL)
    qg = jnp.tile(q_norm_g, HEADS).reshape(1, SB_WIDTH)
    kg = jnp.tile(k_norm_g, HEADS).reshape(1, SB_WIDTH)
    sg = sgu_norm_g.reshape(1, SGU_WIDTH)
    hm = _head_mean_matrix()

    act = lambda dt: jax.ShapeDtypeStruct((b, t, SEG), dt)
    if attn_layout:
        feat = lambda dt: jax.ShapeDtypeStruct((b, SB_WIDTH, t), dt)
        feat_blk = pl.BlockSpec((1, SB_WIDTH, tm), lambda bi, i: (bi, 0, i))
        out_shape = (act(BF16), feat(F32), feat(F32), feat(BF16), act(BF16), act(BF16), act(BF16))
        out_specs = (tok(SEG), feat_blk, feat_blk, feat_blk, tok(SEG), tok(SEG), tok(SEG))
    else:
        out_shape = (act(BF16), act(F32), act(F32), act(BF16), act(BF16), act(F32))
        out_specs = (tok(SEG),) * 6

    block_bytes = (_nbytes((tm, D_MODEL), F32) + _nbytes(w_in_bf.shape, BF16)
                   + _nbytes(hm.shape, BF16) + _nbytes(sgu_w_tiled.shape, F32)
                   + _nbytes(sgu_bias.shape, F32) + 7 * _nbytes((tm, SEG), F32))
    temp_bytes = 12 * _nbytes((tm, SEG), F32)
    return pl.pallas_call(
        functools.partial(_inproj_kernel, tm=tm, period=period, attn_layout=attn_layout),
        grid=grid,
        in_specs=[tok(D_MODEL), const2(ng), const2(w_in_bf), const2(qg), const2(kg), const2(sg),
                  const2(hm), const3(sgu_w_tiled), const2(sgu_bias)],
        out_specs=out_specs,
        out_shape=out_shape,
        compiler_params=pltpu.CompilerParams(
            dimension_semantics=("arbitrary", "arbitrary"),
            vmem_limit_bytes=_vmem_limit(block_bytes, temp_bytes)),
        name="inproj_prompt" if attn_layout else "inproj_sample",
    )(x, ng, w_in_bf, qg, kg, sg, hm, sgu_w_tiled, sgu_bias)


def _neg_log2_keep(z, vis):
    sp = jnp.maximum(z, jnp.log2(1.0 + jnp.exp2(jnp.minimum(z, SOFTPLUS_LINEAR))))
    return sp if vis is None else jnp.where(vis, sp, 0.0)


def _sb_weights(z, vis, sums, rem):
    w = jnp.exp2(z + sums[:, :KSUB] + rem)
    if vis is not None:
        w = jnp.where(vis, w, 0.0)
    return w.astype(BF16), rem + sums[:, KSUB:]


def _attn_prompt_kernel(q_ref, g_ref, uu_ref, kt_hbm, v_hbm, o_ref,
                        kwin, vwin, kold, vold, sem, rem_ref, acc_ref, *, tq, hps, nb, nq):
    qi = pl.program_id(0)
    slot = qi % 2
    nv = nb * hps

    def window(step, slot_):
        first = pl.multiple_of(jnp.maximum(step - 1, 0) * tq, tq)
        return ([pltpu.make_async_copy(kt_hbm.at[r, :, pl.ds(first, 2 * tq)], kwin.at[slot_, r],
                                       sem.at[0, slot_]) for r in range(nb)]
                + [pltpu.make_async_copy(v_hbm.at[r, pl.ds(first, 2 * tq), :], vwin.at[slot_, r],
                                         sem.at[1, slot_]) for r in range(nb)])

    def older(n):
        first = pl.multiple_of((qi - n) * tq, tq)
        return ([pltpu.make_async_copy(kt_hbm.at[r, :, pl.ds(first, tq)], kold.at[r], sem.at[0, 2])
                 for r in range(nb)]
                + [pltpu.make_async_copy(v_hbm.at[r, pl.ds(first, tq), :], vold.at[r], sem.at[1, 2])
                   for r in range(nb)])

    @pl.when(qi == 0)
    def _():
        for c in window(qi, slot):
            c.start()

    for c in window(qi, slot):
        c.wait()

    @pl.when(qi + 1 < nq)
    def _():
        for c in window(qi + 1, 1 - slot):
            c.start()

    qh = [q_ref[v // hps][:, (v % hps) * HEAD_DIM:(v % hps + 1) * HEAD_DIM] for v in range(nv)]
    uu = uu_ref[...]
    lane = lax.broadcasted_iota(jnp.int32, (KSUB, LANES), 1)
    first_head = lane < HEAD_DIM

    def keys_at(src, r, ks):
        return (kwin[slot, r, :, pl.ds(ks, KSUB)] if src == "window"
                else kold[r, :, pl.ds(ks, KSUB)])

    def pair_values(src, ks, p):
        pairs = hps // 2
        lanes = pl.ds((p % pairs) * LANES, LANES)
        vv = (vwin[slot, p // pairs, pl.ds(ks, KSUB), lanes] if src == "window"
              else vold[p // pairs, pl.ds(ks, KSUB), lanes])
        zero = jnp.zeros_like(vv)
        return jnp.concatenate([jnp.where(first_head, vv, zero),
                                jnp.where(first_head, zero, vv)], axis=0)

    def sweep(tiles, rems, accs):
        def put(full, r0, r1, part):
            pieces = ([full[:r0]] if r0 else []) + [part] + ([full[r1:]] if r1 < tq else [])
            return jnp.concatenate(pieces, axis=0) if len(pieces) > 1 else part

        zs, splits = [], []
        for src, ks, r0, r1, diagonal in tiles:
            kts = [keys_at(src, r, ks) for r in range(nb)]
            vis = None
            if diagonal:
                row = lax.broadcasted_iota(jnp.int32, (r1 - r0, KSUB), 0)
                col = lax.broadcasted_iota(jnp.int32, (r1 - r0, KSUB), 1)
                vis = col < row
            for v in range(nv):
                h = v % hps
                z = jnp.dot(qh[v][r0:r1], kts[v // hps][h * HEAD_DIM:(h + 1) * HEAD_DIM],
                            preferred_element_type=F32)
                zs.append((z, vis))
                splits.append(_split_bf16(_neg_log2_keep(z, vis)))
        sums = jnp.dot(jnp.concatenate(splits, axis=0), uu, preferred_element_type=F32)
        rems, accs = list(rems), list(accs)
        off, i = 0, 0
        for src, ks, r0, r1, _ in tiles:
            ws = []
            for h in range(nv):
                z, vis = zs[i]
                i += 1
                w, new = _sb_weights(z, vis, sums[off:off + r1 - r0], rems[h][r0:r1])
                off += r1 - r0
                rems[h] = put(rems[h], r0, r1, new)
                ws.append(w)
            for p in range(nv // 2):
                d = jnp.dot(jnp.concatenate(ws[2 * p:2 * p + 2], axis=1), pair_values(src, ks, p),
                            preferred_element_type=F32)
                accs[p] = put(accs[p], r0, r1, accs[p][r0:r1] + d)
        return rems, accs

    def diagonal_tiles(at):
        return [("window", at + s * KSUB, s * KSUB, tq, True) for s in reversed(range(NSUB))]

    def key_block_tiles(src, at, rows=None):
        rows = rows or {s: (0, tq) for s in range(NSUB)}
        return [(src, at + s * KSUB, *rows[s], False) for s in reversed(range(NSUB)) if s in rows]

    zero_rems = [jnp.zeros((tq, LANES), F32)] * nv
    zero_accs = [jnp.zeros((tq, LANES), F32)] * (nv // 2)

    def finish(accs):
        pairs = hps // 2
        for r in range(nb):
            acc = jnp.concatenate(accs[r * pairs:(r + 1) * pairs], axis=1)
            o_ref[r] = (acc * g_ref[r].astype(F32)).astype(BF16)

    def any_alive(rems):
        most = functools.reduce(jnp.maximum, rems)
        return (jnp.max(most) > REM_DEAD_LOG2).astype(jnp.int32)

    @pl.when(qi == 0)
    def _():
        finish(sweep(diagonal_tiles(0), zero_rems, zero_accs)[1])

    def load_state():
        return [rem_ref[h] for h in range(nv)], [acc_ref[p] for p in range(nv // 2)]

    def store_state(rems, accs):
        for h in range(nv):
            rem_ref[h] = rems[h]
        for p in range(nv // 2):
            acc_ref[p] = accs[p]

    eager_rows = {s: min(tq, EAGER_SEEN - (NSUB - 1 - s) * KSUB) for s in range(NSUB)
                  if EAGER_SEEN > (NSUB - 1 - s) * KSUB}

    @pl.when(qi > 0)
    def _():
        first = key_block_tiles("window", 0, {s: (0, e) for s, e in eager_rows.items()})
        rems, accs = sweep(diagonal_tiles(tq) + first, zero_rems, zero_accs)
        store_state(rems, accs)
        alive = any_alive(rems)
        for s in reversed(range(NSUB)):
            e = eager_rows.get(s, 0)
            if e < tq:
                @pl.when(any_alive([r[e:] for r in rems]) > 0)
                def _():
                    store_state(*sweep(key_block_tiles("window", 0, {s: (e, tq)}), *load_state()))

        def cond(c):
            n, alive = c
            return jnp.logical_and(n <= qi, alive > 0)

        def body(c):
            n, _ = c
            for cp in older(n):
                cp.start()
            for cp in older(n):
                cp.wait()
            rems, accs = sweep(key_block_tiles("older", 0), *load_state())
            store_state(rems, accs)
            return n + 1, any_alive(rems)

        lax.while_loop(cond, body, (jnp.int32(2), alive))
        finish(load_state()[1])


def _attn_prompt(q, kt, vb, gsb, *, tq, hps):
    nb, t, _ = q.shape
    assert t % tq == 0 and t >= 2 * tq and tq == NSUB * KSUB and hps == HEADS
    nv = nb * hps
    nq = t // tq
    uu = _suffix_sum_matrix()
    row_blk = pl.BlockSpec((nb, tq, SB_WIDTH), lambda i: (0, i, 0))
    in_hbm = pl.BlockSpec(memory_space=pltpu.HBM)
    block_bytes = 3 * _nbytes((nb, tq, SB_WIDTH), BF16) + _nbytes(uu.shape, BF16)
    temp_bytes = ((nv + nv // 2) * _nbytes((tq, LANES), F32)
                  + 5 * _nbytes((nb, SB_WIDTH, 2 * tq), BF16)
                  + 5 * (NSUB + 2) * nv * _nbytes((tq, KSUB), F32))
    return pl.pallas_call(
        functools.partial(_attn_prompt_kernel, tq=tq, hps=hps, nb=nb, nq=nq),
        grid=(nq,),
        in_specs=[row_blk, row_blk, pl.BlockSpec(uu.shape, lambda i: (0, 0)), in_hbm, in_hbm],
        out_specs=row_blk,
        out_shape=jax.ShapeDtypeStruct((nb, t, SB_WIDTH), BF16),
        scratch_shapes=[pltpu.VMEM((2, nb, SB_WIDTH, 2 * tq), BF16),
                        pltpu.VMEM((2, nb, 2 * tq, SB_WIDTH), BF16),
                        pltpu.VMEM((nb, SB_WIDTH, tq), BF16),
                        pltpu.VMEM((nb, tq, SB_WIDTH), BF16),
                        pltpu.SemaphoreType.DMA((2, 3)),
                        pltpu.VMEM((nv, tq, LANES), F32), pltpu.VMEM((nv // 2, tq, LANES), F32)],
        compiler_params=pltpu.CompilerParams(
            dimension_semantics=("arbitrary",),
            vmem_limit_bytes=_vmem_limit(block_bytes, temp_bytes)),
        name="sb_attn_prompt",
    )(q, gsb, uu, kt, vb)


def _attn_sample_kernel(q_ref, kn_ref, vn_ref, g_ref, uu_ref, ck_hbm, cv_hbm, o_ref,
                        kwin, vwin, sem, rem_s, acc_s, *, t_new, past, nb):
    rows = HEADS * t_new
    uu = uu_ref[...]
    n_sub = past // KSUB
    contract_last = (((1,), (1,)), ((), ()))

    def fetch(which, b, j, slot):
        src, dst = ((ck_hbm, kwin), (cv_hbm, vwin))[which]
        first = pl.multiple_of(j * KSUB, KSUB)
        return pltpu.make_async_copy(src.at[b, :, :, pl.ds(first, KSUB)], dst.at[b, slot],
                                     sem.at[which, b, slot])

    for b in range(nb):
        for slot in range(EAGER_CACHE_SUBS):
            for which in range(2):
                fetch(which, b, n_sub - 1 - slot, slot).start()

    row = lax.broadcasted_iota(jnp.int32, (rows, KSUB), 0)
    col = lax.broadcasted_iota(jnp.int32, (rows, KSUB), 1)
    new_vis = col < (row % t_new)
    pad = jnp.zeros((KSUB - t_new, SB_WIDTH), F32)

    def new_block(ref, b):
        blk = jnp.concatenate([ref[b], pad], axis=0).astype(BF16)
        return [blk[:, h * HEAD_DIM:(h + 1) * HEAD_DIM] for h in range(HEADS)]

    def cached(win, b, slot):
        return [win[b, slot, h].astype(BF16) for h in range(HEADS)]

    def sweep(b, tiles, rem, acc):
        q = q_ref[b]
        qh = [q[:, h * HEAD_DIM:(h + 1) * HEAD_DIM] for h in range(HEADS)]
        zs = [jnp.concatenate(
            [jnp.dot(qh[h], keys[h], preferred_element_type=F32) if dim_major else
             lax.dot_general(qh[h], keys[h], contract_last, preferred_element_type=F32)
             for h in range(HEADS)], axis=0) for keys, _, dim_major, _ in tiles]
        splits = [_split_bf16(_neg_log2_keep(z, t[3])) for z, t in zip(zs, tiles)]
        sums = jnp.dot(jnp.concatenate(splits, axis=0), uu, preferred_element_type=F32)
        for i, (z, (_, vals, dim_major, vis)) in enumerate(zip(zs, tiles)):
            w, rem = _sb_weights(z, vis, sums[i * rows:(i + 1) * rows], rem)
            wh = [w[h * t_new:(h + 1) * t_new] for h in range(HEADS)]
            acc = [a + (lax.dot_general(wh[h], vals[h], contract_last, preferred_element_type=F32)
                        if dim_major else jnp.dot(wh[h], vals[h], preferred_element_type=F32))
                   for h, a in enumerate(acc)]
        return rem, acc

    def alive_of(rem):
        return (jnp.max(rem) > REM_DEAD_LOG2).astype(jnp.int32)

    alive = jnp.int32(0)
    for b in range(nb):
        for slot in range(EAGER_CACHE_SUBS):
            for which in range(2):
                fetch(which, b, n_sub - 1 - slot, slot).wait()
        tiles = [(new_block(kn_ref, b), new_block(vn_ref, b), False, new_vis)]
        tiles += [(cached(kwin, b, s), cached(vwin, b, s), True, None)
                  for s in range(EAGER_CACHE_SUBS)]
        rem, acc = sweep(b, tiles, jnp.zeros((rows, LANES), F32),
                         [jnp.zeros((t_new, HEAD_DIM), F32)] * HEADS)
        rem_s[b] = rem
        acc_s[b] = jnp.concatenate(acc, axis=1)
        alive = jnp.maximum(alive, alive_of(rem))

    def cond(c):
        j, alive = c
        return jnp.logical_and(j >= 0, alive > 0)

    def body(c):
        j, _ = c
        for b in range(nb):
            for which in range(2):
                fetch(which, b, j, 0).start()
        alive = jnp.int32(0)
        for b in range(nb):
            for which in range(2):
                fetch(which, b, j, 0).wait()
            acc_b = acc_s[b]
            rem, acc = sweep(b, [(cached(kwin, b, 0), cached(vwin, b, 0), True, None)], rem_s[b],
                             [acc_b[:, h * HEAD_DIM:(h + 1) * HEAD_DIM] for h in range(HEADS)])
            rem_s[b] = rem
            acc_s[b] = jnp.concatenate(acc, axis=1)
            alive = jnp.maximum(alive, alive_of(rem))
        return j - 1, alive

    lax.while_loop(cond, body, (jnp.int32(n_sub - 1 - EAGER_CACHE_SUBS), alive))
    for b in range(nb):
        o_ref[b] = (acc_s[b] * g_ref[b].astype(F32)).astype(BF16)


def _attn_sample(q, cache_k, cache_v, k_new, v_new, gsb):
    nb, t_new, _ = q.shape
    past = cache_k.shape[3]
    assert cache_k.shape == (nb, HEADS, HEAD_DIM, past)
    assert past % KSUB == 0 and past // KSUB >= EAGER_CACHE_SUBS and t_new <= KSUB
    uu = _suffix_sum_matrix()
    rows = HEADS * t_new
    whole = lambda a: pl.BlockSpec(a.shape, lambda i: (0,) * a.ndim)
    window = (nb, EAGER_CACHE_SUBS, HEADS, HEAD_DIM, KSUB)
    block_bytes = sum(_nbytes(a.shape, a.dtype) for a in (q, k_new, v_new, gsb, uu, q))
    temp_bytes = (2 * _nbytes(window, F32) + _nbytes((nb, rows, LANES), F32)
                  + _nbytes((nb, t_new, SB_WIDTH), F32)
                  + 16 * (EAGER_CACHE_SUBS + 1) * _nbytes((rows, KSUB), F32))
    return pl.pallas_call(
        functools.partial(_attn_sample_kernel, t_new=t_new, past=past, nb=nb),
        grid=(1,),
        in_specs=[whole(q), whole(k_new), whole(v_new), whole(gsb), whole(uu),
                  pl.BlockSpec(memory_space=pltpu.HBM), pl.BlockSpec(memory_space=pltpu.HBM)],
        out_specs=whole(q),
        out_shape=jax.ShapeDtypeStruct((nb, t_new, SB_WIDTH), BF16),
        scratch_shapes=[pltpu.VMEM(window, F32), pltpu.VMEM(window, F32),
                        pltpu.SemaphoreType.DMA((2, nb, EAGER_CACHE_SUBS)),
                        pltpu.VMEM((nb, rows, LANES), F32),
                        pltpu.VMEM((nb, t_new, SB_WIDTH), F32)],
        compiler_params=pltpu.CompilerParams(
            dimension_semantics=("arbitrary",),
            vmem_limit_bytes=max(_vmem_limit(block_bytes, temp_bytes),
                                 VMEM_BYTES_V7X - _nbytes(cache_k.shape, F32))),
        name="sb_attn_sample",
    )(q, k_new, v_new, gsb, uu, cache_k, cache_v)


def _outproj_kernel(x_ref, osb_ref, osgu_ref, p_ref, wo_ref, pg_ref, wg_ref, wp_ref, y_ref):
    h = (x_ref[0]
         + jnp.dot(osb_ref[0], wo_ref[:SB_WIDTH], preferred_element_type=F32)
         + jnp.dot(osgu_ref[0], wo_ref[SB_WIDTH:], preferred_element_type=F32))
    ms = jnp.mean(h * h, axis=-1, keepdims=True)
    hn = (h * lax.rsqrt(ms + EPS) * pg_ref[...]).astype(BF16)
    gate_logit = jnp.dot(hn, wg_ref[...], preferred_element_type=F32)
    gate = 1.0 / (1.0 + jnp.exp(-gate_logit))
    pp = jnp.dot(p_ref[0].astype(BF16), wp_ref[...], preferred_element_type=F32)
    y_ref[0] = h + gate * pp


def _outproj(x, osb, osgu, p, w_out_bf, ple_norm_g, w_gate_bf, w_proj_bf, *, tm, name):
    b, t, _ = x.shape
    assert t % tm == 0
    tok = lambda w: pl.BlockSpec((1, tm, w), lambda bi, i: (bi, i, 0))
    const2 = lambda a: pl.BlockSpec(a.shape, lambda bi, i: (0, 0))
    pg = ple_norm_g.reshape(1, D_MODEL)
    block_bytes = (2 * _nbytes((tm, D_MODEL), F32) + 2 * _nbytes((tm, SEG), BF16)
                   + _nbytes((tm, PLE_DIM), F32) + _nbytes(w_out_bf.shape, BF16)
                   + _nbytes(w_gate_bf.shape, BF16) + _nbytes(w_proj_bf.shape, BF16))
    temp_bytes = 6 * _nbytes((tm, D_MODEL), F32)
    return pl.pallas_call(
        _outproj_kernel,
        grid=(b, t // tm),
        in_specs=[tok(D_MODEL), tok(SEG), tok(SEG), tok(PLE_DIM), const2(w_out_bf), const2(pg),
                  const2(w_gate_bf), const2(w_proj_bf)],
        out_specs=tok(D_MODEL),
        out_shape=jax.ShapeDtypeStruct((b, t, D_MODEL), F32),
        compiler_params=pltpu.CompilerParams(
            dimension_semantics=("arbitrary", "arbitrary"),
            vmem_limit_bytes=_vmem_limit(block_bytes, temp_bytes)),
        name=name,
    )(x, osb, osgu, p, w_out_bf, pg, w_gate_bf, w_proj_bf)


def _sgu_bias_rows(sgu_b_l, period):
    per_pos = jnp.tile(sgu_b_l[:, :period].T, (SGU_CHUNK // period, 1))
    return jnp.repeat(per_pos, GROUP_W, axis=1)


def kernel(x_prompt, x_sample, cache_k, cache_v, p_prompt, p_sample, norm_g, w_in, q_norm_g,
           k_norm_g, sgu_norm_g, sgu_w, sgu_b, w_out, ple_norm_g, w_ple_gate, w_ple_proj):
    depth = w_in.shape[0]
    assert depth == 1, "one layer per call"
    l = 0
    b_p, t_p, _ = x_prompt.shape
    b_s, t_s, _ = x_sample.shape
    past = cache_k.shape[2]
    n_s = b_s * t_s
    assert n_s == SGU_CHUNK and SGU_CHUNK % t_s == 0

    w_in_bf = w_in[l].astype(BF16)
    w_out_bf = w_out[l].astype(BF16)
    w_gate_bf = w_ple_gate[l].astype(BF16)
    w_proj_bf = w_ple_proj[l].astype(BF16)

    q, k_feat, v_feat, kt, vb, gsb, osgu = _inproj(
        x_prompt, norm_g[l], w_in_bf, q_norm_g[l], k_norm_g[l], sgu_norm_g[l],
        sgu_w[l], _sgu_bias_rows(sgu_b[l], SGU_CHUNK),
        tm=TM_IN, period=SGU_CHUNK, attn_layout=True)
    osb = _attn_prompt(q, kt, vb, gsb, tq=TQ, hps=HEADS_PER_STEP)
    y_prompt = _outproj(x_prompt, osb, osgu, p_prompt[l], w_out_bf, ple_norm_g[l], w_gate_bf,
                        w_proj_bf, tm=TM_OUT, name="outproj_prompt")

    rep = SGU_CHUNK // t_s
    sgu_w_s = jnp.tile(sgu_w[l][:, :t_s, :t_s], (1, rep, rep))
    xs = x_sample.reshape(1, n_s, D_MODEL)
    q_s, k_s, v_s, gsb_s, osgu_s, vs_s = _inproj(
        xs, norm_g[l], w_in_bf, q_norm_g[l], k_norm_g[l], sgu_norm_g[l],
        sgu_w_s, _sgu_bias_rows(sgu_b[l], t_s),
        tm=n_s, period=t_s, attn_layout=False)
    shp = (b_s, t_s, SB_WIDTH)
    to_feat = lambda c: jnp.transpose(c[l], (0, 2, 3, 1))
    osb_s = _attn_sample(q_s.reshape(shp), to_feat(cache_k), to_feat(cache_v),
                         k_s.reshape(shp), v_s.reshape(shp),
                         gsb_s.reshape(shp))
    y_sample = _outproj(xs, osb_s.reshape(1, n_s, SB_WIDTH), osgu_s, p_sample[l].reshape(1, n_s, PLE_DIM),
                        w_out_bf, ple_norm_g[l], w_gate_bf, w_proj_bf, tm=n_s,
                        name="outproj_sample").reshape(b_s, t_s, D_MODEL)

    head_shape = lambda a, bb, tt: a.reshape(1, bb, tt, HEADS, HEAD_DIM)
    from_feat = lambda a: jnp.transpose(a.reshape(b_p, HEADS, HEAD_DIM, t_p), (0, 3, 1, 2))[None]
    return (y_prompt, y_sample,
            from_feat(k_feat), from_feat(v_feat),
            head_shape(k_s, b_s, t_s), head_shape(v_s, b_s, t_s),
            vs_s.reshape(1, b_s, t_s, GROUPS, GROUP_W))
```

```python
import functools
import math

import numpy as np
import jax
import jax.numpy as jnp
from jax import lax
from jax.experimental import pallas as pl
from jax.experimental.pallas import tpu as pltpu

F32 = jnp.float32
BF16 = jnp.bfloat16

LANES = 128
MXU_TILE = 256
VMEM_BYTES_V7X = 64 * 1024 * 1024

D_MODEL = 1024
PLE_DIM = 256
HEADS = 8
HEAD_DIM = 64
SB_WIDTH = HEADS * HEAD_DIM
GROUPS = 4
GROUP_W = 128
SGU_WIDTH = GROUPS * GROUP_W
SGU_CHUNK = 128
SEG = 512
N_SEG = 7
EPS = 1e-6

Q_SCALE = HEAD_DIM ** -0.5 * math.log2(math.e)

KSUB = LANES
NSUB = 2
TQ = NSUB * KSUB
HEADS_PER_STEP = 8
REM_DEAD_LOG2 = -152.0
SOFTPLUS_LINEAR = 64.0
EAGER_CACHE_SUBS = 2
EAGER_SEEN = KSUB + 48
TM_IN = 1024
TM_OUT = 1024


def _vmem_limit(block_bytes, temp_bytes):
    need = 2 * block_bytes + temp_bytes
    return int(min(need, VMEM_BYTES_V7X - 8 * 1024 * 1024))


def _nbytes(shape, dtype):
    return int(np.prod(shape)) * jnp.dtype(dtype).itemsize


def _suffix_sum_matrix():
    j = np.arange(KSUB)[:, None]
    s = np.arange(KSUB)[None, :]
    one = np.concatenate([np.where(j >= s, -1.0, 0.0), -np.ones((KSUB, LANES))], axis=1)
    return jnp.asarray(np.concatenate([one, one], axis=0), dtype=BF16)


def _head_mean_matrix():
    a = np.arange(MXU_TILE)
    bd = np.where(a[:, None] // HEAD_DIM == a[None, :] // HEAD_DIM, 1.0 / HEAD_DIM, 0.0)
    return jnp.asarray(bd, dtype=BF16)


def _split_bf16(x):
    hi = x.astype(BF16)
    lo = (x - hi.astype(F32)).astype(BF16)
    return jnp.concatenate([hi, lo], axis=1)


def _gelu_tanh(x):
    return 0.5 * x * (1.0 + jnp.tanh(math.sqrt(2.0 / math.pi) * (x + 0.044715 * (x * x * x))))


def _silu(x):
    return x / (1.0 + jnp.exp(-x))


def _inproj_kernel(x_ref, ng_ref, w_ref, qg_ref, kg_ref, sg_ref, hm_ref, sw_ref, sb_ref,
                   *out_refs, tm, period, attn_layout):
    if attn_layout:
        q_ref, ktf_ref, vtf_ref, kt_ref, vb_ref, gsb_ref, osgu_ref = out_refs
    else:
        q_ref, k_ref, v_ref, gsb_ref, osgu_ref, vs_ref = out_refs

    x = x_ref[0]
    ms = jnp.mean(x * x, axis=-1, keepdims=True)
    xn = (x * lax.rsqrt(ms + EPS) * ng_ref[...]).astype(BF16)

    def seg(i):
        return jnp.dot(xn, w_ref[:, i * SEG:(i + 1) * SEG], preferred_element_type=F32)

    def head_rms(t, g):
        sq = (t * t).astype(BF16)
        ms_h = jnp.concatenate(
            [jnp.dot(sq[:, c:c + MXU_TILE], hm_ref[...], preferred_element_type=F32)
             for c in range(0, SB_WIDTH, MXU_TILE)], axis=1)
        return t * lax.rsqrt(ms_h + EPS) * g

    q = head_rms(seg(0), qg_ref[...])
    q_ref[0] = (q * Q_SCALE).astype(BF16)

    k = head_rms(seg(1), kg_ref[...])
    v = seg(2)
    if attn_layout:
        k_t = k.T
        ktf_ref[0] = k_t
        kt_ref[0] = k_t.astype(BF16)
        vtf_ref[0] = v.T
        vb_ref[0] = v.astype(BF16)
    else:
        k_ref[0] = k
        v_ref[0] = v

    gsb_ref[0] = _silu(seg(3)).astype(BF16)

    u = _gelu_tanh(seg(4))
    vs_raw = _gelu_tanh(seg(5))
    vs_groups = []
    for g in range(GROUPS):
        t = vs_raw[:, g * GROUP_W:(g + 1) * GROUP_W]
        ms_g = jnp.mean(t * t, axis=-1, keepdims=True)
        vs_groups.append(t * lax.rsqrt(ms_g + EPS) * sg_ref[:, g * GROUP_W:(g + 1) * GROUP_W])
    if not attn_layout:
        vs_ref[0] = jnp.concatenate(vs_groups, axis=1)

    row = lax.broadcasted_iota(jnp.int32, (SGU_CHUNK, SGU_CHUNK), 0)
    col = lax.broadcasted_iota(jnp.int32, (SGU_CHUNK, SGU_CHUNK), 1)
    keep = row >= col
    if period != SGU_CHUNK:
        keep = keep & ((row // period) == (col // period))
    s_groups = []
    for g in range(GROUPS):
        wm = jnp.where(keep, sw_ref[g], 0.0).astype(BF16)
        vg = vs_groups[g].astype(BF16)
        chunks = [jnp.dot(wm, vg[c * SGU_CHUNK:(c + 1) * SGU_CHUNK], preferred_element_type=F32)
                  + sb_ref[:, g * GROUP_W:(g + 1) * GROUP_W]
                  for c in range(tm // SGU_CHUNK)]
        s_groups.append(jnp.concatenate(chunks, axis=0) if len(chunks) > 1 else chunks[0])
    s = jnp.concatenate(s_groups, axis=1)

    osgu_ref[0] = (u * s * _silu(seg(6))).astype(BF16)


def _inproj(x, norm_g, w_in_bf, q_norm_g, k_norm_g, sgu_norm_g, sgu_w_tiled, sgu_bias, *,
            tm, period, attn_layout):
    b, t, _ = x.shape
    assert t % tm == 0 and tm % SGU_CHUNK == 0
    grid = (b, t // tm)
    tok = lambda w: pl.BlockSpec((1, tm, w), lambda bi, i: (bi, i, 0))
    const2 = lambda a: pl.BlockSpec(a.shape, lambda bi, i: (0, 0))
    const3 = lambda a: pl.BlockSpec(a.shape, lambda bi, i: (0, 0, 0))

    ng = norm_g.reshape(1, D_MODEL)
    qg = jnp.tile(q_norm_g, HEADS).reshape(1, SB_WIDTH)
    kg = jnp.tile(k_norm_g, HEADS).reshape(1, SB_WIDTH)
    sg = sgu_norm_g.reshape(1, SGU_WIDTH)
    hm = _head_mean_matrix()

    act = lambda dt: jax.ShapeDtypeStruct((b, t, SEG), dt)
    if attn_layout:
        feat = lambda dt: jax.ShapeDtypeStruct((b, SB_WIDTH, t), dt)
        feat_blk = pl.BlockSpec((1, SB_WIDTH, tm), lambda bi, i: (bi, 0, i))
        out_shape = (act(BF16), feat(F32), feat(F32), feat(BF16), act(BF16), act(BF16), act(BF16))
        out_specs = (tok(SEG), feat_blk, feat_blk, feat_blk, tok(SEG), tok(SEG), tok(SEG))
    else:
        out_shape = (act(BF16), act(F32), act(F32), act(BF16), act(BF16), act(F32))
        out_specs = (tok(SEG),) * 6

    block_bytes = (_nbytes((tm, D_MODEL), F32) + _nbytes(w_in_bf.shape, BF16)
                   + _nbytes(hm.shape, BF16) + _nbytes(sgu_w_tiled.shape, F32)
                   + _nbytes(sgu_bias.shape, F32) + 7 * _nbytes((tm, SEG), F32))
    temp_bytes = 12 * _nbytes((tm, SEG), F32)
    return pl.pallas_call(
        functools.partial(_inproj_kernel, tm=tm, period=period, attn_layout=attn_layout),
        grid=grid,
        in_specs=[tok(D_MODEL), const2(ng), const2(w_in_bf), const2(qg), const2(kg), const2(sg),
                  const2(hm), const3(sgu_w_tiled), const2(sgu_bias)],
        out_specs=out_specs,
        out_shape=out_shape,
        compiler_params=pltpu.CompilerParams(
            dimension_semantics=("arbitrary", "arbitrary"),
            vmem_limit_bytes=_vmem_limit(block_bytes, temp_bytes)),
        name="inproj_prompt" if attn_layout else "inproj_sample",
    )(x, ng, w_in_bf, qg, kg, sg, hm, sgu_w_tiled, sgu_bias)


def _neg_log2_keep(z, vis):
    sp = jnp.maximum(z, jnp.log2(1.0 + jnp.exp2(jnp.minimum(z, SOFTPLUS_LINEAR))))
    return sp if vis is None else jnp.where(vis, sp, 0.0)


def _sb_weights(z, vis, sums, rem):
    w = jnp.exp2(z + sums[:, :KSUB] + rem)
    if vis is not None:
        w = jnp.where(vis, w, 0.0)
    return w.astype(BF16), rem + sums[:, KSUB:]


def _attn_prompt_kernel(q_ref, g_ref, uu_ref, kt_hbm, v_hbm, o_ref,
                        kwin, vwin, kold, vold, sem, rem_ref, acc_ref, *, tq, hps, nb, nq):
    qi = pl.program_id(0)
    slot = qi % 2
    nv = nb * hps

    def window(step, slot_):
        first = pl.multiple_of(jnp.maximum(step - 1, 0) * tq, tq)
        return ([pltpu.make_async_copy(kt_hbm.at[r, :, pl.ds(first, 2 * tq)], kwin.at[slot_, r],
                                       sem.at[0, slot_]) for r in range(nb)]
                + [pltpu.make_async_copy(v_hbm.at[r, pl.ds(first, 2 * tq), :], vwin.at[slot_, r],
                                         sem.at[1, slot_]) for r in range(nb)])

    def older(n):
        first = pl.multiple_of((qi - n) * tq, tq)
        return ([pltpu.make_async_copy(kt_hbm.at[r, :, pl.ds(first, tq)], kold.at[r], sem.at[0, 2])
                 for r in range(nb)]
                + [pltpu.make_async_copy(v_hbm.at[r, pl.ds(first, tq), :], vold.at[r], sem.at[1, 2])
                   for r in range(nb)])

    @pl.when(qi == 0)
    def _():
        for c in window(qi, slot):
            c.start()

    for c in window(qi, slot):
        c.wait()

    @pl.when(qi + 1 < nq)
    def _():
        for c in window(qi + 1, 1 - slot):
            c.start()

    qh = [q_ref[v // hps][:, (v % hps) * HEAD_DIM:(v % hps + 1) * HEAD_DIM] for v in range(nv)]
    uu = uu_ref[...]
    lane = lax.broadcasted_iota(jnp.int32, (KSUB, LANES), 1)
    first_head = lane < HEAD_DIM

    def keys_at(src, r, ks):
        return (kwin[slot, r, :, pl.ds(ks, KSUB)] if src == "window"
                else kold[r, :, pl.ds(ks, KSUB)])

    def pair_values(src, ks, p):
        pairs = hps // 2
        lanes = pl.ds((p % pairs) * LANES, LANES)
        vv = (vwin[slot, p // pairs, pl.ds(ks, KSUB), lanes] if src == "window"
              else vold[p // pairs, pl.ds(ks, KSUB), lanes])
        zero = jnp.zeros_like(vv)
        return jnp.concatenate([jnp.where(first_head, vv, zero),
                                jnp.where(first_head, zero, vv)], axis=0)

    def sweep(tiles, rems, accs):
        def put(full, r0, r1, part):
            pieces = ([full[:r0]] if r0 else []) + [part] + ([full[r1:]] if r1 < tq else [])
            return jnp.concatenate(pieces, axis=0) if len(pieces) > 1 else part

        zs, splits = [], []
        for src, ks, r0, r1, diagonal in tiles:
            kts = [keys_at(src, r, ks) for r in range(nb)]
            vis = None
            if diagonal:
                row = lax.broadcasted_iota(jnp.int32, (r1 - r0, KSUB), 0)
                col = lax.broadcasted_iota(jnp.int32, (r1 - r0, KSUB), 1)
                vis = col < row
            for v in range(nv):
                h = v % hps
                z = jnp.dot(qh[v][r0:r1], kts[v // hps][h * HEAD_DIM:(h + 1) * HEAD_DIM],
                            preferred_element_type=F32)
                zs.append((z, vis))
                splits.append(_split_bf16(_neg_log2_keep(z, vis)))
        sums = jnp.dot(jnp.concatenate(splits, axis=0), uu, preferred_element_type=F32)
        rems, accs = list(rems), list(accs)
        off, i = 0, 0
        for src, ks, r0, r1, _ in tiles:
            ws = []
            for h in range(nv):
                z, vis = zs[i]
                i += 1
                w, new = _sb_weights(z, vis, sums[off:off + r1 - r0], rems[h][r0:r1])
                off += r1 - r0
                rems[h] = put(rems[h], r0, r1, new)
                ws.append(w)
            for p in range(nv // 2):
                d = jnp.dot(jnp.concatenate(ws[2 * p:2 * p + 2], axis=1), pair_values(src, ks, p),
                            preferred_element_type=F32)
                accs[p] = put(accs[p], r0, r1, accs[p][r0:r1] + d)
        return rems, accs

    def diagonal_tiles(at):
        return [("window", at + s * KSUB, s * KSUB, tq, True) for s in reversed(range(NSUB))]

    def key_block_tiles(src, at, rows=None):
        rows = rows or {s: (0, tq) for s in range(NSUB)}
        return [(src, at + s * KSUB, *rows[s], False) for s in reversed(range(NSUB)) if s in rows]

    zero_rems = [jnp.zeros((tq, LANES), F32)] * nv
    zero_accs = [jnp.zeros((tq, LANES), F32)] * (nv // 2)

    def finish(accs):
        pairs = hps // 2
        for r in range(nb):
            acc = jnp.concatenate(accs[r * pairs:(r + 1) * pairs], axis=1)
            o_ref[r] = (acc * g_ref[r].astype(F32)).astype(BF16)

    def any_alive(rems):
        most = functools.reduce(jnp.maximum, rems)
        return (jnp.max(most) > REM_DEAD_LOG2).astype(jnp.int32)

    @pl.when(qi == 0)
    def _():
        finish(sweep(diagonal_tiles(0), zero_rems, zero_accs)[1])

    def load_state():
        return [rem_ref[h] for h in range(nv)], [acc_ref[p] for p in range(nv // 2)]

    def store_state(rems, accs):
        for h in range(nv):
            rem_ref[h] = rems[h]
        for p in range(nv // 2):
            acc_ref[p] = accs[p]

    eager_rows = {s: min(tq, EAGER_SEEN - (NSUB - 1 - s) * KSUB) for s in range(NSUB)
                  if EAGER_SEEN > (NSUB - 1 - s) * KSUB}

    @pl.when(qi > 0)
    def _():
        first = key_block_tiles("window", 0, {s: (0, e) for s, e in eager_rows.items()})
        rems, accs = sweep(diagonal_tiles(tq) + first, zero_rems, zero_accs)
        store_state(rems, accs)
        alive = any_alive(rems)
        for s in reversed(range(NSUB)):
            e = eager_rows.get(s, 0)
            if e < tq:
                @pl.when(any_alive([r[e:] for r in rems]) > 0)
                def _():
                    store_state(*sweep(key_block_tiles("window", 0, {s: (e, tq)}), *load_state()))

        def cond(c):
            n, alive = c
            return jnp.logical_and(n <= qi, alive > 0)

        def body(c):
            n, _ = c
            for cp in older(n):
                cp.start()
            for cp in older(n):
                cp.wait()
            rems, accs = sweep(key_block_tiles("older", 0), *load_state())
            store_state(rems, accs)
            return n + 1, any_alive(rems)

        lax.while_loop(cond, body, (jnp.int32(2), alive))
        finish(load_state()[1])


def _attn_prompt(q, kt, vb, gsb, *, tq, hps):
    nb, t, _ = q.shape
    assert t % tq == 0 and t >= 2 * tq and tq == NSUB * KSUB and hps == HEADS
    nv = nb * hps
    nq = t // tq
    uu = _suffix_sum_matrix()
    row_blk = pl.BlockSpec((nb, tq, SB_WIDTH), lambda i: (0, i, 0))
    in_hbm = pl.BlockSpec(memory_space=pltpu.HBM)
    block_bytes = 3 * _nbytes((nb, tq, SB_WIDTH), BF16) + _nbytes(uu.shape, BF16)
    temp_bytes = ((nv + nv // 2) * _nbytes((tq, LANES), F32)
                  + 5 * _nbytes((nb, SB_WIDTH, 2 * tq), BF16)
                  + 5 * (NSUB + 2) * nv * _nbytes((tq, KSUB), F32))
    return pl.pallas_call(
        functools.partial(_attn_prompt_kernel, tq=tq, hps=hps, nb=nb, nq=nq),
        grid=(nq,),
        in_specs=[row_blk, row_blk, pl.BlockSpec(uu.shape, lambda i: (0, 0)), in_hbm, in_hbm],
        out_specs=row_blk,
        out_shape=jax.ShapeDtypeStruct((nb, t, SB_WIDTH), BF16),
        scratch_shapes=[pltpu.VMEM((2, nb, SB_WIDTH, 2 * tq), BF16),
                        pltpu.VMEM((2, nb, 2 * tq, SB_WIDTH), BF16),
                        pltpu.VMEM((nb, SB_WIDTH, tq), BF16),
                        pltpu.VMEM((nb, tq, SB_WIDTH), BF16),
                        pltpu.SemaphoreType.DMA((2, 3)),
                        pltpu.VMEM((nv, tq, LANES), F32), pltpu.VMEM((nv // 2, tq, LANES), F32)],
        compiler_params=pltpu.CompilerParams(
            dimension_semantics=("arbitrary",),
            vmem_limit_bytes=_vmem_limit(block_bytes, temp_bytes)),
        name="sb_attn_prompt",
    )(q, gsb, uu, kt, vb)


def _attn_sample_kernel(q_ref, kn_ref, vn_ref, g_ref, uu_ref, ck_hbm, cv_hbm, o_ref,
                        kwin, vwin, sem, rem_s, acc_s, *, t_new, past, nb):
    rows = HEADS * t_new
    uu = uu_ref[...]
    n_sub = past // KSUB
    contract_last = (((1,), (1,)), ((), ()))

    def fetch(which, b, j, slot):
        src, dst = ((ck_hbm, kwin), (cv_hbm, vwin))[which]
        first = pl.multiple_of(j * KSUB, KSUB)
        return pltpu.make_async_copy(src.at[b, :, :, pl.ds(first, KSUB)], dst.at[b, slot],
                                     sem.at[which, b, slot])

    for b in range(nb):
        for slot in range(EAGER_CACHE_SUBS):
            for which in range(2):
                fetch(which, b, n_sub - 1 - slot, slot).start()

    row = lax.broadcasted_iota(jnp.int32, (rows, KSUB), 0)
    col = lax.broadcasted_iota(jnp.int32, (rows, KSUB), 1)
    new_vis = col < (row % t_new)
    pad = jnp.zeros((KSUB - t_new, SB_WIDTH), F32)

    def new_block(ref, b):
        blk = jnp.concatenate([ref[b], pad], axis=0).astype(BF16)
        return [blk[:, h * HEAD_DIM:(h + 1) * HEAD_DIM] for h in range(HEADS)]

    def cached(win, b, slot):
        return [win[b, slot, h].astype(BF16) for h in range(HEADS)]

    def sweep(b, tiles, rem, acc):
        q = q_ref[b]
        qh = [q[:, h * HEAD_DIM:(h + 1) * HEAD_DIM] for h in range(HEADS)]
        zs = [jnp.concatenate(
            [jnp.dot(qh[h], keys[h], preferred_element_type=F32) if dim_major else
             lax.dot_general(qh[h], keys[h], contract_last, preferred_element_type=F32)
             for h in range(HEADS)], axis=0) for keys, _, dim_major, _ in tiles]
        splits = [_split_bf16(_neg_log2_keep(z, t[3])) for z, t in zip(zs, tiles)]
        sums = jnp.dot(jnp.concatenate(splits, axis=0), uu, preferred_element_type=F32)
        for i, (z, (_, vals, dim_major, vis)) in enumerate(zip(zs, tiles)):
            w, rem = _sb_weights(z, vis, sums[i * rows:(i + 1) * rows], rem)
            wh = [w[h * t_new:(h + 1) * t_new] for h in range(HEADS)]
            acc = [a + (lax.dot_general(wh[h], vals[h], contract_last, preferred_element_type=F32)
                        if dim_major else jnp.dot(wh[h], vals[h], preferred_element_type=F32))
                   for h, a in enumerate(acc)]
        return rem, acc

    def alive_of(rem):
        return (jnp.max(rem) > REM_DEAD_LOG2).astype(jnp.int32)

    for b in range(nb):
        for slot in range(EAGER_CACHE_SUBS):
            for which in range(2):
                fetch(which, b, n_sub - 1 - slot, slot).wait()
    alive = jnp.int32(0)
    for b in range(nb):
        tiles = [(new_block(kn_ref, b), new_block(vn_ref, b), False, new_vis)]
        tiles += [(cached(kwin, b, s), cached(vwin, b, s), True, None)
                  for s in range(EAGER_CACHE_SUBS)]
        rem, acc = sweep(b, tiles, jnp.zeros((rows, LANES), F32),
                         [jnp.zeros((t_new, HEAD_DIM), F32)] * HEADS)
        rem_s[b] = rem
        acc_s[b] = jnp.concatenate(acc, axis=1)
        alive = jnp.maximum(alive, alive_of(rem))

    def cond(c):
        j, alive = c
        return jnp.logical_and(j >= 0, alive > 0)

    def body(c):
        j, _ = c
        for b in range(nb):
            for which in range(2):
                fetch(which, b, j, 0).start()
        alive = jnp.int32(0)
        for b in range(nb):
            for which in range(2):
                fetch(which, b, j, 0).wait()
            acc_b = acc_s[b]
            rem, acc = sweep(b, [(cached(kwin, b, 0), cached(vwin, b, 0), True, None)], rem_s[b],
                             [acc_b[:, h * HEAD_DIM:(h + 1) * HEAD_DIM] for h in range(HEADS)])
            rem_s[b] = rem
            acc_s[b] = jnp.concatenate(acc, axis=1)
            alive = jnp.maximum(alive, alive_of(rem))
        return j - 1, alive

    lax.while_loop(cond, body, (jnp.int32(n_sub - 1 - EAGER_CACHE_SUBS), alive))
    for b in range(nb):
        o_ref[b] = (acc_s[b] * g_ref[b].astype(F32)).astype(BF16)


def _attn_sample(q, cache_k, cache_v, k_new, v_new, gsb):
    nb, t_new, _ = q.shape
    past = cache_k.shape[3]
    assert cache_k.shape == (nb, HEADS, HEAD_DIM, past)
    assert past % KSUB == 0 and past // KSUB >= EAGER_CACHE_SUBS and t_new <= KSUB
    uu = _suffix_sum_matrix()
    rows = HEADS * t_new
    whole = lambda a: pl.BlockSpec(a.shape, lambda i: (0,) * a.ndim)
    window = (nb, EAGER_CACHE_SUBS, HEADS, HEAD_DIM, KSUB)
    block_bytes = sum(_nbytes(a.shape, a.dtype) for a in (q, k_new, v_new, gsb, uu, q))
    temp_bytes = (2 * _nbytes(window, F32) + _nbytes((nb, rows, LANES), F32)
                  + _nbytes((nb, t_new, SB_WIDTH), F32)
                  + 16 * (EAGER_CACHE_SUBS + 1) * _nbytes((rows, KSUB), F32))
    return pl.pallas_call(
        functools.partial(_attn_sample_kernel, t_new=t_new, past=past, nb=nb),
        grid=(1,),
        in_specs=[whole(q), whole(k_new), whole(v_new), whole(gsb), whole(uu),
                  pl.BlockSpec(memory_space=pltpu.HBM), pl.BlockSpec(memory_space=pltpu.HBM)],
        out_specs=whole(q),
        out_shape=jax.ShapeDtypeStruct((nb, t_new, SB_WIDTH), BF16),
        scratch_shapes=[pltpu.VMEM(window, F32), pltpu.VMEM(window, F32),
                        pltpu.SemaphoreType.DMA((2, nb, EAGER_CACHE_SUBS)),
                        pltpu.VMEM((nb, rows, LANES), F32),
                        pltpu.VMEM((nb, t_new, SB_WIDTH), F32)],
        compiler_params=pltpu.CompilerParams(
            dimension_semantics=("arbitrary",),
            vmem_limit_bytes=max(_vmem_limit(block_bytes, temp_bytes),
                                 VMEM_BYTES_V7X - _nbytes(cache_k.shape, F32))),
        name="sb_attn_sample",
    )(q, k_new, v_new, gsb, uu, cache_k, cache_v)


def _outproj_kernel(x_ref, osb_ref, osgu_ref, p_ref, wo_ref, pg_ref, wg_ref, wp_ref, y_ref):
    h = (x_ref[0]
         + jnp.dot(osb_ref[0], wo_ref[:SB_WIDTH], preferred_element_type=F32)
         + jnp.dot(osgu_ref[0], wo_ref[SB_WIDTH:], preferred_element_type=F32))
    ms = jnp.mean(h * h, axis=-1, keepdims=True)
    hn = (h * lax.rsqrt(ms + EPS) * pg_ref[...]).astype(BF16)
    gate_logit = jnp.dot(hn, wg_ref[...], preferred_element_type=F32)
    gate = 1.0 / (1.0 + jnp.exp(-gate_logit))
    pp = jnp.dot(p_ref[0].astype(BF16), wp_ref[...], preferred_element_type=F32)
    y_ref[0] = h + gate * pp


def _outproj(x, osb, osgu, p, w_out_bf, ple_norm_g, w_gate_bf, w_proj_bf, *, tm, name):
    b, t, _ = x.shape
    assert t % tm == 0
    tok = lambda w: pl.BlockSpec((1, tm, w), lambda bi, i: (bi, i, 0))
    const2 = lambda a: pl.BlockSpec(a.shape, lambda bi, i: (0, 0))
    pg = ple_norm_g.reshape(1, D_MODEL)
    block_bytes = (2 * _nbytes((tm, D_MODEL), F32) + 2 * _nbytes((tm, SEG), BF16)
                   + _nbytes((tm, PLE_DIM), F32) + _nbytes(w_out_bf.shape, BF16)
                   + _nbytes(w_gate_bf.shape, BF16) + _nbytes(w_proj_bf.shape, BF16))
    temp_bytes = 6 * _nbytes((tm, D_MODEL), F32)
    return pl.pallas_call(
        _outproj_kernel,
        grid=(b, t // tm),
        in_specs=[tok(D_MODEL), tok(SEG), tok(SEG), tok(PLE_DIM), const2(w_out_bf), const2(pg),
                  const2(w_gate_bf), const2(w_proj_bf)],
        out_specs=tok(D_MODEL),
        out_shape=jax.ShapeDtypeStruct((b, t, D_MODEL), F32),
        compiler_params=pltpu.CompilerParams(
            dimension_semantics=("arbitrary", "arbitrary"),
            vmem_limit_bytes=_vmem_limit(block_bytes, temp_bytes)),
        name=name,
    )(x, osb, osgu, p, w_out_bf, pg, w_gate_bf, w_proj_bf)


def _sgu_bias_rows(sgu_b_l, period):
    per_pos = jnp.tile(sgu_b_l[:, :period].T, (SGU_CHUNK // period, 1))
    return jnp.repeat(per_pos, GROUP_W, axis=1)


def kernel(x_prompt, x_sample, cache_k, cache_v, p_prompt, p_sample, norm_g, w_in, q_norm_g,
           k_norm_g, sgu_norm_g, sgu_w, sgu_b, w_out, ple_norm_g, w_ple_gate, w_ple_proj):
    depth = w_in.shape[0]
    assert depth == 1, "one layer per call"
    l = 0
    b_p, t_p, _ = x_prompt.shape
    b_s, t_s, _ = x_sample.shape
    past = cache_k.shape[2]
    n_s = b_s * t_s
    assert n_s == SGU_CHUNK and SGU_CHUNK % t_s == 0

    w_in_bf = w_in[l].astype(BF16)
    w_out_bf = w_out[l].astype(BF16)
    w_gate_bf = w_ple_gate[l].astype(BF16)
    w_proj_bf = w_ple_proj[l].astype(BF16)

    q, k_feat, v_feat, kt, vb, gsb, osgu = _inproj(
        x_prompt, norm_g[l], w_in_bf, q_norm_g[l], k_norm_g[l], sgu_norm_g[l],
        sgu_w[l], _sgu_bias_rows(sgu_b[l], SGU_CHUNK),
        tm=TM_IN, period=SGU_CHUNK, attn_layout=True)
    osb = _attn_prompt(q, kt, vb, gsb, tq=TQ, hps=HEADS_PER_STEP)
    y_prompt = _outproj(x_prompt, osb, osgu, p_prompt[l], w_out_bf, ple_norm_g[l], w_gate_bf,
                        w_proj_bf, tm=TM_OUT, name="outproj_prompt")

    rep = SGU_CHUNK // t_s
    sgu_w_s = jnp.tile(sgu_w[l][:, :t_s, :t_s], (1, rep, rep))
    xs = x_sample.reshape(1, n_s, D_MODEL)
    q_s, k_s, v_s, gsb_s, osgu_s, vs_s = _inproj(
        xs, norm_g[l], w_in_bf, q_norm_g[l], k_norm_g[l], sgu_norm_g[l],
        sgu_w_s, _sgu_bias_rows(sgu_b[l], t_s),
        tm=n_s, period=t_s, attn_layout=False)
    shp = (b_s, t_s, SB_WIDTH)
    to_feat = lambda c: jnp.transpose(c[l], (0, 2, 3, 1))
    osb_s = _attn_sample(q_s.reshape(shp), to_feat(cache_k), to_feat(cache_v),
                         k_s.reshape(shp), v_s.reshape(shp),
                         gsb_s.reshape(shp))
    y_sample = _outproj(xs, osb_s.reshape(1, n_s, SB_WIDTH), osgu_s, p_sample[l].reshape(1, n_s, PLE_DIM),
                        w_out_bf, ple_norm_g[l], w_gate_bf, w_proj_bf, tm=n_s,
                        name="outproj_sample").reshape(b_s, t_s, D_MODEL)

    head_shape = lambda a, bb, tt: a.reshape(1, bb, tt, HEADS, HEAD_DIM)
    from_feat = lambda a: jnp.transpose(a.reshape(b_p, HEADS, HEAD_DIM, t_p), (0, 3, 1, 2))[None]
    return (y_prompt, y_sample,
            from_feat(k_feat), from_feat(v_feat),
            head_shape(k_s, b_s, t_s), head_shape(v_s, b_s, t_s),
            vs_s.reshape(1, b_s, t_s, GROUPS, GROUP_W))
```

```python
import functools
import math

import numpy as np
import jax
import jax.numpy as jnp
from jax import lax
from jax.experimental import pallas as pl
from jax.experimental.pallas import tpu as pltpu

F32 = jnp.float32
BF16 = jnp.bfloat16

LANES = 128
MXU_TILE = 256
VMEM_BYTES_V7X = 64 * 1024 * 1024

D_MODEL = 1024
PLE_DIM = 256
HEADS = 8
HEAD_DIM = 64
SB_WIDTH = HEADS * HEAD_DIM
GROUPS = 4
GROUP_W = 128
SGU_WIDTH = GROUPS * GROUP_W
SGU_CHUNK = 128
SEG = 512
EPS = 1e-6

Q_SCALE = HEAD_DIM ** -0.5 * math.log2(math.e)

KSUB = LANES
NSUB = 2
TQ = NSUB * KSUB
HEADS_PER_STEP = 8
REM_DEAD_LOG2 = -152.0
SOFTPLUS_LINEAR = 64.0
EAGER_CACHE_SUBS = 2
EAGER_SEEN = KSUB + 48
TM_IN = 1024
TM_OUT = 1024


def _vmem_limit(block_bytes, temp_bytes):
    need = 2 * block_bytes + temp_bytes
    return int(min(need, VMEM_BYTES_V7X - 8 * 1024 * 1024))


def _nbytes(shape, dtype):
    return int(np.prod(shape)) * jnp.dtype(dtype).itemsize


def _suffix_sum_matrix():
    j = np.arange(KSUB)[:, None]
    s = np.arange(KSUB)[None, :]
    one = np.concatenate([np.where(j >= s, -1.0, 0.0), -np.ones((KSUB, LANES))], axis=1)
    return jnp.asarray(np.concatenate([one, one], axis=0), dtype=BF16)


def _head_mean_matrix():
    a = np.arange(MXU_TILE)
    bd = np.where(a[:, None] // HEAD_DIM == a[None, :] // HEAD_DIM, 1.0 / HEAD_DIM, 0.0)
    return jnp.asarray(bd, dtype=BF16)


def _split_bf16(x):
    hi = x.astype(BF16)
    lo = (x - hi.astype(F32)).astype(BF16)
    return jnp.concatenate([hi, lo], axis=1)


def _gelu_tanh(x):
    return 0.5 * x * (1.0 + jnp.tanh(math.sqrt(2.0 / math.pi) * (x + 0.044715 * (x * x * x))))


def _silu(x):
    return x / (1.0 + jnp.exp(-x))


def _normed_input(x_ref, ng_ref):
    x = x_ref[0]
    ms = jnp.mean(x * x, axis=-1, keepdims=True)
    return (x * lax.rsqrt(ms + EPS) * ng_ref[...]).astype(BF16)


def _inproj_kernel(x_ref, ng_ref, w_ref, qg_ref, kg_ref, sg_ref, hm_ref, sw_ref, sb_ref,
                   *out_refs, tm, period):
    xn = _normed_input(x_ref, ng_ref)

    def seg(i):
        return jnp.dot(xn, w_ref[:, i * SEG:(i + 1) * SEG], preferred_element_type=F32)

    _project_branches(seg, qg_ref, kg_ref, sg_ref, hm_ref, sw_ref, sb_ref, out_refs,
                      tm=tm, period=period, attn_layout=True)


def _inproj_sample_kernel(x_ref, ng_ref, w_ref, qg_ref, kg_ref, sg_ref, hm_ref, sw_ref, sb_ref,
                          *refs, tm, period):
    *out_refs, wbf_ref, xn_s, z_s = refs
    j = pl.program_id(0)

    @pl.when(j == 0)
    def _():
        xn_s[...] = _normed_input(x_ref, ng_ref)

    w_bf = w_ref[...].astype(BF16)
    wbf_ref[...] = w_bf
    z_s[j] = jnp.dot(xn_s[...], w_bf, preferred_element_type=F32)

    @pl.when(j == pl.num_programs(0) - 1)
    def _():
        _project_branches(lambda i: z_s[i], qg_ref, kg_ref, sg_ref, hm_ref, sw_ref, sb_ref,
                          out_refs, tm=tm, period=period, attn_layout=False)


def _project_branches(seg, qg_ref, kg_ref, sg_ref, hm_ref, sw_ref, sb_ref, out_refs, *,
                      tm, period, attn_layout):
    if attn_layout:
        q_ref, ktf_ref, vtf_ref, kt_ref, vb_ref, gsb_ref, osgu_ref = out_refs
    else:
        q_ref, k_ref, v_ref, gsb_ref, osgu_ref, vs_ref = out_refs

    def head_rms(t, g):
        sq = (t * t).astype(BF16)
        ms_h = jnp.concatenate(
            [jnp.dot(sq[:, c:c + MXU_TILE], hm_ref[...], preferred_element_type=F32)
             for c in range(0, SB_WIDTH, MXU_TILE)], axis=1)
        return t * lax.rsqrt(ms_h + EPS) * g

    q = head_rms(seg(0), qg_ref[...])
    q_ref[0] = (q * Q_SCALE).astype(BF16)

    k = head_rms(seg(1), kg_ref[...])
    v = seg(2)
    if attn_layout:
        k_t = k.T
        ktf_ref[0] = k_t
        kt_ref[0] = k_t.astype(BF16)
        vtf_ref[0] = v.T
        vb_ref[0] = v.astype(BF16)
    else:
        k_ref[0] = k
        v_ref[0] = v

    gsb_ref[0] = _silu(seg(3)).astype(BF16)

    u = _gelu_tanh(seg(4))
    vs_raw = _gelu_tanh(seg(5))
    vs_groups = []
    for g in range(GROUPS):
        t = vs_raw[:, g * GROUP_W:(g + 1) * GROUP_W]
        ms_g = jnp.mean(t * t, axis=-1, keepdims=True)
        vs_groups.append(t * lax.rsqrt(ms_g + EPS) * sg_ref[:, g * GROUP_W:(g + 1) * GROUP_W])
    if not attn_layout:
        vs_ref[0] = jnp.concatenate(vs_groups, axis=1)

    row = lax.broadcasted_iota(jnp.int32, (SGU_CHUNK, SGU_CHUNK), 0)
    col = lax.broadcasted_iota(jnp.int32, (SGU_CHUNK, SGU_CHUNK), 1)
    keep = row >= col
    if period != SGU_CHUNK:
        keep = keep & ((row // period) == (col // period))
    s_groups = []
    for g in range(GROUPS):
        wm = jnp.where(keep, sw_ref[g], 0.0).astype(BF16)
        vg = vs_groups[g].astype(BF16)
        chunks = [jnp.dot(wm, vg[c * SGU_CHUNK:(c + 1) * SGU_CHUNK], preferred_element_type=F32)
                  + sb_ref[:, g * GROUP_W:(g + 1) * GROUP_W]
                  for c in range(tm // SGU_CHUNK)]
        s_groups.append(jnp.concatenate(chunks, axis=0) if len(chunks) > 1 else chunks[0])
    s = jnp.concatenate(s_groups, axis=1)

    osgu_ref[0] = (u * s * _silu(seg(6))).astype(BF16)


def _inproj_consts(norm_g, q_norm_g, k_norm_g, sgu_norm_g):
    return (norm_g.reshape(1, D_MODEL), jnp.tile(q_norm_g, HEADS).reshape(1, SB_WIDTH),
            jnp.tile(k_norm_g, HEADS).reshape(1, SB_WIDTH), sgu_norm_g.reshape(1, SGU_WIDTH),
            _head_mean_matrix())


def _inproj_prompt(x, norm_g, w_in_bf, q_norm_g, k_norm_g, sgu_norm_g, sgu_w, sgu_bias, *, tm):
    b, t, _ = x.shape
    assert t % tm == 0 and tm % SGU_CHUNK == 0
    tok = lambda w: pl.BlockSpec((1, tm, w), lambda bi, i: (bi, i, 0))
    const = lambda a: pl.BlockSpec(a.shape, lambda bi, i: (0,) * a.ndim)
    ng, qg, kg, sg, hm = _inproj_consts(norm_g, q_norm_g, k_norm_g, sgu_norm_g)
    act = jax.ShapeDtypeStruct((b, t, SEG), BF16)
    feat = lambda dt: jax.ShapeDtypeStruct((b, SB_WIDTH, t), dt)
    feat_blk = pl.BlockSpec((1, SB_WIDTH, tm), lambda bi, i: (bi, 0, i))
    block_bytes = (_nbytes((tm, D_MODEL), F32) + _nbytes(w_in_bf.shape, BF16)
                   + _nbytes(hm.shape, BF16) + _nbytes(sgu_w.shape, F32)
                   + _nbytes(sgu_bias.shape, F32) + 7 * _nbytes((tm, SEG), F32))
    temp_bytes = 12 * _nbytes((tm, SEG), F32)
    return pl.pallas_call(
        functools.partial(_inproj_kernel, tm=tm, period=SGU_CHUNK),
        grid=(b, t // tm),
        in_specs=[tok(D_MODEL), const(ng), const(w_in_bf), const(qg), const(kg), const(sg),
                  const(hm), const(sgu_w), const(sgu_bias)],
        out_specs=(tok(SEG), feat_blk, feat_blk, feat_blk, tok(SEG), tok(SEG), tok(SEG)),
        out_shape=(act, feat(F32), feat(F32), feat(BF16), act, act, act),
        compiler_params=pltpu.CompilerParams(
            dimension_semantics=("arbitrary", "arbitrary"),
            vmem_limit_bytes=_vmem_limit(block_bytes, temp_bytes)),
        name="inproj_prompt",
    )(x, ng, w_in_bf, qg, kg, sg, hm, sgu_w, sgu_bias)


def _inproj_sample(x, norm_g, w_in_f32, q_norm_g, k_norm_g, sgu_norm_g, sgu_w_tiled, sgu_bias, *,
                   period):
    _, tm, _ = x.shape
    n_seg = w_in_f32.shape[1] // SEG
    assert tm == SGU_CHUNK and w_in_f32.shape[1] % SEG == 0
    const = lambda a: pl.BlockSpec(a.shape, lambda j: (0,) * a.ndim)
    w_seg = pl.BlockSpec((D_MODEL, SEG), lambda j: (0, j))
    ng, qg, kg, sg, hm = _inproj_consts(norm_g, q_norm_g, k_norm_g, sgu_norm_g)
    act = lambda dt: jax.ShapeDtypeStruct((1, tm, SEG), dt)
    act_blk = pl.BlockSpec((1, tm, SEG), lambda j: (0, 0, 0))
    block_bytes = (_nbytes((tm, D_MODEL), F32) + _nbytes((D_MODEL, SEG), F32)
                   + _nbytes((D_MODEL, SEG), BF16) + _nbytes(hm.shape, BF16)
                   + _nbytes(sgu_w_tiled.shape, F32) + _nbytes(sgu_bias.shape, F32)
                   + 6 * _nbytes((tm, SEG), F32))
    temp_bytes = (n_seg + 12) * _nbytes((tm, SEG), F32) + 2 * _nbytes((D_MODEL, SEG), F32)
    return pl.pallas_call(
        functools.partial(_inproj_sample_kernel, tm=tm, period=period),
        grid=(n_seg,),
        in_specs=[const(x), const(ng), w_seg, const(qg), const(kg), const(sg), const(hm),
                  const(sgu_w_tiled), const(sgu_bias)],
        out_specs=(act_blk,) * 6 + (w_seg,),
        out_shape=(act(BF16), act(F32), act(F32), act(BF16), act(BF16), act(F32),
                   jax.ShapeDtypeStruct(w_in_f32.shape, BF16)),
        scratch_shapes=[pltpu.VMEM((tm, D_MODEL), BF16), pltpu.VMEM((n_seg, tm, SEG), F32)],
        compiler_params=pltpu.CompilerParams(
            dimension_semantics=("arbitrary",),
            vmem_limit_bytes=_vmem_limit(block_bytes, temp_bytes)),
        name="inproj_sample",
    )(x, ng, w_in_f32, qg, kg, sg, hm, sgu_w_tiled, sgu_bias)


def _neg_log2_keep(z, vis):
    sp = jnp.maximum(z, jnp.log2(1.0 + jnp.exp2(jnp.minimum(z, SOFTPLUS_LINEAR))))
    return sp if vis is None else jnp.where(vis, sp, 0.0)


def _sb_weights(z, vis, sums, rem):
    w = jnp.exp2(z + sums[:, :KSUB] + rem)
    if vis is not None:
        w = jnp.where(vis, w, 0.0)
    return w.astype(BF16), rem + sums[:, KSUB:]


def _attn_prompt_kernel(q_ref, g_ref, uu_ref, kt_hbm, v_hbm, o_ref,
                        kwin, vwin, kold, vold, sem, rem_ref, acc_ref, *, tq, hps, nb, nq):
    qi = pl.program_id(0)
    slot = qi % 2
    nv = nb * hps

    def window(step, slot_):
        first = pl.multiple_of(jnp.maximum(step - 1, 0) * tq, tq)
        return ([pltpu.make_async_copy(kt_hbm.at[r, :, pl.ds(first, 2 * tq)], kwin.at[slot_, r],
                                       sem.at[0, slot_]) for r in range(nb)]
                + [pltpu.make_async_copy(v_hbm.at[r, pl.ds(first, 2 * tq), :], vwin.at[slot_, r],
                                         sem.at[1, slot_]) for r in range(nb)])

    def older(n):
        first = pl.multiple_of((qi - n) * tq, tq)
        return ([pltpu.make_async_copy(kt_hbm.at[r, :, pl.ds(first, tq)], kold.at[r], sem.at[0, 2])
                 for r in range(nb)]
                + [pltpu.make_async_copy(v_hbm.at[r, pl.ds(first, tq), :], vold.at[r], sem.at[1, 2])
                   for r in range(nb)])

    @pl.when(qi == 0)
    def _():
        for c in window(qi, slot):
            c.start()

    for c in window(qi, slot):
        c.wait()

    @pl.when(qi + 1 < nq)
    def _():
        for c in window(qi + 1, 1 - slot):
            c.start()

    qh = [q_ref[v // hps][:, (v % hps) * HEAD_DIM:(v % hps + 1) * HEAD_DIM] for v in range(nv)]
    uu = uu_ref[...]
    lane = lax.broadcasted_iota(jnp.int32, (KSUB, LANES), 1)
    first_head = lane < HEAD_DIM

    def keys_at(src, r, ks):
        return (kwin[slot, r, :, pl.ds(ks, KSUB)] if src == "window"
                else kold[r, :, pl.ds(ks, KSUB)])

    def pair_values(src, ks, p):
        pairs = hps // 2
        lanes = pl.ds((p % pairs) * LANES, LANES)
        vv = (vwin[slot, p // pairs, pl.ds(ks, KSUB), lanes] if src == "window"
              else vold[p // pairs, pl.ds(ks, KSUB), lanes])
        zero = jnp.zeros_like(vv)
        return jnp.concatenate([jnp.where(first_head, vv, zero),
                                jnp.where(first_head, zero, vv)], axis=0)

    def sweep(tiles, rems, accs):
        def put(full, r0, r1, part):
            pieces = ([full[:r0]] if r0 else []) + [part] + ([full[r1:]] if r1 < tq else [])
            return jnp.concatenate(pieces, axis=0) if len(pieces) > 1 else part

        zs, splits = [], []
        for src, ks, r0, r1, diagonal in tiles:
            kts = [keys_at(src, r, ks) for r in range(nb)]
            vis = None
            if diagonal:
                row = lax.broadcasted_iota(jnp.int32, (r1 - r0, KSUB), 0)
                col = lax.broadcasted_iota(jnp.int32, (r1 - r0, KSUB), 1)
                vis = col < row
            for v in range(nv):
                h = v % hps
                z = jnp.dot(qh[v][r0:r1], kts[v // hps][h * HEAD_DIM:(h + 1) * HEAD_DIM],
                            preferred_element_type=F32)
                zs.append((z, vis))
                splits.append(_split_bf16(_neg_log2_keep(z, vis)))
        sums = jnp.dot(jnp.concatenate(splits, axis=0), uu, preferred_element_type=F32)
        rems, accs = list(rems), list(accs)
        off, i = 0, 0
        for src, ks, r0, r1, _ in tiles:
            ws = []
            for h in range(nv):
                z, vis = zs[i]
                i += 1
                w, new = _sb_weights(z, vis, sums[off:off + r1 - r0], rems[h][r0:r1])
                off += r1 - r0
                rems[h] = put(rems[h], r0, r1, new)
                ws.append(w)
            for p in range(nv // 2):
                d = jnp.dot(jnp.concatenate(ws[2 * p:2 * p + 2], axis=1), pair_values(src, ks, p),
                            preferred_element_type=F32)
                accs[p] = put(accs[p], r0, r1, accs[p][r0:r1] + d)
        return rems, accs

    def diagonal_tiles(at):
        return [("window", at + s * KSUB, s * KSUB, tq, True) for s in reversed(range(NSUB))]

    def key_block_tiles(src, at, rows=None):
        rows = rows or {s: (0, tq) for s in range(NSUB)}
        return [(src, at + s * KSUB, *rows[s], False) for s in reversed(range(NSUB)) if s in rows]

    zero_rems = [jnp.zeros((tq, LANES), F32)] * nv
    zero_accs = [jnp.zeros((tq, LANES), F32)] * (nv // 2)

    def finish(accs):
        pairs = hps // 2
        for r in range(nb):
            acc = jnp.concatenate(accs[r * pairs:(r + 1) * pairs], axis=1)
            o_ref[r] = (acc * g_ref[r].astype(F32)).astype(BF16)

    def any_alive(rems):
        most = functools.reduce(jnp.maximum, rems)
        return (jnp.max(most) > REM_DEAD_LOG2).astype(jnp.int32)

    @pl.when(qi == 0)
    def _():
        finish(sweep(diagonal_tiles(0), zero_rems, zero_accs)[1])

    def load_state():
        return [rem_ref[h] for h in range(nv)], [acc_ref[p] for p in range(nv // 2)]

    def store_state(rems, accs):
        for h in range(nv):
            rem_ref[h] = rems[h]
        for p in range(nv // 2):
            acc_ref[p] = accs[p]

    eager_rows = {s: min(tq, EAGER_SEEN - (NSUB - 1 - s) * KSUB) for s in range(NSUB)
                  if EAGER_SEEN > (NSUB - 1 - s) * KSUB}

    @pl.when(qi > 0)
    def _():
        first = key_block_tiles("window", 0, {s: (0, e) for s, e in eager_rows.items()})
        rems, accs = sweep(diagonal_tiles(tq) + first, zero_rems, zero_accs)
        store_state(rems, accs)
        alive = any_alive(rems)
        for s in reversed(range(NSUB)):
            e = eager_rows.get(s, 0)
            if e < tq:
                @pl.when(any_alive([r[e:] for r in rems]) > 0)
                def _():
                    store_state(*sweep(key_block_tiles("window", 0, {s: (e, tq)}), *load_state()))

        def cond(c):
            n, alive = c
            return jnp.logical_and(n <= qi, alive > 0)

        def body(c):
            n, _ = c
            for cp in older(n):
                cp.start()
            for cp in older(n):
                cp.wait()
            rems, accs = sweep(key_block_tiles("older", 0), *load_state())
            store_state(rems, accs)
            return n + 1, any_alive(rems)

        lax.while_loop(cond, body, (jnp.int32(2), alive))
        finish(load_state()[1])


def _attn_prompt(q, kt, vb, gsb, *, tq, hps):
    nb, t, _ = q.shape
    assert t % tq == 0 and t >= 2 * tq and tq == NSUB * KSUB and hps == HEADS
    nv = nb * hps
    nq = t // tq
    uu = _suffix_sum_matrix()
    row_blk = pl.BlockSpec((nb, tq, SB_WIDTH), lambda i: (0, i, 0))
    in_hbm = pl.BlockSpec(memory_space=pltpu.HBM)
    block_bytes = 3 * _nbytes((nb, tq, SB_WIDTH), BF16) + _nbytes(uu.shape, BF16)
    temp_bytes = ((nv + nv // 2) * _nbytes((tq, LANES), F32)
                  + 5 * _nbytes((nb, SB_WIDTH, 2 * tq), BF16)
                  + 5 * (NSUB + 2) * nv * _nbytes((tq, KSUB), F32))
    return pl.pallas_call(
        functools.partial(_attn_prompt_kernel, tq=tq, hps=hps, nb=nb, nq=nq),
        grid=(nq,),
        in_specs=[row_blk, row_blk, pl.BlockSpec(uu.shape, lambda i: (0, 0)), in_hbm, in_hbm],
        out_specs=row_blk,
        out_shape=jax.ShapeDtypeStruct((nb, t, SB_WIDTH), BF16),
        scratch_shapes=[pltpu.VMEM((2, nb, SB_WIDTH, 2 * tq), BF16),
                        pltpu.VMEM((2, nb, 2 * tq, SB_WIDTH), BF16),
                        pltpu.VMEM((nb, SB_WIDTH, tq), BF16),
                        pltpu.VMEM((nb, tq, SB_WIDTH), BF16),
                        pltpu.SemaphoreType.DMA((2, 3)),
                        pltpu.VMEM((nv, tq, LANES), F32), pltpu.VMEM((nv // 2, tq, LANES), F32)],
        compiler_params=pltpu.CompilerParams(
            dimension_semantics=("arbitrary",),
            vmem_limit_bytes=_vmem_limit(block_bytes, temp_bytes)),
        name="sb_attn_prompt",
    )(q, gsb, uu, kt, vb)


def _attn_sample_kernel(q_ref, kn_ref, vn_ref, g_ref, uu_ref, ck_hbm, cv_hbm, o_ref,
                        kwin, vwin, sem, rem_s, acc_s, *, t_new, past, nb):
    rows = HEADS * t_new
    uu = uu_ref[...]
    n_sub = past // KSUB
    contract_last = (((1,), (1,)), ((), ()))

    def fetch(which, b, j, slot):
        src, dst = ((ck_hbm, kwin), (cv_hbm, vwin))[which]
        first = pl.multiple_of(j * KSUB, KSUB)
        return pltpu.make_async_copy(src.at[b, :, :, pl.ds(first, KSUB)], dst.at[b, slot],
                                     sem.at[which, b, slot])

    for b in range(nb):
        for slot in range(EAGER_CACHE_SUBS):
            for which in range(2):
                fetch(which, b, n_sub - 1 - slot, slot).start()

    row = lax.broadcasted_iota(jnp.int32, (rows, KSUB), 0)
    col = lax.broadcasted_iota(jnp.int32, (rows, KSUB), 1)
    new_vis = col < (row % t_new)
    pad = jnp.zeros((KSUB - t_new, SB_WIDTH), F32)

    def new_block(ref, b):
        blk = jnp.concatenate([ref[b], pad], axis=0).astype(BF16)
        return [blk[:, h * HEAD_DIM:(h + 1) * HEAD_DIM] for h in range(HEADS)]

    def cached(win, b, slot):
        return [win[b, slot, h].astype(BF16) for h in range(HEADS)]

    def sweep(b, tiles, rem, acc):
        q = q_ref[b]
        qh = [q[:, h * HEAD_DIM:(h + 1) * HEAD_DIM] for h in range(HEADS)]
        zs = [jnp.concatenate(
            [jnp.dot(qh[h], keys[h], preferred_element_type=F32) if dim_major else
             lax.dot_general(qh[h], keys[h], contract_last, preferred_element_type=F32)
             for h in range(HEADS)], axis=0) for keys, _, dim_major, _ in tiles]
        splits = [_split_bf16(_neg_log2_keep(z, t[3])) for z, t in zip(zs, tiles)]
        sums = jnp.dot(jnp.concatenate(splits, axis=0), uu, preferred_element_type=F32)
        for i, (z, (_, vals, dim_major, vis)) in enumerate(zip(zs, tiles)):
            w, rem = _sb_weights(z, vis, sums[i * rows:(i + 1) * rows], rem)
            wh = [w[h * t_new:(h + 1) * t_new] for h in range(HEADS)]
            acc = [a + (lax.dot_general(wh[h], vals[h], contract_last, preferred_element_type=F32)
                        if dim_major else jnp.dot(wh[h], vals[h], preferred_element_type=F32))
                   for h, a in enumerate(acc)]
        return rem, acc

    def alive_of(rem):
        return (jnp.max(rem) > REM_DEAD_LOG2).astype(jnp.int32)

    for b in range(nb):
        for slot in range(EAGER_CACHE_SUBS):
            for which in range(2):
                fetch(which, b, n_sub - 1 - slot, slot).wait()
    alive = jnp.int32(0)
    for b in range(nb):
        tiles = [(new_block(kn_ref, b), new_block(vn_ref, b), False, new_vis)]
        tiles += [(cached(kwin, b, s), cached(vwin, b, s), True, None)
                  for s in range(EAGER_CACHE_SUBS)]
        rem, acc = sweep(b, tiles, jnp.zeros((rows, LANES), F32),
                         [jnp.zeros((t_new, HEAD_DIM), F32)] * HEADS)
        rem_s[b] = rem
        acc_s[b] = jnp.concatenate(acc, axis=1)
        alive = jnp.maximum(alive, alive_of(rem))

    def cond(c):
        j, alive = c
        return jnp.logical_and(j >= 0, alive > 0)

    def body(c):
        j, _ = c
        for b in range(nb):
            for which in range(2):
                fetch(which, b, j, 0).start()
        alive = jnp.int32(0)
        for b in range(nb):
            for which in range(2):
                fetch(which, b, j, 0).wait()
            acc_b = acc_s[b]
            rem, acc = sweep(b, [(cached(kwin, b, 0), cached(vwin, b, 0), True, None)], rem_s[b],
                             [acc_b[:, h * HEAD_DIM:(h + 1) * HEAD_DIM] for h in range(HEADS)])
            rem_s[b] = rem
            acc_s[b] = jnp.concatenate(acc, axis=1)
            alive = jnp.maximum(alive, alive_of(rem))
        return j - 1, alive

    lax.while_loop(cond, body, (jnp.int32(n_sub - 1 - EAGER_CACHE_SUBS), alive))
    for b in range(nb):
        o_ref[b] = (acc_s[b] * g_ref[b].astype(F32)).astype(BF16)


def _attn_sample(q, cache_k, cache_v, k_new, v_new, gsb):
    nb, t_new, _ = q.shape
    past = cache_k.shape[3]
    assert cache_k.shape == (nb, HEADS, HEAD_DIM, past)
    assert past % KSUB == 0 and past // KSUB >= EAGER_CACHE_SUBS and t_new <= KSUB
    uu = _suffix_sum_matrix()
    rows = HEADS * t_new
    whole = lambda a: pl.BlockSpec(a.shape, lambda i: (0,) * a.ndim)
    window = (nb, EAGER_CACHE_SUBS, HEADS, HEAD_DIM, KSUB)
    block_bytes = sum(_nbytes(a.shape, a.dtype) for a in (q, k_new, v_new, gsb, uu, q))
    temp_bytes = (2 * _nbytes(window, F32) + _nbytes((nb, rows, LANES), F32)
                  + _nbytes((nb, t_new, SB_WIDTH), F32)
                  + 16 * (EAGER_CACHE_SUBS + 1) * _nbytes((rows, KSUB), F32))
    return pl.pallas_call(
        functools.partial(_attn_sample_kernel, t_new=t_new, past=past, nb=nb),
        grid=(1,),
        in_specs=[whole(q), whole(k_new), whole(v_new), whole(gsb), whole(uu),
                  pl.BlockSpec(memory_space=pltpu.HBM), pl.BlockSpec(memory_space=pltpu.HBM)],
        out_specs=whole(q),
        out_shape=jax.ShapeDtypeStruct((nb, t_new, SB_WIDTH), BF16),
        scratch_shapes=[pltpu.VMEM(window, F32), pltpu.VMEM(window, F32),
                        pltpu.SemaphoreType.DMA((2, nb, EAGER_CACHE_SUBS)),
                        pltpu.VMEM((nb, rows, LANES), F32),
                        pltpu.VMEM((nb, t_new, SB_WIDTH), F32)],
        compiler_params=pltpu.CompilerParams(
            dimension_semantics=("arbitrary",),
            vmem_limit_bytes=max(_vmem_limit(block_bytes, temp_bytes),
                                 VMEM_BYTES_V7X - _nbytes(cache_k.shape, F32))),
        name="sb_attn_sample",
    )(q, k_new, v_new, gsb, uu, cache_k, cache_v)


def _outproj_kernel(x_ref, osb_ref, osgu_ref, p_ref, wo_ref, pg_ref, wg_ref, wp_ref, y_ref):
    h = (x_ref[0]
         + jnp.dot(osb_ref[0], wo_ref[:SB_WIDTH], preferred_element_type=F32)
         + jnp.dot(osgu_ref[0], wo_ref[SB_WIDTH:], preferred_element_type=F32))
    ms = jnp.mean(h * h, axis=-1, keepdims=True)
    hn = (h * lax.rsqrt(ms + EPS) * pg_ref[...]).astype(BF16)
    gate_logit = jnp.dot(hn, wg_ref[...], preferred_element_type=F32)
    gate = 1.0 / (1.0 + jnp.exp(-gate_logit))
    pp = jnp.dot(p_ref[0].astype(BF16), wp_ref[...], preferred_element_type=F32)
    y_ref[0] = h + gate * pp


def _outproj(x, osb, osgu, p, w_out_bf, ple_norm_g, w_gate_bf, w_proj_bf, *, tm, name):
    b, t, _ = x.shape
    assert t % tm == 0
    tok = lambda w: pl.BlockSpec((1, tm, w), lambda bi, i: (bi, i, 0))
    const2 = lambda a: pl.BlockSpec(a.shape, lambda bi, i: (0, 0))
    pg = ple_norm_g.reshape(1, D_MODEL)
    block_bytes = (2 * _nbytes((tm, D_MODEL), F32) + 2 * _nbytes((tm, SEG), BF16)
                   + _nbytes((tm, PLE_DIM), F32) + _nbytes(w_out_bf.shape, BF16)
                   + _nbytes(w_gate_bf.shape, BF16) + _nbytes(w_proj_bf.shape, BF16))
    temp_bytes = 6 * _nbytes((tm, D_MODEL), F32)
    return pl.pallas_call(
        _outproj_kernel,
        grid=(b, t // tm),
        in_specs=[tok(D_MODEL), tok(SEG), tok(SEG), tok(PLE_DIM), const2(w_out_bf), const2(pg),
                  const2(w_gate_bf), const2(w_proj_bf)],
        out_specs=tok(D_MODEL),
        out_shape=jax.ShapeDtypeStruct((b, t, D_MODEL), F32),
        compiler_params=pltpu.CompilerParams(
            dimension_semantics=("arbitrary", "arbitrary"),
            vmem_limit_bytes=_vmem_limit(block_bytes, temp_bytes)),
        name=name,
    )(x, osb, osgu, p, w_out_bf, pg, w_gate_bf, w_proj_bf)


def _sgu_bias_rows(sgu_b_l, period):
    per_pos = jnp.tile(sgu_b_l[:, :period].T, (SGU_CHUNK // period, 1))
    return jnp.repeat(per_pos, GROUP_W, axis=1)


def kernel(x_prompt, x_sample, cache_k, cache_v, p_prompt, p_sample, norm_g, w_in, q_norm_g,
           k_norm_g, sgu_norm_g, sgu_w, sgu_b, w_out, ple_norm_g, w_ple_gate, w_ple_proj):
    depth = w_in.shape[0]
    assert depth == 1, "one layer per call"
    l = 0
    b_p, t_p, _ = x_prompt.shape
    b_s, t_s, _ = x_sample.shape
    past = cache_k.shape[2]
    n_s = b_s * t_s
    assert n_s == SGU_CHUNK and SGU_CHUNK % t_s == 0

    w_out_bf = w_out[l].astype(BF16)
    w_gate_bf = w_ple_gate[l].astype(BF16)
    w_proj_bf = w_ple_proj[l].astype(BF16)

    rep = SGU_CHUNK // t_s
    sgu_w_s = jnp.tile(sgu_w[l][:, :t_s, :t_s], (1, rep, rep))
    xs = x_sample.reshape(1, n_s, D_MODEL)
    q_s, k_s, v_s, gsb_s, osgu_s, vs_s, w_in_bf = _inproj_sample(
        xs, norm_g[l], w_in[l], q_norm_g[l], k_norm_g[l], sgu_norm_g[l],
        sgu_w_s, _sgu_bias_rows(sgu_b[l], t_s), period=t_s)

    q, k_feat, v_feat, kt, vb, gsb, osgu = _inproj_prompt(
        x_prompt, norm_g[l], w_in_bf, q_norm_g[l], k_norm_g[l], sgu_norm_g[l],
        sgu_w[l], _sgu_bias_rows(sgu_b[l], SGU_CHUNK), tm=TM_IN)
    osb = _attn_prompt(q, kt, vb, gsb, tq=TQ, hps=HEADS_PER_STEP)
    y_prompt = _outproj(x_prompt, osb, osgu, p_prompt[l], w_out_bf, ple_norm_g[l], w_gate_bf,
                        w_proj_bf, tm=TM_OUT, name="outproj_prompt")

    shp = (b_s, t_s, SB_WIDTH)
    to_feat = lambda c: jnp.transpose(c[l], (0, 2, 3, 1))
    osb_s = _attn_sample(q_s.reshape(shp), to_feat(cache_k), to_feat(cache_v),
                         k_s.reshape(shp), v_s.reshape(shp),
                         gsb_s.reshape(shp))
    y_sample = _outproj(xs, osb_s.reshape(1, n_s, SB_WIDTH), osgu_s, p_sample[l].reshape(1, n_s, PLE_DIM),
                        w_out_bf, ple_norm_g[l], w_gate_bf, w_proj_bf, tm=n_s,
                        name="outproj_sample").reshape(b_s, t_s, D_MODEL)

    head_shape = lambda a, bb, tt: a.reshape(1, bb, tt, HEADS, HEAD_DIM)
    from_feat = lambda a: jnp.transpose(a.reshape(b_p, HEADS, HEAD_DIM, t_p), (0, 3, 1, 2))[None]
    return (y_prompt, y_sample,
            from_feat(k_feat), from_feat(v_feat),
            head_shape(k_s, b_s, t_s), head_shape(v_s, b_s, t_s),
            vs_s.reshape(1, b_s, t_s, GROUPS, GROUP_W))
```

```python
import functools
import math

import numpy as np
import jax
import jax.numpy as jnp
from jax import lax
from jax.experimental import pallas as pl
from jax.experimental.pallas import tpu as pltpu

F32 = jnp.float32
BF16 = jnp.bfloat16

LANES = 128
MXU_TILE = 256
VMEM_BYTES_V7X = 64 * 1024 * 1024

D_MODEL = 1024
PLE_DIM = 256
HEADS = 8
HEAD_DIM = 64
SB_WIDTH = HEADS * HEAD_DIM
GROUPS = 4
GROUP_W = 128
SGU_WIDTH = GROUPS * GROUP_W
SGU_CHUNK = 128
SEG = 512
EPS = 1e-6

Q_SCALE = HEAD_DIM ** -0.5 * math.log2(math.e)

KSUB = LANES
NSUB = 2
TQ = NSUB * KSUB
HEADS_PER_STEP = 8
REM_DEAD_LOG2 = -152.0
SOFTPLUS_LINEAR = 64.0
EAGER_CACHE_SUBS = 2
EAGER_SEEN = KSUB + 48
TM_IN = 1024
TM_OUT = 1024


def _vmem_limit(block_bytes, temp_bytes):
    need = 2 * block_bytes + temp_bytes
    return int(min(need, VMEM_BYTES_V7X - 8 * 1024 * 1024))


def _nbytes(shape, dtype):
    return int(np.prod(shape)) * jnp.dtype(dtype).itemsize


def _suffix_sum_matrix():
    j = np.arange(KSUB)[:, None]
    s = np.arange(KSUB)[None, :]
    one = np.concatenate([np.where(j >= s, -1.0, 0.0), -np.ones((KSUB, LANES))], axis=1)
    return jnp.asarray(np.concatenate([one, one], axis=0), dtype=BF16)


def _head_mean_matrix():
    a = np.arange(MXU_TILE)
    bd = np.where(a[:, None] // HEAD_DIM == a[None, :] // HEAD_DIM, 1.0 / HEAD_DIM, 0.0)
    return jnp.asarray(bd, dtype=BF16)


def _split_bf16(x):
    hi = x.astype(BF16)
    lo = (x - hi.astype(F32)).astype(BF16)
    return jnp.concatenate([hi, lo], axis=1)


def _gelu_tanh(x):
    return 0.5 * x * (1.0 + jnp.tanh(math.sqrt(2.0 / math.pi) * (x + 0.044715 * (x * x * x))))


def _silu(x):
    return x / (1.0 + jnp.exp(-x))


def _normed_input(x_ref, ng_ref):
    x = x_ref[0]
    ms = jnp.mean(x * x, axis=-1, keepdims=True)
    return (x * lax.rsqrt(ms + EPS) * ng_ref[...]).astype(BF16)


def _inproj_kernel(x_ref, ng_ref, w_ref, qg_ref, kg_ref, sg_ref, hm_ref, sw_ref, sb_ref,
                   *out_refs, tm, period):
    xn = _normed_input(x_ref, ng_ref)

    def seg(i):
        return jnp.dot(xn, w_ref[:, i * SEG:(i + 1) * SEG], preferred_element_type=F32)

    _project_branches(seg, qg_ref, kg_ref, sg_ref, hm_ref, sw_ref, sb_ref, out_refs,
                      tm=tm, period=period, attn_layout=True)


def _inproj_sample_kernel(x_ref, ng_ref, w_ref, qg_ref, kg_ref, sg_ref, hm_ref, sw_ref, sb_ref,
                          *refs, tm, period):
    *out_refs, wbf_ref, xn_s, z_s = refs
    j = pl.program_id(0)

    @pl.when(j == 0)
    def _():
        xn_s[...] = _normed_input(x_ref, ng_ref)

    w_bf = w_ref[...].astype(BF16)
    wbf_ref[...] = w_bf
    z_s[j] = jnp.dot(xn_s[...], w_bf, preferred_element_type=F32)

    @pl.when(j == pl.num_programs(0) - 1)
    def _():
        _project_branches(lambda i: z_s[i], qg_ref, kg_ref, sg_ref, hm_ref, sw_ref, sb_ref,
                          out_refs, tm=tm, period=period, attn_layout=False)


def _project_branches(seg, qg_ref, kg_ref, sg_ref, hm_ref, sw_ref, sb_ref, out_refs, *,
                      tm, period, attn_layout):
    if attn_layout:
        q_ref, ktf_ref, vtf_ref, kt_ref, vb_ref, gsb_ref, osgu_ref = out_refs
    else:
        q_ref, k_ref, v_ref, gsb_ref, osgu_ref, vs_ref = out_refs

    def head_rms(t, g):
        sq = (t * t).astype(BF16)
        ms_h = jnp.concatenate(
            [jnp.dot(sq[:, c:c + MXU_TILE], hm_ref[...], preferred_element_type=F32)
             for c in range(0, SB_WIDTH, MXU_TILE)], axis=1)
        return t * lax.rsqrt(ms_h + EPS) * g

    q = head_rms(seg(0), qg_ref[...])
    q_ref[0] = (q * Q_SCALE).astype(BF16)

    k = head_rms(seg(1), kg_ref[...])
    v = seg(2)
    if attn_layout:
        k_t = k.T
        ktf_ref[0] = k_t
        kt_ref[0] = k_t.astype(BF16)
        vtf_ref[0] = v.T
        vb_ref[0] = v.astype(BF16)
    else:
        k_ref[0] = k
        v_ref[0] = v

    gsb_ref[0] = _silu(seg(3)).astype(BF16)

    u = _gelu_tanh(seg(4))
    vs_raw = _gelu_tanh(seg(5))
    vs_groups = []
    for g in range(GROUPS):
        t = vs_raw[:, g * GROUP_W:(g + 1) * GROUP_W]
        ms_g = jnp.mean(t * t, axis=-1, keepdims=True)
        vs_groups.append(t * lax.rsqrt(ms_g + EPS) * sg_ref[:, g * GROUP_W:(g + 1) * GROUP_W])
    if not attn_layout:
        vs_ref[0] = jnp.concatenate(vs_groups, axis=1)

    row = lax.broadcasted_iota(jnp.int32, (SGU_CHUNK, SGU_CHUNK), 0)
    col = lax.broadcasted_iota(jnp.int32, (SGU_CHUNK, SGU_CHUNK), 1)
    keep = row >= col
    if period != SGU_CHUNK:
        keep = keep & ((row // period) == (col // period))
    s_groups = []
    for g in range(GROUPS):
        wm = jnp.where(keep, sw_ref[g], 0.0).astype(BF16)
        vg = vs_groups[g].astype(BF16)
        chunks = [jnp.dot(wm, vg[c * SGU_CHUNK:(c + 1) * SGU_CHUNK], preferred_element_type=F32)
                  + sb_ref[:, g * GROUP_W:(g + 1) * GROUP_W]
                  for c in range(tm // SGU_CHUNK)]
        s_groups.append(jnp.concatenate(chunks, axis=0) if len(chunks) > 1 else chunks[0])
    s = jnp.concatenate(s_groups, axis=1)

    osgu_ref[0] = (u * s * _silu(seg(6))).astype(BF16)


def _inproj_consts(norm_g, q_norm_g, k_norm_g, sgu_norm_g):
    return (norm_g.reshape(1, D_MODEL), jnp.tile(q_norm_g, HEADS).reshape(1, SB_WIDTH),
            jnp.tile(k_norm_g, HEADS).reshape(1, SB_WIDTH), sgu_norm_g.reshape(1, SGU_WIDTH),
            _head_mean_matrix())


def _inproj_prompt(x, norm_g, w_in_bf, q_norm_g, k_norm_g, sgu_norm_g, sgu_w, sgu_bias, *, tm):
    b, t, _ = x.shape
    assert t % tm == 0 and tm % SGU_CHUNK == 0
    tok = lambda w: pl.BlockSpec((1, tm, w), lambda bi, i: (bi, i, 0))
    const = lambda a: pl.BlockSpec(a.shape, lambda bi, i: (0,) * a.ndim)
    ng, qg, kg, sg, hm = _inproj_consts(norm_g, q_norm_g, k_norm_g, sgu_norm_g)
    act = jax.ShapeDtypeStruct((b, t, SEG), BF16)
    feat = lambda dt: jax.ShapeDtypeStruct((b, SB_WIDTH, t), dt)
    feat_blk = pl.BlockSpec((1, SB_WIDTH, tm), lambda bi, i: (bi, 0, i))
    block_bytes = (_nbytes((tm, D_MODEL), F32) + _nbytes(w_in_bf.shape, BF16)
                   + _nbytes(hm.shape, BF16) + _nbytes(sgu_w.shape, F32)
                   + _nbytes(sgu_bias.shape, F32) + 7 * _nbytes((tm, SEG), F32))
    temp_bytes = 12 * _nbytes((tm, SEG), F32)
    return pl.pallas_call(
        functools.partial(_inproj_kernel, tm=tm, period=SGU_CHUNK),
        grid=(b, t // tm),
        in_specs=[tok(D_MODEL), const(ng), const(w_in_bf), const(qg), const(kg), const(sg),
                  const(hm), const(sgu_w), const(sgu_bias)],
        out_specs=(tok(SEG), feat_blk, feat_blk, feat_blk, tok(SEG), tok(SEG), tok(SEG)),
        out_shape=(act, feat(F32), feat(F32), feat(BF16), act, act, act),
        compiler_params=pltpu.CompilerParams(
            dimension_semantics=("arbitrary", "arbitrary"),
            vmem_limit_bytes=_vmem_limit(block_bytes, temp_bytes)),
        name="inproj_prompt",
    )(x, ng, w_in_bf, qg, kg, sg, hm, sgu_w, sgu_bias)


def _inproj_sample(x, norm_g, w_in_f32, q_norm_g, k_norm_g, sgu_norm_g, sgu_w_tiled, sgu_bias, *,
                   period):
    _, tm, _ = x.shape
    n_seg = w_in_f32.shape[1] // SEG
    assert tm == SGU_CHUNK and w_in_f32.shape[1] % SEG == 0
    const = lambda a: pl.BlockSpec(a.shape, lambda j: (0,) * a.ndim)
    w_seg = pl.BlockSpec((D_MODEL, SEG), lambda j: (0, j))
    ng, qg, kg, sg, hm = _inproj_consts(norm_g, q_norm_g, k_norm_g, sgu_norm_g)
    act = lambda dt: jax.ShapeDtypeStruct((1, tm, SEG), dt)
    act_blk = pl.BlockSpec((1, tm, SEG), lambda j: (0, 0, 0))
    block_bytes = (_nbytes((tm, D_MODEL), F32) + _nbytes((D_MODEL, SEG), F32)
                   + _nbytes((D_MODEL, SEG), BF16) + _nbytes(hm.shape, BF16)
                   + _nbytes(sgu_w_tiled.shape, F32) + _nbytes(sgu_bias.shape, F32)
                   + 6 * _nbytes((tm, SEG), F32))
    temp_bytes = (n_seg + 12) * _nbytes((tm, SEG), F32) + 2 * _nbytes((D_MODEL, SEG), F32)
    return pl.pallas_call(
        functools.partial(_inproj_sample_kernel, tm=tm, period=period),
        grid=(n_seg,),
        in_specs=[const(x), const(ng), w_seg, const(qg), const(kg), const(sg), const(hm),
                  const(sgu_w_tiled), const(sgu_bias)],
        out_specs=(act_blk,) * 6 + (w_seg,),
        out_shape=(act(BF16), act(F32), act(F32), act(BF16), act(BF16), act(F32),
                   jax.ShapeDtypeStruct(w_in_f32.shape, BF16)),
        scratch_shapes=[pltpu.VMEM((tm, D_MODEL), BF16), pltpu.VMEM((n_seg, tm, SEG), F32)],
        compiler_params=pltpu.CompilerParams(
            dimension_semantics=("arbitrary",),
            vmem_limit_bytes=max(_vmem_limit(block_bytes, temp_bytes),
                                 VMEM_BYTES_V7X - _nbytes(w_in_f32.shape, F32))),
        name="inproj_sample",
    )(x, ng, w_in_f32, qg, kg, sg, hm, sgu_w_tiled, sgu_bias)


def _neg_log2_keep(z, vis):
    sp = jnp.maximum(z, jnp.log2(1.0 + jnp.exp2(jnp.minimum(z, SOFTPLUS_LINEAR))))
    return sp if vis is None else jnp.where(vis, sp, 0.0)


def _sb_weights(z, vis, sums, rem):
    w = jnp.exp2(z + sums[:, :KSUB] + rem)
    if vis is not None:
        w = jnp.where(vis, w, 0.0)
    return w.astype(BF16), rem + sums[:, KSUB:]


def _attn_prompt_kernel(q_ref, g_ref, uu_ref, kt_hbm, v_hbm, o_ref,
                        kwin, vwin, kold, vold, sem, rem_ref, acc_ref, *, tq, hps, nb, nq):
    qi = pl.program_id(0)
    slot = qi % 2
    nv = nb * hps

    def window(step, slot_):
        first = pl.multiple_of(jnp.maximum(step - 1, 0) * tq, tq)
        return ([pltpu.make_async_copy(kt_hbm.at[r, :, pl.ds(first, 2 * tq)], kwin.at[slot_, r],
                                       sem.at[0, slot_]) for r in range(nb)]
                + [pltpu.make_async_copy(v_hbm.at[r, pl.ds(first, 2 * tq), :], vwin.at[slot_, r],
                                         sem.at[1, slot_]) for r in range(nb)])

    def older(n):
        first = pl.multiple_of((qi - n) * tq, tq)
        return ([pltpu.make_async_copy(kt_hbm.at[r, :, pl.ds(first, tq)], kold.at[r], sem.at[0, 2])
                 for r in range(nb)]
                + [pltpu.make_async_copy(v_hbm.at[r, pl.ds(first, tq), :], vold.at[r], sem.at[1, 2])
                   for r in range(nb)])

    @pl.when(qi == 0)
    def _():
        for c in window(qi, slot):
            c.start()

    for c in window(qi, slot):
        c.wait()

    @pl.when(qi + 1 < nq)
    def _():
        for c in window(qi + 1, 1 - slot):
            c.start()

    qh = [q_ref[v // hps][:, (v % hps) * HEAD_DIM:(v % hps + 1) * HEAD_DIM] for v in range(nv)]
    uu = uu_ref[...]
    lane = lax.broadcasted_iota(jnp.int32, (KSUB, LANES), 1)
    first_head = lane < HEAD_DIM

    def keys_at(src, r, ks):
        return (kwin[slot, r, :, pl.ds(ks, KSUB)] if src == "window"
                else kold[r, :, pl.ds(ks, KSUB)])

    def pair_values(src, ks, p):
        pairs = hps // 2
        lanes = pl.ds((p % pairs) * LANES, LANES)
        vv = (vwin[slot, p // pairs, pl.ds(ks, KSUB), lanes] if src == "window"
              else vold[p // pairs, pl.ds(ks, KSUB), lanes])
        zero = jnp.zeros_like(vv)
        return jnp.concatenate([jnp.where(first_head, vv, zero),
                                jnp.where(first_head, zero, vv)], axis=0)

    def sweep(tiles, rems, accs):
        def put(full, r0, r1, part):
            pieces = ([full[:r0]] if r0 else []) + [part] + ([full[r1:]] if r1 < tq else [])
            return jnp.concatenate(pieces, axis=0) if len(pieces) > 1 else part

        zs, splits = [], []
        for src, ks, r0, r1, diagonal in tiles:
            kts = [keys_at(src, r, ks) for r in range(nb)]
            vis = None
            if diagonal:
                row = lax.broadcasted_iota(jnp.int32, (r1 - r0, KSUB), 0)
                col = lax.broadcasted_iota(jnp.int32, (r1 - r0, KSUB), 1)
                vis = col < row
            for v in range(nv):
                h = v % hps
                z = jnp.dot(qh[v][r0:r1], kts[v // hps][h * HEAD_DIM:(h + 1) * HEAD_DIM],
                            preferred_element_type=F32)
                zs.append((z, vis))
                splits.append(_split_bf16(_neg_log2_keep(z, vis)))
        sums = jnp.dot(jnp.concatenate(splits, axis=0), uu, preferred_element_type=F32)
        rems, accs = list(rems), list(accs)
        off, i = 0, 0
        for src, ks, r0, r1, _ in tiles:
            ws = []
            for h in range(nv):
                z, vis = zs[i]
                i += 1
                w, new = _sb_weights(z, vis, sums[off:off + r1 - r0], rems[h][r0:r1])
                off += r1 - r0
                rems[h] = put(rems[h], r0, r1, new)
                ws.append(w)
            for p in range(nv // 2):
                d = jnp.dot(jnp.concatenate(ws[2 * p:2 * p + 2], axis=1), pair_values(src, ks, p),
                            preferred_element_type=F32)
                accs[p] = put(accs[p], r0, r1, accs[p][r0:r1] + d)
        return rems, accs

    def diagonal_tiles(at):
        return [("window", at + s * KSUB, s * KSUB, tq, True) for s in reversed(range(NSUB))]

    def key_block_tiles(src, at, rows=None):
        rows = rows or {s: (0, tq) for s in range(NSUB)}
        return [(src, at + s * KSUB, *rows[s], False) for s in reversed(range(NSUB)) if s in rows]

    zero_rems = [jnp.zeros((tq, LANES), F32)] * nv
    zero_accs = [jnp.zeros((tq, LANES), F32)] * (nv // 2)

    def finish(accs):
        pairs = hps // 2
        for r in range(nb):
            acc = jnp.concatenate(accs[r * pairs:(r + 1) * pairs], axis=1)
            o_ref[r] = (acc * g_ref[r].astype(F32)).astype(BF16)

    def any_alive(rems):
        most = functools.reduce(jnp.maximum, rems)
        return (jnp.max(most) > REM_DEAD_LOG2).astype(jnp.int32)

    @pl.when(qi == 0)
    def _():
        finish(sweep(diagonal_tiles(0), zero_rems, zero_accs)[1])

    def load_state():
        return [rem_ref[h] for h in range(nv)], [acc_ref[p] for p in range(nv // 2)]

    def store_state(rems, accs):
        for h in range(nv):
            rem_ref[h] = rems[h]
        for p in range(nv // 2):
            acc_ref[p] = accs[p]

    eager_rows = {s: min(tq, EAGER_SEEN - (NSUB - 1 - s) * KSUB) for s in range(NSUB)
                  if EAGER_SEEN > (NSUB - 1 - s) * KSUB}

    @pl.when(qi > 0)
    def _():
        first = key_block_tiles("window", 0, {s: (0, e) for s, e in eager_rows.items()})
        rems, accs = sweep(diagonal_tiles(tq) + first, zero_rems, zero_accs)
        store_state(rems, accs)
        alive = any_alive(rems)
        for s in reversed(range(NSUB)):
            e = eager_rows.get(s, 0)
            if e < tq:
                @pl.when(any_alive([r[e:] for r in rems]) > 0)
                def _():
                    store_state(*sweep(key_block_tiles("window", 0, {s: (e, tq)}), *load_state()))

        def cond(c):
            n, alive = c
            return jnp.logical_and(n <= qi, alive > 0)

        def body(c):
            n, _ = c
            for cp in older(n):
                cp.start()
            for cp in older(n):
                cp.wait()
            rems, accs = sweep(key_block_tiles("older", 0), *load_state())
            store_state(rems, accs)
            return n + 1, any_alive(rems)

        lax.while_loop(cond, body, (jnp.int32(2), alive))
        finish(load_state()[1])


def _attn_prompt(q, kt, vb, gsb, *, tq, hps):
    nb, t, _ = q.shape
    assert t % tq == 0 and t >= 2 * tq and tq == NSUB * KSUB and hps == HEADS
    nv = nb * hps
    nq = t // tq
    uu = _suffix_sum_matrix()
    row_blk = pl.BlockSpec((nb, tq, SB_WIDTH), lambda i: (0, i, 0))
    in_hbm = pl.BlockSpec(memory_space=pltpu.HBM)
    block_bytes = 3 * _nbytes((nb, tq, SB_WIDTH), BF16) + _nbytes(uu.shape, BF16)
    temp_bytes = ((nv + nv // 2) * _nbytes((tq, LANES), F32)
                  + 5 * _nbytes((nb, SB_WIDTH, 2 * tq), BF16)
                  + 5 * (NSUB + 2) * nv * _nbytes((tq, KSUB), F32))
    return pl.pallas_call(
        functools.partial(_attn_prompt_kernel, tq=tq, hps=hps, nb=nb, nq=nq),
        grid=(nq,),
        in_specs=[row_blk, row_blk, pl.BlockSpec(uu.shape, lambda i: (0, 0)), in_hbm, in_hbm],
        out_specs=row_blk,
        out_shape=jax.ShapeDtypeStruct((nb, t, SB_WIDTH), BF16),
        scratch_shapes=[pltpu.VMEM((2, nb, SB_WIDTH, 2 * tq), BF16),
                        pltpu.VMEM((2, nb, 2 * tq, SB_WIDTH), BF16),
                        pltpu.VMEM((nb, SB_WIDTH, tq), BF16),
                        pltpu.VMEM((nb, tq, SB_WIDTH), BF16),
                        pltpu.SemaphoreType.DMA((2, 3)),
                        pltpu.VMEM((nv, tq, LANES), F32), pltpu.VMEM((nv // 2, tq, LANES), F32)],
        compiler_params=pltpu.CompilerParams(
            dimension_semantics=("arbitrary",),
            vmem_limit_bytes=_vmem_limit(block_bytes, temp_bytes)),
        name="sb_attn_prompt",
    )(q, gsb, uu, kt, vb)


def _attn_sample_kernel(q_ref, kn_ref, vn_ref, g_ref, uu_ref, ck_hbm, cv_hbm, o_ref,
                        kwin, vwin, sem, rem_s, acc_s, *, t_new, past, nb):
    rows = HEADS * t_new
    uu = uu_ref[...]
    n_sub = past // KSUB
    contract_last = (((1,), (1,)), ((), ()))

    def fetch(which, b, j, slot):
        src, dst = ((ck_hbm, kwin), (cv_hbm, vwin))[which]
        first = pl.multiple_of(j * KSUB, KSUB)
        return pltpu.make_async_copy(src.at[b, :, :, pl.ds(first, KSUB)], dst.at[b, slot],
                                     sem.at[which, b, slot])

    for b in range(nb):
        for slot in range(EAGER_CACHE_SUBS):
            for which in range(2):
                fetch(which, b, n_sub - 1 - slot, slot).start()

    row = lax.broadcasted_iota(jnp.int32, (rows, KSUB), 0)
    col = lax.broadcasted_iota(jnp.int32, (rows, KSUB), 1)
    new_vis = col < (row % t_new)
    pad = jnp.zeros((KSUB - t_new, SB_WIDTH), F32)

    def new_block(ref, b):
        blk = jnp.concatenate([ref[b], pad], axis=0).astype(BF16)
        return [blk[:, h * HEAD_DIM:(h + 1) * HEAD_DIM] for h in range(HEADS)]

    def cached(win, b, slot):
        return [win[b, slot, h].astype(BF16) for h in range(HEADS)]

    def sweep(b, tiles, rem, acc):
        q = q_ref[b]
        qh = [q[:, h * HEAD_DIM:(h + 1) * HEAD_DIM] for h in range(HEADS)]
        zs = [jnp.concatenate(
            [jnp.dot(qh[h], keys[h], preferred_element_type=F32) if dim_major else
             lax.dot_general(qh[h], keys[h], contract_last, preferred_element_type=F32)
             for h in range(HEADS)], axis=0) for keys, _, dim_major, _ in tiles]
        splits = [_split_bf16(_neg_log2_keep(z, t[3])) for z, t in zip(zs, tiles)]
        sums = jnp.dot(jnp.concatenate(splits, axis=0), uu, preferred_element_type=F32)
        for i, (z, (_, vals, dim_major, vis)) in enumerate(zip(zs, tiles)):
            w, rem = _sb_weights(z, vis, sums[i * rows:(i + 1) * rows], rem)
            wh = [w[h * t_new:(h + 1) * t_new] for h in range(HEADS)]
            acc = [a + (lax.dot_general(wh[h], vals[h], contract_last, preferred_element_type=F32)
                        if dim_major else jnp.dot(wh[h], vals[h], preferred_element_type=F32))
                   for h, a in enumerate(acc)]
        return rem, acc

    def alive_of(rem):
        return (jnp.max(rem) > REM_DEAD_LOG2).astype(jnp.int32)

    for b in range(nb):
        for slot in range(EAGER_CACHE_SUBS):
            for which in range(2):
                fetch(which, b, n_sub - 1 - slot, slot).wait()
    alive = jnp.int32(0)
    for b in range(nb):
        tiles = [(new_block(kn_ref, b), new_block(vn_ref, b), False, new_vis)]
        tiles += [(cached(kwin, b, s), cached(vwin, b, s), True, None)
                  for s in range(EAGER_CACHE_SUBS)]
        rem, acc = sweep(b, tiles, jnp.zeros((rows, LANES), F32),
                         [jnp.zeros((t_new, HEAD_DIM), F32)] * HEADS)
        rem_s[b] = rem
        acc_s[b] = jnp.concatenate(acc, axis=1)
        alive = jnp.maximum(alive, alive_of(rem))

    def cond(c):
        j, alive = c
        return jnp.logical_and(j >= 0, alive > 0)

    def body(c):
        j, _ = c
        for b in range(nb):
            for which in range(2):
                fetch(which, b, j, 0).start()
        alive = jnp.int32(0)
        for b in range(nb):
            for which in range(2):
                fetch(which, b, j, 0).wait()
            acc_b = acc_s[b]
            rem, acc = sweep(b, [(cached(kwin, b, 0), cached(vwin, b, 0), True, None)], rem_s[b],
                             [acc_b[:, h * HEAD_DIM:(h + 1) * HEAD_DIM] for h in range(HEADS)])
            rem_s[b] = rem
            acc_s[b] = jnp.concatenate(acc, axis=1)
            alive = jnp.maximum(alive, alive_of(rem))
        return j - 1, alive

    lax.while_loop(cond, body, (jnp.int32(n_sub - 1 - EAGER_CACHE_SUBS), alive))
    for b in range(nb):
        o_ref[b] = (acc_s[b] * g_ref[b].astype(F32)).astype(BF16)


def _attn_sample(q, cache_k, cache_v, k_new, v_new, gsb):
    nb, t_new, _ = q.shape
    past = cache_k.shape[3]
    assert cache_k.shape == (nb, HEADS, HEAD_DIM, past)
    assert past % KSUB == 0 and past // KSUB >= EAGER_CACHE_SUBS and t_new <= KSUB
    uu = _suffix_sum_matrix()
    rows = HEADS * t_new
    whole = lambda a: pl.BlockSpec(a.shape, lambda i: (0,) * a.ndim)
    window = (nb, EAGER_CACHE_SUBS, HEADS, HEAD_DIM, KSUB)
    block_bytes = sum(_nbytes(a.shape, a.dtype) for a in (q, k_new, v_new, gsb, uu, q))
    temp_bytes = (2 * _nbytes(window, F32) + _nbytes((nb, rows, LANES), F32)
                  + _nbytes((nb, t_new, SB_WIDTH), F32)
                  + 16 * (EAGER_CACHE_SUBS + 1) * _nbytes((rows, KSUB), F32))
    return pl.pallas_call(
        functools.partial(_attn_sample_kernel, t_new=t_new, past=past, nb=nb),
        grid=(1,),
        in_specs=[whole(q), whole(k_new), whole(v_new), whole(gsb), whole(uu),
                  pl.BlockSpec(memory_space=pltpu.HBM), pl.BlockSpec(memory_space=pltpu.HBM)],
        out_specs=whole(q),
        out_shape=jax.ShapeDtypeStruct((nb, t_new, SB_WIDTH), BF16),
        scratch_shapes=[pltpu.VMEM(window, F32), pltpu.VMEM(window, F32),
                        pltpu.SemaphoreType.DMA((2, nb, EAGER_CACHE_SUBS)),
                        pltpu.VMEM((nb, rows, LANES), F32),
                        pltpu.VMEM((nb, t_new, SB_WIDTH), F32)],
        compiler_params=pltpu.CompilerParams(
            dimension_semantics=("arbitrary",),
            vmem_limit_bytes=max(_vmem_limit(block_bytes, temp_bytes),
                                 VMEM_BYTES_V7X - _nbytes(cache_k.shape, F32))),
        name="sb_attn_sample",
    )(q, k_new, v_new, gsb, uu, cache_k, cache_v)


def _outproj_kernel(x_ref, osb_ref, osgu_ref, p_ref, wo_ref, pg_ref, wg_ref, wp_ref, y_ref):
    h = (x_ref[0]
         + jnp.dot(osb_ref[0], wo_ref[:SB_WIDTH], preferred_element_type=F32)
         + jnp.dot(osgu_ref[0], wo_ref[SB_WIDTH:], preferred_element_type=F32))
    ms = jnp.mean(h * h, axis=-1, keepdims=True)
    hn = (h * lax.rsqrt(ms + EPS) * pg_ref[...]).astype(BF16)
    gate_logit = jnp.dot(hn, wg_ref[...], preferred_element_type=F32)
    gate = 1.0 / (1.0 + jnp.exp(-gate_logit))
    pp = jnp.dot(p_ref[0].astype(BF16), wp_ref[...], preferred_element_type=F32)
    y_ref[0] = h + gate * pp


def _outproj(x, osb, osgu, p, w_out_bf, ple_norm_g, w_gate_bf, w_proj_bf, *, tm, name):
    b, t, _ = x.shape
    assert t % tm == 0
    tok = lambda w: pl.BlockSpec((1, tm, w), lambda bi, i: (bi, i, 0))
    const2 = lambda a: pl.BlockSpec(a.shape, lambda bi, i: (0, 0))
    pg = ple_norm_g.reshape(1, D_MODEL)
    block_bytes = (2 * _nbytes((tm, D_MODEL), F32) + 2 * _nbytes((tm, SEG), BF16)
                   + _nbytes((tm, PLE_DIM), F32) + _nbytes(w_out_bf.shape, BF16)
                   + _nbytes(w_gate_bf.shape, BF16) + _nbytes(w_proj_bf.shape, BF16))
    temp_bytes = 6 * _nbytes((tm, D_MODEL), F32)
    return pl.pallas_call(
        _outproj_kernel,
        grid=(b, t // tm),
        in_specs=[tok(D_MODEL), tok(SEG), tok(SEG), tok(PLE_DIM), const2(w_out_bf), const2(pg),
                  const2(w_gate_bf), const2(w_proj_bf)],
        out_specs=tok(D_MODEL),
        out_shape=jax.ShapeDtypeStruct((b, t, D_MODEL), F32),
        compiler_params=pltpu.CompilerParams(
            dimension_semantics=("arbitrary", "arbitrary"),
            vmem_limit_bytes=_vmem_limit(block_bytes, temp_bytes)),
        name=name,
    )(x, osb, osgu, p, w_out_bf, pg, w_gate_bf, w_proj_bf)


def _sgu_bias_rows(sgu_b_l, period):
    per_pos = jnp.tile(sgu_b_l[:, :period].T, (SGU_CHUNK // period, 1))
    return jnp.repeat(per_pos, GROUP_W, axis=1)


def kernel(x_prompt, x_sample, cache_k, cache_v, p_prompt, p_sample, norm_g, w_in, q_norm_g,
           k_norm_g, sgu_norm_g, sgu_w, sgu_b, w_out, ple_norm_g, w_ple_gate, w_ple_proj):
    depth = w_in.shape[0]
    assert depth == 1, "one layer per call"
    l = 0
    b_p, t_p, _ = x_prompt.shape
    b_s, t_s, _ = x_sample.shape
    past = cache_k.shape[2]
    n_s = b_s * t_s
    assert n_s == SGU_CHUNK and SGU_CHUNK % t_s == 0

    w_out_bf = w_out[l].astype(BF16)
    w_gate_bf = w_ple_gate[l].astype(BF16)
    w_proj_bf = w_ple_proj[l].astype(BF16)

    rep = SGU_CHUNK // t_s
    sgu_w_s = jnp.tile(sgu_w[l][:, :t_s, :t_s], (1, rep, rep))
    xs = x_sample.reshape(1, n_s, D_MODEL)
    q_s, k_s, v_s, gsb_s, osgu_s, vs_s, w_in_bf = _inproj_sample(
        xs, norm_g[l], w_in[l], q_norm_g[l], k_norm_g[l], sgu_norm_g[l],
        sgu_w_s, _sgu_bias_rows(sgu_b[l], t_s), period=t_s)

    q, k_feat, v_feat, kt, vb, gsb, osgu = _inproj_prompt(
        x_prompt, norm_g[l], w_in_bf, q_norm_g[l], k_norm_g[l], sgu_norm_g[l],
        sgu_w[l], _sgu_bias_rows(sgu_b[l], SGU_CHUNK), tm=TM_IN)
    osb = _attn_prompt(q, kt, vb, gsb, tq=TQ, hps=HEADS_PER_STEP)
    y_prompt = _outproj(x_prompt, osb, osgu, p_prompt[l], w_out_bf, ple_norm_g[l], w_gate_bf,
                        w_proj_bf, tm=TM_OUT, name="outproj_prompt")

    shp = (b_s, t_s, SB_WIDTH)
    to_feat = lambda c: jnp.transpose(c[l], (0, 2, 3, 1))
    osb_s = _attn_sample(q_s.reshape(shp), to_feat(cache_k), to_feat(cache_v),
                         k_s.reshape(shp), v_s.reshape(shp),
                         gsb_s.reshape(shp))
    y_sample = _outproj(xs, osb_s.reshape(1, n_s, SB_WIDTH), osgu_s, p_sample[l].reshape(1, n_s, PLE_DIM),
                        w_out_bf, ple_norm_g[l], w_gate_bf, w_proj_bf, tm=n_s,
                        name="outproj_sample").reshape(b_s, t_s, D_MODEL)

    head_shape = lambda a, bb, tt: a.reshape(1, bb, tt, HEADS, HEAD_DIM)
    from_feat = lambda a: jnp.transpose(a.reshape(b_p, HEADS, HEAD_DIM, t_p), (0, 3, 1, 2))[None]
    return (y_prompt, y_sample,
            from_feat(k_feat), from_feat(v_feat),
            head_shape(k_s, b_s, t_s), head_shape(v_s, b_s, t_s),
            vs_s.reshape(1, b_s, t_s, GROUPS, GROUP_W))
```

```python
import functools
import math

import numpy as np
import jax
import jax.numpy as jnp
from jax import lax
from jax.experimental import pallas as pl
from jax.experimental.pallas import tpu as pltpu

F32 = jnp.float32
BF16 = jnp.bfloat16

LANES = 128
MXU_TILE = 256
VMEM_BYTES_V7X = 64 * 1024 * 1024

D_MODEL = 1024
PLE_DIM = 256
HEADS = 8
HEAD_DIM = 64
SB_WIDTH = HEADS * HEAD_DIM
GROUPS = 4
GROUP_W = 128
SGU_WIDTH = GROUPS * GROUP_W
SGU_CHUNK = 128
SEG = 512
EPS = 1e-6

Q_SCALE = HEAD_DIM ** -0.5 * math.log2(math.e)

KSUB = LANES
NSUB = 2
TQ = NSUB * KSUB
HEADS_PER_STEP = 8
REM_DEAD_LOG2 = -152.0
SOFTPLUS_LINEAR = 64.0
EAGER_CACHE_SUBS = 2
EAGER_SEEN = KSUB + 48
TM_IN = 1024
TM_OUT = 1024


def _vmem_limit(block_bytes, temp_bytes):
    need = 2 * block_bytes + temp_bytes
    return int(min(need, VMEM_BYTES_V7X - 8 * 1024 * 1024))


def _nbytes(shape, dtype):
    return int(np.prod(shape)) * jnp.dtype(dtype).itemsize


def _suffix_sum_matrix():
    j = np.arange(KSUB)[:, None]
    s = np.arange(KSUB)[None, :]
    one = np.concatenate([np.where(j >= s, -1.0, 0.0), -np.ones((KSUB, LANES))], axis=1)
    return jnp.asarray(np.concatenate([one, one], axis=0), dtype=BF16)


def _head_mean_matrix():
    a = np.arange(MXU_TILE)
    bd = np.where(a[:, None] // HEAD_DIM == a[None, :] // HEAD_DIM, 1.0 / HEAD_DIM, 0.0)
    return jnp.asarray(bd, dtype=BF16)


def _split_bf16(x):
    hi = x.astype(BF16)
    lo = (x - hi.astype(F32)).astype(BF16)
    return jnp.concatenate([hi, lo], axis=1)


def _gelu_tanh(x):
    return 0.5 * x * (1.0 + jnp.tanh(math.sqrt(2.0 / math.pi) * (x + 0.044715 * (x * x * x))))


def _silu(x):
    return x / (1.0 + jnp.exp(-x))


def _inproj_kernel(x_ref, ng_ref, w_ref, qg_ref, kg_ref, sg_ref, hm_ref, sw_ref, sb_ref,
                   *out_refs, tm, period, attn_layout):
    if attn_layout:
        q_ref, ktf_ref, vtf_ref, kt_ref, vb_ref, gsb_ref, osgu_ref = out_refs
    else:
        q_ref, k_ref, v_ref, gsb_ref, osgu_ref, vs_ref = out_refs

    x = x_ref[0]
    ms = jnp.mean(x * x, axis=-1, keepdims=True)
    xn = (x * lax.rsqrt(ms + EPS) * ng_ref[...]).astype(BF16)

    def seg(i):
        return jnp.dot(xn, w_ref[:, i * SEG:(i + 1) * SEG], preferred_element_type=F32)

    def head_rms(t, g):
        sq = (t * t).astype(BF16)
        ms_h = jnp.concatenate(
            [jnp.dot(sq[:, c:c + MXU_TILE], hm_ref[...], preferred_element_type=F32)
             for c in range(0, SB_WIDTH, MXU_TILE)], axis=1)
        return t * lax.rsqrt(ms_h + EPS) * g

    q = head_rms(seg(0), qg_ref[...])
    q_ref[0] = (q * Q_SCALE).astype(BF16)

    k = head_rms(seg(1), kg_ref[...])
    v = seg(2)
    if attn_layout:
        k_t = k.T
        ktf_ref[0] = k_t
        kt_ref[0] = k_t.astype(BF16)
        vtf_ref[0] = v.T
        vb_ref[0] = v.astype(BF16)
    else:
        k_ref[0] = k
        v_ref[0] = v

    gsb_ref[0] = _silu(seg(3)).astype(BF16)

    u = _gelu_tanh(seg(4))
    vs_raw = _gelu_tanh(seg(5))
    vs_groups = []
    for g in range(GROUPS):
        t = vs_raw[:, g * GROUP_W:(g + 1) * GROUP_W]
        ms_g = jnp.mean(t * t, axis=-1, keepdims=True)
        vs_groups.append(t * lax.rsqrt(ms_g + EPS) * sg_ref[:, g * GROUP_W:(g + 1) * GROUP_W])
    if not attn_layout:
        vs_ref[0] = jnp.concatenate(vs_groups, axis=1)

    row = lax.broadcasted_iota(jnp.int32, (SGU_CHUNK, SGU_CHUNK), 0)
    col = lax.broadcasted_iota(jnp.int32, (SGU_CHUNK, SGU_CHUNK), 1)
    keep = row >= col
    if period != SGU_CHUNK:
        keep = keep & ((row // period) == (col // period))
    s_groups = []
    for g in range(GROUPS):
        wm = jnp.where(keep, sw_ref[g], 0.0).astype(BF16)
        vg = vs_groups[g].astype(BF16)
        chunks = [jnp.dot(wm, vg[c * SGU_CHUNK:(c + 1) * SGU_CHUNK], preferred_element_type=F32)
                  + sb_ref[:, g * GROUP_W:(g + 1) * GROUP_W]
                  for c in range(tm // SGU_CHUNK)]
        s_groups.append(jnp.concatenate(chunks, axis=0) if len(chunks) > 1 else chunks[0])
    s = jnp.concatenate(s_groups, axis=1)

    osgu_ref[0] = (u * s * _silu(seg(6))).astype(BF16)


def _inproj(x, norm_g, w_in_bf, q_norm_g, k_norm_g, sgu_norm_g, sgu_w_tiled, sgu_bias, *,
            tm, period, attn_layout):
    b, t, _ = x.shape
    assert t % tm == 0 and tm % SGU_CHUNK == 0
    grid = (b, t // tm)
    tok = lambda w: pl.BlockSpec((1, tm, w), lambda bi, i: (bi, i, 0))
    const2 = lambda a: pl.BlockSpec(a.shape, lambda bi, i: (0, 0))
    const3 = lambda a: pl.BlockSpec(a.shape, lambda bi, i: (0, 0, 0))

    ng = norm_g.reshape(1, D_MODEL)
    qg = jnp.tile(q_norm_g, HEADS).reshape(1, SB_WIDTH)
    kg = jnp.tile(k_norm_g, HEADS).reshape(1, SB_WIDTH)
    sg = sgu_norm_g.reshape(1, SGU_WIDTH)
    hm = _head_mean_matrix()

    act = lambda dt: jax.ShapeDtypeStruct((b, t, SEG), dt)
    if attn_layout:
        feat = lambda dt: jax.ShapeDtypeStruct((b, SB_WIDTH, t), dt)
        feat_blk = pl.BlockSpec((1, SB_WIDTH, tm), lambda bi, i: (bi, 0, i))
        out_shape = (act(BF16), feat(F32), feat(F32), feat(BF16), act(BF16), act(BF16), act(BF16))
        out_specs = (tok(SEG), feat_blk, feat_blk, feat_blk, tok(SEG), tok(SEG), tok(SEG))
    else:
        out_shape = (act(BF16), act(F32), act(F32), act(BF16), act(BF16), act(F32))
        out_specs = (tok(SEG),) * 6

    block_bytes = (_nbytes((tm, D_MODEL), F32) + _nbytes(w_in_bf.shape, BF16)
                   + _nbytes(hm.shape, BF16) + _nbytes(sgu_w_tiled.shape, F32)
                   + _nbytes(sgu_bias.shape, F32) + 7 * _nbytes((tm, SEG), F32))
    temp_bytes = 12 * _nbytes((tm, SEG), F32)
    return pl.pallas_call(
        functools.partial(_inproj_kernel, tm=tm, period=period, attn_layout=attn_layout),
        grid=grid,
        in_specs=[tok(D_MODEL), const2(ng), const2(w_in_bf), const2(qg), const2(kg), const2(sg),
                  const2(hm), const3(sgu_w_tiled), const2(sgu_bias)],
        out_specs=out_specs,
        out_shape=out_shape,
        compiler_params=pltpu.CompilerParams(
            dimension_semantics=("arbitrary", "arbitrary"),
            vmem_limit_bytes=_vmem_limit(block_bytes, temp_bytes)),
        name="inproj_prompt" if attn_layout else "inproj_sample",
    )(x, ng, w_in_bf, qg, kg, sg, hm, sgu_w_tiled, sgu_bias)


def _neg_log2_keep(z, vis):
    sp = jnp.maximum(z, jnp.log2(1.0 + jnp.exp2(jnp.minimum(z, SOFTPLUS_LINEAR))))
    return sp if vis is None else jnp.where(vis, sp, 0.0)


def _sb_weights(z, vis, sums, rem):
    w = jnp.exp2(z + sums[:, :KSUB] + rem)
    if vis is not None:
        w = jnp.where(vis, w, 0.0)
    return w.astype(BF16), rem + sums[:, KSUB:]


def _attn_prompt_kernel(q_ref, g_ref, uu_ref, kt_hbm, v_hbm, o_ref,
                        kwin, vwin, kold, vold, sem, rem_ref, acc_ref, *, tq, hps, nb, nq):
    qi = pl.program_id(0)
    slot = qi % 2
    nv = nb * hps

    def window(step, slot_):
        first = pl.multiple_of(jnp.maximum(step - 1, 0) * tq, tq)
        return ([pltpu.make_async_copy(kt_hbm.at[r, :, pl.ds(first, 2 * tq)], kwin.at[slot_, r],
                                       sem.at[0, slot_]) for r in range(nb)]
                + [pltpu.make_async_copy(v_hbm.at[r, pl.ds(first, 2 * tq), :], vwin.at[slot_, r],
                                         sem.at[1, slot_]) for r in range(nb)])

    def older(n):
        first = pl.multiple_of((qi - n) * tq, tq)
        return ([pltpu.make_async_copy(kt_hbm.at[r, :, pl.ds(first, tq)], kold.at[r], sem.at[0, 2])
                 for r in range(nb)]
                + [pltpu.make_async_copy(v_hbm.at[r, pl.ds(first, tq), :], vold.at[r], sem.at[1, 2])
                   for r in range(nb)])

    @pl.when(qi == 0)
    def _():
        for c in window(qi, slot):
            c.start()

    for c in window(qi, slot):
        c.wait()

    @pl.when(qi + 1 < nq)
    def _():
        for c in window(qi + 1, 1 - slot):
            c.start()

    qh = [q_ref[v // hps][:, (v % hps) * HEAD_DIM:(v % hps + 1) * HEAD_DIM] for v in range(nv)]
    uu = uu_ref[...]
    lane = lax.broadcasted_iota(jnp.int32, (KSUB, LANES), 1)
    first_head = lane < HEAD_DIM

    def keys_at(src, r, ks):
        return (kwin[slot, r, :, pl.ds(ks, KSUB)] if src == "window"
                else kold[r, :, pl.ds(ks, KSUB)])

    def pair_values(src, ks, p):
        pairs = hps // 2
        lanes = pl.ds((p % pairs) * LANES, LANES)
        vv = (vwin[slot, p // pairs, pl.ds(ks, KSUB), lanes] if src == "window"
              else vold[p // pairs, pl.ds(ks, KSUB), lanes])
        zero = jnp.zeros_like(vv)
        return jnp.concatenate([jnp.where(first_head, vv, zero),
                                jnp.where(first_head, zero, vv)], axis=0)

    def sweep(tiles, rems, accs):
        def put(full, r0, r1, part):
            pieces = ([full[:r0]] if r0 else []) + [part] + ([full[r1:]] if r1 < tq else [])
            return jnp.concatenate(pieces, axis=0) if len(pieces) > 1 else part

        zs, splits = [], []
        for src, ks, r0, r1, diagonal in tiles:
            kts = [keys_at(src, r, ks) for r in range(nb)]
            vis = None
            if diagonal:
                row = lax.broadcasted_iota(jnp.int32, (r1 - r0, KSUB), 0)
                col = lax.broadcasted_iota(jnp.int32, (r1 - r0, KSUB), 1)
                vis = col < row
            for v in range(nv):
                h = v % hps
                z = jnp.dot(qh[v][r0:r1], kts[v // hps][h * HEAD_DIM:(h + 1) * HEAD_DIM],
                            preferred_element_type=F32)
                zs.append((z, vis))
                splits.append(_split_bf16(_neg_log2_keep(z, vis)))
        sums = jnp.dot(jnp.concatenate(splits, axis=0), uu, preferred_element_type=F32)
        rems, accs = list(rems), list(accs)
        off, i = 0, 0
        for src, ks, r0, r1, _ in tiles:
            ws = []
            for h in range(nv):
                z, vis = zs[i]
                i += 1
                w, new = _sb_weights(z, vis, sums[off:off + r1 - r0], rems[h][r0:r1])
                off += r1 - r0
                rems[h] = put(rems[h], r0, r1, new)
                ws.append(w)
            for p in range(nv // 2):
                d = jnp.dot(jnp.concatenate(ws[2 * p:2 * p + 2], axis=1), pair_values(src, ks, p),
                            preferred_element_type=F32)
                accs[p] = put(accs[p], r0, r1, accs[p][r0:r1] + d)
        return rems, accs

    def diagonal_tiles(at):
        return [("window", at + s * KSUB, s * KSUB, tq, True) for s in reversed(range(NSUB))]

    def key_block_tiles(src, at, rows=None):
        rows = rows or {s: (0, tq) for s in range(NSUB)}
        return [(src, at + s * KSUB, *rows[s], False) for s in reversed(range(NSUB)) if s in rows]

    zero_rems = [jnp.zeros((tq, LANES), F32)] * nv
    zero_accs = [jnp.zeros((tq, LANES), F32)] * (nv // 2)

    def finish(accs):
        pairs = hps // 2
        for r in range(nb):
            acc = jnp.concatenate(accs[r * pairs:(r + 1) * pairs], axis=1)
            o_ref[r] = (acc * g_ref[r].astype(F32)).astype(BF16)

    def any_alive(rems):
        most = functools.reduce(jnp.maximum, rems)
        return (jnp.max(most) > REM_DEAD_LOG2).astype(jnp.int32)

    @pl.when(qi == 0)
    def _():
        finish(sweep(diagonal_tiles(0), zero_rems, zero_accs)[1])

    def load_state():
        return [rem_ref[h] for h in range(nv)], [acc_ref[p] for p in range(nv // 2)]

    def store_state(rems, accs):
        for h in range(nv):
            rem_ref[h] = rems[h]
        for p in range(nv // 2):
            acc_ref[p] = accs[p]

    eager_rows = {s: min(tq, EAGER_SEEN - (NSUB - 1 - s) * KSUB) for s in range(NSUB)
                  if EAGER_SEEN > (NSUB - 1 - s) * KSUB}

    @pl.when(qi > 0)
    def _():
        first = key_block_tiles("window", 0, {s: (0, e) for s, e in eager_rows.items()})
        rems, accs = sweep(diagonal_tiles(tq) + first, zero_rems, zero_accs)
        store_state(rems, accs)
        alive = any_alive(rems)
        for s in reversed(range(NSUB)):
            e = eager_rows.get(s, 0)
            if e < tq:
                @pl.when(any_alive([r[e:] for r in rems]) > 0)
                def _():
                    store_state(*sweep(key_block_tiles("window", 0, {s: (e, tq)}), *load_state()))

        def cond(c):
            n, alive = c
            return jnp.logical_and(n <= qi, alive > 0)

        def body(c):
            n, _ = c
            for cp in older(n):
                cp.start()
            for cp in older(n):
                cp.wait()
            rems, accs = sweep(key_block_tiles("older", 0), *load_state())
            store_state(rems, accs)
            return n + 1, any_alive(rems)

        lax.while_loop(cond, body, (jnp.int32(2), alive))
        finish(load_state()[1])


def _attn_prompt(q, kt, vb, gsb, *, tq, hps):
    nb, t, _ = q.shape
    assert t % tq == 0 and t >= 2 * tq and tq == NSUB * KSUB and hps == HEADS
    nv = nb * hps
    nq = t // tq
    uu = _suffix_sum_matrix()
    row_blk = pl.BlockSpec((nb, tq, SB_WIDTH), lambda i: (0, i, 0))
    in_hbm = pl.BlockSpec(memory_space=pltpu.HBM)
    block_bytes = 3 * _nbytes((nb, tq, SB_WIDTH), BF16) + _nbytes(uu.shape, BF16)
    temp_bytes = ((nv + nv // 2) * _nbytes((tq, LANES), F32)
                  + 5 * _nbytes((nb, SB_WIDTH, 2 * tq), BF16)
                  + 5 * (NSUB + 2) * nv * _nbytes((tq, KSUB), F32))
    return pl.pallas_call(
        functools.partial(_attn_prompt_kernel, tq=tq, hps=hps, nb=nb, nq=nq),
        grid=(nq,),
        in_specs=[row_blk, row_blk, pl.BlockSpec(uu.shape, lambda i: (0, 0)), in_hbm, in_hbm],
        out_specs=row_blk,
        out_shape=jax.ShapeDtypeStruct((nb, t, SB_WIDTH), BF16),
        scratch_shapes=[pltpu.VMEM((2, nb, SB_WIDTH, 2 * tq), BF16),
                        pltpu.VMEM((2, nb, 2 * tq, SB_WIDTH), BF16),
                        pltpu.VMEM((nb, SB_WIDTH, tq), BF16),
                        pltpu.VMEM((nb, tq, SB_WIDTH), BF16),
                        pltpu.SemaphoreType.DMA((2, 3)),
                        pltpu.VMEM((nv, tq, LANES), F32), pltpu.VMEM((nv // 2, tq, LANES), F32)],
        compiler_params=pltpu.CompilerParams(
            dimension_semantics=("arbitrary",),
            vmem_limit_bytes=_vmem_limit(block_bytes, temp_bytes)),
        name="sb_attn_prompt",
    )(q, gsb, uu, kt, vb)


def _attn_sample_kernel(q_ref, kn_ref, vn_ref, g_ref, uu_ref, ck_hbm, cv_hbm, o_ref,
                        kwin, vwin, sem, rem_s, acc_s, *, t_new, past, nb):
    rows = HEADS * t_new
    uu = uu_ref[...]
    n_sub = past // KSUB
    contract_last = (((1,), (1,)), ((), ()))

    def fetch(which, b, j, slot):
        src, dst = ((ck_hbm, kwin), (cv_hbm, vwin))[which]
        first = pl.multiple_of(j * KSUB, KSUB)
        return pltpu.make_async_copy(src.at[b, :, :, pl.ds(first, KSUB)], dst.at[b, slot],
                                     sem.at[which, b, slot])

    for b in range(nb):
        for slot in range(EAGER_CACHE_SUBS):
            for which in range(2):
                fetch(which, b, n_sub - 1 - slot, slot).start()

    row = lax.broadcasted_iota(jnp.int32, (rows, KSUB), 0)
    col = lax.broadcasted_iota(jnp.int32, (rows, KSUB), 1)
    new_vis = col < (row % t_new)
    pad = jnp.zeros((KSUB - t_new, SB_WIDTH), F32)

    def new_block(ref, b):
        blk = jnp.concatenate([ref[b], pad], axis=0).astype(BF16)
        return [blk[:, h * HEAD_DIM:(h + 1) * HEAD_DIM] for h in range(HEADS)]

    def cached(win, b, slot):
        return [win[b, slot, h].astype(BF16) for h in range(HEADS)]

    def sweep(b, tiles, rem, acc):
        q = q_ref[b]
        qh = [q[:, h * HEAD_DIM:(h + 1) * HEAD_DIM] for h in range(HEADS)]
        zs = [jnp.concatenate(
            [jnp.dot(qh[h], keys[h], preferred_element_type=F32) if dim_major else
             lax.dot_general(qh[h], keys[h], contract_last, preferred_element_type=F32)
             for h in range(HEADS)], axis=0) for keys, _, dim_major, _ in tiles]
        splits = [_split_bf16(_neg_log2_keep(z, t[3])) for z, t in zip(zs, tiles)]
        sums = jnp.dot(jnp.concatenate(splits, axis=0), uu, preferred_element_type=F32)
        for i, (z, (_, vals, dim_major, vis)) in enumerate(zip(zs, tiles)):
            w, rem = _sb_weights(z, vis, sums[i * rows:(i + 1) * rows], rem)
            wh = [w[h * t_new:(h + 1) * t_new] for h in range(HEADS)]
            acc = [a + (lax.dot_general(wh[h], vals[h], contract_last, preferred_element_type=F32)
                        if dim_major else jnp.dot(wh[h], vals[h], preferred_element_type=F32))
                   for h, a in enumerate(acc)]
        return rem, acc

    def alive_of(rem):
        return (jnp.max(rem) > REM_DEAD_LOG2).astype(jnp.int32)

    for b in range(nb):
        for slot in range(EAGER_CACHE_SUBS):
            for which in range(2):
                fetch(which, b, n_sub - 1 - slot, slot).wait()
    alive = jnp.int32(0)
    for b in range(nb):
        tiles = [(new_block(kn_ref, b), new_block(vn_ref, b), False, new_vis)]
        tiles += [(cached(kwin, b, s), cached(vwin, b, s), True, None)
                  for s in range(EAGER_CACHE_SUBS)]
        rem, acc = sweep(b, tiles, jnp.zeros((rows, LANES), F32),
                         [jnp.zeros((t_new, HEAD_DIM), F32)] * HEADS)
        rem_s[b] = rem
        acc_s[b] = jnp.concatenate(acc, axis=1)
        alive = jnp.maximum(alive, alive_of(rem))

    def cond(c):
        j, alive = c
        return jnp.logical_and(j >= 0, alive > 0)

    def body(c):
        j, _ = c
        for b in range(nb):
            for which in range(2):
                fetch(which, b, j, 0).start()
        alive = jnp.int32(0)
        for b in range(nb):
            for which in range(2):
                fetch(which, b, j, 0).wait()
            acc_b = acc_s[b]
            rem, acc = sweep(b, [(cached(kwin, b, 0), cached(vwin, b, 0), True, None)], rem_s[b],
                             [acc_b[:, h * HEAD_DIM:(h + 1) * HEAD_DIM] for h in range(HEADS)])
            rem_s[b] = rem
            acc_s[b] = jnp.concatenate(acc, axis=1)
            alive = jnp.maximum(alive, alive_of(rem))
        return j - 1, alive

    lax.while_loop(cond, body, (jnp.int32(n_sub - 1 - EAGER_CACHE_SUBS), alive))
    for b in range(nb):
        o_ref[b] = (acc_s[b] * g_ref[b].astype(F32)).astype(BF16)


def _attn_sample(q, cache_k, cache_v, k_new, v_new, gsb):
    nb, t_new, _ = q.shape
    past = cache_k.shape[3]
    assert cache_k.shape == (nb, HEADS, HEAD_DIM, past)
    assert past % KSUB == 0 and past // KSUB >= EAGER_CACHE_SUBS and t_new <= KSUB
    uu = _suffix_sum_matrix()
    rows = HEADS * t_new
    whole = lambda a: pl.BlockSpec(a.shape, lambda i: (0,) * a.ndim)
    window = (nb, EAGER_CACHE_SUBS, HEADS, HEAD_DIM, KSUB)
    block_bytes = sum(_nbytes(a.shape, a.dtype) for a in (q, k_new, v_new, gsb, uu, q))
    temp_bytes = (2 * _nbytes(window, F32) + _nbytes((nb, rows, LANES), F32)
                  + _nbytes((nb, t_new, SB_WIDTH), F32)
                  + 16 * (EAGER_CACHE_SUBS + 1) * _nbytes((rows, KSUB), F32))
    return pl.pallas_call(
        functools.partial(_attn_sample_kernel, t_new=t_new, past=past, nb=nb),
        grid=(1,),
        in_specs=[whole(q), whole(k_new), whole(v_new), whole(gsb), whole(uu),
                  pl.BlockSpec(memory_space=pltpu.HBM), pl.BlockSpec(memory_space=pltpu.HBM)],
        out_specs=whole(q),
        out_shape=jax.ShapeDtypeStruct((nb, t_new, SB_WIDTH), BF16),
        scratch_shapes=[pltpu.VMEM(window, F32), pltpu.VMEM(window, F32),
                        pltpu.SemaphoreType.DMA((2, nb, EAGER_CACHE_SUBS)),
                        pltpu.VMEM((nb, rows, LANES), F32),
                        pltpu.VMEM((nb, t_new, SB_WIDTH), F32)],
        compiler_params=pltpu.CompilerParams(
            dimension_semantics=("arbitrary",),
            vmem_limit_bytes=max(_vmem_limit(block_bytes, temp_bytes),
                                 VMEM_BYTES_V7X - _nbytes(cache_k.shape, F32))),
        name="sb_attn_sample",
    )(q, k_new, v_new, gsb, uu, cache_k, cache_v)


def _outproj_kernel(x_ref, osb_ref, osgu_ref, p_ref, wo_ref, pg_ref, wg_ref, wp_ref, y_ref):
    h = (x_ref[0]
         + jnp.dot(osb_ref[0], wo_ref[:SB_WIDTH], preferred_element_type=F32)
         + jnp.dot(osgu_ref[0], wo_ref[SB_WIDTH:], preferred_element_type=F32))
    ms = jnp.mean(h * h, axis=-1, keepdims=True)
    hn = (h * lax.rsqrt(ms + EPS) * pg_ref[...]).astype(BF16)
    gate_logit = jnp.dot(hn, wg_ref[...], preferred_element_type=F32)
    gate = 1.0 / (1.0 + jnp.exp(-gate_logit))
    pp = jnp.dot(p_ref[0].astype(BF16), wp_ref[...], preferred_element_type=F32)
    y_ref[0] = h + gate * pp


def _outproj(x, osb, osgu, p, w_out_bf, ple_norm_g, w_gate_bf, w_proj_bf, *, tm, name):
    b, t, _ = x.shape
    assert t % tm == 0
    tok = lambda w: pl.BlockSpec((1, tm, w), lambda bi, i: (bi, i, 0))
    const2 = lambda a: pl.BlockSpec(a.shape, lambda bi, i: (0, 0))
    pg = ple_norm_g.reshape(1, D_MODEL)
    block_bytes = (2 * _nbytes((tm, D_MODEL), F32) + 2 * _nbytes((tm, SEG), BF16)
                   + _nbytes((tm, PLE_DIM), F32) + _nbytes(w_out_bf.shape, BF16)
                   + _nbytes(w_gate_bf.shape, BF16) + _nbytes(w_proj_bf.shape, BF16))
    temp_bytes = 6 * _nbytes((tm, D_MODEL), F32)
    return pl.pallas_call(
        _outproj_kernel,
        grid=(b, t // tm),
        in_specs=[tok(D_MODEL), tok(SEG), tok(SEG), tok(PLE_DIM), const2(w_out_bf), const2(pg),
                  const2(w_gate_bf), const2(w_proj_bf)],
        out_specs=tok(D_MODEL),
        out_shape=jax.ShapeDtypeStruct((b, t, D_MODEL), F32),
        compiler_params=pltpu.CompilerParams(
            dimension_semantics=("arbitrary", "arbitrary"),
            vmem_limit_bytes=_vmem_limit(block_bytes, temp_bytes)),
        name=name,
    )(x, osb, osgu, p, w_out_bf, pg, w_gate_bf, w_proj_bf)


def _sgu_bias_rows(sgu_b_l, period):
    per_pos = jnp.tile(sgu_b_l[:, :period].T, (SGU_CHUNK // period, 1))
    return jnp.repeat(per_pos, GROUP_W, axis=1)


def kernel(x_prompt, x_sample, cache_k, cache_v, p_prompt, p_sample, norm_g, w_in, q_norm_g,
           k_norm_g, sgu_norm_g, sgu_w, sgu_b, w_out, ple_norm_g, w_ple_gate, w_ple_proj):
    depth = w_in.shape[0]
    assert depth == 1, "one layer per call"
    l = 0
    b_p, t_p, _ = x_prompt.shape
    b_s, t_s, _ = x_sample.shape
    past = cache_k.shape[2]
    n_s = b_s * t_s
    assert n_s == SGU_CHUNK and SGU_CHUNK % t_s == 0

    w_in_bf = w_in[l].astype(BF16)
    w_out_bf = w_out[l].astype(BF16)
    w_gate_bf = w_ple_gate[l].astype(BF16)
    w_proj_bf = w_ple_proj[l].astype(BF16)

    q, k_feat, v_feat, kt, vb, gsb, osgu = _inproj(
        x_prompt, norm_g[l], w_in_bf, q_norm_g[l], k_norm_g[l], sgu_norm_g[l],
        sgu_w[l], _sgu_bias_rows(sgu_b[l], SGU_CHUNK),
        tm=TM_IN, period=SGU_CHUNK, attn_layout=True)
    osb = _attn_prompt(q, kt, vb, gsb, tq=TQ, hps=HEADS_PER_STEP)
    y_prompt = _outproj(x_prompt, osb, osgu, p_prompt[l], w_out_bf, ple_norm_g[l], w_gate_bf,
                        w_proj_bf, tm=TM_OUT, name="outproj_prompt")

    rep = SGU_CHUNK // t_s
    sgu_w_s = jnp.tile(sgu_w[l][:, :t_s, :t_s], (1, rep, rep))
    xs = x_sample.reshape(1, n_s, D_MODEL)
    q_s, k_s, v_s, gsb_s, osgu_s, vs_s = _inproj(
        xs, norm_g[l], w_in_bf, q_norm_g[l], k_norm_g[l], sgu_norm_g[l],
        sgu_w_s, _sgu_bias_rows(sgu_b[l], t_s),
        tm=n_s, period=t_s, attn_layout=False)
    shp = (b_s, t_s, SB_WIDTH)
    to_feat = lambda c: jnp.transpose(c[l], (0, 2, 3, 1))
    osb_s = _attn_sample(q_s.reshape(shp), to_feat(cache_k), to_feat(cache_v),
                         k_s.reshape(shp), v_s.reshape(shp),
                         gsb_s.reshape(shp))
    y_sample = _outproj(xs, osb_s.reshape(1, n_s, SB_WIDTH), osgu_s, p_sample[l].reshape(1, n_s, PLE_DIM),
                        w_out_bf, ple_norm_g[l], w_gate_bf, w_proj_bf, tm=n_s,
                        name="outproj_sample").reshape(b_s, t_s, D_MODEL)

    head_shape = lambda a, bb, tt: a.reshape(1, bb, tt, HEADS, HEAD_DIM)
    from_feat = lambda a: jnp.transpose(a.reshape(b_p, HEADS, HEAD_DIM, t_p), (0, 3, 1, 2))[None]
    return (y_prompt, y_sample,
            from_feat(k_feat), from_feat(v_feat),
            head_shape(k_s, b_s, t_s), head_shape(v_s, b_s, t_s),
            vs_s.reshape(1, b_s, t_s, GROUPS, GROUP_W))
```

```python
import functools
import math

import numpy as np
import jax
import jax.numpy as jnp
from jax import lax
from jax.experimental import pallas as pl
from jax.experimental.pallas import tpu as pltpu

F32 = jnp.float32
BF16 = jnp.bfloat16

LANES = 128
MXU_TILE = 256
VMEM_BYTES_V7X = 64 * 1024 * 1024

D_MODEL = 1024
PLE_DIM = 256
HEADS = 8
HEAD_DIM = 64
SB_WIDTH = HEADS * HEAD_DIM
GROUPS = 4
GROUP_W = 128
SGU_WIDTH = GROUPS * GROUP_W
SGU_CHUNK = 128
SEG = 512
EPS = 1e-6

Q_SCALE = HEAD_DIM ** -0.5 * math.log2(math.e)

KSUB = LANES
NSUB = 2
TQ = NSUB * KSUB
HEADS_PER_STEP = 8
REM_DEAD_LOG2 = -152.0
SOFTPLUS_LINEAR = 64.0
EAGER_CACHE_SUBS = 2
EAGER_SEEN = KSUB + 48
TM_IN = 1024
TM_OUT = 1024


def _vmem_limit(block_bytes, temp_bytes):
    need = 2 * block_bytes + temp_bytes
    return int(min(need, VMEM_BYTES_V7X - 8 * 1024 * 1024))


def _nbytes(shape, dtype):
    return int(np.prod(shape)) * jnp.dtype(dtype).itemsize


def _suffix_sum_matrix():
    j = np.arange(KSUB)[:, None]
    s = np.arange(KSUB)[None, :]
    one = np.concatenate([np.where(j >= s, -1.0, 0.0), -np.ones((KSUB, LANES))], axis=1)
    return jnp.asarray(np.concatenate([one, one], axis=0), dtype=BF16)


def _head_mean_matrix():
    a = np.arange(MXU_TILE)
    bd = np.where(a[:, None] // HEAD_DIM == a[None, :] // HEAD_DIM, 1.0 / HEAD_DIM, 0.0)
    return jnp.asarray(bd, dtype=BF16)


def _split_bf16(x):
    hi = x.astype(BF16)
    lo = (x - hi.astype(F32)).astype(BF16)
    return jnp.concatenate([hi, lo], axis=1)


def _gelu_tanh(x):
    return 0.5 * x * (1.0 + jnp.tanh(math.sqrt(2.0 / math.pi) * (x + 0.044715 * (x * x * x))))


def _silu(x):
    return x / (1.0 + jnp.exp(-x))


def _inproj_kernel(x_ref, ng_ref, w_ref, qg_ref, kg_ref, sg_ref, hm_ref, sw_ref, sb_ref,
                   *out_refs, tm, period, attn_layout):
    if attn_layout:
        q_ref, ktf_ref, vtf_ref, kt_ref, vb_ref, gsb_ref, osgu_ref = out_refs
    else:
        q_ref, k_ref, v_ref, gsb_ref, osgu_ref, vs_ref = out_refs

    x = x_ref[0]
    ms = jnp.mean(x * x, axis=-1, keepdims=True)
    xn = (x * lax.rsqrt(ms + EPS) * ng_ref[...]).astype(BF16)

    def seg(i):
        return jnp.dot(xn, w_ref[:, i * SEG:(i + 1) * SEG], preferred_element_type=F32)

    def head_rms(t, g):
        sq = (t * t).astype(BF16)
        ms_h = jnp.concatenate(
            [jnp.dot(sq[:, c:c + MXU_TILE], hm_ref[...], preferred_element_type=F32)
             for c in range(0, SB_WIDTH, MXU_TILE)], axis=1)
        return t * lax.rsqrt(ms_h + EPS) * g

    q = head_rms(seg(0), qg_ref[...])
    q_ref[0] = (q * Q_SCALE).astype(BF16)

    k = head_rms(seg(1), kg_ref[...])
    v = seg(2)
    if attn_layout:
        k_t = k.T
        ktf_ref[0] = k_t
        kt_ref[0] = k_t.astype(BF16)
        vtf_ref[0] = v.T
        vb_ref[0] = v.astype(BF16)
    else:
        k_ref[0] = k
        v_ref[0] = v

    gsb_ref[0] = _silu(seg(3)).astype(BF16)

    u = _gelu_tanh(seg(4))
    vs_raw = _gelu_tanh(seg(5))
    vs_groups = []
    for g in range(GROUPS):
        t = vs_raw[:, g * GROUP_W:(g + 1) * GROUP_W]
        ms_g = jnp.mean(t * t, axis=-1, keepdims=True)
        vs_groups.append(t * lax.rsqrt(ms_g + EPS) * sg_ref[:, g * GROUP_W:(g + 1) * GROUP_W])
    if not attn_layout:
        vs_ref[0] = jnp.concatenate(vs_groups, axis=1)

    row = lax.broadcasted_iota(jnp.int32, (SGU_CHUNK, SGU_CHUNK), 0)
    col = lax.broadcasted_iota(jnp.int32, (SGU_CHUNK, SGU_CHUNK), 1)
    keep = row >= col
    if period != SGU_CHUNK:
        keep = keep & ((row // period) == (col // period))
    s_groups = []
    for g in range(GROUPS):
        wm = jnp.where(keep, sw_ref[g], 0.0).astype(BF16)
        vg = vs_groups[g].astype(BF16)
        chunks = [jnp.dot(wm, vg[c * SGU_CHUNK:(c + 1) * SGU_CHUNK], preferred_element_type=F32)
                  + sb_ref[:, g * GROUP_W:(g + 1) * GROUP_W]
                  for c in range(tm // SGU_CHUNK)]
        s_groups.append(jnp.concatenate(chunks, axis=0) if len(chunks) > 1 else chunks[0])
    s = jnp.concatenate(s_groups, axis=1)

    osgu_ref[0] = (u * s * _silu(seg(6))).astype(BF16)


def _inproj(x, norm_g, w_in_bf, q_norm_g, k_norm_g, sgu_norm_g, sgu_w_tiled, sgu_bias, *,
            tm, period, attn_layout):
    b, t, _ = x.shape
    assert t % tm == 0 and tm % SGU_CHUNK == 0
    grid = (b, t // tm)
    tok = lambda w: pl.BlockSpec((1, tm, w), lambda bi, i: (bi, i, 0))
    const2 = lambda a: pl.BlockSpec(a.shape, lambda bi, i: (0, 0))
    const3 = lambda a: pl.BlockSpec(a.shape, lambda bi, i: (0, 0, 0))

    ng = norm_g.reshape(1, D_MODEL)
    qg = jnp.tile(q_norm_g, HEADS).reshape(1, SB_WIDTH)
    kg = jnp.tile(k_norm_g, HEADS).reshape(1, SB_WIDTH)
    sg = sgu_norm_g.reshape(1, SGU_WIDTH)
    hm = _head_mean_matrix()

    act = lambda dt: jax.ShapeDtypeStruct((b, t, SEG), dt)
    if attn_layout:
        feat = lambda dt: jax.ShapeDtypeStruct((b, SB_WIDTH, t), dt)
        feat_blk = pl.BlockSpec((1, SB_WIDTH, tm), lambda bi, i: (bi, 0, i))
        out_shape = (act(BF16), feat(F32), feat(F32), feat(BF16), act(BF16), act(BF16), act(BF16))
        out_specs = (tok(SEG), feat_blk, feat_blk, feat_blk, tok(SEG), tok(SEG), tok(SEG))
    else:
        out_shape = (act(BF16), act(F32), act(F32), act(BF16), act(BF16), act(F32))
        out_specs = (tok(SEG),) * 6

    block_bytes = (_nbytes((tm, D_MODEL), F32) + _nbytes(w_in_bf.shape, BF16)
                   + _nbytes(hm.shape, BF16) + _nbytes(sgu_w_tiled.shape, F32)
                   + _nbytes(sgu_bias.shape, F32) + 7 * _nbytes((tm, SEG), F32))
    temp_bytes = 12 * _nbytes((tm, SEG), F32)
    return pl.pallas_call(
        functools.partial(_inproj_kernel, tm=tm, period=period, attn_layout=attn_layout),
        grid=grid,
        in_specs=[tok(D_MODEL), const2(ng), const2(w_in_bf), const2(qg), const2(kg), const2(sg),
                  const2(hm), const3(sgu_w_tiled), const2(sgu_bias)],
        out_specs=out_specs,
        out_shape=out_shape,
        compiler_params=pltpu.CompilerParams(
            dimension_semantics=("arbitrary", "arbitrary"),
            vmem_limit_bytes=_vmem_limit(block_bytes, temp_bytes)),
        name="inproj_prompt" if attn_layout else "inproj_sample",
    )(x, ng, w_in_bf, qg, kg, sg, hm, sgu_w_tiled, sgu_bias)


def _neg_log2_keep(z, vis):
    sp = jnp.maximum(z, jnp.log2(1.0 + jnp.exp2(jnp.minimum(z, SOFTPLUS_LINEAR))))
    return sp if vis is None else jnp.where(vis, sp, 0.0)


def _sb_weights(z, vis, sums, rem):
    w = jnp.exp2(z + sums[:, :KSUB] + rem)
    if vis is not None:
        w = jnp.where(vis, w, 0.0)
    return w.astype(BF16), rem + sums[:, KSUB:]


def _attn_prompt_kernel(q_ref, g_ref, uu_ref, kt_hbm, v_hbm, o_ref,
                        kwin, vwin, kold, vold, sem, rem_ref, acc_ref, *, tq, hps, nb, nq):
    qi = pl.program_id(0)
    slot = qi % 2
    nv = nb * hps

    def window(step, slot_):
        first = pl.multiple_of(jnp.maximum(step - 1, 0) * tq, tq)
        return ([pltpu.make_async_copy(kt_hbm.at[r, :, pl.ds(first, 2 * tq)], kwin.at[slot_, r],
                                       sem.at[0, slot_]) for r in range(nb)]
                + [pltpu.make_async_copy(v_hbm.at[r, pl.ds(first, 2 * tq), :], vwin.at[slot_, r],
                                         sem.at[1, slot_]) for r in range(nb)])

    def older(n):
        first = pl.multiple_of((qi - n) * tq, tq)
        return ([pltpu.make_async_copy(kt_hbm.at[r, :, pl.ds(first, tq)], kold.at[r], sem.at[0, 2])
                 for r in range(nb)]
                + [pltpu.make_async_copy(v_hbm.at[r, pl.ds(first, tq), :], vold.at[r], sem.at[1, 2])
                   for r in range(nb)])

    @pl.when(qi == 0)
    def _():
        for c in window(qi, slot):
            c.start()

    for c in window(qi, slot):
        c.wait()

    @pl.when(qi + 1 < nq)
    def _():
        for c in window(qi + 1, 1 - slot):
            c.start()

    qh = [q_ref[v // hps][:, (v % hps) * HEAD_DIM:(v % hps + 1) * HEAD_DIM] for v in range(nv)]
    uu = uu_ref[...]
    lane = lax.broadcasted_iota(jnp.int32, (KSUB, LANES), 1)
    first_head = lane < HEAD_DIM

    def keys_at(src, r, ks):
        return (kwin[slot, r, :, pl.ds(ks, KSUB)] if src == "window"
                else kold[r, :, pl.ds(ks, KSUB)])

    def pair_values(src, ks, p):
        pairs = hps // 2
        lanes = pl.ds((p % pairs) * LANES, LANES)
        vv = (vwin[slot, p // pairs, pl.ds(ks, KSUB), lanes] if src == "window"
              else vold[p // pairs, pl.ds(ks, KSUB), lanes])
        zero = jnp.zeros_like(vv)
        return jnp.concatenate([jnp.where(first_head, vv, zero),
                                jnp.where(first_head, zero, vv)], axis=0)

    def sweep(tiles, rems, accs):
        def put(full, r0, r1, part):
            pieces = ([full[:r0]] if r0 else []) + [part] + ([full[r1:]] if r1 < tq else [])
            return jnp.concatenate(pieces, axis=0) if len(pieces) > 1 else part

        zs, splits = [], []
        for src, ks, r0, r1, diagonal in tiles:
            kts = [keys_at(src, r, ks) for r in range(nb)]
            vis = None
            if diagonal:
                row = lax.broadcasted_iota(jnp.int32, (r1 - r0, KSUB), 0)
                col = lax.broadcasted_iota(jnp.int32, (r1 - r0, KSUB), 1)
                vis = col < row
            for v in range(nv):
                h = v % hps
                z = jnp.dot(qh[v][r0:r1], kts[v // hps][h * HEAD_DIM:(h + 1) * HEAD_DIM],
                            preferred_element_type=F32)
                zs.append((z, vis))
                splits.append(_split_bf16(_neg_log2_keep(z, vis)))
        sums = jnp.dot(jnp.concatenate(splits, axis=0), uu, preferred_element_type=F32)
        rems, accs = list(rems), list(accs)
        off, i = 0, 0
        for src, ks, r0, r1, _ in tiles:
            ws = []
            for h in range(nv):
                z, vis = zs[i]
                i += 1
                w, new = _sb_weights(z, vis, sums[off:off + r1 - r0], rems[h][r0:r1])
                off += r1 - r0
                rems[h] = put(rems[h], r0, r1, new)
                ws.append(w)
            for p in range(nv // 2):
                d = jnp.dot(jnp.concatenate(ws[2 * p:2 * p + 2], axis=1), pair_values(src, ks, p),
                            preferred_element_type=F32)
                accs[p] = put(accs[p], r0, r1, accs[p][r0:r1] + d)
        return rems, accs

    def diagonal_tiles(at):
        return [("window", at + s * KSUB, s * KSUB, tq, True) for s in reversed(range(NSUB))]

    def key_block_tiles(src, at, rows=None):
        rows = rows or {s: (0, tq) for s in range(NSUB)}
        return [(src, at + s * KSUB, *rows[s], False) for s in reversed(range(NSUB)) if s in rows]

    zero_rems = [jnp.zeros((tq, LANES), F32)] * nv
    zero_accs = [jnp.zeros((tq, LANES), F32)] * (nv // 2)

    def finish(accs):
        pairs = hps // 2
        for r in range(nb):
            acc = jnp.concatenate(accs[r * pairs:(r + 1) * pairs], axis=1)
            o_ref[r] = (acc * g_ref[r].astype(F32)).astype(BF16)

    def any_alive(rems):
        most = functools.reduce(jnp.maximum, rems)
        return (jnp.max(most) > REM_DEAD_LOG2).astype(jnp.int32)

    @pl.when(qi == 0)
    def _():
        finish(sweep(diagonal_tiles(0), zero_rems, zero_accs)[1])

    def load_state():
        return [rem_ref[h] for h in range(nv)], [acc_ref[p] for p in range(nv // 2)]

    def store_state(rems, accs):
        for h in range(nv):
            rem_ref[h] = rems[h]
        for p in range(nv // 2):
            acc_ref[p] = accs[p]

    eager_rows = {s: min(tq, EAGER_SEEN - (NSUB - 1 - s) * KSUB) for s in range(NSUB)
                  if EAGER_SEEN > (NSUB - 1 - s) * KSUB}

    @pl.when(qi > 0)
    def _():
        first = key_block_tiles("window", 0, {s: (0, e) for s, e in eager_rows.items()})
        rems, accs = sweep(diagonal_tiles(tq) + first, zero_rems, zero_accs)
        store_state(rems, accs)
        alive = any_alive(rems)
        late = {s: (eager_rows.get(s, 0), tq) for s in range(NSUB) if eager_rows.get(s, 0) < tq}
        if late:
            @pl.when(any_alive([r[min(e for e, _ in late.values()):] for r in rems]) > 0)
            def _():
                store_state(*sweep(key_block_tiles("window", 0, late), *load_state()))

        def cond(c):
            n, alive = c
            return jnp.logical_and(n <= qi, alive > 0)

        def body(c):
            n, _ = c
            for cp in older(n):
                cp.start()
            for cp in older(n):
                cp.wait()
            rems, accs = sweep(key_block_tiles("older", 0), *load_state())
            store_state(rems, accs)
            return n + 1, any_alive(rems)

        lax.while_loop(cond, body, (jnp.int32(2), alive))
        finish(load_state()[1])


def _attn_prompt(q, kt, vb, gsb, *, tq, hps):
    nb, t, _ = q.shape
    assert t % tq == 0 and t >= 2 * tq and tq == NSUB * KSUB and hps == HEADS
    nv = nb * hps
    nq = t // tq
    uu = _suffix_sum_matrix()
    row_blk = pl.BlockSpec((nb, tq, SB_WIDTH), lambda i: (0, i, 0))
    in_hbm = pl.BlockSpec(memory_space=pltpu.HBM)
    block_bytes = 3 * _nbytes((nb, tq, SB_WIDTH), BF16) + _nbytes(uu.shape, BF16)
    temp_bytes = ((nv + nv // 2) * _nbytes((tq, LANES), F32)
                  + 5 * _nbytes((nb, SB_WIDTH, 2 * tq), BF16)
                  + 5 * (NSUB + 2) * nv * _nbytes((tq, KSUB), F32))
    return pl.pallas_call(
        functools.partial(_attn_prompt_kernel, tq=tq, hps=hps, nb=nb, nq=nq),
        grid=(nq,),
        in_specs=[row_blk, row_blk, pl.BlockSpec(uu.shape, lambda i: (0, 0)), in_hbm, in_hbm],
        out_specs=row_blk,
        out_shape=jax.ShapeDtypeStruct((nb, t, SB_WIDTH), BF16),
        scratch_shapes=[pltpu.VMEM((2, nb, SB_WIDTH, 2 * tq), BF16),
                        pltpu.VMEM((2, nb, 2 * tq, SB_WIDTH), BF16),
                        pltpu.VMEM((nb, SB_WIDTH, tq), BF16),
                        pltpu.VMEM((nb, tq, SB_WIDTH), BF16),
                        pltpu.SemaphoreType.DMA((2, 3)),
                        pltpu.VMEM((nv, tq, LANES), F32), pltpu.VMEM((nv // 2, tq, LANES), F32)],
        compiler_params=pltpu.CompilerParams(
            dimension_semantics=("arbitrary",),
            vmem_limit_bytes=_vmem_limit(block_bytes, temp_bytes)),
        name="sb_attn_prompt",
    )(q, gsb, uu, kt, vb)


def _attn_sample_kernel(q_ref, kn_ref, vn_ref, g_ref, uu_ref, ck_hbm, cv_hbm, o_ref,
                        kwin, vwin, sem, rem_s, acc_s, *, t_new, past, nb):
    rows = HEADS * t_new
    uu = uu_ref[...]
    n_sub = past // KSUB
    contract_last = (((1,), (1,)), ((), ()))

    def fetch(which, b, j, slot):
        src, dst = ((ck_hbm, kwin), (cv_hbm, vwin))[which]
        first = pl.multiple_of(j * KSUB, KSUB)
        return pltpu.make_async_copy(src.at[b, :, :, pl.ds(first, KSUB)], dst.at[b, slot],
                                     sem.at[which, b, slot])

    for b in range(nb):
        for slot in range(EAGER_CACHE_SUBS):
            for which in range(2):
                fetch(which, b, n_sub - 1 - slot, slot).start()

    row = lax.broadcasted_iota(jnp.int32, (rows, KSUB), 0)
    col = lax.broadcasted_iota(jnp.int32, (rows, KSUB), 1)
    new_vis = col < (row % t_new)
    pad = jnp.zeros((KSUB - t_new, SB_WIDTH), F32)

    def new_block(ref, b):
        blk = jnp.concatenate([ref[b], pad], axis=0).astype(BF16)
        return [blk[:, h * HEAD_DIM:(h + 1) * HEAD_DIM] for h in range(HEADS)]

    def cached(win, b, slot):
        return [win[b, slot, h].astype(BF16) for h in range(HEADS)]

    def sweep(b, tiles, rem, acc):
        q = q_ref[b]
        qh = [q[:, h * HEAD_DIM:(h + 1) * HEAD_DIM] for h in range(HEADS)]
        zs = [jnp.concatenate(
            [jnp.dot(qh[h], keys[h], preferred_element_type=F32) if dim_major else
             lax.dot_general(qh[h], keys[h], contract_last, preferred_element_type=F32)
             for h in range(HEADS)], axis=0) for keys, _, dim_major, _ in tiles]
        splits = [_split_bf16(_neg_log2_keep(z, t[3])) for z, t in zip(zs, tiles)]
        sums = jnp.dot(jnp.concatenate(splits, axis=0), uu, preferred_element_type=F32)
        for i, (z, (_, vals, dim_major, vis)) in enumerate(zip(zs, tiles)):
            w, rem = _sb_weights(z, vis, sums[i * rows:(i + 1) * rows], rem)
            wh = [w[h * t_new:(h + 1) * t_new] for h in range(HEADS)]
            acc = [a + (lax.dot_general(wh[h], vals[h], contract_last, preferred_element_type=F32)
                        if dim_major else jnp.dot(wh[h], vals[h], preferred_element_type=F32))
                   for h, a in enumerate(acc)]
        return rem, acc

    def alive_of(rem):
        return (jnp.max(rem) > REM_DEAD_LOG2).astype(jnp.int32)

    for b in range(nb):
        for slot in range(EAGER_CACHE_SUBS):
            for which in range(2):
                fetch(which, b, n_sub - 1 - slot, slot).wait()
    alive = jnp.int32(0)
    for b in range(nb):
        tiles = [(new_block(kn_ref, b), new_block(vn_ref, b), False, new_vis)]
        tiles += [(cached(kwin, b, s), cached(vwin, b, s), True, None)
                  for s in range(EAGER_CACHE_SUBS)]
        rem, acc = sweep(b, tiles, jnp.zeros((rows, LANES), F32),
                         [jnp.zeros((t_new, HEAD_DIM), F32)] * HEADS)
        rem_s[b] = rem
        acc_s[b] = jnp.concatenate(acc, axis=1)
        alive = jnp.maximum(alive, alive_of(rem))

    def cond(c):
        j, alive = c
        return jnp.logical_and(j >= 0, alive > 0)

    def body(c):
        j, _ = c
        for b in range(nb):
            for which in range(2):
                fetch(which, b, j, 0).start()
        alive = jnp.int32(0)
        for b in range(nb):
            for which in range(2):
                fetch(which, b, j, 0).wait()
            acc_b = acc_s[b]
            rem, acc = sweep(b, [(cached(kwin, b, 0), cached(vwin, b, 0), True, None)], rem_s[b],
                             [acc_b[:, h * HEAD_DIM:(h + 1) * HEAD_DIM] for h in range(HEADS)])
            rem_s[b] = rem
            acc_s[b] = jnp.concatenate(acc, axis=1)
            alive = jnp.maximum(alive, alive_of(rem))
        return j - 1, alive

    lax.while_loop(cond, body, (jnp.int32(n_sub - 1 - EAGER_CACHE_SUBS), alive))
    for b in range(nb):
        o_ref[b] = (acc_s[b] * g_ref[b].astype(F32)).astype(BF16)


def _attn_sample(q, cache_k, cache_v, k_new, v_new, gsb):
    nb, t_new, _ = q.shape
    past = cache_k.shape[3]
    assert cache_k.shape == (nb, HEADS, HEAD_DIM, past)
    assert past % KSUB == 0 and past // KSUB >= EAGER_CACHE_SUBS and t_new <= KSUB
    uu = _suffix_sum_matrix()
    rows = HEADS * t_new
    whole = lambda a: pl.BlockSpec(a.shape, lambda i: (0,) * a.ndim)
    window = (nb, EAGER_CACHE_SUBS, HEADS, HEAD_DIM, KSUB)
    block_bytes = sum(_nbytes(a.shape, a.dtype) for a in (q, k_new, v_new, gsb, uu, q))
    temp_bytes = (2 * _nbytes(window, F32) + _nbytes((nb, rows, LANES), F32)
                  + _nbytes((nb, t_new, SB_WIDTH), F32)
                  + 16 * (EAGER_CACHE_SUBS + 1) * _nbytes((rows, KSUB), F32))
    return pl.pallas_call(
        functools.partial(_attn_sample_kernel, t_new=t_new, past=past, nb=nb),
        grid=(1,),
        in_specs=[whole(q), whole(k_new), whole(v_new), whole(gsb), whole(uu),
                  pl.BlockSpec(memory_space=pltpu.HBM), pl.BlockSpec(memory_space=pltpu.HBM)],
        out_specs=whole(q),
        out_shape=jax.ShapeDtypeStruct((nb, t_new, SB_WIDTH), BF16),
        scratch_shapes=[pltpu.VMEM(window, F32), pltpu.VMEM(window, F32),
                        pltpu.SemaphoreType.DMA((2, nb, EAGER_CACHE_SUBS)),
                        pltpu.VMEM((nb, rows, LANES), F32),
                        pltpu.VMEM((nb, t_new, SB_WIDTH), F32)],
        compiler_params=pltpu.CompilerParams(
            dimension_semantics=("arbitrary",),
            vmem_limit_bytes=max(_vmem_limit(block_bytes, temp_bytes),
                                 VMEM_BYTES_V7X - _nbytes(cache_k.shape, F32))),
        name="sb_attn_sample",
    )(q, k_new, v_new, gsb, uu, cache_k, cache_v)


def _outproj_kernel(x_ref, osb_ref, osgu_ref, p_ref, wo_ref, pg_ref, wg_ref, wp_ref, y_ref):
    h = (x_ref[0]
         + jnp.dot(osb_ref[0], wo_ref[:SB_WIDTH], preferred_element_type=F32)
         + jnp.dot(osgu_ref[0], wo_ref[SB_WIDTH:], preferred_element_type=F32))
    ms = jnp.mean(h * h, axis=-1, keepdims=True)
    hn = (h * lax.rsqrt(ms + EPS) * pg_ref[...]).astype(BF16)
    gate_logit = jnp.dot(hn, wg_ref[...], preferred_element_type=F32)
    gate = 1.0 / (1.0 + jnp.exp(-gate_logit))
    pp = jnp.dot(p_ref[0].astype(BF16), wp_ref[...], preferred_element_type=F32)
    y_ref[0] = h + gate * pp


def _outproj(x, osb, osgu, p, w_out_bf, ple_norm_g, w_gate_bf, w_proj_bf, *, tm, name):
    b, t, _ = x.shape
    assert t % tm == 0
    tok = lambda w: pl.BlockSpec((1, tm, w), lambda bi, i: (bi, i, 0))
    const2 = lambda a: pl.BlockSpec(a.shape, lambda bi, i: (0, 0))
    pg = ple_norm_g.reshape(1, D_MODEL)
    block_bytes = (2 * _nbytes((tm, D_MODEL), F32) + 2 * _nbytes((tm, SEG), BF16)
                   + _nbytes((tm, PLE_DIM), F32) + _nbytes(w_out_bf.shape, BF16)
                   + _nbytes(w_gate_bf.shape, BF16) + _nbytes(w_proj_bf.shape, BF16))
    temp_bytes = 6 * _nbytes((tm, D_MODEL), F32)
    return pl.pallas_call(
        _outproj_kernel,
        grid=(b, t // tm),
        in_specs=[tok(D_MODEL), tok(SEG), tok(SEG), tok(PLE_DIM), const2(w_out_bf), const2(pg),
                  const2(w_gate_bf), const2(w_proj_bf)],
        out_specs=tok(D_MODEL),
        out_shape=jax.ShapeDtypeStruct((b, t, D_MODEL), F32),
        compiler_params=pltpu.CompilerParams(
            dimension_semantics=("arbitrary", "arbitrary"),
            vmem_limit_bytes=_vmem_limit(block_bytes, temp_bytes)),
        name=name,
    )(x, osb, osgu, p, w_out_bf, pg, w_gate_bf, w_proj_bf)


def _sgu_bias_rows(sgu_b_l, period):
    per_pos = jnp.tile(sgu_b_l[:, :period].T, (SGU_CHUNK // period, 1))
    return jnp.repeat(per_pos, GROUP_W, axis=1)


def kernel(x_prompt, x_sample, cache_k, cache_v, p_prompt, p_sample, norm_g, w_in, q_norm_g,
           k_norm_g, sgu_norm_g, sgu_w, sgu_b, w_out, ple_norm_g, w_ple_gate, w_ple_proj):
    depth = w_in.shape[0]
    assert depth == 1, "one layer per call"
    l = 0
    b_p, t_p, _ = x_prompt.shape
    b_s, t_s, _ = x_sample.shape
    past = cache_k.shape[2]
    n_s = b_s * t_s
    assert n_s == SGU_CHUNK and SGU_CHUNK % t_s == 0

    w_in_bf = w_in[l].astype(BF16)
    w_out_bf = w_out[l].astype(BF16)
    w_gate_bf = w_ple_gate[l].astype(BF16)
    w_proj_bf = w_ple_proj[l].astype(BF16)

    q, k_feat, v_feat, kt, vb, gsb, osgu = _inproj(
        x_prompt, norm_g[l], w_in_bf, q_norm_g[l], k_norm_g[l], sgu_norm_g[l],
        sgu_w[l], _sgu_bias_rows(sgu_b[l], SGU_CHUNK),
        tm=TM_IN, period=SGU_CHUNK, attn_layout=True)
    osb = _attn_prompt(q, kt, vb, gsb, tq=TQ, hps=HEADS_PER_STEP)
    y_prompt = _outproj(x_prompt, osb, osgu, p_prompt[l], w_out_bf, ple_norm_g[l], w_gate_bf,
                        w_proj_bf, tm=TM_OUT, name="outproj_prompt")

    rep = SGU_CHUNK // t_s
    sgu_w_s = jnp.tile(sgu_w[l][:, :t_s, :t_s], (1, rep, rep))
    xs = x_sample.reshape(1, n_s, D_MODEL)
    q_s, k_s, v_s, gsb_s, osgu_s, vs_s = _inproj(
        xs, norm_g[l], w_in_bf, q_norm_g[l], k_norm_g[l], sgu_norm_g[l],
        sgu_w_s, _sgu_bias_rows(sgu_b[l], t_s),
        tm=n_s, period=t_s, attn_layout=False)
    shp = (b_s, t_s, SB_WIDTH)
    to_feat = lambda c: jnp.transpose(c[l], (0, 2, 3, 1))
    osb_s = _attn_sample(q_s.reshape(shp), to_feat(cache_k), to_feat(cache_v),
                         k_s.reshape(shp), v_s.reshape(shp),
                         gsb_s.reshape(shp))
    y_sample = _outproj(xs, osb_s.reshape(1, n_s, SB_WIDTH), osgu_s, p_sample[l].reshape(1, n_s, PLE_DIM),
                        w_out_bf, ple_norm_g[l], w_gate_bf, w_proj_bf, tm=n_s,
                        name="outproj_sample").reshape(b_s, t_s, D_MODEL)

    head_shape = lambda a, bb, tt: a.reshape(1, bb, tt, HEADS, HEAD_DIM)
    from_feat = lambda a: jnp.transpose(a.reshape(b_p, HEADS, HEAD_DIM, t_p), (0, 3, 1, 2))[None]
    return (y_prompt, y_sample,
            from_feat(k_feat), from_feat(v_feat),
            head_shape(k_s, b_s, t_s), head_shape(v_s, b_s, t_s),
            vs_s.reshape(1, b_s, t_s, GROUPS, GROUP_W))
```

```python
import functools
import math

import numpy as np
import jax
import jax.numpy as jnp
from jax import lax
from jax.experimental import pallas as pl
from jax.experimental.pallas import tpu as pltpu

F32 = jnp.float32
BF16 = jnp.bfloat16

LANES = 128
MXU_TILE = 256
VMEM_BYTES_V7X = 64 * 1024 * 1024

D_MODEL = 1024
PLE_DIM = 256
HEADS = 8
HEAD_DIM = 64
SB_WIDTH = HEADS * HEAD_DIM
GROUPS = 4
GROUP_W = 128
SGU_WIDTH = GROUPS * GROUP_W
SGU_CHUNK = 128
SEG = 512
EPS = 1e-6

Q_SCALE = HEAD_DIM ** -0.5 * math.log2(math.e)

KSUB = LANES
NSUB = 2
TQ = NSUB * KSUB
HEADS_PER_STEP = 8
REM_DEAD_LOG2 = -152.0
SOFTPLUS_LINEAR = 64.0
EAGER_CACHE_SUBS = 2
EAGER_SEEN = KSUB + 48
TM_IN = 1024
TM_OUT = 1024


def _vmem_limit(block_bytes, temp_bytes):
    need = 2 * block_bytes + temp_bytes
    return int(min(need, VMEM_BYTES_V7X - 8 * 1024 * 1024))


def _nbytes(shape, dtype):
    return int(np.prod(shape)) * jnp.dtype(dtype).itemsize


def _suffix_sum_matrix():
    j = np.arange(KSUB)[:, None]
    s = np.arange(KSUB)[None, :]
    one = np.concatenate([np.where(j >= s, -1.0, 0.0), -np.ones((KSUB, LANES))], axis=1)
    return jnp.asarray(np.concatenate([one, one], axis=0), dtype=BF16)


def _head_mean_matrix():
    a = np.arange(MXU_TILE)
    bd = np.where(a[:, None] // HEAD_DIM == a[None, :] // HEAD_DIM, 1.0 / HEAD_DIM, 0.0)
    return jnp.asarray(bd, dtype=BF16)


def _split_bf16(x):
    hi = x.astype(BF16)
    lo = (x - hi.astype(F32)).astype(BF16)
    return jnp.concatenate([hi, lo], axis=1)


def _gelu_tanh(x):
    return 0.5 * x * (1.0 + jnp.tanh(math.sqrt(2.0 / math.pi) * (x + 0.044715 * (x * x * x))))


def _silu(x):
    return x / (1.0 + jnp.exp(-x))


def _inproj_kernel(x_ref, ng_ref, w_ref, qg_ref, kg_ref, sg_ref, hm_ref, sw_ref, sb_ref,
                   *out_refs, tm, period, attn_layout):
    if attn_layout:
        q_ref, ktf_ref, vtf_ref, kt_ref, vb_ref, gsb_ref, osgu_ref = out_refs
    else:
        q_ref, k_ref, v_ref, gsb_ref, osgu_ref, vs_ref = out_refs

    x = x_ref[0]
    ms = jnp.mean(x * x, axis=-1, keepdims=True)
    xn = (x * lax.rsqrt(ms + EPS) * ng_ref[...]).astype(BF16)

    def seg(i):
        return jnp.dot(xn, w_ref[:, i * SEG:(i + 1) * SEG], preferred_element_type=F32)

    def head_rms(t, g):
        sq = (t * t).astype(BF16)
        ms_h = jnp.concatenate(
            [jnp.dot(sq[:, c:c + MXU_TILE], hm_ref[...], preferred_element_type=F32)
             for c in range(0, SB_WIDTH, MXU_TILE)], axis=1)
        return t * lax.rsqrt(ms_h + EPS) * g

    q = head_rms(seg(0), qg_ref[...])
    q_ref[0] = (q * Q_SCALE).astype(BF16)

    k = head_rms(seg(1), kg_ref[...])
    v = seg(2)
    if attn_layout:
        k_t = k.T
        ktf_ref[0] = k_t
        kt_ref[0] = k_t.astype(BF16)
        vtf_ref[0] = v.T
        vb_ref[0] = v.astype(BF16)
    else:
        k_ref[0] = k
        v_ref[0] = v

    gsb_ref[0] = _silu(seg(3)).astype(BF16)

    u = _gelu_tanh(seg(4))
    vs_raw = _gelu_tanh(seg(5))
    vs_groups = []
    for g in range(GROUPS):
        t = vs_raw[:, g * GROUP_W:(g + 1) * GROUP_W]
        ms_g = jnp.mean(t * t, axis=-1, keepdims=True)
        vs_groups.append(t * lax.rsqrt(ms_g + EPS) * sg_ref[:, g * GROUP_W:(g + 1) * GROUP_W])
    if not attn_layout:
        vs_ref[0] = jnp.concatenate(vs_groups, axis=1)

    row = lax.broadcasted_iota(jnp.int32, (SGU_CHUNK, SGU_CHUNK), 0)
    col = lax.broadcasted_iota(jnp.int32, (SGU_CHUNK, SGU_CHUNK), 1)
    keep = row >= col
    if period != SGU_CHUNK:
        keep = keep & ((row // period) == (col // period))
    s_groups = []
    for g in range(GROUPS):
        wm = jnp.where(keep, sw_ref[g], 0.0).astype(BF16)
        vg = vs_groups[g].astype(BF16)
        chunks = [jnp.dot(wm, vg[c * SGU_CHUNK:(c + 1) * SGU_CHUNK], preferred_element_type=F32)
                  + sb_ref[:, g * GROUP_W:(g + 1) * GROUP_W]
                  for c in range(tm // SGU_CHUNK)]
        s_groups.append(jnp.concatenate(chunks, axis=0) if len(chunks) > 1 else chunks[0])
    s = jnp.concatenate(s_groups, axis=1)

    osgu_ref[0] = (u * s * _silu(seg(6))).astype(BF16)


def _inproj(x, norm_g, w_in_bf, q_norm_g, k_norm_g, sgu_norm_g, sgu_w_tiled, sgu_bias, *,
            tm, period, attn_layout):
    b, t, _ = x.shape
    assert t % tm == 0 and tm % SGU_CHUNK == 0
    grid = (b, t // tm)
    tok = lambda w: pl.BlockSpec((1, tm, w), lambda bi, i: (bi, i, 0))
    const2 = lambda a: pl.BlockSpec(a.shape, lambda bi, i: (0, 0))
    const3 = lambda a: pl.BlockSpec(a.shape, lambda bi, i: (0, 0, 0))

    ng = norm_g.reshape(1, D_MODEL)
    qg = jnp.tile(q_norm_g, HEADS).reshape(1, SB_WIDTH)
    kg = jnp.tile(k_norm_g, HEADS).reshape(1, SB_WIDTH)
    sg = sgu_norm_g.reshape(1, SGU_WIDTH)
    hm = _head_mean_matrix()

    act = lambda dt: jax.ShapeDtypeStruct((b, t, SEG), dt)
    if attn_layout:
        feat = lambda dt: jax.ShapeDtypeStruct((b, SB_WIDTH, t), dt)
        feat_blk = pl.BlockSpec((1, SB_WIDTH, tm), lambda bi, i: (bi, 0, i))
        out_shape = (act(BF16), feat(F32), feat(F32), feat(BF16), act(BF16), act(BF16), act(BF16))
        out_specs = (tok(SEG), feat_blk, feat_blk, feat_blk, tok(SEG), tok(SEG), tok(SEG))
    else:
        out_shape = (act(BF16), act(F32), act(F32), act(BF16), act(BF16), act(F32))
        out_specs = (tok(SEG),) * 6

    block_bytes = (_nbytes((tm, D_MODEL), F32) + _nbytes(w_in_bf.shape, BF16)
                   + _nbytes(hm.shape, BF16) + _nbytes(sgu_w_tiled.shape, F32)
                   + _nbytes(sgu_bias.shape, F32) + 7 * _nbytes((tm, SEG), F32))
    temp_bytes = 12 * _nbytes((tm, SEG), F32)
    return pl.pallas_call(
        functools.partial(_inproj_kernel, tm=tm, period=period, attn_layout=attn_layout),
        grid=grid,
        in_specs=[tok(D_MODEL), const2(ng), const2(w_in_bf), const2(qg), const2(kg), const2(sg),
                  const2(hm), const3(sgu_w_tiled), const2(sgu_bias)],
        out_specs=out_specs,
        out_shape=out_shape,
        compiler_params=pltpu.CompilerParams(
            dimension_semantics=("arbitrary", "arbitrary"),
            vmem_limit_bytes=_vmem_limit(block_bytes, temp_bytes)),
        name="inproj_prompt" if attn_layout else "inproj_sample",
    )(x, ng, w_in_bf, qg, kg, sg, hm, sgu_w_tiled, sgu_bias)


def _neg_log2_keep(z, vis):
    sp = jnp.maximum(z, jnp.log2(1.0 + jnp.exp2(jnp.minimum(z, SOFTPLUS_LINEAR))))
    return sp if vis is None else jnp.where(vis, sp, 0.0)


def _sb_weights(z, vis, sums, rem):
    w = jnp.exp2(z + sums[:, :KSUB] + rem)
    if vis is not None:
        w = jnp.where(vis, w, 0.0)
    return w.astype(BF16), rem + sums[:, KSUB:]


def _attn_prompt_kernel(q_ref, g_ref, uu_ref, kt_hbm, v_hbm, o_ref,
                        kwin, vwin, kold, vold, sem, rem_ref, acc_ref, *, tq, hps, nb, nq):
    qi = pl.program_id(0)
    slot = qi % 2
    nv = nb * hps

    def window(step, slot_):
        first = pl.multiple_of(jnp.maximum(step - 1, 0) * tq, tq)
        return ([pltpu.make_async_copy(kt_hbm.at[r, :, pl.ds(first, 2 * tq)], kwin.at[slot_, r],
                                       sem.at[0, slot_]) for r in range(nb)]
                + [pltpu.make_async_copy(v_hbm.at[r, pl.ds(first, 2 * tq), :], vwin.at[slot_, r],
                                         sem.at[1, slot_]) for r in range(nb)])

    def older(n):
        first = pl.multiple_of((qi - n) * tq, tq)
        return ([pltpu.make_async_copy(kt_hbm.at[r, :, pl.ds(first, tq)], kold.at[r], sem.at[0, 2])
                 for r in range(nb)]
                + [pltpu.make_async_copy(v_hbm.at[r, pl.ds(first, tq), :], vold.at[r], sem.at[1, 2])
                   for r in range(nb)])

    @pl.when(qi == 0)
    def _():
        for c in window(qi, slot):
            c.start()

    for c in window(qi, slot):
        c.wait()

    @pl.when(qi + 1 < nq)
    def _():
        for c in window(qi + 1, 1 - slot):
            c.start()

    qh = [q_ref[v // hps][:, (v % hps) * HEAD_DIM:(v % hps + 1) * HEAD_DIM] for v in range(nv)]
    uu = uu_ref[...]
    lane = lax.broadcasted_iota(jnp.int32, (KSUB, LANES), 1)
    first_head = lane < HEAD_DIM

    def keys_at(src, r, ks):
        return (kwin[slot, r, :, pl.ds(ks, KSUB)] if src == "window"
                else kold[r, :, pl.ds(ks, KSUB)])

    def pair_values(src, ks, p):
        pairs = hps // 2
        lanes = pl.ds((p % pairs) * LANES, LANES)
        vv = (vwin[slot, p // pairs, pl.ds(ks, KSUB), lanes] if src == "window"
              else vold[p // pairs, pl.ds(ks, KSUB), lanes])
        zero = jnp.zeros_like(vv)
        return jnp.concatenate([jnp.where(first_head, vv, zero),
                                jnp.where(first_head, zero, vv)], axis=0)

    def sweep(tiles, rems, accs):
        def put(full, r0, r1, part):
            pieces = ([full[:r0]] if r0 else []) + [part] + ([full[r1:]] if r1 < tq else [])
            return jnp.concatenate(pieces, axis=0) if len(pieces) > 1 else part

        zs, splits = [], []
        for src, ks, r0, r1, diagonal in tiles:
            kts = [keys_at(src, r, ks) for r in range(nb)]
            vis = None
            if diagonal:
                row = lax.broadcasted_iota(jnp.int32, (r1 - r0, KSUB), 0)
                col = lax.broadcasted_iota(jnp.int32, (r1 - r0, KSUB), 1)
                vis = col < row
            for v in range(nv):
                h = v % hps
                z = jnp.dot(qh[v][r0:r1], kts[v // hps][h * HEAD_DIM:(h + 1) * HEAD_DIM],
                            preferred_element_type=F32)
                zs.append((z, vis))
                splits.append(_split_bf16(_neg_log2_keep(z, vis)))
        sums = jnp.dot(jnp.concatenate(splits, axis=0), uu, preferred_element_type=F32)
        rems, accs = list(rems), list(accs)
        off, i = 0, 0
        for src, ks, r0, r1, _ in tiles:
            ws = []
            for h in range(nv):
                z, vis = zs[i]
                i += 1
                w, new = _sb_weights(z, vis, sums[off:off + r1 - r0], rems[h][r0:r1])
                off += r1 - r0
                rems[h] = put(rems[h], r0, r1, new)
                ws.append(w)
            for p in range(nv // 2):
                d = jnp.dot(jnp.concatenate(ws[2 * p:2 * p + 2], axis=1), pair_values(src, ks, p),
                            preferred_element_type=F32)
                accs[p] = put(accs[p], r0, r1, accs[p][r0:r1] + d)
        return rems, accs

    def diagonal_tiles(at):
        return [("window", at + s * KSUB, s * KSUB, tq, True) for s in reversed(range(NSUB))]

    def key_block_tiles(src, at, rows=None):
        rows = rows or {s: (0, tq) for s in range(NSUB)}
        return [(src, at + s * KSUB, *rows[s], False) for s in reversed(range(NSUB)) if s in rows]

    zero_rems = [jnp.zeros((tq, LANES), F32)] * nv
    zero_accs = [jnp.zeros((tq, LANES), F32)] * (nv // 2)

    def finish(accs):
        pairs = hps // 2
        for r in range(nb):
            acc = jnp.concatenate(accs[r * pairs:(r + 1) * pairs], axis=1)
            o_ref[r] = (acc * g_ref[r].astype(F32)).astype(BF16)

    def any_alive(rems):
        most = functools.reduce(jnp.maximum, rems)
        return (jnp.max(most) > REM_DEAD_LOG2).astype(jnp.int32)

    @pl.when(qi == 0)
    def _():
        finish(sweep(diagonal_tiles(0), zero_rems, zero_accs)[1])

    def load_state():
        return [rem_ref[h] for h in range(nv)], [acc_ref[p] for p in range(nv // 2)]

    def store_state(rems, accs):
        for h in range(nv):
            rem_ref[h] = rems[h]
        for p in range(nv // 2):
            acc_ref[p] = accs[p]

    eager_rows = {s: min(tq, EAGER_SEEN - (NSUB - 1 - s) * KSUB) for s in range(NSUB)
                  if EAGER_SEEN > (NSUB - 1 - s) * KSUB}

    @pl.when(qi > 0)
    def _():
        first = key_block_tiles("window", 0, {s: (0, e) for s, e in eager_rows.items()})
        rems, accs = sweep(diagonal_tiles(tq) + first, zero_rems, zero_accs)
        alive = any_alive(rems)

        @pl.when(alive == 0)
        def _():
            finish(accs)

        @pl.when(alive > 0)
        def _():
            store_state(rems, accs)
            late = {s: (eager_rows.get(s, 0), tq) for s in range(NSUB)
                    if eager_rows.get(s, 0) < tq}
            if late:
                @pl.when(any_alive([r[min(e for e, _ in late.values()):] for r in rems]) > 0)
                def _():
                    store_state(*sweep(key_block_tiles("window", 0, late), *load_state()))

            def cond(c):
                n, alive = c
                return jnp.logical_and(n <= qi, alive > 0)

            def body(c):
                n, _ = c
                for cp in older(n):
                    cp.start()
                for cp in older(n):
                    cp.wait()
                rems, accs = sweep(key_block_tiles("older", 0), *load_state())
                store_state(rems, accs)
                return n + 1, any_alive(rems)

            lax.while_loop(cond, body, (jnp.int32(2), alive))
            finish(load_state()[1])


def _attn_prompt(q, kt, vb, gsb, *, tq, hps):
    nb, t, _ = q.shape
    assert t % tq == 0 and t >= 2 * tq and tq == NSUB * KSUB and hps == HEADS
    nv = nb * hps
    nq = t // tq
    uu = _suffix_sum_matrix()
    row_blk = pl.BlockSpec((nb, tq, SB_WIDTH), lambda i: (0, i, 0))
    in_hbm = pl.BlockSpec(memory_space=pltpu.HBM)
    block_bytes = 3 * _nbytes((nb, tq, SB_WIDTH), BF16) + _nbytes(uu.shape, BF16)
    temp_bytes = ((nv + nv // 2) * _nbytes((tq, LANES), F32)
                  + 5 * _nbytes((nb, SB_WIDTH, 2 * tq), BF16)
                  + 5 * (NSUB + 2) * nv * _nbytes((tq, KSUB), F32))
    return pl.pallas_call(
        functools.partial(_attn_prompt_kernel, tq=tq, hps=hps, nb=nb, nq=nq),
        grid=(nq,),
        in_specs=[row_blk, row_blk, pl.BlockSpec(uu.shape, lambda i: (0, 0)), in_hbm, in_hbm],
        out_specs=row_blk,
        out_shape=jax.ShapeDtypeStruct((nb, t, SB_WIDTH), BF16),
        scratch_shapes=[pltpu.VMEM((2, nb, SB_WIDTH, 2 * tq), BF16),
                        pltpu.VMEM((2, nb, 2 * tq, SB_WIDTH), BF16),
                        pltpu.VMEM((nb, SB_WIDTH, tq), BF16),
                        pltpu.VMEM((nb, tq, SB_WIDTH), BF16),
                        pltpu.SemaphoreType.DMA((2, 3)),
                        pltpu.VMEM((nv, tq, LANES), F32), pltpu.VMEM((nv // 2, tq, LANES), F32)],
        compiler_params=pltpu.CompilerParams(
            dimension_semantics=("arbitrary",),
            vmem_limit_bytes=_vmem_limit(block_bytes, temp_bytes)),
        name="sb_attn_prompt",
    )(q, gsb, uu, kt, vb)


def _attn_sample_kernel(q_ref, kn_ref, vn_ref, g_ref, uu_ref, ck_hbm, cv_hbm, o_ref,
                        kwin, vwin, sem, rem_s, acc_s, *, t_new, past, nb):
    rows = HEADS * t_new
    uu = uu_ref[...]
    n_sub = past // KSUB
    contract_last = (((1,), (1,)), ((), ()))

    def fetch(which, b, j, slot):
        src, dst = ((ck_hbm, kwin), (cv_hbm, vwin))[which]
        first = pl.multiple_of(j * KSUB, KSUB)
        return pltpu.make_async_copy(src.at[b, :, :, pl.ds(first, KSUB)], dst.at[b, slot],
                                     sem.at[which, b, slot])

    for b in range(nb):
        for slot in range(EAGER_CACHE_SUBS):
            for which in range(2):
                fetch(which, b, n_sub - 1 - slot, slot).start()

    row = lax.broadcasted_iota(jnp.int32, (rows, KSUB), 0)
    col = lax.broadcasted_iota(jnp.int32, (rows, KSUB), 1)
    new_vis = col < (row % t_new)
    pad = jnp.zeros((KSUB - t_new, SB_WIDTH), F32)

    def new_block(ref, b):
        blk = jnp.concatenate([ref[b], pad], axis=0).astype(BF16)
        return [blk[:, h * HEAD_DIM:(h + 1) * HEAD_DIM] for h in range(HEADS)]

    def cached(win, b, slot):
        return [win[b, slot, h].astype(BF16) for h in range(HEADS)]

    def sweep(b, tiles, rem, acc):
        q = q_ref[b]
        qh = [q[:, h * HEAD_DIM:(h + 1) * HEAD_DIM] for h in range(HEADS)]
        zs = [jnp.concatenate(
            [jnp.dot(qh[h], keys[h], preferred_element_type=F32) if dim_major else
             lax.dot_general(qh[h], keys[h], contract_last, preferred_element_type=F32)
             for h in range(HEADS)], axis=0) for keys, _, dim_major, _ in tiles]
        splits = [_split_bf16(_neg_log2_keep(z, t[3])) for z, t in zip(zs, tiles)]
        sums = jnp.dot(jnp.concatenate(splits, axis=0), uu, preferred_element_type=F32)
        for i, (z, (_, vals, dim_major, vis)) in enumerate(zip(zs, tiles)):
            w, rem = _sb_weights(z, vis, sums[i * rows:(i + 1) * rows], rem)
            wh = [w[h * t_new:(h + 1) * t_new] for h in range(HEADS)]
            acc = [a + (lax.dot_general(wh[h], vals[h], contract_last, preferred_element_type=F32)
                        if dim_major else jnp.dot(wh[h], vals[h], preferred_element_type=F32))
                   for h, a in enumerate(acc)]
        return rem, acc

    def alive_of(rem):
        return (jnp.max(rem) > REM_DEAD_LOG2).astype(jnp.int32)

    for b in range(nb):
        for slot in range(EAGER_CACHE_SUBS):
            for which in range(2):
                fetch(which, b, n_sub - 1 - slot, slot).wait()
    alive = jnp.int32(0)
    for b in range(nb):
        tiles = [(new_block(kn_ref, b), new_block(vn_ref, b), False, new_vis)]
        tiles += [(cached(kwin, b, s), cached(vwin, b, s), True, None)
                  for s in range(EAGER_CACHE_SUBS)]
        rem, acc = sweep(b, tiles, jnp.zeros((rows, LANES), F32),
                         [jnp.zeros((t_new, HEAD_DIM), F32)] * HEADS)
        rem_s[b] = rem
        acc_s[b] = jnp.concatenate(acc, axis=1)
        alive = jnp.maximum(alive, alive_of(rem))

    def cond(c):
        j, alive = c
        return jnp.logical_and(j >= 0, alive > 0)

    def body(c):
        j, _ = c
        for b in range(nb):
            for which in range(2):
                fetch(which, b, j, 0).start()
        alive = jnp.int32(0)
        for b in range(nb):
            for which in range(2):
                fetch(which, b, j, 0).wait()
            acc_b = acc_s[b]
            rem, acc = sweep(b, [(cached(kwin, b, 0), cached(vwin, b, 0), True, None)], rem_s[b],
                             [acc_b[:, h * HEAD_DIM:(h + 1) * HEAD_DIM] for h in range(HEADS)])
            rem_s[b] = rem
            acc_s[b] = jnp.concatenate(acc, axis=1)
            alive = jnp.maximum(alive, alive_of(rem))
        return j - 1, alive

    lax.while_loop(cond, body, (jnp.int32(n_sub - 1 - EAGER_CACHE_SUBS), alive))
    for b in range(nb):
        o_ref[b] = (acc_s[b] * g_ref[b].astype(F32)).astype(BF16)


def _attn_sample(q, cache_k, cache_v, k_new, v_new, gsb):
    nb, t_new, _ = q.shape
    past = cache_k.shape[3]
    assert cache_k.shape == (nb, HEADS, HEAD_DIM, past)
    assert past % KSUB == 0 and past // KSUB >= EAGER_CACHE_SUBS and t_new <= KSUB
    uu = _suffix_sum_matrix()
    rows = HEADS * t_new
    whole = lambda a: pl.BlockSpec(a.shape, lambda i: (0,) * a.ndim)
    window = (nb, EAGER_CACHE_SUBS, HEADS, HEAD_DIM, KSUB)
    block_bytes = sum(_nbytes(a.shape, a.dtype) for a in (q, k_new, v_new, gsb, uu, q))
    temp_bytes = (2 * _nbytes(window, F32) + _nbytes((nb, rows, LANES), F32)
                  + _nbytes((nb, t_new, SB_WIDTH), F32)
                  + 16 * (EAGER_CACHE_SUBS + 1) * _nbytes((rows, KSUB), F32))
    return pl.pallas_call(
        functools.partial(_attn_sample_kernel, t_new=t_new, past=past, nb=nb),
        grid=(1,),
        in_specs=[whole(q), whole(k_new), whole(v_new), whole(gsb), whole(uu),
                  pl.BlockSpec(memory_space=pltpu.HBM), pl.BlockSpec(memory_space=pltpu.HBM)],
        out_specs=whole(q),
        out_shape=jax.ShapeDtypeStruct((nb, t_new, SB_WIDTH), BF16),
        scratch_shapes=[pltpu.VMEM(window, F32), pltpu.VMEM(window, F32),
                        pltpu.SemaphoreType.DMA((2, nb, EAGER_CACHE_SUBS)),
                        pltpu.VMEM((nb, rows, LANES), F32),
                        pltpu.VMEM((nb, t_new, SB_WIDTH), F32)],
        compiler_params=pltpu.CompilerParams(
            dimension_semantics=("arbitrary",),
            vmem_limit_bytes=max(_vmem_limit(block_bytes, temp_bytes),
                                 VMEM_BYTES_V7X - _nbytes(cache_k.shape, F32))),
        name="sb_attn_sample",
    )(q, k_new, v_new, gsb, uu, cache_k, cache_v)


def _outproj_kernel(x_ref, osb_ref, osgu_ref, p_ref, wo_ref, pg_ref, wg_ref, wp_ref, y_ref):
    h = (x_ref[0]
         + jnp.dot(osb_ref[0], wo_ref[:SB_WIDTH], preferred_element_type=F32)
         + jnp.dot(osgu_ref[0], wo_ref[SB_WIDTH:], preferred_element_type=F32))
    ms = jnp.mean(h * h, axis=-1, keepdims=True)
    hn = (h * lax.rsqrt(ms + EPS) * pg_ref[...]).astype(BF16)
    gate_logit = jnp.dot(hn, wg_ref[...], preferred_element_type=F32)
    gate = 1.0 / (1.0 + jnp.exp(-gate_logit))
    pp = jnp.dot(p_ref[0].astype(BF16), wp_ref[...], preferred_element_type=F32)
    y_ref[0] = h + gate * pp


def _outproj(x, osb, osgu, p, w_out_bf, ple_norm_g, w_gate_bf, w_proj_bf, *, tm, name):
    b, t, _ = x.shape
    assert t % tm == 0
    tok = lambda w: pl.BlockSpec((1, tm, w), lambda bi, i: (bi, i, 0))
    const2 = lambda a: pl.BlockSpec(a.shape, lambda bi, i: (0, 0))
    pg = ple_norm_g.reshape(1, D_MODEL)
    block_bytes = (2 * _nbytes((tm, D_MODEL), F32) + 2 * _nbytes((tm, SEG), BF16)
                   + _nbytes((tm, PLE_DIM), F32) + _nbytes(w_out_bf.shape, BF16)
                   + _nbytes(w_gate_bf.shape, BF16) + _nbytes(w_proj_bf.shape, BF16))
    temp_bytes = 6 * _nbytes((tm, D_MODEL), F32)
    return pl.pallas_call(
        _outproj_kernel,
        grid=(b, t // tm),
        in_specs=[tok(D_MODEL), tok(SEG), tok(SEG), tok(PLE_DIM), const2(w_out_bf), const2(pg),
                  const2(w_gate_bf), const2(w_proj_bf)],
        out_specs=tok(D_MODEL),
        out_shape=jax.ShapeDtypeStruct((b, t, D_MODEL), F32),
        compiler_params=pltpu.CompilerParams(
            dimension_semantics=("arbitrary", "arbitrary"),
            vmem_limit_bytes=_vmem_limit(block_bytes, temp_bytes)),
        name=name,
    )(x, osb, osgu, p, w_out_bf, pg, w_gate_bf, w_proj_bf)


def _sgu_bias_rows(sgu_b_l, period):
    per_pos = jnp.tile(sgu_b_l[:, :period].T, (SGU_CHUNK // period, 1))
    return jnp.repeat(per_pos, GROUP_W, axis=1)


def kernel(x_prompt, x_sample, cache_k, cache_v, p_prompt, p_sample, norm_g, w_in, q_norm_g,
           k_norm_g, sgu_norm_g, sgu_w, sgu_b, w_out, ple_norm_g, w_ple_gate, w_ple_proj):
    depth = w_in.shape[0]
    assert depth == 1, "one layer per call"
    l = 0
    b_p, t_p, _ = x_prompt.shape
    b_s, t_s, _ = x_sample.shape
    past = cache_k.shape[2]
    n_s = b_s * t_s
    assert n_s == SGU_CHUNK and SGU_CHUNK % t_s == 0

    w_in_bf = w_in[l].astype(BF16)
    w_out_bf = w_out[l].astype(BF16)
    w_gate_bf = w_ple_gate[l].astype(BF16)
    w_proj_bf = w_ple_proj[l].astype(BF16)

    q, k_feat, v_feat, kt, vb, gsb, osgu = _inproj(
        x_prompt, norm_g[l], w_in_bf, q_norm_g[l], k_norm_g[l], sgu_norm_g[l],
        sgu_w[l], _sgu_bias_rows(sgu_b[l], SGU_CHUNK),
        tm=TM_IN, period=SGU_CHUNK, attn_layout=True)
    osb = _attn_prompt(q, kt, vb, gsb, tq=TQ, hps=HEADS_PER_STEP)
    y_prompt = _outproj(x_prompt, osb, osgu, p_prompt[l], w_out_bf, ple_norm_g[l], w_gate_bf,
                        w_proj_bf, tm=TM_OUT, name="outproj_prompt")

    rep = SGU_CHUNK // t_s
    sgu_w_s = jnp.tile(sgu_w[l][:, :t_s, :t_s], (1, rep, rep))
    xs = x_sample.reshape(1, n_s, D_MODEL)
    q_s, k_s, v_s, gsb_s, osgu_s, vs_s = _inproj(
        xs, norm_g[l], w_in_bf, q_norm_g[l], k_norm_g[l], sgu_norm_g[l],
        sgu_w_s, _sgu_bias_rows(sgu_b[l], t_s),
        tm=n_s, period=t_s, attn_layout=False)
    shp = (b_s, t_s, SB_WIDTH)
    to_feat = lambda c: jnp.transpose(c[l], (0, 2, 3, 1))
    osb_s = _attn_sample(q_s.reshape(shp), to_feat(cache_k), to_feat(cache_v),
                         k_s.reshape(shp), v_s.reshape(shp),
                         gsb_s.reshape(shp))
    y_sample = _outproj(xs, osb_s.reshape(1, n_s, SB_WIDTH), osgu_s, p_sample[l].reshape(1, n_s, PLE_DIM),
                        w_out_bf, ple_norm_g[l], w_gate_bf, w_proj_bf, tm=n_s,
                        name="outproj_sample").reshape(b_s, t_s, D_MODEL)

    head_shape = lambda a, bb, tt: a.reshape(1, bb, tt, HEADS, HEAD_DIM)
    from_feat = lambda a: jnp.transpose(a.reshape(b_p, HEADS, HEAD_DIM, t_p), (0, 3, 1, 2))[None]
    return (y_prompt, y_sample,
            from_feat(k_feat), from_feat(v_feat),
            head_shape(k_s, b_s, t_s), head_shape(v_s, b_s, t_s),
            vs_s.reshape(1, b_s, t_s, GROUPS, GROUP_W))
```

```python
import functools
import math

import numpy as np
import jax
import jax.numpy as jnp
from jax import lax
from jax.experimental import pallas as pl
from jax.experimental.pallas import tpu as pltpu

F32 = jnp.float32
BF16 = jnp.bfloat16

LANES = 128
MXU_TILE = 256
VMEM_BYTES_V7X = 64 * 1024 * 1024

D_MODEL = 1024
PLE_DIM = 256
HEADS = 8
HEAD_DIM = 64
SB_WIDTH = HEADS * HEAD_DIM
GROUPS = 4
GROUP_W = 128
SGU_WIDTH = GROUPS * GROUP_W
SGU_CHUNK = 128
SEG = 512
EPS = 1e-6

Q_SCALE = HEAD_DIM ** -0.5 * math.log2(math.e)

KSUB = LANES
NSUB = 2
TQ = NSUB * KSUB
HEADS_PER_STEP = 8
REM_DEAD_LOG2 = -152.0
SOFTPLUS_LINEAR = 64.0
EAGER_CACHE_SUBS = 2
EAGER_SEEN = KSUB + 48
TM_IN = 1024
TM_OUT = 1024


def _vmem_limit(block_bytes, temp_bytes):
    need = 2 * block_bytes + temp_bytes
    return int(min(need, VMEM_BYTES_V7X - 8 * 1024 * 1024))


def _nbytes(shape, dtype):
    return int(np.prod(shape)) * jnp.dtype(dtype).itemsize


def _suffix_sum_matrix():
    j = np.arange(KSUB)[:, None]
    s = np.arange(KSUB)[None, :]
    one = np.concatenate([np.where(j >= s, -1.0, 0.0), -np.ones((KSUB, LANES))], axis=1)
    return jnp.asarray(np.concatenate([one, one], axis=0), dtype=BF16)


def _head_mean_matrix():
    a = np.arange(MXU_TILE)
    bd = np.where(a[:, None] // HEAD_DIM == a[None, :] // HEAD_DIM, 1.0 / HEAD_DIM, 0.0)
    return jnp.asarray(bd, dtype=BF16)


def _split_bf16(x):
    hi = x.astype(BF16)
    lo = (x - hi.astype(F32)).astype(BF16)
    return jnp.concatenate([hi, lo], axis=1)


def _gelu_tanh(x):
    return 0.5 * x * (1.0 + jnp.tanh(math.sqrt(2.0 / math.pi) * (x + 0.044715 * (x * x * x))))


def _silu(x):
    return x / (1.0 + jnp.exp(-x))


def _inproj_kernel(x_ref, ng_ref, w_ref, qg_ref, kg_ref, sg_ref, hm_ref, sw_ref, sb_ref,
                   *out_refs, tm, period, attn_layout):
    if attn_layout:
        q_ref, ktf_ref, vtf_ref, kt_ref, vb_ref, gsb_ref, osgu_ref = out_refs
    else:
        q_ref, k_ref, v_ref, gsb_ref, osgu_ref, vs_ref = out_refs

    x = x_ref[0]
    ms = jnp.mean(x * x, axis=-1, keepdims=True)
    xn = (x * lax.rsqrt(ms + EPS) * ng_ref[...]).astype(BF16)

    def seg(i):
        return jnp.dot(xn, w_ref[:, i * SEG:(i + 1) * SEG], preferred_element_type=F32)

    def head_rms(t, g):
        sq = (t * t).astype(BF16)
        ms_h = jnp.concatenate(
            [jnp.dot(sq[:, c:c + MXU_TILE], hm_ref[...], preferred_element_type=F32)
             for c in range(0, SB_WIDTH, MXU_TILE)], axis=1)
        return t * lax.rsqrt(ms_h + EPS) * g

    q = head_rms(seg(0), qg_ref[...])
    q_ref[0] = (q * Q_SCALE).astype(BF16)

    k = head_rms(seg(1), kg_ref[...])
    v = seg(2)
    if attn_layout:
        k_t = k.T
        ktf_ref[0] = k_t
        kt_ref[0] = k_t.astype(BF16)
        vtf_ref[0] = v.T
        vb_ref[0] = v.astype(BF16)
    else:
        k_ref[0] = k
        v_ref[0] = v

    gsb_ref[0] = _silu(seg(3)).astype(BF16)

    u = _gelu_tanh(seg(4))
    vs_raw = _gelu_tanh(seg(5))
    vs_groups = []
    for g in range(GROUPS):
        t = vs_raw[:, g * GROUP_W:(g + 1) * GROUP_W]
        ms_g = jnp.mean(t * t, axis=-1, keepdims=True)
        vs_groups.append(t * lax.rsqrt(ms_g + EPS) * sg_ref[:, g * GROUP_W:(g + 1) * GROUP_W])
    if not attn_layout:
        vs_ref[0] = jnp.concatenate(vs_groups, axis=1)

    row = lax.broadcasted_iota(jnp.int32, (SGU_CHUNK, SGU_CHUNK), 0)
    col = lax.broadcasted_iota(jnp.int32, (SGU_CHUNK, SGU_CHUNK), 1)
    keep = row >= col
    if period != SGU_CHUNK:
        keep = keep & ((row // period) == (col // period))
    s_groups = []
    for g in range(GROUPS):
        wm = jnp.where(keep, sw_ref[g], 0.0).astype(BF16)
        vg = vs_groups[g].astype(BF16)
        chunks = [jnp.dot(wm, vg[c * SGU_CHUNK:(c + 1) * SGU_CHUNK], preferred_element_type=F32)
                  + sb_ref[:, g * GROUP_W:(g + 1) * GROUP_W]
                  for c in range(tm // SGU_CHUNK)]
        s_groups.append(jnp.concatenate(chunks, axis=0) if len(chunks) > 1 else chunks[0])
    s = jnp.concatenate(s_groups, axis=1)

    osgu_ref[0] = (u * s * _silu(seg(6))).astype(BF16)


def _inproj(x, norm_g, w_in_bf, q_norm_g, k_norm_g, sgu_norm_g, sgu_w_tiled, sgu_bias, *,
            tm, period, attn_layout):
    b, t, _ = x.shape
    assert t % tm == 0 and tm % SGU_CHUNK == 0
    grid = (b, t // tm)
    tok = lambda w: pl.BlockSpec((1, tm, w), lambda bi, i: (bi, i, 0))
    const2 = lambda a: pl.BlockSpec(a.shape, lambda bi, i: (0, 0))
    const3 = lambda a: pl.BlockSpec(a.shape, lambda bi, i: (0, 0, 0))

    ng = norm_g.reshape(1, D_MODEL)
    qg = jnp.tile(q_norm_g, HEADS).reshape(1, SB_WIDTH)
    kg = jnp.tile(k_norm_g, HEADS).reshape(1, SB_WIDTH)
    sg = sgu_norm_g.reshape(1, SGU_WIDTH)
    hm = _head_mean_matrix()

    act = lambda dt: jax.ShapeDtypeStruct((b, t, SEG), dt)
    if attn_layout:
        feat = lambda dt: jax.ShapeDtypeStruct((b, SB_WIDTH, t), dt)
        feat_blk = pl.BlockSpec((1, SB_WIDTH, tm), lambda bi, i: (bi, 0, i))
        out_shape = (act(BF16), feat(F32), feat(F32), feat(BF16), act(BF16), act(BF16), act(BF16))
        out_specs = (tok(SEG), feat_blk, feat_blk, feat_blk, tok(SEG), tok(SEG), tok(SEG))
    else:
        out_shape = (act(BF16), act(F32), act(F32), act(BF16), act(BF16), act(F32))
        out_specs = (tok(SEG),) * 6

    block_bytes = (_nbytes((tm, D_MODEL), F32) + _nbytes(w_in_bf.shape, BF16)
                   + _nbytes(hm.shape, BF16) + _nbytes(sgu_w_tiled.shape, F32)
                   + _nbytes(sgu_bias.shape, F32) + 7 * _nbytes((tm, SEG), F32))
    temp_bytes = 12 * _nbytes((tm, SEG), F32)
    return pl.pallas_call(
        functools.partial(_inproj_kernel, tm=tm, period=period, attn_layout=attn_layout),
        grid=grid,
        in_specs=[tok(D_MODEL), const2(ng), const2(w_in_bf), const2(qg), const2(kg), const2(sg),
                  const2(hm), const3(sgu_w_tiled), const2(sgu_bias)],
        out_specs=out_specs,
        out_shape=out_shape,
        compiler_params=pltpu.CompilerParams(
            dimension_semantics=("arbitrary", "arbitrary"),
            vmem_limit_bytes=_vmem_limit(block_bytes, temp_bytes)),
        name="inproj_prompt" if attn_layout else "inproj_sample",
    )(x, ng, w_in_bf, qg, kg, sg, hm, sgu_w_tiled, sgu_bias)


def _neg_log2_keep(z, vis):
    sp = jnp.maximum(z, jnp.log2(1.0 + jnp.exp2(jnp.minimum(z, SOFTPLUS_LINEAR))))
    return sp if vis is None else jnp.where(vis, sp, 0.0)


def _sb_weights(z, vis, sums, rem):
    w = jnp.exp2(z + sums[:, :KSUB] + rem)
    if vis is not None:
        w = jnp.where(vis, w, 0.0)
    return w.astype(BF16), rem + sums[:, KSUB:]


def _attn_prompt_kernel(q_ref, g_ref, uu_ref, kt_hbm, v_hbm, o_ref,
                        kwin, vwin, kold, vold, sem, rem_ref, acc_ref, *, tq, hps, nb, nq):
    qi = pl.program_id(0)
    slot = qi % 2
    nv = nb * hps

    def window(step, slot_):
        first = pl.multiple_of(jnp.maximum(step - 1, 0) * tq, tq)
        return ([pltpu.make_async_copy(kt_hbm.at[r, :, pl.ds(first, 2 * tq)], kwin.at[slot_, r],
                                       sem.at[0, slot_]) for r in range(nb)]
                + [pltpu.make_async_copy(v_hbm.at[r, pl.ds(first, 2 * tq), :], vwin.at[slot_, r],
                                         sem.at[1, slot_]) for r in range(nb)])

    def older(n):
        first = pl.multiple_of((qi - n) * tq, tq)
        return ([pltpu.make_async_copy(kt_hbm.at[r, :, pl.ds(first, tq)], kold.at[r], sem.at[0, 2])
                 for r in range(nb)]
                + [pltpu.make_async_copy(v_hbm.at[r, pl.ds(first, tq), :], vold.at[r], sem.at[1, 2])
                   for r in range(nb)])

    @pl.when(qi == 0)
    def _():
        for c in window(qi, slot):
            c.start()

    for c in window(qi, slot):
        c.wait()

    @pl.when(qi + 1 < nq)
    def _():
        for c in window(qi + 1, 1 - slot):
            c.start()

    qh = [q_ref[v // hps][:, (v % hps) * HEAD_DIM:(v % hps + 1) * HEAD_DIM] for v in range(nv)]
    uu = uu_ref[...]
    lane = lax.broadcasted_iota(jnp.int32, (KSUB, LANES), 1)
    first_head = lane < HEAD_DIM

    def keys_at(src, r, ks):
        return (kwin[slot, r, :, pl.ds(ks, KSUB)] if src == "window"
                else kold[r, :, pl.ds(ks, KSUB)])

    def pair_values(src, ks, p):
        pairs = hps // 2
        lanes = pl.ds((p % pairs) * LANES, LANES)
        vv = (vwin[slot, p // pairs, pl.ds(ks, KSUB), lanes] if src == "window"
              else vold[p // pairs, pl.ds(ks, KSUB), lanes])
        zero = jnp.zeros_like(vv)
        return jnp.concatenate([jnp.where(first_head, vv, zero),
                                jnp.where(first_head, zero, vv)], axis=0)

    def sweep(tiles, rems, accs):
        def put(full, r0, r1, part):
            pieces = ([full[:r0]] if r0 else []) + [part] + ([full[r1:]] if r1 < tq else [])
            return jnp.concatenate(pieces, axis=0) if len(pieces) > 1 else part

        zs, splits = [], []
        for src, ks, r0, r1, diagonal in tiles:
            kts = [keys_at(src, r, ks) for r in range(nb)]
            vis = None
            if diagonal:
                row = lax.broadcasted_iota(jnp.int32, (r1 - r0, KSUB), 0)
                col = lax.broadcasted_iota(jnp.int32, (r1 - r0, KSUB), 1)
                vis = col < row
            for v in range(nv):
                h = v % hps
                z = jnp.dot(qh[v][r0:r1], kts[v // hps][h * HEAD_DIM:(h + 1) * HEAD_DIM],
                            preferred_element_type=F32)
                zs.append((z, vis))
                splits.append(_split_bf16(_neg_log2_keep(z, vis)))
        sums = jnp.dot(jnp.concatenate(splits, axis=0), uu, preferred_element_type=F32)
        rems, accs = list(rems), list(accs)
        off, i = 0, 0
        for src, ks, r0, r1, _ in tiles:
            ws = []
            for h in range(nv):
                z, vis = zs[i]
                i += 1
                w, new = _sb_weights(z, vis, sums[off:off + r1 - r0], rems[h][r0:r1])
                off += r1 - r0
                rems[h] = put(rems[h], r0, r1, new)
                ws.append(w)
            for p in range(nv // 2):
                d = jnp.dot(jnp.concatenate(ws[2 * p:2 * p + 2], axis=1), pair_values(src, ks, p),
                            preferred_element_type=F32)
                accs[p] = put(accs[p], r0, r1, accs[p][r0:r1] + d)
        return rems, accs

    def diagonal_tiles(at):
        return [("window", at + s * KSUB, s * KSUB, tq, True) for s in reversed(range(NSUB))]

    def key_block_tiles(src, at, rows=None):
        rows = rows or {s: (0, tq) for s in range(NSUB)}
        return [(src, at + s * KSUB, *rows[s], False) for s in reversed(range(NSUB)) if s in rows]

    zero_rems = [jnp.zeros((tq, LANES), F32)] * nv
    zero_accs = [jnp.zeros((tq, LANES), F32)] * (nv // 2)

    def finish(accs):
        pairs = hps // 2
        for r in range(nb):
            acc = jnp.concatenate(accs[r * pairs:(r + 1) * pairs], axis=1)
            o_ref[r] = (acc * g_ref[r].astype(F32)).astype(BF16)

    def any_alive(rems):
        most = functools.reduce(jnp.maximum, rems)
        return (jnp.max(most) > REM_DEAD_LOG2).astype(jnp.int32)

    @pl.when(qi == 0)
    def _():
        finish(sweep(diagonal_tiles(0), zero_rems, zero_accs)[1])

    def load_state():
        return [rem_ref[h] for h in range(nv)], [acc_ref[p] for p in range(nv // 2)]

    def store_state(rems, accs):
        for h in range(nv):
            rem_ref[h] = rems[h]
        for p in range(nv // 2):
            acc_ref[p] = accs[p]

    eager_rows = {s: min(tq, EAGER_SEEN - (NSUB - 1 - s) * KSUB) for s in range(NSUB)
                  if EAGER_SEEN > (NSUB - 1 - s) * KSUB}

    @pl.when(qi > 0)
    def _():
        first = key_block_tiles("window", 0, {s: (0, e) for s, e in eager_rows.items()})
        rems, accs = sweep(diagonal_tiles(tq) + first, zero_rems, zero_accs)
        store_state(rems, accs)
        alive = any_alive(rems)
        late = {s: (eager_rows.get(s, 0), tq) for s in range(NSUB) if eager_rows.get(s, 0) < tq}
        if late:
            @pl.when(any_alive([r[min(e for e, _ in late.values()):] for r in rems]) > 0)
            def _():
                store_state(*sweep(key_block_tiles("window", 0, late), *load_state()))

        def cond(c):
            n, alive = c
            return jnp.logical_and(n <= qi, alive > 0)

        def body(c):
            n, _ = c
            for cp in older(n):
                cp.start()
            for cp in older(n):
                cp.wait()
            rems, accs = sweep(key_block_tiles("older", 0), *load_state())
            store_state(rems, accs)
            return n + 1, any_alive(rems)

        lax.while_loop(cond, body, (jnp.int32(2), alive))
        finish(load_state()[1])


def _attn_prompt(q, kt, vb, gsb, *, tq, hps):
    nb, t, _ = q.shape
    assert t % tq == 0 and t >= 2 * tq and tq == NSUB * KSUB and hps == HEADS
    nv = nb * hps
    nq = t // tq
    uu = _suffix_sum_matrix()
    row_blk = pl.BlockSpec((nb, tq, SB_WIDTH), lambda i: (0, i, 0))
    in_hbm = pl.BlockSpec(memory_space=pltpu.HBM)
    block_bytes = 3 * _nbytes((nb, tq, SB_WIDTH), BF16) + _nbytes(uu.shape, BF16)
    temp_bytes = ((nv + nv // 2) * _nbytes((tq, LANES), F32)
                  + 5 * _nbytes((nb, SB_WIDTH, 2 * tq), BF16)
                  + 5 * (NSUB + 2) * nv * _nbytes((tq, KSUB), F32))
    return pl.pallas_call(
        functools.partial(_attn_prompt_kernel, tq=tq, hps=hps, nb=nb, nq=nq),
        grid=(nq,),
        in_specs=[row_blk, row_blk, pl.BlockSpec(uu.shape, lambda i: (0, 0)), in_hbm, in_hbm],
        out_specs=row_blk,
        out_shape=jax.ShapeDtypeStruct((nb, t, SB_WIDTH), BF16),
        scratch_shapes=[pltpu.VMEM((2, nb, SB_WIDTH, 2 * tq), BF16),
                        pltpu.VMEM((2, nb, 2 * tq, SB_WIDTH), BF16),
                        pltpu.VMEM((nb, SB_WIDTH, tq), BF16),
                        pltpu.VMEM((nb, tq, SB_WIDTH), BF16),
                        pltpu.SemaphoreType.DMA((2, 3)),
                        pltpu.VMEM((nv, tq, LANES), F32), pltpu.VMEM((nv // 2, tq, LANES), F32)],
        compiler_params=pltpu.CompilerParams(
            dimension_semantics=("arbitrary",),
            vmem_limit_bytes=_vmem_limit(block_bytes, temp_bytes)),
        name="sb_attn_prompt",
    )(q, gsb, uu, kt, vb)


def _attn_sample_kernel(q_ref, kn_ref, vn_ref, g_ref, uu_ref, ck_hbm, cv_hbm, o_ref,
                        kwin, vwin, sem, rem_s, acc_s, *, t_new, past, nb):
    rows = HEADS * t_new
    uu = uu_ref[...]
    n_sub = past // KSUB
    contract_last = (((1,), (1,)), ((), ()))

    def fetch(which, b, j, slot):
        src, dst = ((ck_hbm, kwin), (cv_hbm, vwin))[which]
        first = pl.multiple_of(j * KSUB, KSUB)
        return pltpu.make_async_copy(src.at[b, :, :, pl.ds(first, KSUB)], dst.at[b, slot],
                                     sem.at[which, b, slot])

    for b in range(nb):
        for slot in range(EAGER_CACHE_SUBS):
            for which in range(2):
                fetch(which, b, n_sub - 1 - slot, slot).start()

    row = lax.broadcasted_iota(jnp.int32, (rows, KSUB), 0)
    col = lax.broadcasted_iota(jnp.int32, (rows, KSUB), 1)
    new_vis = col < (row % t_new)
    pad = jnp.zeros((KSUB - t_new, SB_WIDTH), F32)

    def new_block(ref, b):
        blk = jnp.concatenate([ref[b], pad], axis=0).astype(BF16)
        return [blk[:, h * HEAD_DIM:(h + 1) * HEAD_DIM] for h in range(HEADS)]

    def cached(win, b, slot):
        return [win[b, slot, h].astype(BF16) for h in range(HEADS)]

    def sweep(b, tiles, rem, acc):
        q = q_ref[b]
        qh = [q[:, h * HEAD_DIM:(h + 1) * HEAD_DIM] for h in range(HEADS)]
        zs = [jnp.concatenate(
            [jnp.dot(qh[h], keys[h], preferred_element_type=F32) if dim_major else
             lax.dot_general(qh[h], keys[h], contract_last, preferred_element_type=F32)
             for h in range(HEADS)], axis=0) for keys, _, dim_major, _ in tiles]
        splits = [_split_bf16(_neg_log2_keep(z, t[3])) for z, t in zip(zs, tiles)]
        sums = jnp.dot(jnp.concatenate(splits, axis=0), uu, preferred_element_type=F32)
        for i, (z, (_, vals, dim_major, vis)) in enumerate(zip(zs, tiles)):
            w, rem = _sb_weights(z, vis, sums[i * rows:(i + 1) * rows], rem)
            wh = [w[h * t_new:(h + 1) * t_new] for h in range(HEADS)]
            acc = [a + (lax.dot_general(wh[h], vals[h], contract_last, preferred_element_type=F32)
                        if dim_major else jnp.dot(wh[h], vals[h], preferred_element_type=F32))
                   for h, a in enumerate(acc)]
        return rem, acc

    def alive_of(rem):
        return (jnp.max(rem) > REM_DEAD_LOG2).astype(jnp.int32)

    for b in range(nb):
        for slot in range(EAGER_CACHE_SUBS):
            for which in range(2):
                fetch(which, b, n_sub - 1 - slot, slot).wait()
    alive = jnp.int32(0)
    for b in range(nb):
        tiles = [(new_block(kn_ref, b), new_block(vn_ref, b), False, new_vis)]
        tiles += [(cached(kwin, b, s), cached(vwin, b, s), True, None)
                  for s in range(EAGER_CACHE_SUBS)]
        rem, acc = sweep(b, tiles, jnp.zeros((rows, LANES), F32),
                         [jnp.zeros((t_new, HEAD_DIM), F32)] * HEADS)
        rem_s[b] = rem
        acc_s[b] = jnp.concatenate(acc, axis=1)
        alive = jnp.maximum(alive, alive_of(rem))

    def cond(c):
        j, alive = c
        return jnp.logical_and(j >= 0, alive > 0)

    def body(c):
        j, _ = c
        for b in range(nb):
            for which in range(2):
                fetch(which, b, j, 0).start()
        alive = jnp.int32(0)
        for b in range(nb):
            for which in range(2):
                fetch(which, b, j, 0).wait()
            acc_b = acc_s[b]
            rem, acc = sweep(b, [(cached(kwin, b, 0), cached(vwin, b, 0), True, None)], rem_s[b],
                             [acc_b[:, h * HEAD_DIM:(h + 1) * HEAD_DIM] for h in range(HEADS)])
            rem_s[b] = rem
            acc_s[b] = jnp.concatenate(acc, axis=1)
            alive = jnp.maximum(alive, alive_of(rem))
        return j - 1, alive

    lax.while_loop(cond, body, (jnp.int32(n_sub - 1 - EAGER_CACHE_SUBS), alive))
    for b in range(nb):
        o_ref[b] = (acc_s[b] * g_ref[b].astype(F32)).astype(BF16)


def _attn_sample(q, cache_k, cache_v, k_new, v_new, gsb):
    nb, t_new, _ = q.shape
    past = cache_k.shape[3]
    assert cache_k.shape == (nb, HEADS, HEAD_DIM, past)
    assert past % KSUB == 0 and past // KSUB >= EAGER_CACHE_SUBS and t_new <= KSUB
    uu = _suffix_sum_matrix()
    rows = HEADS * t_new
    whole = lambda a: pl.BlockSpec(a.shape, lambda i: (0,) * a.ndim)
    window = (nb, EAGER_CACHE_SUBS, HEADS, HEAD_DIM, KSUB)
    block_bytes = sum(_nbytes(a.shape, a.dtype) for a in (q, k_new, v_new, gsb, uu, q))
    temp_bytes = (2 * _nbytes(window, F32) + _nbytes((nb, rows, LANES), F32)
                  + _nbytes((nb, t_new, SB_WIDTH), F32)
                  + 16 * (EAGER_CACHE_SUBS + 1) * _nbytes((rows, KSUB), F32))
    return pl.pallas_call(
        functools.partial(_attn_sample_kernel, t_new=t_new, past=past, nb=nb),
        grid=(1,),
        in_specs=[whole(q), whole(k_new), whole(v_new), whole(gsb), whole(uu),
                  pl.BlockSpec(memory_space=pltpu.HBM), pl.BlockSpec(memory_space=pltpu.HBM)],
        out_specs=whole(q),
        out_shape=jax.ShapeDtypeStruct((nb, t_new, SB_WIDTH), BF16),
        scratch_shapes=[pltpu.VMEM(window, F32), pltpu.VMEM(window, F32),
                        pltpu.SemaphoreType.DMA((2, nb, EAGER_CACHE_SUBS)),
                        pltpu.VMEM((nb, rows, LANES), F32),
                        pltpu.VMEM((nb, t_new, SB_WIDTH), F32)],
        compiler_params=pltpu.CompilerParams(
            dimension_semantics=("arbitrary",),
            vmem_limit_bytes=max(_vmem_limit(block_bytes, temp_bytes),
                                 VMEM_BYTES_V7X - _nbytes(cache_k.shape, F32))),
        name="sb_attn_sample",
    )(q, k_new, v_new, gsb, uu, cache_k, cache_v)


def _outproj_kernel(x_ref, osb_ref, osgu_ref, p_ref, wo_ref, pg_ref, wg_ref, wp_ref, y_ref):
    o = jnp.concatenate([osb_ref[0], osgu_ref[0]], axis=1)
    h = x_ref[0] + jnp.dot(o, wo_ref[...], preferred_element_type=F32)
    ms = jnp.mean(h * h, axis=-1, keepdims=True)
    hn = (h * lax.rsqrt(ms + EPS) * pg_ref[...]).astype(BF16)
    gate_logit = jnp.dot(hn, wg_ref[...], preferred_element_type=F32)
    gate = 1.0 / (1.0 + jnp.exp(-gate_logit))
    pp = jnp.dot(p_ref[0].astype(BF16), wp_ref[...], preferred_element_type=F32)
    y_ref[0] = h + gate * pp


def _outproj(x, osb, osgu, p, w_out_bf, ple_norm_g, w_gate_bf, w_proj_bf, *, tm, name):
    b, t, _ = x.shape
    assert t % tm == 0
    tok = lambda w: pl.BlockSpec((1, tm, w), lambda bi, i: (bi, i, 0))
    const2 = lambda a: pl.BlockSpec(a.shape, lambda bi, i: (0, 0))
    pg = ple_norm_g.reshape(1, D_MODEL)
    block_bytes = (2 * _nbytes((tm, D_MODEL), F32) + 2 * _nbytes((tm, SEG), BF16)
                   + _nbytes((tm, PLE_DIM), F32) + _nbytes(w_out_bf.shape, BF16)
                   + _nbytes(w_gate_bf.shape, BF16) + _nbytes(w_proj_bf.shape, BF16))
    temp_bytes = 6 * _nbytes((tm, D_MODEL), F32)
    return pl.pallas_call(
        _outproj_kernel,
        grid=(b, t // tm),
        in_specs=[tok(D_MODEL), tok(SEG), tok(SEG), tok(PLE_DIM), const2(w_out_bf), const2(pg),
                  const2(w_gate_bf), const2(w_proj_bf)],
        out_specs=tok(D_MODEL),
        out_shape=jax.ShapeDtypeStruct((b, t, D_MODEL), F32),
        compiler_params=pltpu.CompilerParams(
            dimension_semantics=("arbitrary", "arbitrary"),
            vmem_limit_bytes=_vmem_limit(block_bytes, temp_bytes)),
        name=name,
    )(x, osb, osgu, p, w_out_bf, pg, w_gate_bf, w_proj_bf)


def _sgu_bias_rows(sgu_b_l, period):
    per_pos = jnp.tile(sgu_b_l[:, :period].T, (SGU_CHUNK // period, 1))
    return jnp.repeat(per_pos, GROUP_W, axis=1)


def kernel(x_prompt, x_sample, cache_k, cache_v, p_prompt, p_sample, norm_g, w_in, q_norm_g,
           k_norm_g, sgu_norm_g, sgu_w, sgu_b, w_out, ple_norm_g, w_ple_gate, w_ple_proj):
    depth = w_in.shape[0]
    assert depth == 1, "one layer per call"
    l = 0
    b_p, t_p, _ = x_prompt.shape
    b_s, t_s, _ = x_sample.shape
    past = cache_k.shape[2]
    n_s = b_s * t_s
    assert n_s == SGU_CHUNK and SGU_CHUNK % t_s == 0

    w_in_bf = w_in[l].astype(BF16)
    w_out_bf = w_out[l].astype(BF16)
    w_gate_bf = w_ple_gate[l].astype(BF16)
    w_proj_bf = w_ple_proj[l].astype(BF16)

    q, k_feat, v_feat, kt, vb, gsb, osgu = _inproj(
        x_prompt, norm_g[l], w_in_bf, q_norm_g[l], k_norm_g[l], sgu_norm_g[l],
        sgu_w[l], _sgu_bias_rows(sgu_b[l], SGU_CHUNK),
        tm=TM_IN, period=SGU_CHUNK, attn_layout=True)
    osb = _attn_prompt(q, kt, vb, gsb, tq=TQ, hps=HEADS_PER_STEP)
    y_prompt = _outproj(x_prompt, osb, osgu, p_prompt[l], w_out_bf, ple_norm_g[l], w_gate_bf,
                        w_proj_bf, tm=TM_OUT, name="outproj_prompt")

    rep = SGU_CHUNK // t_s
    sgu_w_s = jnp.tile(sgu_w[l][:, :t_s, :t_s], (1, rep, rep))
    xs = x_sample.reshape(1, n_s, D_MODEL)
    q_s, k_s, v_s, gsb_s, osgu_s, vs_s = _inproj(
        xs, norm_g[l], w_in_bf, q_norm_g[l], k_norm_g[l], sgu_norm_g[l],
        sgu_w_s, _sgu_bias_rows(sgu_b[l], t_s),
        tm=n_s, period=t_s, attn_layout=False)
    shp = (b_s, t_s, SB_WIDTH)
    to_feat = lambda c: jnp.transpose(c[l], (0, 2, 3, 1))
    osb_s = _attn_sample(q_s.reshape(shp), to_feat(cache_k), to_feat(cache_v),
                         k_s.reshape(shp), v_s.reshape(shp),
                         gsb_s.reshape(shp))
    y_sample = _outproj(xs, osb_s.reshape(1, n_s, SB_WIDTH), osgu_s, p_sample[l].reshape(1, n_s, PLE_DIM),
                        w_out_bf, ple_norm_g[l], w_gate_bf, w_proj_bf, tm=n_s,
                        name="outproj_sample").reshape(b_s, t_s, D_MODEL)

    head_shape = lambda a, bb, tt: a.reshape(1, bb, tt, HEADS, HEAD_DIM)
    from_feat = lambda a: jnp.transpose(a.reshape(b_p, HEADS, HEAD_DIM, t_p), (0, 3, 1, 2))[None]
    return (y_prompt, y_sample,
            from_feat(k_feat), from_feat(v_feat),
            head_shape(k_s, b_s, t_s), head_shape(v_s, b_s, t_s),
            vs_s.reshape(1, b_s, t_s, GROUPS, GROUP_W))
```

```python
import functools
import math

import numpy as np
import jax
import jax.numpy as jnp
from jax import lax
from jax.experimental import pallas as pl
from jax.experimental.pallas import tpu as pltpu

F32 = jnp.float32
BF16 = jnp.bfloat16

LANES = 128
MXU_TILE = 256
VMEM_BYTES_V7X = 64 * 1024 * 1024

D_MODEL = 1024
PLE_DIM = 256
HEADS = 8
HEAD_DIM = 64
SB_WIDTH = HEADS * HEAD_DIM
GROUPS = 4
GROUP_W = 128
SGU_WIDTH = GROUPS * GROUP_W
SGU_CHUNK = 128
SEG = 512
EPS = 1e-6

Q_SCALE = HEAD_DIM ** -0.5 * math.log2(math.e)

KSUB = LANES
NSUB = 2
TQ = NSUB * KSUB
HEADS_PER_STEP = 8
REM_DEAD_LOG2 = -152.0
SOFTPLUS_LINEAR = 64.0
EAGER_CACHE_SUBS = 2
EAGER_SEEN = KSUB + 48
TM_IN = 1024
X_RING = 3
TM_OUT = 1024


def _vmem_limit(block_bytes, temp_bytes):
    need = 2 * block_bytes + temp_bytes
    return int(min(need, VMEM_BYTES_V7X - 8 * 1024 * 1024))


def _nbytes(shape, dtype):
    return int(np.prod(shape)) * jnp.dtype(dtype).itemsize


def _suffix_sum_matrix():
    j = np.arange(KSUB)[:, None]
    s = np.arange(KSUB)[None, :]
    one = np.concatenate([np.where(j >= s, -1.0, 0.0), -np.ones((KSUB, LANES))], axis=1)
    return jnp.asarray(np.concatenate([one, one], axis=0), dtype=BF16)


def _head_mean_matrix():
    a = np.arange(MXU_TILE)
    bd = np.where(a[:, None] // HEAD_DIM == a[None, :] // HEAD_DIM, 1.0 / HEAD_DIM, 0.0)
    return jnp.asarray(bd, dtype=BF16)


def _split_bf16(x):
    hi = x.astype(BF16)
    lo = (x - hi.astype(F32)).astype(BF16)
    return jnp.concatenate([hi, lo], axis=1)


def _gelu_tanh(x):
    return 0.5 * x * (1.0 + jnp.tanh(math.sqrt(2.0 / math.pi) * (x + 0.044715 * (x * x * x))))


def _silu(x):
    return x / (1.0 + jnp.exp(-x))


def _inproj_kernel(x_ref, ng_ref, w_ref, qg_ref, kg_ref, sg_ref, hm_ref, sw_ref, sb_ref,
                   *out_refs, tm, period, attn_layout):
    if attn_layout:
        q_ref, ktf_ref, vtf_ref, kt_ref, vb_ref, gsb_ref, osgu_ref, xring, xsem = out_refs
        nt = pl.num_programs(1)
        step = pl.program_id(0) * nt + pl.program_id(1)
        n_steps = pl.num_programs(0) * nt

        def x_copy(s):
            rows = pl.ds(pl.multiple_of((s % nt) * tm, tm), tm)
            return pltpu.make_async_copy(x_ref.at[s // nt, rows], xring.at[s % X_RING],
                                         xsem.at[s % X_RING])

        @pl.when(step == 0)
        def _():
            for s in range(X_RING - 1):
                @pl.when(s < n_steps)
                def _():
                    x_copy(s).start()

        @pl.when(step + X_RING - 1 < n_steps)
        def _():
            x_copy(step + X_RING - 1).start()

        x_copy(step).wait()
        x = xring[step % X_RING]
    else:
        q_ref, k_ref, v_ref, gsb_ref, osgu_ref, vs_ref = out_refs
        x = x_ref[0]
    ms = jnp.mean(x * x, axis=-1, keepdims=True)
    xn = (x * lax.rsqrt(ms + EPS) * ng_ref[...]).astype(BF16)

    def seg(i):
        return jnp.dot(xn, w_ref[:, i * SEG:(i + 1) * SEG], preferred_element_type=F32)

    def head_rms(t, g):
        sq = (t * t).astype(BF16)
        ms_h = jnp.concatenate(
            [jnp.dot(sq[:, c:c + MXU_TILE], hm_ref[...], preferred_element_type=F32)
             for c in range(0, SB_WIDTH, MXU_TILE)], axis=1)
        return t * lax.rsqrt(ms_h + EPS) * g

    q = head_rms(seg(0), qg_ref[...])
    q_ref[0] = (q * Q_SCALE).astype(BF16)

    k = head_rms(seg(1), kg_ref[...])
    v = seg(2)
    if attn_layout:
        k_t = k.T
        ktf_ref[0] = k_t
        kt_ref[0] = k_t.astype(BF16)
        vtf_ref[0] = v.T
        vb_ref[0] = v.astype(BF16)
    else:
        k_ref[0] = k
        v_ref[0] = v

    gsb_ref[0] = _silu(seg(3)).astype(BF16)

    u = _gelu_tanh(seg(4))
    vs_raw = _gelu_tanh(seg(5))
    vs_groups = []
    for g in range(GROUPS):
        t = vs_raw[:, g * GROUP_W:(g + 1) * GROUP_W]
        ms_g = jnp.mean(t * t, axis=-1, keepdims=True)
        vs_groups.append(t * lax.rsqrt(ms_g + EPS) * sg_ref[:, g * GROUP_W:(g + 1) * GROUP_W])
    if not attn_layout:
        vs_ref[0] = jnp.concatenate(vs_groups, axis=1)

    row = lax.broadcasted_iota(jnp.int32, (SGU_CHUNK, SGU_CHUNK), 0)
    col = lax.broadcasted_iota(jnp.int32, (SGU_CHUNK, SGU_CHUNK), 1)
    keep = row >= col
    if period != SGU_CHUNK:
        keep = keep & ((row // period) == (col // period))
    s_groups = []
    for g in range(GROUPS):
        wm = jnp.where(keep, sw_ref[g], 0.0).astype(BF16)
        vg = vs_groups[g].astype(BF16)
        chunks = [jnp.dot(wm, vg[c * SGU_CHUNK:(c + 1) * SGU_CHUNK], preferred_element_type=F32)
                  + sb_ref[:, g * GROUP_W:(g + 1) * GROUP_W]
                  for c in range(tm // SGU_CHUNK)]
        s_groups.append(jnp.concatenate(chunks, axis=0) if len(chunks) > 1 else chunks[0])
    s = jnp.concatenate(s_groups, axis=1)

    osgu_ref[0] = (u * s * _silu(seg(6))).astype(BF16)


def _inproj(x, norm_g, w_in_bf, q_norm_g, k_norm_g, sgu_norm_g, sgu_w_tiled, sgu_bias, *,
            tm, period, attn_layout):
    b, t, _ = x.shape
    assert t % tm == 0 and tm % SGU_CHUNK == 0
    grid = (b, t // tm)
    tok = lambda w: pl.BlockSpec((1, tm, w), lambda bi, i: (bi, i, 0))
    const2 = lambda a: pl.BlockSpec(a.shape, lambda bi, i: (0, 0))
    const3 = lambda a: pl.BlockSpec(a.shape, lambda bi, i: (0, 0, 0))

    ng = norm_g.reshape(1, D_MODEL)
    qg = jnp.tile(q_norm_g, HEADS).reshape(1, SB_WIDTH)
    kg = jnp.tile(k_norm_g, HEADS).reshape(1, SB_WIDTH)
    sg = sgu_norm_g.reshape(1, SGU_WIDTH)
    hm = _head_mean_matrix()

    act = lambda dt: jax.ShapeDtypeStruct((b, t, SEG), dt)
    if attn_layout:
        feat = lambda dt: jax.ShapeDtypeStruct((b, SB_WIDTH, t), dt)
        feat_blk = pl.BlockSpec((1, SB_WIDTH, tm), lambda bi, i: (bi, 0, i))
        out_shape = (act(BF16), feat(F32), feat(F32), feat(BF16), act(BF16), act(BF16), act(BF16))
        out_specs = (tok(SEG), feat_blk, feat_blk, feat_blk, tok(SEG), tok(SEG), tok(SEG))
    else:
        out_shape = (act(BF16), act(F32), act(F32), act(BF16), act(BF16), act(F32))
        out_specs = (tok(SEG),) * 6

    block_bytes = (_nbytes((tm, D_MODEL), F32) + _nbytes(w_in_bf.shape, BF16)
                   + _nbytes(hm.shape, BF16) + _nbytes(sgu_w_tiled.shape, F32)
                   + _nbytes(sgu_bias.shape, F32) + 7 * _nbytes((tm, SEG), F32))
    temp_bytes = 12 * _nbytes((tm, SEG), F32)
    x_spec, scratch = tok(D_MODEL), []
    if attn_layout:
        x_spec = pl.BlockSpec(memory_space=pltpu.HBM)
        scratch = [pltpu.VMEM((X_RING, tm, D_MODEL), F32), pltpu.SemaphoreType.DMA((X_RING,))]
        block_bytes -= _nbytes((tm, D_MODEL), F32)
        temp_bytes += X_RING * _nbytes((tm, D_MODEL), F32)
    return pl.pallas_call(
        functools.partial(_inproj_kernel, tm=tm, period=period, attn_layout=attn_layout),
        grid=grid,
        in_specs=[x_spec, const2(ng), const2(w_in_bf), const2(qg), const2(kg), const2(sg),
                  const2(hm), const3(sgu_w_tiled), const2(sgu_bias)],
        out_specs=out_specs,
        out_shape=out_shape,
        scratch_shapes=scratch,
        compiler_params=pltpu.CompilerParams(
            dimension_semantics=("arbitrary", "arbitrary"),
            vmem_limit_bytes=_vmem_limit(block_bytes, temp_bytes)),
        name="inproj_prompt" if attn_layout else "inproj_sample",
    )(x, ng, w_in_bf, qg, kg, sg, hm, sgu_w_tiled, sgu_bias)


def _neg_log2_keep(z, vis):
    sp = jnp.maximum(z, jnp.log2(1.0 + jnp.exp2(jnp.minimum(z, SOFTPLUS_LINEAR))))
    return sp if vis is None else jnp.where(vis, sp, 0.0)


def _sb_weights(z, vis, sums, rem):
    w = jnp.exp2(z + sums[:, :KSUB] + rem)
    if vis is not None:
        w = jnp.where(vis, w, 0.0)
    return w.astype(BF16), rem + sums[:, KSUB:]


def _attn_prompt_kernel(q_ref, g_ref, uu_ref, kt_hbm, v_hbm, o_ref,
                        kwin, vwin, kold, vold, sem, rem_ref, acc_ref, *, tq, hps, nb, nq):
    qi = pl.program_id(0)
    slot = qi % 2
    nv = nb * hps

    def window(step, slot_):
        first = pl.multiple_of(jnp.maximum(step - 1, 0) * tq, tq)
        return ([pltpu.make_async_copy(kt_hbm.at[r, :, pl.ds(first, 2 * tq)], kwin.at[slot_, r],
                                       sem.at[0, slot_]) for r in range(nb)]
                + [pltpu.make_async_copy(v_hbm.at[r, pl.ds(first, 2 * tq), :], vwin.at[slot_, r],
                                         sem.at[1, slot_]) for r in range(nb)])

    def older(n):
        first = pl.multiple_of((qi - n) * tq, tq)
        return ([pltpu.make_async_copy(kt_hbm.at[r, :, pl.ds(first, tq)], kold.at[r], sem.at[0, 2])
                 for r in range(nb)]
                + [pltpu.make_async_copy(v_hbm.at[r, pl.ds(first, tq), :], vold.at[r], sem.at[1, 2])
                   for r in range(nb)])

    @pl.when(qi == 0)
    def _():
        for c in window(qi, slot):
            c.start()

    for c in window(qi, slot):
        c.wait()

    @pl.when(qi + 1 < nq)
    def _():
        for c in window(qi + 1, 1 - slot):
            c.start()

    qh = [q_ref[v // hps][:, (v % hps) * HEAD_DIM:(v % hps + 1) * HEAD_DIM] for v in range(nv)]
    uu = uu_ref[...]
    lane = lax.broadcasted_iota(jnp.int32, (KSUB, LANES), 1)
    first_head = lane < HEAD_DIM

    def keys_at(src, r, ks):
        return (kwin[slot, r, :, pl.ds(ks, KSUB)] if src == "window"
                else kold[r, :, pl.ds(ks, KSUB)])

    def pair_values(src, ks, p):
        pairs = hps // 2
        lanes = pl.ds((p % pairs) * LANES, LANES)
        vv = (vwin[slot, p // pairs, pl.ds(ks, KSUB), lanes] if src == "window"
              else vold[p // pairs, pl.ds(ks, KSUB), lanes])
        zero = jnp.zeros_like(vv)
        return jnp.concatenate([jnp.where(first_head, vv, zero),
                                jnp.where(first_head, zero, vv)], axis=0)

    def sweep(tiles, rems, accs):
        def put(full, r0, r1, part):
            pieces = ([full[:r0]] if r0 else []) + [part] + ([full[r1:]] if r1 < tq else [])
            return jnp.concatenate(pieces, axis=0) if len(pieces) > 1 else part

        zs, splits = [], []
        for src, ks, r0, r1, diagonal in tiles:
            kts = [keys_at(src, r, ks) for r in range(nb)]
            vis = None
            if diagonal:
                row = lax.broadcasted_iota(jnp.int32, (r1 - r0, KSUB), 0)
                col = lax.broadcasted_iota(jnp.int32, (r1 - r0, KSUB), 1)
                vis = col < row
            for v in range(nv):
                h = v % hps
                z = jnp.dot(qh[v][r0:r1], kts[v // hps][h * HEAD_DIM:(h + 1) * HEAD_DIM],
                            preferred_element_type=F32)
                zs.append((z, vis))
                splits.append(_split_bf16(_neg_log2_keep(z, vis)))
        sums = jnp.dot(jnp.concatenate(splits, axis=0), uu, preferred_element_type=F32)
        rems, accs = list(rems), list(accs)
        off, i = 0, 0
        for src, ks, r0, r1, _ in tiles:
            ws = []
            for h in range(nv):
                z, vis = zs[i]
                i += 1
                w, new = _sb_weights(z, vis, sums[off:off + r1 - r0], rems[h][r0:r1])
                off += r1 - r0
                rems[h] = put(rems[h], r0, r1, new)
                ws.append(w)
            for p in range(nv // 2):
                d = jnp.dot(jnp.concatenate(ws[2 * p:2 * p + 2], axis=1), pair_values(src, ks, p),
                            preferred_element_type=F32)
                accs[p] = put(accs[p], r0, r1, accs[p][r0:r1] + d)
        return rems, accs

    def diagonal_tiles(at):
        return [("window", at + s * KSUB, s * KSUB, tq, True) for s in reversed(range(NSUB))]

    def key_block_tiles(src, at, rows=None):
        rows = rows or {s: (0, tq) for s in range(NSUB)}
        return [(src, at + s * KSUB, *rows[s], False) for s in reversed(range(NSUB)) if s in rows]

    zero_rems = [jnp.zeros((tq, LANES), F32)] * nv
    zero_accs = [jnp.zeros((tq, LANES), F32)] * (nv // 2)

    def finish(accs):
        pairs = hps // 2
        for r in range(nb):
            acc = jnp.concatenate(accs[r * pairs:(r + 1) * pairs], axis=1)
            o_ref[r] = (acc * g_ref[r].astype(F32)).astype(BF16)

    def any_alive(rems):
        most = functools.reduce(jnp.maximum, rems)
        return (jnp.max(most) > REM_DEAD_LOG2).astype(jnp.int32)

    @pl.when(qi == 0)
    def _():
        finish(sweep(diagonal_tiles(0), zero_rems, zero_accs)[1])

    def load_state():
        return [rem_ref[h] for h in range(nv)], [acc_ref[p] for p in range(nv // 2)]

    def store_state(rems, accs):
        for h in range(nv):
            rem_ref[h] = rems[h]
        for p in range(nv // 2):
            acc_ref[p] = accs[p]

    eager_rows = {s: min(tq, EAGER_SEEN - (NSUB - 1 - s) * KSUB) for s in range(NSUB)
                  if EAGER_SEEN > (NSUB - 1 - s) * KSUB}

    @pl.when(qi > 0)
    def _():
        first = key_block_tiles("window", 0, {s: (0, e) for s, e in eager_rows.items()})
        rems, accs = sweep(diagonal_tiles(tq) + first, zero_rems, zero_accs)
        store_state(rems, accs)
        alive = any_alive(rems)
        late = {s: (eager_rows.get(s, 0), tq) for s in range(NSUB) if eager_rows.get(s, 0) < tq}
        if late:
            @pl.when(any_alive([r[min(e for e, _ in late.values()):] for r in rems]) > 0)
            def _():
                store_state(*sweep(key_block_tiles("window", 0, late), *load_state()))

        def cond(c):
            n, alive = c
            return jnp.logical_and(n <= qi, alive > 0)

        def body(c):
            n, _ = c
            for cp in older(n):
                cp.start()
            for cp in older(n):
                cp.wait()
            rems, accs = sweep(key_block_tiles("older", 0), *load_state())
            store_state(rems, accs)
            return n + 1, any_alive(rems)

        lax.while_loop(cond, body, (jnp.int32(2), alive))
        finish(load_state()[1])


def _attn_prompt(q, kt, vb, gsb, *, tq, hps):
    nb, t, _ = q.shape
    assert t % tq == 0 and t >= 2 * tq and tq == NSUB * KSUB and hps == HEADS
    nv = nb * hps
    nq = t // tq
    uu = _suffix_sum_matrix()
    row_blk = pl.BlockSpec((nb, tq, SB_WIDTH), lambda i: (0, i, 0))
    in_hbm = pl.BlockSpec(memory_space=pltpu.HBM)
    block_bytes = 3 * _nbytes((nb, tq, SB_WIDTH), BF16) + _nbytes(uu.shape, BF16)
    temp_bytes = ((nv + nv // 2) * _nbytes((tq, LANES), F32)
                  + 5 * _nbytes((nb, SB_WIDTH, 2 * tq), BF16)
                  + 5 * (NSUB + 2) * nv * _nbytes((tq, KSUB), F32))
    return pl.pallas_call(
        functools.partial(_attn_prompt_kernel, tq=tq, hps=hps, nb=nb, nq=nq),
        grid=(nq,),
        in_specs=[row_blk, row_blk, pl.BlockSpec(uu.shape, lambda i: (0, 0)), in_hbm, in_hbm],
        out_specs=row_blk,
        out_shape=jax.ShapeDtypeStruct((nb, t, SB_WIDTH), BF16),
        scratch_shapes=[pltpu.VMEM((2, nb, SB_WIDTH, 2 * tq), BF16),
                        pltpu.VMEM((2, nb, 2 * tq, SB_WIDTH), BF16),
                        pltpu.VMEM((nb, SB_WIDTH, tq), BF16),
                        pltpu.VMEM((nb, tq, SB_WIDTH), BF16),
                        pltpu.SemaphoreType.DMA((2, 3)),
                        pltpu.VMEM((nv, tq, LANES), F32), pltpu.VMEM((nv // 2, tq, LANES), F32)],
        compiler_params=pltpu.CompilerParams(
            dimension_semantics=("arbitrary",),
            vmem_limit_bytes=_vmem_limit(block_bytes, temp_bytes)),
        name="sb_attn_prompt",
    )(q, gsb, uu, kt, vb)


def _attn_sample_kernel(q_ref, kn_ref, vn_ref, g_ref, uu_ref, ck_hbm, cv_hbm, o_ref,
                        kwin, vwin, sem, rem_s, acc_s, *, t_new, past, nb):
    rows = HEADS * t_new
    uu = uu_ref[...]
    n_sub = past // KSUB
    contract_last = (((1,), (1,)), ((), ()))

    def fetch(which, b, j, slot):
        src, dst = ((ck_hbm, kwin), (cv_hbm, vwin))[which]
        first = pl.multiple_of(j * KSUB, KSUB)
        return pltpu.make_async_copy(src.at[b, :, :, pl.ds(first, KSUB)], dst.at[b, slot],
                                     sem.at[which, b, slot])

    for b in range(nb):
        for slot in range(EAGER_CACHE_SUBS):
            for which in range(2):
                fetch(which, b, n_sub - 1 - slot, slot).start()

    row = lax.broadcasted_iota(jnp.int32, (rows, KSUB), 0)
    col = lax.broadcasted_iota(jnp.int32, (rows, KSUB), 1)
    new_vis = col < (row % t_new)
    pad = jnp.zeros((KSUB - t_new, SB_WIDTH), F32)

    def new_block(ref, b):
        blk = jnp.concatenate([ref[b], pad], axis=0).astype(BF16)
        return [blk[:, h * HEAD_DIM:(h + 1) * HEAD_DIM] for h in range(HEADS)]

    def cached(win, b, slot):
        return [win[b, slot, h].astype(BF16) for h in range(HEADS)]

    def sweep(b, tiles, rem, acc):
        q = q_ref[b]
        qh = [q[:, h * HEAD_DIM:(h + 1) * HEAD_DIM] for h in range(HEADS)]
        zs = [jnp.concatenate(
            [jnp.dot(qh[h], keys[h], preferred_element_type=F32) if dim_major else
             lax.dot_general(qh[h], keys[h], contract_last, preferred_element_type=F32)
             for h in range(HEADS)], axis=0) for keys, _, dim_major, _ in tiles]
        splits = [_split_bf16(_neg_log2_keep(z, t[3])) for z, t in zip(zs, tiles)]
        sums = jnp.dot(jnp.concatenate(splits, axis=0), uu, preferred_element_type=F32)
        for i, (z, (_, vals, dim_major, vis)) in enumerate(zip(zs, tiles)):
            w, rem = _sb_weights(z, vis, sums[i * rows:(i + 1) * rows], rem)
            wh = [w[h * t_new:(h + 1) * t_new] for h in range(HEADS)]
            acc = [a + (lax.dot_general(wh[h], vals[h], contract_last, preferred_element_type=F32)
                        if dim_major else jnp.dot(wh[h], vals[h], preferred_element_type=F32))
                   for h, a in enumerate(acc)]
        return rem, acc

    def alive_of(rem):
        return (jnp.max(rem) > REM_DEAD_LOG2).astype(jnp.int32)

    for b in range(nb):
        for slot in range(EAGER_CACHE_SUBS):
            for which in range(2):
                fetch(which, b, n_sub - 1 - slot, slot).wait()
    alive = jnp.int32(0)
    for b in range(nb):
        tiles = [(new_block(kn_ref, b), new_block(vn_ref, b), False, new_vis)]
        tiles += [(cached(kwin, b, s), cached(vwin, b, s), True, None)
                  for s in range(EAGER_CACHE_SUBS)]
        rem, acc = sweep(b, tiles, jnp.zeros((rows, LANES), F32),
                         [jnp.zeros((t_new, HEAD_DIM), F32)] * HEADS)
        rem_s[b] = rem
        acc_s[b] = jnp.concatenate(acc, axis=1)
        alive = jnp.maximum(alive, alive_of(rem))

    def cond(c):
        j, alive = c
        return jnp.logical_and(j >= 0, alive > 0)

    def body(c):
        j, _ = c
        for b in range(nb):
            for which in range(2):
                fetch(which, b, j, 0).start()
        alive = jnp.int32(0)
        for b in range(nb):
            for which in range(2):
                fetch(which, b, j, 0).wait()
            acc_b = acc_s[b]
            rem, acc = sweep(b, [(cached(kwin, b, 0), cached(vwin, b, 0), True, None)], rem_s[b],
                             [acc_b[:, h * HEAD_DIM:(h + 1) * HEAD_DIM] for h in range(HEADS)])
            rem_s[b] = rem
            acc_s[b] = jnp.concatenate(acc, axis=1)
            alive = jnp.maximum(alive, alive_of(rem))
        return j - 1, alive

    lax.while_loop(cond, body, (jnp.int32(n_sub - 1 - EAGER_CACHE_SUBS), alive))
    for b in range(nb):
        o_ref[b] = (acc_s[b] * g_ref[b].astype(F32)).astype(BF16)


def _attn_sample(q, cache_k, cache_v, k_new, v_new, gsb):
    nb, t_new, _ = q.shape
    past = cache_k.shape[3]
    assert cache_k.shape == (nb, HEADS, HEAD_DIM, past)
    assert past % KSUB == 0 and past // KSUB >= EAGER_CACHE_SUBS and t_new <= KSUB
    uu = _suffix_sum_matrix()
    rows = HEADS * t_new
    whole = lambda a: pl.BlockSpec(a.shape, lambda i: (0,) * a.ndim)
    window = (nb, EAGER_CACHE_SUBS, HEADS, HEAD_DIM, KSUB)
    block_bytes = sum(_nbytes(a.shape, a.dtype) for a in (q, k_new, v_new, gsb, uu, q))
    temp_bytes = (2 * _nbytes(window, F32) + _nbytes((nb, rows, LANES), F32)
                  + _nbytes((nb, t_new, SB_WIDTH), F32)
                  + 16 * (EAGER_CACHE_SUBS + 1) * _nbytes((rows, KSUB), F32))
    return pl.pallas_call(
        functools.partial(_attn_sample_kernel, t_new=t_new, past=past, nb=nb),
        grid=(1,),
        in_specs=[whole(q), whole(k_new), whole(v_new), whole(gsb), whole(uu),
                  pl.BlockSpec(memory_space=pltpu.HBM), pl.BlockSpec(memory_space=pltpu.HBM)],
        out_specs=whole(q),
        out_shape=jax.ShapeDtypeStruct((nb, t_new, SB_WIDTH), BF16),
        scratch_shapes=[pltpu.VMEM(window, F32), pltpu.VMEM(window, F32),
                        pltpu.SemaphoreType.DMA((2, nb, EAGER_CACHE_SUBS)),
                        pltpu.VMEM((nb, rows, LANES), F32),
                        pltpu.VMEM((nb, t_new, SB_WIDTH), F32)],
        compiler_params=pltpu.CompilerParams(
            dimension_semantics=("arbitrary",),
            vmem_limit_bytes=max(_vmem_limit(block_bytes, temp_bytes),
                                 VMEM_BYTES_V7X - _nbytes(cache_k.shape, F32))),
        name="sb_attn_sample",
    )(q, k_new, v_new, gsb, uu, cache_k, cache_v)


def _outproj_kernel(x_ref, osb_ref, osgu_ref, p_ref, wo_ref, pg_ref, wg_ref, wp_ref, y_ref):
    o = jnp.concatenate([osb_ref[0], osgu_ref[0]], axis=1)
    h = x_ref[0] + jnp.dot(o, wo_ref[...], preferred_element_type=F32)
    ms = jnp.mean(h * h, axis=-1, keepdims=True)
    hn = (h * lax.rsqrt(ms + EPS) * pg_ref[...]).astype(BF16)
    gate_logit = jnp.dot(hn, wg_ref[...], preferred_element_type=F32)
    gate = 1.0 / (1.0 + jnp.exp(-gate_logit))
    pp = jnp.dot(p_ref[0].astype(BF16), wp_ref[...], preferred_element_type=F32)
    y_ref[0] = h + gate * pp


def _outproj(x, osb, osgu, p, w_out_bf, ple_norm_g, w_gate_bf, w_proj_bf, *, tm, name):
    b, t, _ = x.shape
    assert t % tm == 0
    tok = lambda w: pl.BlockSpec((1, tm, w), lambda bi, i: (bi, i, 0))
    const2 = lambda a: pl.BlockSpec(a.shape, lambda bi, i: (0, 0))
    pg = ple_norm_g.reshape(1, D_MODEL)
    block_bytes = (2 * _nbytes((tm, D_MODEL), F32) + 2 * _nbytes((tm, SEG), BF16)
                   + _nbytes((tm, PLE_DIM), F32) + _nbytes(w_out_bf.shape, BF16)
                   + _nbytes(w_gate_bf.shape, BF16) + _nbytes(w_proj_bf.shape, BF16))
    temp_bytes = 6 * _nbytes((tm, D_MODEL), F32)
    return pl.pallas_call(
        _outproj_kernel,
        grid=(b, t // tm),
        in_specs=[tok(D_MODEL), tok(SEG), tok(SEG), tok(PLE_DIM), const2(w_out_bf), const2(pg),
                  const2(w_gate_bf), const2(w_proj_bf)],
        out_specs=tok(D_MODEL),
        out_shape=jax.ShapeDtypeStruct((b, t, D_MODEL), F32),
        compiler_params=pltpu.CompilerParams(
            dimension_semantics=("arbitrary", "arbitrary"),
            vmem_limit_bytes=_vmem_limit(block_bytes, temp_bytes)),
        name=name,
    )(x, osb, osgu, p, w_out_bf, pg, w_gate_bf, w_proj_bf)


def _sgu_bias_rows(sgu_b_l, period):
    per_pos = jnp.tile(sgu_b_l[:, :period].T, (SGU_CHUNK // period, 1))
    return jnp.repeat(per_pos, GROUP_W, axis=1)


def kernel(x_prompt, x_sample, cache_k, cache_v, p_prompt, p_sample, norm_g, w_in, q_norm_g,
           k_norm_g, sgu_norm_g, sgu_w, sgu_b, w_out, ple_norm_g, w_ple_gate, w_ple_proj):
    depth = w_in.shape[0]
    assert depth == 1, "one layer per call"
    l = 0
    b_p, t_p, _ = x_prompt.shape
    b_s, t_s, _ = x_sample.shape
    past = cache_k.shape[2]
    n_s = b_s * t_s
    assert n_s == SGU_CHUNK and SGU_CHUNK % t_s == 0

    w_in_bf = w_in[l].astype(BF16)
    w_out_bf = w_out[l].astype(BF16)
    w_gate_bf = w_ple_gate[l].astype(BF16)
    w_proj_bf = w_ple_proj[l].astype(BF16)

    q, k_feat, v_feat, kt, vb, gsb, osgu = _inproj(
        x_prompt, norm_g[l], w_in_bf, q_norm_g[l], k_norm_g[l], sgu_norm_g[l],
        sgu_w[l], _sgu_bias_rows(sgu_b[l], SGU_CHUNK),
        tm=TM_IN, period=SGU_CHUNK, attn_layout=True)
    osb = _attn_prompt(q, kt, vb, gsb, tq=TQ, hps=HEADS_PER_STEP)
    y_prompt = _outproj(x_prompt, osb, osgu, p_prompt[l], w_out_bf, ple_norm_g[l], w_gate_bf,
                        w_proj_bf, tm=TM_OUT, name="outproj_prompt")

    rep = SGU_CHUNK // t_s
    sgu_w_s = jnp.tile(sgu_w[l][:, :t_s, :t_s], (1, rep, rep))
    xs = x_sample.reshape(1, n_s, D_MODEL)
    q_s, k_s, v_s, gsb_s, osgu_s, vs_s = _inproj(
        xs, norm_g[l], w_in_bf, q_norm_g[l], k_norm_g[l], sgu_norm_g[l],
        sgu_w_s, _sgu_bias_rows(sgu_b[l], t_s),
        tm=n_s, period=t_s, attn_layout=False)
    shp = (b_s, t_s, SB_WIDTH)
    to_feat = lambda c: jnp.transpose(c[l], (0, 2, 3, 1))
    osb_s = _attn_sample(q_s.reshape(shp), to_feat(cache_k), to_feat(cache_v),
                         k_s.reshape(shp), v_s.reshape(shp),
                         gsb_s.reshape(shp))
    y_sample = _outproj(xs, osb_s.reshape(1, n_s, SB_WIDTH), osgu_s, p_sample[l].reshape(1, n_s, PLE_DIM),
                        w_out_bf, ple_norm_g[l], w_gate_bf, w_proj_bf, tm=n_s,
                        name="outproj_sample").reshape(b_s, t_s, D_MODEL)

    head_shape = lambda a, bb, tt: a.reshape(1, bb, tt, HEADS, HEAD_DIM)
    from_feat = lambda a: jnp.transpose(a.reshape(b_p, HEADS, HEAD_DIM, t_p), (0, 3, 1, 2))[None]
    return (y_prompt, y_sample,
            from_feat(k_feat), from_feat(v_feat),
            head_shape(k_s, b_s, t_s), head_shape(v_s, b_s, t_s),
            vs_s.reshape(1, b_s, t_s, GROUPS, GROUP_W))
```

```python
import functools
import math

import numpy as np
import jax
import jax.numpy as jnp
from jax import lax
from jax.experimental import pallas as pl
from jax.experimental.pallas import tpu as pltpu

F32 = jnp.float32
BF16 = jnp.bfloat16

LANES = 128
MXU_TILE = 256
VMEM_BYTES_V7X = 64 * 1024 * 1024

D_MODEL = 1024
PLE_DIM = 256
HEADS = 8
HEAD_DIM = 64
SB_WIDTH = HEADS * HEAD_DIM
GROUPS = 4
GROUP_W = 128
SGU_WIDTH = GROUPS * GROUP_W
SGU_CHUNK = 128
SEG = 512
EPS = 1e-6

Q_SCALE = HEAD_DIM ** -0.5 * math.log2(math.e)

KSUB = LANES
NSUB = 2
TQ = NSUB * KSUB
HEADS_PER_STEP = 8
REM_DEAD_LOG2 = -152.0
SOFTPLUS_LINEAR = 64.0
EAGER_CACHE_SUBS = 2
EAGER_SEEN = KSUB + 48
TM_IN = 1024
TM_OUT = 1024


def _vmem_limit(block_bytes, temp_bytes):
    need = 2 * block_bytes + temp_bytes
    return int(min(need, VMEM_BYTES_V7X - 8 * 1024 * 1024))


def _nbytes(shape, dtype):
    return int(np.prod(shape)) * jnp.dtype(dtype).itemsize


def _suffix_sum_matrix():
    j = np.arange(KSUB)[:, None]
    s = np.arange(KSUB)[None, :]
    one = np.concatenate([np.where(j >= s, -1.0, 0.0), -np.ones((KSUB, LANES))], axis=1)
    return jnp.asarray(np.concatenate([one, one], axis=0), dtype=BF16)


def _head_mean_matrix():
    a = np.arange(MXU_TILE)
    bd = np.where(a[:, None] // HEAD_DIM == a[None, :] // HEAD_DIM, 1.0 / HEAD_DIM, 0.0)
    return jnp.asarray(bd, dtype=BF16)


def _split_bf16(x):
    hi = x.astype(BF16)
    lo = (x - hi.astype(F32)).astype(BF16)
    return jnp.concatenate([hi, lo], axis=1)


def _gelu_tanh(x):
    return 0.5 * x * (1.0 + jnp.tanh(math.sqrt(2.0 / math.pi) * (x + 0.044715 * (x * x * x))))


def _silu(x):
    return x / (1.0 + jnp.exp(-x))


def _inproj_kernel(x_ref, ng_ref, w_ref, qg_ref, kg_ref, sg_ref, hm_ref, sw_ref, sb_ref,
                   *out_refs, tm, period, attn_layout):
    if attn_layout:
        q_ref, ktf_ref, vtf_ref, kt_ref, vb_ref, gsb_ref, osgu_ref = out_refs
    else:
        q_ref, k_ref, v_ref, gsb_ref, osgu_ref, vs_ref = out_refs

    x = x_ref[0]
    ms = jnp.mean(x * x, axis=-1, keepdims=True)
    xn = (x * lax.rsqrt(ms + EPS) * ng_ref[...]).astype(BF16)

    def seg(i):
        return jnp.dot(xn, w_ref[:, i * SEG:(i + 1) * SEG], preferred_element_type=F32)

    def head_rms(t, g):
        sq = (t * t).astype(BF16)
        ms_h = jnp.concatenate(
            [jnp.dot(sq[:, c:c + MXU_TILE], hm_ref[...], preferred_element_type=F32)
             for c in range(0, SB_WIDTH, MXU_TILE)], axis=1)
        return t * lax.rsqrt(ms_h + EPS) * g

    q = head_rms(seg(0), qg_ref[...])
    q_ref[0] = (q * Q_SCALE).astype(BF16)

    k = head_rms(seg(1), kg_ref[...])
    v = seg(2)
    if attn_layout:
        k_t = k.T
        ktf_ref[0] = k_t
        kt_ref[0] = k_t.astype(BF16)
        vtf_ref[0] = v.T
        vb_ref[0] = v.astype(BF16)
    else:
        k_ref[0] = k
        v_ref[0] = v

    gsb_ref[0] = _silu(seg(3)).astype(BF16)

    u = _gelu_tanh(seg(4))
    vs_raw = _gelu_tanh(seg(5))
    vs_groups = []
    for g in range(GROUPS):
        t = vs_raw[:, g * GROUP_W:(g + 1) * GROUP_W]
        ms_g = jnp.mean(t * t, axis=-1, keepdims=True)
        vs_groups.append(t * lax.rsqrt(ms_g + EPS) * sg_ref[:, g * GROUP_W:(g + 1) * GROUP_W])
    if not attn_layout:
        vs_ref[0] = jnp.concatenate(vs_groups, axis=1)

    row = lax.broadcasted_iota(jnp.int32, (SGU_CHUNK, SGU_CHUNK), 0)
    col = lax.broadcasted_iota(jnp.int32, (SGU_CHUNK, SGU_CHUNK), 1)
    keep = row >= col
    if period != SGU_CHUNK:
        keep = keep & ((row // period) == (col // period))
    s_groups = []
    for g in range(GROUPS):
        wm = jnp.where(keep, sw_ref[g], 0.0).astype(BF16)
        vg = vs_groups[g].astype(BF16)
        chunks = [jnp.dot(wm, vg[c * SGU_CHUNK:(c + 1) * SGU_CHUNK], preferred_element_type=F32)
                  + sb_ref[:, g * GROUP_W:(g + 1) * GROUP_W]
                  for c in range(tm // SGU_CHUNK)]
        s_groups.append(jnp.concatenate(chunks, axis=0) if len(chunks) > 1 else chunks[0])
    s = jnp.concatenate(s_groups, axis=1)

    osgu_ref[0] = (u * s * _silu(seg(6))).astype(BF16)


def _inproj(x, norm_g, w_in_bf, q_norm_g, k_norm_g, sgu_norm_g, sgu_w_tiled, sgu_bias, *,
            tm, period, attn_layout):
    b, t, _ = x.shape
    assert t % tm == 0 and tm % SGU_CHUNK == 0
    grid = (b, t // tm)
    tok = lambda w: pl.BlockSpec((1, tm, w), lambda bi, i: (bi, i, 0))
    const2 = lambda a: pl.BlockSpec(a.shape, lambda bi, i: (0, 0), pipeline_mode=pl.Buffered(1))
    const3 = lambda a: pl.BlockSpec(a.shape, lambda bi, i: (0, 0, 0))

    ng = norm_g.reshape(1, D_MODEL)
    qg = jnp.tile(q_norm_g, HEADS).reshape(1, SB_WIDTH)
    kg = jnp.tile(k_norm_g, HEADS).reshape(1, SB_WIDTH)
    sg = sgu_norm_g.reshape(1, SGU_WIDTH)
    hm = _head_mean_matrix()

    act = lambda dt: jax.ShapeDtypeStruct((b, t, SEG), dt)
    if attn_layout:
        feat = lambda dt: jax.ShapeDtypeStruct((b, SB_WIDTH, t), dt)
        feat_blk = pl.BlockSpec((1, SB_WIDTH, tm), lambda bi, i: (bi, 0, i))
        out_shape = (act(BF16), feat(F32), feat(F32), feat(BF16), act(BF16), act(BF16), act(BF16))
        out_specs = (tok(SEG), feat_blk, feat_blk, feat_blk, tok(SEG), tok(SEG), tok(SEG))
    else:
        out_shape = (act(BF16), act(F32), act(F32), act(BF16), act(BF16), act(F32))
        out_specs = (tok(SEG),) * 6

    block_bytes = (_nbytes((tm, D_MODEL), F32) + _nbytes(w_in_bf.shape, BF16)
                   + _nbytes(hm.shape, BF16) + _nbytes(sgu_w_tiled.shape, F32)
                   + _nbytes(sgu_bias.shape, F32) + 7 * _nbytes((tm, SEG), F32))
    temp_bytes = 12 * _nbytes((tm, SEG), F32)
    return pl.pallas_call(
        functools.partial(_inproj_kernel, tm=tm, period=period, attn_layout=attn_layout),
        grid=grid,
        in_specs=[tok(D_MODEL), const2(ng), const2(w_in_bf), const2(qg), const2(kg), const2(sg),
                  const2(hm), const3(sgu_w_tiled), const2(sgu_bias)],
        out_specs=out_specs,
        out_shape=out_shape,
        compiler_params=pltpu.CompilerParams(
            dimension_semantics=("arbitrary", "arbitrary"),
            vmem_limit_bytes=_vmem_limit(block_bytes, temp_bytes)),
        name="inproj_prompt" if attn_layout else "inproj_sample",
    )(x, ng, w_in_bf, qg, kg, sg, hm, sgu_w_tiled, sgu_bias)


def _neg_log2_keep(z, vis):
    sp = jnp.maximum(z, jnp.log2(1.0 + jnp.exp2(jnp.minimum(z, SOFTPLUS_LINEAR))))
    return sp if vis is None else jnp.where(vis, sp, 0.0)


def _sb_weights(z, vis, sums, rem):
    w = jnp.exp2(z + sums[:, :KSUB] + rem)
    if vis is not None:
        w = jnp.where(vis, w, 0.0)
    return w.astype(BF16), rem + sums[:, KSUB:]


def _attn_prompt_kernel(q_ref, g_ref, uu_ref, kt_hbm, v_hbm, o_ref,
                        kwin, vwin, kold, vold, sem, rem_ref, acc_ref, *, tq, hps, nb, nq):
    qi = pl.program_id(0)
    slot = qi % 2
    nv = nb * hps

    def window(step, slot_):
        first = pl.multiple_of(jnp.maximum(step - 1, 0) * tq, tq)
        return ([pltpu.make_async_copy(kt_hbm.at[r, :, pl.ds(first, 2 * tq)], kwin.at[slot_, r],
                                       sem.at[0, slot_]) for r in range(nb)]
                + [pltpu.make_async_copy(v_hbm.at[r, pl.ds(first, 2 * tq), :], vwin.at[slot_, r],
                                         sem.at[1, slot_]) for r in range(nb)])

    def older(n):
        first = pl.multiple_of((qi - n) * tq, tq)
        return ([pltpu.make_async_copy(kt_hbm.at[r, :, pl.ds(first, tq)], kold.at[r], sem.at[0, 2])
                 for r in range(nb)]
                + [pltpu.make_async_copy(v_hbm.at[r, pl.ds(first, tq), :], vold.at[r], sem.at[1, 2])
                   for r in range(nb)])

    @pl.when(qi == 0)
    def _():
        for c in window(qi, slot):
            c.start()

    for c in window(qi, slot):
        c.wait()

    @pl.when(qi + 1 < nq)
    def _():
        for c in window(qi + 1, 1 - slot):
            c.start()

    qh = [q_ref[v // hps][:, (v % hps) * HEAD_DIM:(v % hps + 1) * HEAD_DIM] for v in range(nv)]
    uu = uu_ref[...]
    lane = lax.broadcasted_iota(jnp.int32, (KSUB, LANES), 1)
    first_head = lane < HEAD_DIM

    def keys_at(src, r, ks):
        return (kwin[slot, r, :, pl.ds(ks, KSUB)] if src == "window"
                else kold[r, :, pl.ds(ks, KSUB)])

    def pair_values(src, ks, p):
        pairs = hps // 2
        lanes = pl.ds((p % pairs) * LANES, LANES)
        vv = (vwin[slot, p // pairs, pl.ds(ks, KSUB), lanes] if src == "window"
              else vold[p // pairs, pl.ds(ks, KSUB), lanes])
        zero = jnp.zeros_like(vv)
        return jnp.concatenate([jnp.where(first_head, vv, zero),
                                jnp.where(first_head, zero, vv)], axis=0)

    def sweep(tiles, rems, accs):
        def put(full, r0, r1, part):
            pieces = ([full[:r0]] if r0 else []) + [part] + ([full[r1:]] if r1 < tq else [])
            return jnp.concatenate(pieces, axis=0) if len(pieces) > 1 else part

        zs, splits = [], []
        for src, ks, r0, r1, diagonal in tiles:
            kts = [keys_at(src, r, ks) for r in range(nb)]
            vis = None
            if diagonal:
                row = lax.broadcasted_iota(jnp.int32, (r1 - r0, KSUB), 0)
                col = lax.broadcasted_iota(jnp.int32, (r1 - r0, KSUB), 1)
                vis = col < row
            for v in range(nv):
                h = v % hps
                z = jnp.dot(qh[v][r0:r1], kts[v // hps][h * HEAD_DIM:(h + 1) * HEAD_DIM],
                            preferred_element_type=F32)
                zs.append((z, vis))
                splits.append(_split_bf16(_neg_log2_keep(z, vis)))
        sums = jnp.dot(jnp.concatenate(splits, axis=0), uu, preferred_element_type=F32)
        rems, accs = list(rems), list(accs)
        off, i = 0, 0
        for src, ks, r0, r1, _ in tiles:
            ws = []
            for h in range(nv):
                z, vis = zs[i]
                i += 1
                w, new = _sb_weights(z, vis, sums[off:off + r1 - r0], rems[h][r0:r1])
                off += r1 - r0
                rems[h] = put(rems[h], r0, r1, new)
                ws.append(w)
            for p in range(nv // 2):
                d = jnp.dot(jnp.concatenate(ws[2 * p:2 * p + 2], axis=1), pair_values(src, ks, p),
                            preferred_element_type=F32)
                accs[p] = put(accs[p], r0, r1, accs[p][r0:r1] + d)
        return rems, accs

    def diagonal_tiles(at):
        return [("window", at + s * KSUB, s * KSUB, tq, True) for s in reversed(range(NSUB))]

    def key_block_tiles(src, at, rows=None):
        rows = rows or {s: (0, tq) for s in range(NSUB)}
        return [(src, at + s * KSUB, *rows[s], False) for s in reversed(range(NSUB)) if s in rows]

    zero_rems = [jnp.zeros((tq, LANES), F32)] * nv
    zero_accs = [jnp.zeros((tq, LANES), F32)] * (nv // 2)

    def finish(accs):
        pairs = hps // 2
        for r in range(nb):
            acc = jnp.concatenate(accs[r * pairs:(r + 1) * pairs], axis=1)
            o_ref[r] = (acc * g_ref[r].astype(F32)).astype(BF16)

    def any_alive(rems):
        most = functools.reduce(jnp.maximum, rems)
        return (jnp.max(most) > REM_DEAD_LOG2).astype(jnp.int32)

    @pl.when(qi == 0)
    def _():
        finish(sweep(diagonal_tiles(0), zero_rems, zero_accs)[1])

    def load_state():
        return [rem_ref[h] for h in range(nv)], [acc_ref[p] for p in range(nv // 2)]

    def store_state(rems, accs):
        for h in range(nv):
            rem_ref[h] = rems[h]
        for p in range(nv // 2):
            acc_ref[p] = accs[p]

    eager_rows = {s: min(tq, EAGER_SEEN - (NSUB - 1 - s) * KSUB) for s in range(NSUB)
                  if EAGER_SEEN > (NSUB - 1 - s) * KSUB}

    @pl.when(qi > 0)
    def _():
        first = key_block_tiles("window", 0, {s: (0, e) for s, e in eager_rows.items()})
        rems, accs = sweep(diagonal_tiles(tq) + first, zero_rems, zero_accs)
        store_state(rems, accs)
        alive = any_alive(rems)
        late = {s: (eager_rows.get(s, 0), tq) for s in range(NSUB) if eager_rows.get(s, 0) < tq}
        if late:
            @pl.when(any_alive([r[min(e for e, _ in late.values()):] for r in rems]) > 0)
            def _():
                store_state(*sweep(key_block_tiles("window", 0, late), *load_state()))

        def cond(c):
            n, alive = c
            return jnp.logical_and(n <= qi, alive > 0)

        def body(c):
            n, _ = c
            for cp in older(n):
                cp.start()
            for cp in older(n):
                cp.wait()
            rems, accs = sweep(key_block_tiles("older", 0), *load_state())
            store_state(rems, accs)
            return n + 1, any_alive(rems)

        lax.while_loop(cond, body, (jnp.int32(2), alive))
        finish(load_state()[1])


def _attn_prompt(q, kt, vb, gsb, *, tq, hps):
    nb, t, _ = q.shape
    assert t % tq == 0 and t >= 2 * tq and tq == NSUB * KSUB and hps == HEADS
    nv = nb * hps
    nq = t // tq
    uu = _suffix_sum_matrix()
    row_blk = pl.BlockSpec((nb, tq, SB_WIDTH), lambda i: (0, i, 0))
    in_hbm = pl.BlockSpec(memory_space=pltpu.HBM)
    block_bytes = 3 * _nbytes((nb, tq, SB_WIDTH), BF16) + _nbytes(uu.shape, BF16)
    temp_bytes = ((nv + nv // 2) * _nbytes((tq, LANES), F32)
                  + 5 * _nbytes((nb, SB_WIDTH, 2 * tq), BF16)
                  + 5 * (NSUB + 2) * nv * _nbytes((tq, KSUB), F32))
    return pl.pallas_call(
        functools.partial(_attn_prompt_kernel, tq=tq, hps=hps, nb=nb, nq=nq),
        grid=(nq,),
        in_specs=[row_blk, row_blk, pl.BlockSpec(uu.shape, lambda i: (0, 0)), in_hbm, in_hbm],
        out_specs=row_blk,
        out_shape=jax.ShapeDtypeStruct((nb, t, SB_WIDTH), BF16),
        scratch_shapes=[pltpu.VMEM((2, nb, SB_WIDTH, 2 * tq), BF16),
                        pltpu.VMEM((2, nb, 2 * tq, SB_WIDTH), BF16),
                        pltpu.VMEM((nb, SB_WIDTH, tq), BF16),
                        pltpu.VMEM((nb, tq, SB_WIDTH), BF16),
                        pltpu.SemaphoreType.DMA((2, 3)),
                        pltpu.VMEM((nv, tq, LANES), F32), pltpu.VMEM((nv // 2, tq, LANES), F32)],
        compiler_params=pltpu.CompilerParams(
            dimension_semantics=("arbitrary",),
            vmem_limit_bytes=_vmem_limit(block_bytes, temp_bytes)),
        name="sb_attn_prompt",
    )(q, gsb, uu, kt, vb)


def _attn_sample_kernel(q_ref, kn_ref, vn_ref, g_ref, uu_ref, ck_hbm, cv_hbm, o_ref,
                        kwin, vwin, sem, rem_s, acc_s, *, t_new, past, nb):
    rows = HEADS * t_new
    uu = uu_ref[...]
    n_sub = past // KSUB
    contract_last = (((1,), (1,)), ((), ()))

    def fetch(which, b, j, slot):
        src, dst = ((ck_hbm, kwin), (cv_hbm, vwin))[which]
        first = pl.multiple_of(j * KSUB, KSUB)
        return pltpu.make_async_copy(src.at[b, :, :, pl.ds(first, KSUB)], dst.at[b, slot],
                                     sem.at[which, b, slot])

    for b in range(nb):
        for slot in range(EAGER_CACHE_SUBS):
            for which in range(2):
                fetch(which, b, n_sub - 1 - slot, slot).start()

    row = lax.broadcasted_iota(jnp.int32, (rows, KSUB), 0)
    col = lax.broadcasted_iota(jnp.int32, (rows, KSUB), 1)
    new_vis = col < (row % t_new)
    pad = jnp.zeros((KSUB - t_new, SB_WIDTH), F32)

    def new_block(ref, b):
        blk = jnp.concatenate([ref[b], pad], axis=0).astype(BF16)
        return [blk[:, h * HEAD_DIM:(h + 1) * HEAD_DIM] for h in range(HEADS)]

    def cached(win, b, slot):
        return [win[b, slot, h].astype(BF16) for h in range(HEADS)]

    def sweep(b, tiles, rem, acc):
        q = q_ref[b]
        qh = [q[:, h * HEAD_DIM:(h + 1) * HEAD_DIM] for h in range(HEADS)]
        zs = [jnp.concatenate(
            [jnp.dot(qh[h], keys[h], preferred_element_type=F32) if dim_major else
             lax.dot_general(qh[h], keys[h], contract_last, preferred_element_type=F32)
             for h in range(HEADS)], axis=0) for keys, _, dim_major, _ in tiles]
        splits = [_split_bf16(_neg_log2_keep(z, t[3])) for z, t in zip(zs, tiles)]
        sums = jnp.dot(jnp.concatenate(splits, axis=0), uu, preferred_element_type=F32)
        for i, (z, (_, vals, dim_major, vis)) in enumerate(zip(zs, tiles)):
            w, rem = _sb_weights(z, vis, sums[i * rows:(i + 1) * rows], rem)
            wh = [w[h * t_new:(h + 1) * t_new] for h in range(HEADS)]
            acc = [a + (lax.dot_general(wh[h], vals[h], contract_last, preferred_element_type=F32)
                        if dim_major else jnp.dot(wh[h], vals[h], preferred_element_type=F32))
                   for h, a in enumerate(acc)]
        return rem, acc

    def alive_of(rem):
        return (jnp.max(rem) > REM_DEAD_LOG2).astype(jnp.int32)

    for b in range(nb):
        for slot in range(EAGER_CACHE_SUBS):
            for which in range(2):
                fetch(which, b, n_sub - 1 - slot, slot).wait()
    alive = jnp.int32(0)
    for b in range(nb):
        tiles = [(new_block(kn_ref, b), new_block(vn_ref, b), False, new_vis)]
        tiles += [(cached(kwin, b, s), cached(vwin, b, s), True, None)
                  for s in range(EAGER_CACHE_SUBS)]
        rem, acc = sweep(b, tiles, jnp.zeros((rows, LANES), F32),
                         [jnp.zeros((t_new, HEAD_DIM), F32)] * HEADS)
        rem_s[b] = rem
        acc_s[b] = jnp.concatenate(acc, axis=1)
        alive = jnp.maximum(alive, alive_of(rem))

    def cond(c):
        j, alive = c
        return jnp.logical_and(j >= 0, alive > 0)

    def body(c):
        j, _ = c
        for b in range(nb):
            for which in range(2):
                fetch(which, b, j, 0).start()
        alive = jnp.int32(0)
        for b in range(nb):
            for which in range(2):
                fetch(which, b, j, 0).wait()
            acc_b = acc_s[b]
            rem, acc = sweep(b, [(cached(kwin, b, 0), cached(vwin, b, 0), True, None)], rem_s[b],
                             [acc_b[:, h * HEAD_DIM:(h + 1) * HEAD_DIM] for h in range(HEADS)])
            rem_s[b] = rem
            acc_s[b] = jnp.concatenate(acc, axis=1)
            alive = jnp.maximum(alive, alive_of(rem))
        return j - 1, alive

    lax.while_loop(cond, body, (jnp.int32(n_sub - 1 - EAGER_CACHE_SUBS), alive))
    for b in range(nb):
        o_ref[b] = (acc_s[b] * g_ref[b].astype(F32)).astype(BF16)


def _attn_sample(q, cache_k, cache_v, k_new, v_new, gsb):
    nb, t_new, _ = q.shape
    past = cache_k.shape[3]
    assert cache_k.shape == (nb, HEADS, HEAD_DIM, past)
    assert past % KSUB == 0 and past // KSUB >= EAGER_CACHE_SUBS and t_new <= KSUB
    uu = _suffix_sum_matrix()
    rows = HEADS * t_new
    whole = lambda a: pl.BlockSpec(a.shape, lambda i: (0,) * a.ndim)
    window = (nb, EAGER_CACHE_SUBS, HEADS, HEAD_DIM, KSUB)
    block_bytes = sum(_nbytes(a.shape, a.dtype) for a in (q, k_new, v_new, gsb, uu, q))
    temp_bytes = (2 * _nbytes(window, F32) + _nbytes((nb, rows, LANES), F32)
                  + _nbytes((nb, t_new, SB_WIDTH), F32)
                  + 16 * (EAGER_CACHE_SUBS + 1) * _nbytes((rows, KSUB), F32))
    return pl.pallas_call(
        functools.partial(_attn_sample_kernel, t_new=t_new, past=past, nb=nb),
        grid=(1,),
        in_specs=[whole(q), whole(k_new), whole(v_new), whole(gsb), whole(uu),
                  pl.BlockSpec(memory_space=pltpu.HBM), pl.BlockSpec(memory_space=pltpu.HBM)],
        out_specs=whole(q),
        out_shape=jax.ShapeDtypeStruct((nb, t_new, SB_WIDTH), BF16),
        scratch_shapes=[pltpu.VMEM(window, F32), pltpu.VMEM(window, F32),
                        pltpu.SemaphoreType.DMA((2, nb, EAGER_CACHE_SUBS)),
                        pltpu.VMEM((nb, rows, LANES), F32),
                        pltpu.VMEM((nb, t_new, SB_WIDTH), F32)],
        compiler_params=pltpu.CompilerParams(
            dimension_semantics=("arbitrary",),
            vmem_limit_bytes=max(_vmem_limit(block_bytes, temp_bytes),
                                 VMEM_BYTES_V7X - _nbytes(cache_k.shape, F32))),
        name="sb_attn_sample",
    )(q, k_new, v_new, gsb, uu, cache_k, cache_v)


def _outproj_kernel(x_ref, osb_ref, osgu_ref, p_ref, wo_ref, pg_ref, wg_ref, wp_ref, y_ref):
    o = jnp.concatenate([osb_ref[0], osgu_ref[0]], axis=1)
    h = x_ref[0] + jnp.dot(o, wo_ref[...], preferred_element_type=F32)
    ms = jnp.mean(h * h, axis=-1, keepdims=True)
    hn = (h * lax.rsqrt(ms + EPS) * pg_ref[...]).astype(BF16)
    gate_logit = jnp.dot(hn, wg_ref[...], preferred_element_type=F32)
    gate = 1.0 / (1.0 + jnp.exp(-gate_logit))
    pp = jnp.dot(p_ref[0].astype(BF16), wp_ref[...], preferred_element_type=F32)
    y_ref[0] = h + gate * pp


def _outproj(x, osb, osgu, p, w_out_bf, ple_norm_g, w_gate_bf, w_proj_bf, *, tm, name):
    b, t, _ = x.shape
    assert t % tm == 0
    tok = lambda w: pl.BlockSpec((1, tm, w), lambda bi, i: (bi, i, 0))
    const2 = lambda a: pl.BlockSpec(a.shape, lambda bi, i: (0, 0), pipeline_mode=pl.Buffered(1))
    pg = ple_norm_g.reshape(1, D_MODEL)
    block_bytes = (2 * _nbytes((tm, D_MODEL), F32) + 2 * _nbytes((tm, SEG), BF16)
                   + _nbytes((tm, PLE_DIM), F32) + _nbytes(w_out_bf.shape, BF16)
                   + _nbytes(w_gate_bf.shape, BF16) + _nbytes(w_proj_bf.shape, BF16))
    temp_bytes = 6 * _nbytes((tm, D_MODEL), F32)
    return pl.pallas_call(
        _outproj_kernel,
        grid=(b, t // tm),
        in_specs=[tok(D_MODEL), tok(SEG), tok(SEG), tok(PLE_DIM), const2(w_out_bf), const2(pg),
                  const2(w_gate_bf), const2(w_proj_bf)],
        out_specs=tok(D_MODEL),
        out_shape=jax.ShapeDtypeStruct((b, t, D_MODEL), F32),
        compiler_params=pltpu.CompilerParams(
            dimension_semantics=("arbitrary", "arbitrary"),
            vmem_limit_bytes=_vmem_limit(block_bytes, temp_bytes)),
        name=name,
    )(x, osb, osgu, p, w_out_bf, pg, w_gate_bf, w_proj_bf)


def _sgu_bias_rows(sgu_b_l, period):
    per_pos = jnp.tile(sgu_b_l[:, :period].T, (SGU_CHUNK // period, 1))
    return jnp.repeat(per_pos, GROUP_W, axis=1)


def kernel(x_prompt, x_sample, cache_k, cache_v, p_prompt, p_sample, norm_g, w_in, q_norm_g,
           k_norm_g, sgu_norm_g, sgu_w, sgu_b, w_out, ple_norm_g, w_ple_gate, w_ple_proj):
    depth = w_in.shape[0]
    assert depth == 1, "one layer per call"
    l = 0
    b_p, t_p, _ = x_prompt.shape
    b_s, t_s, _ = x_sample.shape
    past = cache_k.shape[2]
    n_s = b_s * t_s
    assert n_s == SGU_CHUNK and SGU_CHUNK % t_s == 0

    w_in_bf = w_in[l].astype(BF16)
    w_out_bf = w_out[l].astype(BF16)
    w_gate_bf = w_ple_gate[l].astype(BF16)
    w_proj_bf = w_ple_proj[l].astype(BF16)

    q, k_feat, v_feat, kt, vb, gsb, osgu = _inproj(
        x_prompt, norm_g[l], w_in_bf, q_norm_g[l], k_norm_g[l], sgu_norm_g[l],
        sgu_w[l], _sgu_bias_rows(sgu_b[l], SGU_CHUNK),
        tm=TM_IN, period=SGU_CHUNK, attn_layout=True)
    osb = _attn_prompt(q, kt, vb, gsb, tq=TQ, hps=HEADS_PER_STEP)
    y_prompt = _outproj(x_prompt, osb, osgu, p_prompt[l], w_out_bf, ple_norm_g[l], w_gate_bf,
                        w_proj_bf, tm=TM_OUT, name="outproj_prompt")

    rep = SGU_CHUNK // t_s
    sgu_w_s = jnp.tile(sgu_w[l][:, :t_s, :t_s], (1, rep, rep))
    xs = x_sample.reshape(1, n_s, D_MODEL)
    q_s, k_s, v_s, gsb_s, osgu_s, vs_s = _inproj(
        xs, norm_g[l], w_in_bf, q_norm_g[l], k_norm_g[l], sgu_norm_g[l],
        sgu_w_s, _sgu_bias_rows(sgu_b[l], t_s),
        tm=n_s, period=t_s, attn_layout=False)
    shp = (b_s, t_s, SB_WIDTH)
    to_feat = lambda c: jnp.transpose(c[l], (0, 2, 3, 1))
    osb_s = _attn_sample(q_s.reshape(shp), to_feat(cache_k), to_feat(cache_v),
                         k_s.reshape(shp), v_s.reshape(shp),
                         gsb_s.reshape(shp))
    y_sample = _outproj(xs, osb_s.reshape(1, n_s, SB_WIDTH), osgu_s, p_sample[l].reshape(1, n_s, PLE_DIM),
                        w_out_bf, ple_norm_g[l], w_gate_bf, w_proj_bf, tm=n_s,
                        name="outproj_sample").reshape(b_s, t_s, D_MODEL)

    head_shape = lambda a, bb, tt: a.reshape(1, bb, tt, HEADS, HEAD_DIM)
    from_feat = lambda a: jnp.transpose(a.reshape(b_p, HEADS, HEAD_DIM, t_p), (0, 3, 1, 2))[None]
    return (y_prompt, y_sample,
            from_feat(k_feat), from_feat(v_feat),
            head_shape(k_s, b_s, t_s), head_shape(v_s, b_s, t_s),
            vs_s.reshape(1, b_s, t_s, GROUPS, GROUP_W))
```

```python
import functools
import math

import numpy as np
import jax
import jax.numpy as jnp
from jax import lax
from jax.experimental import pallas as pl
from jax.experimental.pallas import tpu as pltpu

F32 = jnp.float32
BF16 = jnp.bfloat16

LANES = 128
MXU_TILE = 256
VMEM_BYTES_V7X = 64 * 1024 * 1024

D_MODEL = 1024
PLE_DIM = 256
HEADS = 8
HEAD_DIM = 64
SB_WIDTH = HEADS * HEAD_DIM
GROUPS = 4
GROUP_W = 128
SGU_WIDTH = GROUPS * GROUP_W
SGU_CHUNK = 128
SEG = 512
EPS = 1e-6

Q_SCALE = HEAD_DIM ** -0.5 * math.log2(math.e)

KSUB = LANES
NSUB = 2
TQ = NSUB * KSUB
HEADS_PER_STEP = 8
REM_DEAD_LOG2 = -152.0
SOFTPLUS_LINEAR = 64.0
EAGER_CACHE_SUBS = 2
EAGER_SEEN = KSUB + 48
TM_IN = 1024
TM_OUT = 1024


def _vmem_limit(block_bytes, temp_bytes):
    need = 2 * block_bytes + temp_bytes
    return int(min(need, VMEM_BYTES_V7X - 8 * 1024 * 1024))


def _nbytes(shape, dtype):
    return int(np.prod(shape)) * jnp.dtype(dtype).itemsize


def _suffix_sum_matrix():
    j = np.arange(KSUB)[:, None]
    s = np.arange(KSUB)[None, :]
    one = np.concatenate([np.where(j >= s, -1.0, 0.0), -np.ones((KSUB, LANES))], axis=1)
    return jnp.asarray(np.concatenate([one, one], axis=0), dtype=BF16)


def _head_mean_matrix():
    a = np.arange(MXU_TILE)
    bd = np.where(a[:, None] // HEAD_DIM == a[None, :] // HEAD_DIM, 1.0 / HEAD_DIM, 0.0)
    return jnp.asarray(bd, dtype=BF16)


def _split_bf16(x):
    hi = x.astype(BF16)
    lo = (x - hi.astype(F32)).astype(BF16)
    return jnp.concatenate([hi, lo], axis=1)


def _gelu_tanh(x):
    return 0.5 * x * (1.0 + jnp.tanh(math.sqrt(2.0 / math.pi) * (x + 0.044715 * (x * x * x))))


def _silu(x):
    return x / (1.0 + jnp.exp(-x))


def _inproj_kernel(x_ref, ng_ref, w_ref, qg_ref, kg_ref, sg_ref, hm_ref, sw_ref, sb_ref,
                   *out_refs, tm, period, attn_layout):
    if attn_layout:
        q_ref, ktf_ref, vtf_ref, kt_ref, vb_ref, gsb_ref, osgu_ref = out_refs
    else:
        q_ref, k_ref, v_ref, gsb_ref, osgu_ref, vs_ref = out_refs

    x = x_ref[0]
    ms = jnp.mean(x * x, axis=-1, keepdims=True)
    xn = (x * lax.rsqrt(ms + EPS) * ng_ref[...]).astype(BF16)

    def seg(i):
        return jnp.dot(xn, w_ref[:, i * SEG:(i + 1) * SEG], preferred_element_type=F32)

    def head_rms(t, g):
        sq = (t * t).astype(BF16)
        ms_h = jnp.concatenate(
            [jnp.dot(sq[:, c:c + MXU_TILE], hm_ref[...], preferred_element_type=F32)
             for c in range(0, SB_WIDTH, MXU_TILE)], axis=1)
        return t * lax.rsqrt(ms_h + EPS) * g

    q = head_rms(seg(0), qg_ref[...])
    q_ref[0] = (q * Q_SCALE).astype(BF16)

    k = head_rms(seg(1), kg_ref[...])
    v = seg(2)
    if attn_layout:
        k_t = k.T
        ktf_ref[0] = k_t
        kt_ref[0] = k_t.astype(BF16)
        vtf_ref[0] = v.T
        vb_ref[0] = v.astype(BF16)
    else:
        k_ref[0] = k
        v_ref[0] = v

    gsb_ref[0] = _silu(seg(3)).astype(BF16)

    u = _gelu_tanh(seg(4))
    vs_raw = _gelu_tanh(seg(5))
    vs_groups = []
    for g in range(GROUPS):
        t = vs_raw[:, g * GROUP_W:(g + 1) * GROUP_W]
        ms_g = jnp.mean(t * t, axis=-1, keepdims=True)
        vs_groups.append(t * lax.rsqrt(ms_g + EPS) * sg_ref[:, g * GROUP_W:(g + 1) * GROUP_W])
    if not attn_layout:
        vs_ref[0] = jnp.concatenate(vs_groups, axis=1)

    row = lax.broadcasted_iota(jnp.int32, (SGU_CHUNK, SGU_CHUNK), 0)
    col = lax.broadcasted_iota(jnp.int32, (SGU_CHUNK, SGU_CHUNK), 1)
    keep = row >= col
    if period != SGU_CHUNK:
        keep = keep & ((row // period) == (col // period))
    s_groups = []
    for g in range(GROUPS):
        wm = jnp.where(keep, sw_ref[g], 0.0).astype(BF16)
        vg = vs_groups[g].astype(BF16)
        chunks = [jnp.dot(wm, vg[c * SGU_CHUNK:(c + 1) * SGU_CHUNK], preferred_element_type=F32)
                  + sb_ref[:, g * GROUP_W:(g + 1) * GROUP_W]
                  for c in range(tm // SGU_CHUNK)]
        s_groups.append(jnp.concatenate(chunks, axis=0) if len(chunks) > 1 else chunks[0])
    s = jnp.concatenate(s_groups, axis=1)

    osgu_ref[0] = (u * s * _silu(seg(6))).astype(BF16)


def _inproj(x, norm_g, w_in_bf, q_norm_g, k_norm_g, sgu_norm_g, sgu_w_tiled, sgu_bias, *,
            tm, period, attn_layout):
    b, t, _ = x.shape
    assert t % tm == 0 and tm % SGU_CHUNK == 0
    grid = (b, t // tm)
    tok = lambda w: pl.BlockSpec((1, tm, w), lambda bi, i: (bi, i, 0))
    const2 = lambda a: pl.BlockSpec(a.shape, lambda bi, i: (0, 0))
    const3 = lambda a: pl.BlockSpec(a.shape, lambda bi, i: (0, 0, 0))

    ng = norm_g.reshape(1, D_MODEL)
    qg = jnp.tile(q_norm_g, HEADS).reshape(1, SB_WIDTH)
    kg = jnp.tile(k_norm_g, HEADS).reshape(1, SB_WIDTH)
    sg = sgu_norm_g.reshape(1, SGU_WIDTH)
    hm = _head_mean_matrix()

    act = lambda dt: jax.ShapeDtypeStruct((b, t, SEG), dt)
    if attn_layout:
        feat = lambda dt: jax.ShapeDtypeStruct((b, SB_WIDTH, t), dt)
        feat_blk = pl.BlockSpec((1, SB_WIDTH, tm), lambda bi, i: (bi, 0, i))
        out_shape = (act(BF16), feat(F32), feat(F32), feat(BF16), act(BF16), act(BF16), act(BF16))
        out_specs = (tok(SEG), feat_blk, feat_blk, feat_blk, tok(SEG), tok(SEG), tok(SEG))
    else:
        out_shape = (act(BF16), act(F32), act(F32), act(BF16), act(BF16), act(F32))
        out_specs = (tok(SEG),) * 6

    block_bytes = (_nbytes((tm, D_MODEL), F32) + _nbytes(w_in_bf.shape, BF16)
                   + _nbytes(hm.shape, BF16) + _nbytes(sgu_w_tiled.shape, F32)
                   + _nbytes(sgu_bias.shape, F32) + 7 * _nbytes((tm, SEG), F32))
    temp_bytes = 12 * _nbytes((tm, SEG), F32)
    return pl.pallas_call(
        functools.partial(_inproj_kernel, tm=tm, period=period, attn_layout=attn_layout),
        grid=grid,
        in_specs=[tok(D_MODEL), const2(ng), const2(w_in_bf), const2(qg), const2(kg), const2(sg),
                  const2(hm), const3(sgu_w_tiled), const2(sgu_bias)],
        out_specs=out_specs,
        out_shape=out_shape,
        compiler_params=pltpu.CompilerParams(
            dimension_semantics=("parallel", "parallel"),
            vmem_limit_bytes=_vmem_limit(block_bytes, temp_bytes)),
        name="inproj_prompt" if attn_layout else "inproj_sample",
    )(x, ng, w_in_bf, qg, kg, sg, hm, sgu_w_tiled, sgu_bias)


def _neg_log2_keep(z, vis):
    sp = jnp.maximum(z, jnp.log2(1.0 + jnp.exp2(jnp.minimum(z, SOFTPLUS_LINEAR))))
    return sp if vis is None else jnp.where(vis, sp, 0.0)


def _sb_weights(z, vis, sums, rem):
    w = jnp.exp2(z + sums[:, :KSUB] + rem)
    if vis is not None:
        w = jnp.where(vis, w, 0.0)
    return w.astype(BF16), rem + sums[:, KSUB:]


def _attn_prompt_kernel(q_ref, g_ref, uu_ref, kt_hbm, v_hbm, o_ref,
                        kwin, vwin, kold, vold, sem, rem_ref, acc_ref, *, tq, hps, nb, nq):
    qi = pl.program_id(0)
    slot = qi % 2
    nv = nb * hps

    def window(step, slot_):
        first = pl.multiple_of(jnp.maximum(step - 1, 0) * tq, tq)
        return ([pltpu.make_async_copy(kt_hbm.at[r, :, pl.ds(first, 2 * tq)], kwin.at[slot_, r],
                                       sem.at[0, slot_]) for r in range(nb)]
                + [pltpu.make_async_copy(v_hbm.at[r, pl.ds(first, 2 * tq), :], vwin.at[slot_, r],
                                         sem.at[1, slot_]) for r in range(nb)])

    def older(n):
        first = pl.multiple_of((qi - n) * tq, tq)
        return ([pltpu.make_async_copy(kt_hbm.at[r, :, pl.ds(first, tq)], kold.at[r], sem.at[0, 2])
                 for r in range(nb)]
                + [pltpu.make_async_copy(v_hbm.at[r, pl.ds(first, tq), :], vold.at[r], sem.at[1, 2])
                   for r in range(nb)])

    @pl.when(qi == 0)
    def _():
        for c in window(qi, slot):
            c.start()

    for c in window(qi, slot):
        c.wait()

    @pl.when(qi + 1 < nq)
    def _():
        for c in window(qi + 1, 1 - slot):
            c.start()

    qh = [q_ref[v // hps][:, (v % hps) * HEAD_DIM:(v % hps + 1) * HEAD_DIM] for v in range(nv)]
    uu = uu_ref[...]
    lane = lax.broadcasted_iota(jnp.int32, (KSUB, LANES), 1)
    first_head = lane < HEAD_DIM

    def keys_at(src, r, ks):
        return (kwin[slot, r, :, pl.ds(ks, KSUB)] if src == "window"
                else kold[r, :, pl.ds(ks, KSUB)])

    def pair_values(src, ks, p):
        pairs = hps // 2
        lanes = pl.ds((p % pairs) * LANES, LANES)
        vv = (vwin[slot, p // pairs, pl.ds(ks, KSUB), lanes] if src == "window"
              else vold[p // pairs, pl.ds(ks, KSUB), lanes])
        zero = jnp.zeros_like(vv)
        return jnp.concatenate([jnp.where(first_head, vv, zero),
                                jnp.where(first_head, zero, vv)], axis=0)

    def sweep(tiles, rems, accs):
        def put(full, r0, r1, part):
            pieces = ([full[:r0]] if r0 else []) + [part] + ([full[r1:]] if r1 < tq else [])
            return jnp.concatenate(pieces, axis=0) if len(pieces) > 1 else part

        zs, splits = [], []
        for src, ks, r0, r1, diagonal in tiles:
            kts = [keys_at(src, r, ks) for r in range(nb)]
            vis = None
            if diagonal:
                row = lax.broadcasted_iota(jnp.int32, (r1 - r0, KSUB), 0)
                col = lax.broadcasted_iota(jnp.int32, (r1 - r0, KSUB), 1)
                vis = col < row
            for v in range(nv):
                h = v % hps
                z = jnp.dot(qh[v][r0:r1], kts[v // hps][h * HEAD_DIM:(h + 1) * HEAD_DIM],
                            preferred_element_type=F32)
                zs.append((z, vis))
                splits.append(_split_bf16(_neg_log2_keep(z, vis)))
        sums = jnp.dot(jnp.concatenate(splits, axis=0), uu, preferred_element_type=F32)
        rems, accs = list(rems), list(accs)
        off, i = 0, 0
        for src, ks, r0, r1, _ in tiles:
            ws = []
            for h in range(nv):
                z, vis = zs[i]
                i += 1
                w, new = _sb_weights(z, vis, sums[off:off + r1 - r0], rems[h][r0:r1])
                off += r1 - r0
                rems[h] = put(rems[h], r0, r1, new)
                ws.append(w)
            for p in range(nv // 2):
                d = jnp.dot(jnp.concatenate(ws[2 * p:2 * p + 2], axis=1), pair_values(src, ks, p),
                            preferred_element_type=F32)
                accs[p] = put(accs[p], r0, r1, accs[p][r0:r1] + d)
        return rems, accs

    def diagonal_tiles(at):
        return [("window", at + s * KSUB, s * KSUB, tq, True) for s in reversed(range(NSUB))]

    def key_block_tiles(src, at, rows=None):
        rows = rows or {s: (0, tq) for s in range(NSUB)}
        return [(src, at + s * KSUB, *rows[s], False) for s in reversed(range(NSUB)) if s in rows]

    zero_rems = [jnp.zeros((tq, LANES), F32)] * nv
    zero_accs = [jnp.zeros((tq, LANES), F32)] * (nv // 2)

    def finish(accs):
        pairs = hps // 2
        for r in range(nb):
            acc = jnp.concatenate(accs[r * pairs:(r + 1) * pairs], axis=1)
            o_ref[r] = (acc * g_ref[r].astype(F32)).astype(BF16)

    def any_alive(rems):
        most = functools.reduce(jnp.maximum, rems)
        return (jnp.max(most) > REM_DEAD_LOG2).astype(jnp.int32)

    @pl.when(qi == 0)
    def _():
        finish(sweep(diagonal_tiles(0), zero_rems, zero_accs)[1])

    def load_state():
        return [rem_ref[h] for h in range(nv)], [acc_ref[p] for p in range(nv // 2)]

    def store_state(rems, accs):
        for h in range(nv):
            rem_ref[h] = rems[h]
        for p in range(nv // 2):
            acc_ref[p] = accs[p]

    eager_rows = {s: min(tq, EAGER_SEEN - (NSUB - 1 - s) * KSUB) for s in range(NSUB)
                  if EAGER_SEEN > (NSUB - 1 - s) * KSUB}

    @pl.when(qi > 0)
    def _():
        first = key_block_tiles("window", 0, {s: (0, e) for s, e in eager_rows.items()})
        rems, accs = sweep(diagonal_tiles(tq) + first, zero_rems, zero_accs)
        store_state(rems, accs)
        alive = any_alive(rems)
        late = {s: (eager_rows.get(s, 0), tq) for s in range(NSUB) if eager_rows.get(s, 0) < tq}
        if late:
            @pl.when(any_alive([r[min(e for e, _ in late.values()):] for r in rems]) > 0)
            def _():
                store_state(*sweep(key_block_tiles("window", 0, late), *load_state()))

        def cond(c):
            n, alive = c
            return jnp.logical_and(n <= qi, alive > 0)

        def body(c):
            n, _ = c
            for cp in older(n):
                cp.start()
            for cp in older(n):
                cp.wait()
            rems, accs = sweep(key_block_tiles("older", 0), *load_state())
            store_state(rems, accs)
            return n + 1, any_alive(rems)

        lax.while_loop(cond, body, (jnp.int32(2), alive))
        finish(load_state()[1])


def _attn_prompt(q, kt, vb, gsb, *, tq, hps):
    nb, t, _ = q.shape
    assert t % tq == 0 and t >= 2 * tq and tq == NSUB * KSUB and hps == HEADS
    nv = nb * hps
    nq = t // tq
    uu = _suffix_sum_matrix()
    row_blk = pl.BlockSpec((nb, tq, SB_WIDTH), lambda i: (0, i, 0))
    in_hbm = pl.BlockSpec(memory_space=pltpu.HBM)
    block_bytes = 3 * _nbytes((nb, tq, SB_WIDTH), BF16) + _nbytes(uu.shape, BF16)
    temp_bytes = ((nv + nv // 2) * _nbytes((tq, LANES), F32)
                  + 5 * _nbytes((nb, SB_WIDTH, 2 * tq), BF16)
                  + 5 * (NSUB + 2) * nv * _nbytes((tq, KSUB), F32))
    return pl.pallas_call(
        functools.partial(_attn_prompt_kernel, tq=tq, hps=hps, nb=nb, nq=nq),
        grid=(nq,),
        in_specs=[row_blk, row_blk, pl.BlockSpec(uu.shape, lambda i: (0, 0)), in_hbm, in_hbm],
        out_specs=row_blk,
        out_shape=jax.ShapeDtypeStruct((nb, t, SB_WIDTH), BF16),
        scratch_shapes=[pltpu.VMEM((2, nb, SB_WIDTH, 2 * tq), BF16),
                        pltpu.VMEM((2, nb, 2 * tq, SB_WIDTH), BF16),
                        pltpu.VMEM((nb, SB_WIDTH, tq), BF16),
                        pltpu.VMEM((nb, tq, SB_WIDTH), BF16),
                        pltpu.SemaphoreType.DMA((2, 3)),
                        pltpu.VMEM((nv, tq, LANES), F32), pltpu.VMEM((nv // 2, tq, LANES), F32)],
        compiler_params=pltpu.CompilerParams(
            dimension_semantics=("arbitrary",),
            vmem_limit_bytes=_vmem_limit(block_bytes, temp_bytes)),
        name="sb_attn_prompt",
    )(q, gsb, uu, kt, vb)


def _attn_sample_kernel(q_ref, kn_ref, vn_ref, g_ref, uu_ref, ck_hbm, cv_hbm, o_ref,
                        kwin, vwin, sem, rem_s, acc_s, *, t_new, past, nb):
    rows = HEADS * t_new
    uu = uu_ref[...]
    n_sub = past // KSUB
    contract_last = (((1,), (1,)), ((), ()))

    def fetch(which, b, j, slot):
        src, dst = ((ck_hbm, kwin), (cv_hbm, vwin))[which]
        first = pl.multiple_of(j * KSUB, KSUB)
        return pltpu.make_async_copy(src.at[b, :, :, pl.ds(first, KSUB)], dst.at[b, slot],
                                     sem.at[which, b, slot])

    for b in range(nb):
        for slot in range(EAGER_CACHE_SUBS):
            for which in range(2):
                fetch(which, b, n_sub - 1 - slot, slot).start()

    row = lax.broadcasted_iota(jnp.int32, (rows, KSUB), 0)
    col = lax.broadcasted_iota(jnp.int32, (rows, KSUB), 1)
    new_vis = col < (row % t_new)
    pad = jnp.zeros((KSUB - t_new, SB_WIDTH), F32)

    def new_block(ref, b):
        blk = jnp.concatenate([ref[b], pad], axis=0).astype(BF16)
        return [blk[:, h * HEAD_DIM:(h + 1) * HEAD_DIM] for h in range(HEADS)]

    def cached(win, b, slot):
        return [win[b, slot, h].astype(BF16) for h in range(HEADS)]

    def sweep(b, tiles, rem, acc):
        q = q_ref[b]
        qh = [q[:, h * HEAD_DIM:(h + 1) * HEAD_DIM] for h in range(HEADS)]
        zs = [jnp.concatenate(
            [jnp.dot(qh[h], keys[h], preferred_element_type=F32) if dim_major else
             lax.dot_general(qh[h], keys[h], contract_last, preferred_element_type=F32)
             for h in range(HEADS)], axis=0) for keys, _, dim_major, _ in tiles]
        splits = [_split_bf16(_neg_log2_keep(z, t[3])) for z, t in zip(zs, tiles)]
        sums = jnp.dot(jnp.concatenate(splits, axis=0), uu, preferred_element_type=F32)
        for i, (z, (_, vals, dim_major, vis)) in enumerate(zip(zs, tiles)):
            w, rem = _sb_weights(z, vis, sums[i * rows:(i + 1) * rows], rem)
            wh = [w[h * t_new:(h + 1) * t_new] for h in range(HEADS)]
            acc = [a + (lax.dot_general(wh[h], vals[h], contract_last, preferred_element_type=F32)
                        if dim_major else jnp.dot(wh[h], vals[h], preferred_element_type=F32))
                   for h, a in enumerate(acc)]
        return rem, acc

    def alive_of(rem):
        return (jnp.max(rem) > REM_DEAD_LOG2).astype(jnp.int32)

    for b in range(nb):
        for slot in range(EAGER_CACHE_SUBS):
            for which in range(2):
                fetch(which, b, n_sub - 1 - slot, slot).wait()
    alive = jnp.int32(0)
    for b in range(nb):
        tiles = [(new_block(kn_ref, b), new_block(vn_ref, b), False, new_vis)]
        tiles += [(cached(kwin, b, s), cached(vwin, b, s), True, None)
                  for s in range(EAGER_CACHE_SUBS)]
        rem, acc = sweep(b, tiles, jnp.zeros((rows, LANES), F32),
                         [jnp.zeros((t_new, HEAD_DIM), F32)] * HEADS)
        rem_s[b] = rem
        acc_s[b] = jnp.concatenate(acc, axis=1)
        alive = jnp.maximum(alive, alive_of(rem))

    def cond(c):
        j, alive = c
        return jnp.logical_and(j >= 0, alive > 0)

    def body(c):
        j, _ = c
        for b in range(nb):
            for which in range(2):
                fetch(which, b, j, 0).start()
        alive = jnp.int32(0)
        for b in range(nb):
            for which in range(2):
                fetch(which, b, j, 0).wait()
            acc_b = acc_s[b]
            rem, acc = sweep(b, [(cached(kwin, b, 0), cached(vwin, b, 0), True, None)], rem_s[b],
                             [acc_b[:, h * HEAD_DIM:(h + 1) * HEAD_DIM] for h in range(HEADS)])
            rem_s[b] = rem
            acc_s[b] = jnp.concatenate(acc, axis=1)
            alive = jnp.maximum(alive, alive_of(rem))
        return j - 1, alive

    lax.while_loop(cond, body, (jnp.int32(n_sub - 1 - EAGER_CACHE_SUBS), alive))
    for b in range(nb):
        o_ref[b] = (acc_s[b] * g_ref[b].astype(F32)).astype(BF16)


def _attn_sample(q, cache_k, cache_v, k_new, v_new, gsb):
    nb, t_new, _ = q.shape
    past = cache_k.shape[3]
    assert cache_k.shape == (nb, HEADS, HEAD_DIM, past)
    assert past % KSUB == 0 and past // KSUB >= EAGER_CACHE_SUBS and t_new <= KSUB
    uu = _suffix_sum_matrix()
    rows = HEADS * t_new
    whole = lambda a: pl.BlockSpec(a.shape, lambda i: (0,) * a.ndim)
    window = (nb, EAGER_CACHE_SUBS, HEADS, HEAD_DIM, KSUB)
    block_bytes = sum(_nbytes(a.shape, a.dtype) for a in (q, k_new, v_new, gsb, uu, q))
    temp_bytes = (2 * _nbytes(window, F32) + _nbytes((nb, rows, LANES), F32)
                  + _nbytes((nb, t_new, SB_WIDTH), F32)
                  + 16 * (EAGER_CACHE_SUBS + 1) * _nbytes((rows, KSUB), F32))
    return pl.pallas_call(
        functools.partial(_attn_sample_kernel, t_new=t_new, past=past, nb=nb),
        grid=(1,),
        in_specs=[whole(q), whole(k_new), whole(v_new), whole(gsb), whole(uu),
                  pl.BlockSpec(memory_space=pltpu.HBM), pl.BlockSpec(memory_space=pltpu.HBM)],
        out_specs=whole(q),
        out_shape=jax.ShapeDtypeStruct((nb, t_new, SB_WIDTH), BF16),
        scratch_shapes=[pltpu.VMEM(window, F32), pltpu.VMEM(window, F32),
                        pltpu.SemaphoreType.DMA((2, nb, EAGER_CACHE_SUBS)),
                        pltpu.VMEM((nb, rows, LANES), F32),
                        pltpu.VMEM((nb, t_new, SB_WIDTH), F32)],
        compiler_params=pltpu.CompilerParams(
            dimension_semantics=("arbitrary",),
            vmem_limit_bytes=max(_vmem_limit(block_bytes, temp_bytes),
                                 VMEM_BYTES_V7X - _nbytes(cache_k.shape, F32))),
        name="sb_attn_sample",
    )(q, k_new, v_new, gsb, uu, cache_k, cache_v)


def _outproj_kernel(x_ref, osb_ref, osgu_ref, p_ref, wo_ref, pg_ref, wg_ref, wp_ref, y_ref):
    o = jnp.concatenate([osb_ref[0], osgu_ref[0]], axis=1)
    h = x_ref[0] + jnp.dot(o, wo_ref[...], preferred_element_type=F32)
    ms = jnp.mean(h * h, axis=-1, keepdims=True)
    hn = (h * lax.rsqrt(ms + EPS) * pg_ref[...]).astype(BF16)
    gate_logit = jnp.dot(hn, wg_ref[...], preferred_element_type=F32)
    gate = 1.0 / (1.0 + jnp.exp(-gate_logit))
    pp = jnp.dot(p_ref[0].astype(BF16), wp_ref[...], preferred_element_type=F32)
    y_ref[0] = h + gate * pp


def _outproj(x, osb, osgu, p, w_out_bf, ple_norm_g, w_gate_bf, w_proj_bf, *, tm, name):
    b, t, _ = x.shape
    assert t % tm == 0
    tok = lambda w: pl.BlockSpec((1, tm, w), lambda bi, i: (bi, i, 0))
    const2 = lambda a: pl.BlockSpec(a.shape, lambda bi, i: (0, 0))
    pg = ple_norm_g.reshape(1, D_MODEL)
    block_bytes = (2 * _nbytes((tm, D_MODEL), F32) + 2 * _nbytes((tm, SEG), BF16)
                   + _nbytes((tm, PLE_DIM), F32) + _nbytes(w_out_bf.shape, BF16)
                   + _nbytes(w_gate_bf.shape, BF16) + _nbytes(w_proj_bf.shape, BF16))
    temp_bytes = 6 * _nbytes((tm, D_MODEL), F32)
    return pl.pallas_call(
        _outproj_kernel,
        grid=(b, t // tm),
        in_specs=[tok(D_MODEL), tok(SEG), tok(SEG), tok(PLE_DIM), const2(w_out_bf), const2(pg),
                  const2(w_gate_bf), const2(w_proj_bf)],
        out_specs=tok(D_MODEL),
        out_shape=jax.ShapeDtypeStruct((b, t, D_MODEL), F32),
        compiler_params=pltpu.CompilerParams(
            dimension_semantics=("parallel", "parallel"),
            vmem_limit_bytes=_vmem_limit(block_bytes, temp_bytes)),
        name=name,
    )(x, osb, osgu, p, w_out_bf, pg, w_gate_bf, w_proj_bf)


def _sgu_bias_rows(sgu_b_l, period):
    per_pos = jnp.tile(sgu_b_l[:, :period].T, (SGU_CHUNK // period, 1))
    return jnp.repeat(per_pos, GROUP_W, axis=1)


def kernel(x_prompt, x_sample, cache_k, cache_v, p_prompt, p_sample, norm_g, w_in, q_norm_g,
           k_norm_g, sgu_norm_g, sgu_w, sgu_b, w_out, ple_norm_g, w_ple_gate, w_ple_proj):
    depth = w_in.shape[0]
    assert depth == 1, "one layer per call"
    l = 0
    b_p, t_p, _ = x_prompt.shape
    b_s, t_s, _ = x_sample.shape
    past = cache_k.shape[2]
    n_s = b_s * t_s
    assert n_s == SGU_CHUNK and SGU_CHUNK % t_s == 0

    w_in_bf = w_in[l].astype(BF16)
    w_out_bf = w_out[l].astype(BF16)
    w_gate_bf = w_ple_gate[l].astype(BF16)
    w_proj_bf = w_ple_proj[l].astype(BF16)

    q, k_feat, v_feat, kt, vb, gsb, osgu = _inproj(
        x_prompt, norm_g[l], w_in_bf, q_norm_g[l], k_norm_g[l], sgu_norm_g[l],
        sgu_w[l], _sgu_bias_rows(sgu_b[l], SGU_CHUNK),
        tm=TM_IN, period=SGU_CHUNK, attn_layout=True)
    osb = _attn_prompt(q, kt, vb, gsb, tq=TQ, hps=HEADS_PER_STEP)
    y_prompt = _outproj(x_prompt, osb, osgu, p_prompt[l], w_out_bf, ple_norm_g[l], w_gate_bf,
                        w_proj_bf, tm=TM_OUT, name="outproj_prompt")

    rep = SGU_CHUNK // t_s
    sgu_w_s = jnp.tile(sgu_w[l][:, :t_s, :t_s], (1, rep, rep))
    xs = x_sample.reshape(1, n_s, D_MODEL)
    q_s, k_s, v_s, gsb_s, osgu_s, vs_s = _inproj(
        xs, norm_g[l], w_in_bf, q_norm_g[l], k_norm_g[l], sgu_norm_g[l],
        sgu_w_s, _sgu_bias_rows(sgu_b[l], t_s),
        tm=n_s, period=t_s, attn_layout=False)
    shp = (b_s, t_s, SB_WIDTH)
    to_feat = lambda c: jnp.transpose(c[l], (0, 2, 3, 1))
    osb_s = _attn_sample(q_s.reshape(shp), to_feat(cache_k), to_feat(cache_v),
                         k_s.reshape(shp), v_s.reshape(shp),
                         gsb_s.reshape(shp))
    y_sample = _outproj(xs, osb_s.reshape(1, n_s, SB_WIDTH), osgu_s, p_sample[l].reshape(1, n_s, PLE_DIM),
                        w_out_bf, ple_norm_g[l], w_gate_bf, w_proj_bf, tm=n_s,
                        name="outproj_sample").reshape(b_s, t_s, D_MODEL)

    head_shape = lambda a, bb, tt: a.reshape(1, bb, tt, HEADS, HEAD_DIM)
    from_feat = lambda a: jnp.transpose(a.reshape(b_p, HEADS, HEAD_DIM, t_p), (0, 3, 1, 2))[None]
    return (y_prompt, y_sample,
            from_feat(k_feat), from_feat(v_feat),
            head_shape(k_s, b_s, t_s), head_shape(v_s, b_s, t_s),
            vs_s.reshape(1, b_s, t_s, GROUPS, GROUP_W))
```
